```python
import jax, jax.numpy as jnp
from jax import lax
import numpy as np

D_MODEL = 1024
BATCH = 4
SEQ = 8192
DEPTH = 1

ATT_HEADS = 8
ATT_HEAD_DIM = 64
IDX_HEADS = 4
IDX_DIM = 64
IDX_TOPK_MAX = 256
Q_BLOCK = 128
ROPE_THETA = 10000.0
MLSTM_HEADS = 8
MLSTM_QK_DIM = 64
MLSTM_V_DIM = 128
MLSTM_CHUNK = 128
CONV_WIDTH = 4
N_GROUPS = 4
EXPERTS_PER_GROUP = 4
N_EXPERTS = N_GROUPS * EXPERTS_PER_GROUP
TOP_K_IN_GROUP = 2
EXPERT_DIM = 512
DEEPNORM_ALPHA = (2.0 * DEPTH) ** 0.25
DEEPNORM_BETA = (8.0 * DEPTH) ** -0.25
LN_EPS = 1e-5
GN_EPS = 1e-6

ATT_W = ATT_HEADS * ATT_HEAD_DIM
MLSTM_QK_W = MLSTM_HEADS * MLSTM_QK_DIM
MLSTM_V_W = MLSTM_HEADS * MLSTM_V_DIM
IN_SPLITS = (ATT_W, ATT_W, ATT_W, IDX_HEADS * IDX_DIM, IDX_DIM, IDX_HEADS,
             MLSTM_QK_W, MLSTM_QK_W, MLSTM_V_W, MLSTM_HEADS, MLSTM_HEADS, MLSTM_V_W,
             D_MODEL, D_MODEL)
IN_WIDTH = sum(IN_SPLITS)

kernel_name = 'hybrid_dsa_mlstm_hmoe_block'


def _layer_norm(x, gain, bias):
    xf = x.astype(jnp.float32)
    mu = xf.mean(-1, keepdims=True)
    var = jnp.square(xf - mu).mean(-1, keepdims=True)
    return (xf - mu) * lax.rsqrt(var + LN_EPS) * gain + bias


def _rope_tables(seq, dim):
    inv = ROPE_THETA ** (-jnp.arange(0, dim, 2, dtype=jnp.float32) / dim)
    ang = jnp.arange(seq, dtype=jnp.float32)[:, None] * inv[None, :]
    return jnp.cos(ang), jnp.sin(ang)


def _rope(x, cos, sin):
    x1, x2 = jnp.split(x, 2, axis=-1)
    c = cos[None, :, None, :]
    s = sin[None, :, None, :]
    return jnp.concatenate([x1 * c - x2 * s, x1 * s + x2 * c], axis=-1)


def _dsa_sparse_attention(q, k, v, q_idx, k_idx, w_idx):
    B, S, H, Dh = q.shape
    topk = min(IDX_TOPK_MAX, S // 4)
    nb = S // Q_BLOCK
    f32 = jnp.float32

    def to_blocks(a):
        return jnp.moveaxis(a.reshape((B, nb, Q_BLOCK) + a.shape[2:]), 1, 0)

    k_idx_f = k_idx.astype(f32)
    b_ix = jnp.arange(B)[:, None, None]
    key_pos = jnp.arange(S)

    def block_fn(args):
        qb, qib, wb, start = args
        t = start + jnp.arange(Q_BLOCK)
        rel = jax.nn.relu(jnp.einsum('bqhd,bsd->bqhs', qib.astype(f32), k_idx_f))
        score = jnp.einsum('bqhs,bqh->bqs', rel, wb.astype(f32))
        causal = key_pos[None, :] <= t[:, None]
        score = jnp.where(causal[None], score, -jnp.inf)
        _, sel = lax.top_k(score, topk)
        valid = sel <= t[None, :, None]
        k_sel = k[b_ix, sel]
        v_sel = v[b_ix, sel]
        logits = jnp.einsum('bqhd,bqkhd->bqhk', qb, k_sel).astype(f32) * (Dh ** -0.5)
        logits = jnp.where(valid[:, :, None, :], logits, -jnp.inf)
        p = jax.nn.softmax(logits, axis=-1).astype(v.dtype)
        return jnp.einsum('bqhk,bqkhd->bqhd', p, v_sel)

    starts = jnp.arange(nb, dtype=jnp.int32) * Q_BLOCK
    out = lax.map(block_fn, (to_blocks(q), to_blocks(q_idx), to_blocks(w_idx), starts))
    return jnp.moveaxis(out, 0, 1).reshape(B, S, H * Dh)


def _causal_depthwise_conv(x, w):
    C = x.shape[-1]
    return lax.conv_general_dilated(x, w[:, None, :].astype(x.dtype), window_strides=(1,),
                                    padding=[(CONV_WIDTH - 1, 0)],
                                    dimension_numbers=('NWC', 'WIO', 'NWC'),
                                    feature_group_count=C)


def _mlstm_chunkwise(q, k, v, i_pre, f_pre):
    B, S, H, Dk = q.shape
    Dv = v.shape[-1]
    L = MLSTM_CHUNK
    nc = S // L
    f32 = jnp.float32

    def chunks(a):
        a = a.astype(f32).reshape((B, nc, L, H) + a.shape[3:])
        return jnp.moveaxis(jnp.moveaxis(a, 1, 0), 3, 2)

    qc = chunks(q.astype(f32) * (Dk ** -0.5))
    kc = chunks(k)
    vc = chunks(v)
    ic = chunks(i_pre)
    lfc = chunks(jax.nn.log_sigmoid(f_pre.astype(f32)))
    causal = jnp.tril(jnp.ones((L, L), dtype=bool))

    def step(carry, xs):
        C, n, m = carry
        qb, kb, vb, ib, lfb = xs
        b = jnp.cumsum(lfb, axis=-1)
        dmat = jnp.where(causal, b[..., :, None] - b[..., None, :] + ib[..., None, :], -jnp.inf)
        inter = b + m[..., None]
        m_t = jnp.maximum(inter, dmat.max(-1))
        decay_in = jnp.exp(dmat - m_t[..., None])
        w_inter = jnp.exp(inter - m_t)
        s = jnp.einsum('bhjd,bhsd->bhjs', qb, kb) * decay_in
        num = (jnp.einsum('bhjs,bhsv->bhjv', s, vb)
               + w_inter[..., None] * jnp.einsum('bhjd,bhdv->bhjv', qb, C))
        den = s.sum(-1) + w_inter * jnp.einsum('bhjd,bhd->bhj', qb, n)
        h = num / jnp.maximum(jnp.abs(den), jnp.exp(-m_t))[..., None]
        b_last = b[..., -1]
        g = b_last[..., None] - b + ib
        m_new = jnp.maximum(b_last + m, g.max(-1))
        carry_decay = jnp.exp(b_last + m - m_new)
        wk = jnp.exp(g - m_new[..., None])[..., None] * kb
        C_new = carry_decay[..., None, None] * C + jnp.einsum('bhsd,bhsv->bhdv', wk, vb)
        n_new = carry_decay[..., None] * n + wk.sum(-2)
        return (C_new, n_new, m_new), h

    init = (jnp.zeros((B, H, Dk, Dv), f32), jnp.zeros((B, H, Dk), f32), jnp.zeros((B, H), f32))
    _, h = lax.scan(step, init, (qc, kc, vc, ic, lfc))
    return jnp.transpose(h, (1, 0, 3, 2, 4)).reshape(B, S, H, Dv)


def _hierarchical_moe(x, w_router_group, b_router_group, w_router_expert, b_router_expert,
                      w_exp_gate, w_exp_up, w_exp_down):
    B, S, D = x.shape
    f32 = jnp.float32
    t = x.reshape(B * S, D)
    g_logits = (t @ w_router_group + b_router_group).astype(f32)
    g_prob = jax.nn.softmax(g_logits, axis=-1)
    g_w, g_sel = lax.top_k(g_prob, 1)
    e_logits = (t @ w_router_expert + b_router_expert).astype(f32)
    e_logits = e_logits.reshape(-1, N_GROUPS, EXPERTS_PER_GROUP)
    e_logits = jnp.take_along_axis(e_logits, g_sel[:, :, None], axis=1)[:, 0]
    e_val, e_idx = lax.top_k(e_logits, TOP_K_IN_GROUP)
    e_w = jax.nn.softmax(e_val, axis=-1) * g_w
    expert_id = g_sel * EXPERTS_PER_GROUP + e_idx
    combine = jnp.einsum('tk,tke->te', e_w, jax.nn.one_hot(expert_id, N_EXPERTS, dtype=f32))
    out = jnp.zeros((B * S, D), f32)
    for e in range(N_EXPERTS):
        hdn = jax.nn.silu(t @ w_exp_gate[e]) * (t @ w_exp_up[e])
        out = out + combine[:, e:e + 1] * (hdn @ w_exp_down[e])
    return out.reshape(B, S, D)


def setup_inputs(seed: int = 0) -> dict:
    key = jax.random.key(seed)
    ks = jax.random.split(key, 20)
    f32 = jnp.float32

    def dense(k, shape, fan_in, scale=1.0):
        return jax.random.normal(k, shape, f32) * (scale * fan_in ** -0.5)

    x = jax.random.normal(ks[0], (BATCH, SEQ, D_MODEL), f32)
    w_in = dense(ks[1], (D_MODEL, IN_WIDTH), D_MODEL)
    f_off = sum(IN_SPLITS[:10])
    b_in = 0.02 * jax.random.normal(ks[2], (IN_WIDTH,), f32)
    b_in = b_in.at[f_off:f_off + MLSTM_HEADS].add(jnp.linspace(3.0, 6.0, MLSTM_HEADS, dtype=f32))
    conv_m = dense(ks[3], (CONV_WIDTH, 2 * MLSTM_QK_W), CONV_WIDTH)
    gn_m_gain = 1.0 + 0.02 * jax.random.normal(ks[4], (MLSTM_V_W,), f32)
    w_branch_attn = dense(ks[5], (ATT_W, D_MODEL), ATT_W)
    w_branch_mlstm = dense(ks[6], (MLSTM_V_W, D_MODEL), MLSTM_V_W)
    w_out = dense(ks[7], (D_MODEL, D_MODEL), D_MODEL, DEEPNORM_BETA)
    ln1_gain = 1.0 + 0.02 * jax.random.normal(ks[8], (D_MODEL,), f32)
    ln1_bias = 0.02 * jax.random.normal(ks[9], (D_MODEL,), f32)
    w_router_group = dense(ks[10], (D_MODEL, N_GROUPS), D_MODEL)
    b_router_group = 0.01 * jax.random.normal(ks[11], (N_GROUPS,), f32)
    w_router_expert = dense(ks[12], (D_MODEL, N_EXPERTS), D_MODEL)
    b_router_expert = 0.01 * jax.random.normal(ks[13], (N_EXPERTS,), f32)
    w_exp_gate = dense(ks[14], (N_EXPERTS, D_MODEL, EXPERT_DIM), D_MODEL)
    w_exp_up = dense(ks[15], (N_EXPERTS, D_MODEL, EXPERT_DIM), D_MODEL)
    w_exp_down = dense(ks[16], (N_EXPERTS, EXPERT_DIM, D_MODEL), EXPERT_DIM, DEEPNORM_BETA)
    ln2_gain = 1.0 + 0.02 * jax.random.normal(ks[17], (D_MODEL,), f32)
    ln2_bias = 0.02 * jax.random.normal(ks[18], (D_MODEL,), f32)
    return {'x': x, 'w_in': w_in, 'b_in': b_in, 'conv_m': conv_m, 'gn_m_gain': gn_m_gain,
            'w_branch_attn': w_branch_attn, 'w_branch_mlstm': w_branch_mlstm, 'w_out': w_out,
            'ln1_gain': ln1_gain, 'ln1_bias': ln1_bias,
            'w_router_group': w_router_group, 'b_router_group': b_router_group,
            'w_router_expert': w_router_expert, 'b_router_expert': b_router_expert,
            'w_exp_gate': w_exp_gate, 'w_exp_up': w_exp_up, 'w_exp_down': w_exp_down,
            'ln2_gain': ln2_gain, 'ln2_bias': ln2_bias}


def reference(x, w_in, b_in, conv_m, gn_m_gain, w_branch_attn, w_branch_mlstm, w_out,
              ln1_gain, ln1_bias, w_router_group, b_router_group, w_router_expert,
              b_router_expert, w_exp_gate, w_exp_up, w_exp_down, ln2_gain, ln2_bias):
    B, S, _ = x.shape
    splits = np.cumsum(IN_SPLITS)[:-1].tolist()
    for _layer in range(DEPTH):
        proj = jnp.einsum('bsd,de->bse', x, w_in) + b_in
        (a_q, a_k, a_v, i_q, i_k, i_w, m_q, m_k, m_v, m_i, m_f, m_o,
         g_attn, g_mlstm) = jnp.split(proj, splits, axis=-1)

        cos_a, sin_a = _rope_tables(S, ATT_HEAD_DIM)
        cos_i, sin_i = _rope_tables(S, IDX_DIM)
        q = _rope(a_q.reshape(B, S, ATT_HEADS, ATT_HEAD_DIM), cos_a, sin_a)
        k = _rope(a_k.reshape(B, S, ATT_HEADS, ATT_HEAD_DIM), cos_a, sin_a)
        v = a_v.reshape(B, S, ATT_HEADS, ATT_HEAD_DIM)
        qi = _rope(i_q.reshape(B, S, IDX_HEADS, IDX_DIM), cos_i, sin_i)
        ki = _rope(i_k.reshape(B, S, 1, IDX_DIM), cos_i, sin_i)[:, :, 0]
        wi = i_w * (IDX_HEADS ** -0.5 * IDX_DIM ** -0.5)
        y_attn = _dsa_sparse_attention(q, k, v, qi, ki, wi)

        qk_m = jax.nn.silu(_causal_depthwise_conv(jnp.concatenate([m_q, m_k], axis=-1), conv_m))
        mq, mk = jnp.split(qk_m, 2, axis=-1)
        h = _mlstm_chunkwise(mq.reshape(B, S, MLSTM_HEADS, MLSTM_QK_DIM),
                             mk.reshape(B, S, MLSTM_HEADS, MLSTM_QK_DIM),
                             m_v.reshape(B, S, MLSTM_HEADS, MLSTM_V_DIM), m_i, m_f)
        mu = h.mean(-1, keepdims=True)
        var = jnp.square(h - mu).mean(-1, keepdims=True)
        h = ((h - mu) * lax.rsqrt(var + GN_EPS)).reshape(B, S, MLSTM_V_W) * gn_m_gain
        y_mlstm = jax.nn.sigmoid(m_o) * h

        merged = (jax.nn.sigmoid(g_attn) * (y_attn @ w_branch_attn)
                  + jax.nn.sigmoid(g_mlstm) * (y_mlstm @ w_branch_mlstm))
        x = _layer_norm(DEEPNORM_ALPHA * x + merged @ w_out, ln1_gain, ln1_bias)

        moe = _hierarchical_moe(x, w_router_group, b_router_group, w_router_expert,
                                b_router_expert, w_exp_gate, w_exp_up, w_exp_down)
        x = _layer_norm(DEEPNORM_ALPHA * x + moe, ln2_gain, ln2_bias)
    return x
```

```python
import functools

import jax
import jax.numpy as jnp
from jax import lax
from jax.experimental import pallas as pl
from jax.experimental.pallas import tpu as pltpu

F32 = jnp.float32
BF16 = jnp.bfloat16
NEG_INF = float("-inf")

ATT_HEADS = 8
ATT_HEAD_DIM = 64
IDX_HEADS = 4
IDX_DIM = 64
IDX_TOPK_MAX = 256
Q_BLOCK = 128
ROPE_THETA = 10000.0

LANES = 128


def _key_to_f32(u):
    ks = u ^ jnp.int32(-2 ** 31)
    bits = ks ^ ((ks >> 31) & jnp.int32(0x7FFFFFFF))
    return lax.bitcast_convert_type(bits, F32)


def _dsa_kernel(q_ref, kt_ref, v_ref, qi_ref, kit_ref, wi_ref, o_ref,
                sc_ref, m_ref, l_ref, acc_ref, *, topk, kchunk):
    qb = pl.program_id(1)
    tq = q_ref.shape[1]
    n_slab = qb + 1
    n_chunk = (qb * tq + tq + kchunk - 1) // kchunk
    heads = acc_ref.shape[1] // ATT_HEAD_DIM

    row = lax.broadcasted_iota(jnp.int32, (tq, 1), 0) + qb * tq
    lane = lax.broadcasted_iota(jnp.int32, (tq, LANES), 1)

    qi = qi_ref[0]
    wi = wi_ref[0]

    def score_body(c, carry):
        off = pl.multiple_of(c * kchunk, kchunk)
        kit = kit_ref[0, :, pl.ds(off, kchunk)]
        tot = None
        for h in range(IDX_HEADS):
            s = jnp.dot(qi[:, h * IDX_DIM:(h + 1) * IDX_DIM], kit,
                        preferred_element_type=F32)
            s = jnp.maximum(s, 0.0) * wi[:, h:h + 1]
            tot = s if tot is None else tot + s
        kpos = lax.broadcasted_iota(jnp.int32, (tq, kchunk), 1) + off
        tot = jnp.where(kpos <= row, tot + 0.0, NEG_INF)
        sc_ref[:, pl.ds(off, kchunk)] = tot
        return carry

    lax.fori_loop(0, n_chunk, score_body, 0)

    def count_ge(thr):
        thr_b = jnp.broadcast_to(thr, (tq, LANES))

        def body(j, acc):
            x = sc_ref[:, pl.ds(pl.multiple_of(j * LANES, LANES), LANES)]
            return acc + jnp.where(x >= thr_b, 1.0, 0.0)

        acc = lax.fori_loop(0, n_slab, body, jnp.zeros((tq, LANES), F32))
        return jnp.sum(acc, axis=1, keepdims=True)

    def bit_body(i, carry):
        u, cge = carry
        cand = u | (jnp.int32(1) << (31 - i))
        cnt = count_ge(_key_to_f32(cand))
        ok = cnt >= float(topk)
        return jnp.where(ok, cand, u), jnp.where(ok, cnt, cge)

    u0 = jnp.zeros((tq, 1), jnp.int32)
    u, cge = lax.fori_loop(0, 32, bit_body, (u0, jnp.zeros((tq, 1), F32)))
    short = row < topk
    tau = jnp.where(short, NEG_INF, _key_to_f32(u))
    tau_b = jnp.broadcast_to(tau, (tq, LANES))

    def count_gt():
        def body(j, acc):
            x = sc_ref[:, pl.ds(pl.multiple_of(j * LANES, LANES), LANES)]
            return acc + jnp.where(x > tau_b, 1.0, 0.0)
        acc = lax.fori_loop(0, n_slab, body, jnp.zeros((tq, LANES), F32))
        return jnp.sum(acc, axis=1, keepdims=True)

    has_tie = jnp.logical_and(jnp.logical_not(short), cge > float(topk))
    any_tie = jnp.max(jnp.where(has_tie, 1.0, 0.0)) > 0.0

    def tie_cut():
        need = float(topk) - count_gt()

        def count_tie_le(jmax):
            def body(j, acc):
                x = sc_ref[:, pl.ds(pl.multiple_of(j * LANES, LANES), LANES)]
                hit = jnp.logical_and(x == tau_b, lane + j * LANES <= jmax)
                return acc + jnp.where(hit, 1.0, 0.0)
            acc = lax.fori_loop(0, n_slab, body, jnp.zeros((tq, LANES), F32))
            return jnp.sum(acc, axis=1, keepdims=True)

        nbits = max(1, (sc_ref.shape[1] - 1).bit_length())

        def jbit(i, jlo):
            cand = jlo | (jnp.int32(1) << (nbits - 1 - i))
            cnt = count_tie_le(cand - 1)
            return jnp.where(cnt < need, cand, jlo)

        return lax.fori_loop(0, nbits, jbit, jnp.zeros((tq, 1), jnp.int32))

    big = jnp.full((tq, 1), sc_ref.shape[1], jnp.int32)
    jcut = lax.cond(any_tie, tie_cut, lambda: big)
    jcut = jnp.where(has_tie, jcut, big)
    jcut_b = jnp.broadcast_to(jcut, (tq, LANES))

    def bias_body(j, carry):
        sl = pl.ds(pl.multiple_of(j * LANES, LANES), LANES)
        x = sc_ref[:, sl]
        idx = lane + j * LANES
        keep = jnp.logical_or(x > tau_b, jnp.logical_and(x == tau_b, idx <= jcut_b))
        keep = jnp.logical_and(keep, idx <= row)
        sc_ref[:, sl] = jnp.where(keep, 0.0, NEG_INF)
        return carry

    lax.fori_loop(0, n_slab, bias_body, 0)
    def tail_body(j, carry):
        sl = pl.ds(pl.multiple_of(j * LANES, LANES), LANES)
        sc_ref[:, sl] = jnp.full((tq, LANES), NEG_INF, F32)
        return carry

    lax.fori_loop(n_slab, n_chunk * (kchunk // LANES), tail_body, 0)

    m_ref[...] = jnp.full(m_ref.shape, NEG_INF, F32)
    l_ref[...] = jnp.zeros(l_ref.shape, F32)
    acc_ref[...] = jnp.zeros(acc_ref.shape, F32)
    q = q_ref[0]

    def attn_body(c, carry):
        off = pl.multiple_of(c * kchunk, kchunk)
        bias = sc_ref[:, pl.ds(off, kchunk)]
        for h in range(heads):
            hs = slice(h * ATT_HEAD_DIM, (h + 1) * ATT_HEAD_DIM)
            kt = kt_ref[0, hs, pl.ds(off, kchunk)]
            s = jnp.dot(q[:, hs], kt, preferred_element_type=F32) + bias
            m_old = m_ref[h]
            m_new = jnp.maximum(m_old, jnp.max(s, axis=1, keepdims=True))
            m_use = jnp.where(m_new == NEG_INF, 0.0, m_new)
            p = jnp.exp(s - m_use)
            alpha = jnp.exp(m_old - m_use)
            l_ref[h] = alpha * l_ref[h] + jnp.sum(p, axis=1, keepdims=True)
            pv = jnp.dot(p.astype(BF16), v_ref[0, pl.ds(off, kchunk), hs],
                         preferred_element_type=F32)
            acc_ref[:, hs] = alpha * acc_ref[:, hs] + pv
            m_ref[h] = m_new
        return carry

    lax.fori_loop(0, n_chunk, attn_body, 0)

    for h in range(heads):
        hs = slice(h * ATT_HEAD_DIM, (h + 1) * ATT_HEAD_DIM)
        o_ref[0, :, hs] = (acc_ref[:, hs] / l_ref[h]).astype(o_ref.dtype)


def dsa_attention(q, kt, v, qi, kit, wi, *, topk, kchunk=512):
    B, S, W = q.shape
    tq = Q_BLOCK
    kchunk = min(kchunk, S)
    heads = W // ATT_HEAD_DIM
    kern = functools.partial(_dsa_kernel, topk=topk, kchunk=kchunk)
    return pl.pallas_call(
        kern,
        grid=(B, S // tq),
        in_specs=[
            pl.BlockSpec((1, tq, W), lambda b, i: (b, i, 0)),
            pl.BlockSpec((1, W, S), lambda b, i: (b, 0, 0)),
            pl.BlockSpec((1, S, W), lambda b, i: (b, 0, 0)),
            pl.BlockSpec((1, tq, qi.shape[2]), lambda b, i: (b, i, 0)),
            pl.BlockSpec((1, kit.shape[1], S), lambda b, i: (b, 0, 0)),
            pl.BlockSpec((1, tq, wi.shape[2]), lambda b, i: (b, i, 0)),
        ],
        out_specs=pl.BlockSpec((1, tq, W), lambda b, i: (b, i, 0)),
        out_shape=jax.ShapeDtypeStruct((B, S, W), BF16),
        scratch_shapes=[
            pltpu.VMEM((tq, S), F32),
            pltpu.VMEM((heads, tq, 1), F32),
            pltpu.VMEM((heads, tq, 1), F32),
            pltpu.VMEM((tq, W), F32),
        ],
        compiler_params=pltpu.CompilerParams(
            dimension_semantics=("arbitrary", "arbitrary"),
            vmem_limit_bytes=56 * 1024 * 1024),
        name="dsa_attention",
    )(q, kt, v, qi, kit, wi)


def _rope_rows(x, cos_t, sin_t):
    lane = lax.broadcasted_iota(jnp.int32, x.shape, 1)
    swapped = jnp.where((lane % ATT_HEAD_DIM) < ATT_HEAD_DIM // 2,
                        pltpu.roll(x, LANES - ATT_HEAD_DIM // 2, 1),
                        pltpu.roll(x, ATT_HEAD_DIM // 2, 1))
    return x * cos_t + swapped * sin_t


def _attn_proj_kernel(x_ref, wr_ref, br_ref, wc_ref, bc_ref, cos_ref, sin_ref, cost_ref, sint_ref,
                      q_ref, v_ref, qi_ref, wi_ref, kt_ref, kit_ref, *, q_scale, wi_scale):
    xb = x_ref[0].astype(BF16)
    aw = q_ref.shape[2]
    iw = qi_ref.shape[2]
    cos_t = cos_ref[...]
    sin_t = sin_ref[...]
    half = ATT_HEAD_DIM // 2

    pq = jnp.dot(xb, wr_ref[:, 0:aw], preferred_element_type=F32) + br_ref[:, 0:aw]
    for j in range(aw // LANES):
        sl = slice(j * LANES, (j + 1) * LANES)
        q_ref[0, :, sl] = (_rope_rows(pq[:, sl], cos_t, sin_t) * q_scale).astype(q_ref.dtype)
    pv = jnp.dot(xb, wr_ref[:, aw:2 * aw], preferred_element_type=F32) + br_ref[:, aw:2 * aw]
    v_ref[0] = pv.astype(v_ref.dtype)
    pi = (jnp.dot(xb, wr_ref[:, 2 * aw:2 * aw + iw], preferred_element_type=F32)
          + br_ref[:, 2 * aw:2 * aw + iw])
    for j in range(iw // LANES):
        sl = slice(j * LANES, (j + 1) * LANES)
        qi_ref[0, :, sl] = _rope_rows(pi[:, sl], cos_t, sin_t).astype(qi_ref.dtype)
    pw = (jnp.dot(xb, wr_ref[:, 2 * aw + iw:], preferred_element_type=F32)
          + br_ref[:, 2 * aw + iw:])
    wi_ref[0] = pw * wi_scale

    pk = lax.dot_general(wc_ref[...], xb, (((1,), (1,)), ((), ())),
                         preferred_element_type=F32) + bc_ref[...]
    ct = cost_ref[...]
    st = sint_ref[...]
    for h in range(aw // ATT_HEAD_DIM + 1):
        x1 = pk[h * ATT_HEAD_DIM:h * ATT_HEAD_DIM + half]
        x2 = pk[h * ATT_HEAD_DIM + half:(h + 1) * ATT_HEAD_DIM]
        o1 = (x1 * ct - x2 * st).astype(BF16)
        o2 = (x1 * st + x2 * ct).astype(BF16)
        if h < aw // ATT_HEAD_DIM:
            kt_ref[0, h * ATT_HEAD_DIM:h * ATT_HEAD_DIM + half, :] = o1
            kt_ref[0, h * ATT_HEAD_DIM + half:(h + 1) * ATT_HEAD_DIM, :] = o2
        else:
            kit_ref[0, 0:half, :] = o1
            kit_ref[0, half:2 * half, :] = o2


def attn_projections(x, wr, br, wc, bc, cos_r, sin_r, cos_c, sin_c, *, tm=512):
    B, S, D = x.shape
    aw = ATT_HEADS * ATT_HEAD_DIM
    iw = IDX_HEADS * IDX_DIM
    tm = min(tm, S)
    kern = functools.partial(_attn_proj_kernel, q_scale=ATT_HEAD_DIM ** -0.5,
                             wi_scale=IDX_HEADS ** -0.5 * IDX_DIM ** -0.5)
    full = lambda a: pl.BlockSpec(a.shape, lambda b, i: (0,) * a.ndim)
    row = lambda w: pl.BlockSpec((1, tm, w), lambda b, i: (b, i, 0))
    return pl.pallas_call(
        kern,
        grid=(B, S // tm),
        in_specs=[row(D), full(wr), full(br), full(wc), full(bc),
                  pl.BlockSpec((tm, LANES), lambda b, i: (i, 0)),
                  pl.BlockSpec((tm, LANES), lambda b, i: (i, 0)),
                  pl.BlockSpec((ATT_HEAD_DIM // 2, tm), lambda b, i: (0, i)),
                  pl.BlockSpec((ATT_HEAD_DIM // 2, tm), lambda b, i: (0, i))],
        out_specs=[row(aw), row(aw), row(iw), row(LANES),
                   pl.BlockSpec((1, aw, tm), lambda b, i: (b, 0, i)),
                   pl.BlockSpec((1, IDX_DIM, tm), lambda b, i: (b, 0, i))],
        out_shape=[jax.ShapeDtypeStruct((B, S, aw), BF16),
                   jax.ShapeDtypeStruct((B, S, aw), BF16),
                   jax.ShapeDtypeStruct((B, S, iw), BF16),
                   jax.ShapeDtypeStruct((B, S, LANES), F32),
                   jax.ShapeDtypeStruct((B, aw, S), BF16),
                   jax.ShapeDtypeStruct((B, IDX_DIM, S), BF16)],
        compiler_params=pltpu.CompilerParams(
            dimension_semantics=("arbitrary", "arbitrary"),
            vmem_limit_bytes=48 * 1024 * 1024),
        name="attn_projections",
    )(x, wr, br, wc, bc, cos_r, sin_r, cos_c, sin_c)


MLSTM_HEADS = 8
MLSTM_QK_DIM = 64
MLSTM_V_DIM = 128
MLSTM_CHUNK = 128
CONV_WIDTH = 4
HALO = 8


def _silu(x):
    return x / (1.0 + jnp.exp(-x))


def _sigmoid(x):
    return 1.0 / (1.0 + jnp.exp(-x))


def _log_sigmoid(x):
    return jnp.minimum(x, 0.0) - jnp.log(1.0 + jnp.exp(-jnp.abs(x)))


def _mlstm_proj_kernel(x_ref, w_ref, b_ref, wt_ref, bt_ref, conv_ref,
                       mq_ref, mkt_ref, mv_ref, og_ref, ifc_ref, ift_ref, ext_ref, *, q_scale):
    i = pl.program_id(1)
    tm = x_ref.shape[1]
    qkw = 2 * mq_ref.shape[2]
    vw = mv_ref.shape[2]
    nh = ift_ref.shape[1] // 2
    xb = x_ref[0].astype(BF16)

    @pl.when(i == 0)
    def _():
        ext_ref[0:HALO, :] = jnp.zeros((HALO, qkw), F32)

    pqk = jnp.dot(xb, w_ref[:, 0:qkw], preferred_element_type=F32) + b_ref[:, 0:qkw]
    ext_ref[HALO:HALO + tm, :] = pqk
    acc = None
    for j in range(CONV_WIDTH):
        term = ext_ref[pl.ds(HALO - CONV_WIDTH + 1 + j, tm), :] * conv_ref[j:j + 1, :]
        acc = term if acc is None else acc + term
    ext_ref[0:HALO, :] = pqk[tm - HALO:tm, :]
    qk = _silu(acc)
    mq_ref[0] = (qk[:, 0:qkw // 2] * q_scale).astype(mq_ref.dtype)
    mkt_ref[0] = jnp.transpose(qk[:, qkw // 2:qkw]).astype(mkt_ref.dtype)

    pv = jnp.dot(xb, w_ref[:, qkw:qkw + vw], preferred_element_type=F32) + b_ref[:, qkw:qkw + vw]
    mv_ref[0] = pv.astype(mv_ref.dtype)
    po = (jnp.dot(xb, w_ref[:, qkw + vw:qkw + 2 * vw], preferred_element_type=F32)
          + b_ref[:, qkw + vw:qkw + 2 * vw])
    og_ref[0] = _sigmoid(po).astype(og_ref.dtype)

    pg = (jnp.dot(xb, w_ref[:, qkw + 2 * vw:], preferred_element_type=F32)
          + b_ref[:, qkw + 2 * vw:])
    lane = lax.broadcasted_iota(jnp.int32, pg.shape, 1)
    ifc_ref[0] = jnp.where(lane < nh, pg, _log_sigmoid(pg))
    pt = lax.dot_general(wt_ref[...], xb, (((1,), (1,)), ((), ())),
                         preferred_element_type=F32) + bt_ref[...]
    rowi = lax.broadcasted_iota(jnp.int32, pt.shape, 0)
    ift_ref[0] = jnp.where(rowi < nh, pt, _log_sigmoid(pt))


def mlstm_projections(x, w, b, wt, bt, conv, *, tm=512):
    B, S, D = x.shape
    qw = MLSTM_HEADS * MLSTM_QK_DIM
    vw = MLSTM_HEADS * MLSTM_V_DIM
    tm = min(tm, S)
    kern = functools.partial(_mlstm_proj_kernel, q_scale=MLSTM_QK_DIM ** -0.5)
    full = lambda a: pl.BlockSpec(a.shape, lambda b_, i: (0,) * a.ndim)
    row = lambda w_: pl.BlockSpec((1, tm, w_), lambda b_, i: (b_, i, 0))
    return pl.pallas_call(
        kern,
        grid=(B, S // tm),
        in_specs=[row(D), full(w), full(b), full(wt), full(bt), full(conv)],
        out_specs=[row(qw), pl.BlockSpec((1, qw, tm), lambda b_, i: (b_, 0, i)),
                   row(vw), row(vw), row(LANES),
                   pl.BlockSpec((1, 2 * MLSTM_HEADS, tm), lambda b_, i: (b_, 0, i))],
        out_shape=[jax.ShapeDtypeStruct((B, S, qw), BF16),
                   jax.ShapeDtypeStruct((B, qw, S), BF16),
                   jax.ShapeDtypeStruct((B, S, vw), BF16),
                   jax.ShapeDtypeStruct((B, S, vw), BF16),
                   jax.ShapeDtypeStruct((B, S, LANES), F32),
                   jax.ShapeDtypeStruct((B, 2 * MLSTM_HEADS, S), F32)],
        scratch_shapes=[pltpu.VMEM((HALO + tm, 2 * qw), F32)],
        compiler_params=pltpu.CompilerParams(
            dimension_semantics=("arbitrary", "arbitrary"),
            vmem_limit_bytes=56 * 1024 * 1024),
        name="mlstm_projections",
    )(x, w, b, wt, bt, conv)


GN_EPS = 1e-6


def _mlstm_kernel(mq_ref, mkt_ref, mv_ref, og_ref, ifc_ref, ift_ref, gain_ref, y_ref,
                  c_ref, m_ref):
    c = pl.program_id(1)
    L = mq_ref.shape[1]
    nh = ift_ref.shape[1] // 2
    dk = mq_ref.shape[2] // nh
    dv = mv_ref.shape[2] // nh

    @pl.when(c == 0)
    def _():
        c_ref[...] = jnp.zeros(c_ref.shape, F32)
        m_ref[...] = jnp.zeros(m_ref.shape, F32)

    r_i = lax.broadcasted_iota(jnp.int32, (L, L), 0)
    c_i = lax.broadcasted_iota(jnp.int32, (L, L), 1)
    causal = c_i <= r_i
    tril = jnp.where(causal, 1.0, 0.0)
    triu = jnp.where(r_i <= c_i, 1.0, 0.0)
    ifc = ifc_ref[0]
    ift = ift_ref[0]
    b_cols = jnp.dot(tril, ifc, preferred_element_type=F32, precision=lax.Precision.HIGHEST)
    b_rows = jnp.dot(ift, triu, preferred_element_type=F32, precision=lax.Precision.HIGHEST)
    ones_col = jnp.where(lax.broadcasted_iota(jnp.int32, (L, dv), 1) == 0, 1.0, 0.0).astype(BF16)

    for h in range(nh):
        m = m_ref[h]
        b_col = b_cols[:, nh + h:nh + h + 1]
        a_col = ifc[:, h:h + 1] - b_col
        a_row = ift[h:h + 1, :] - b_rows[nh + h:nh + h + 1, :]
        amat = jnp.where(causal, a_row, NEG_INF)
        big_m = jnp.maximum(m, jnp.max(amat, axis=1, keepdims=True))
        decay = jnp.exp(amat - big_m)
        w_inter = jnp.exp(m - big_m)
        q = mq_ref[0, :, h * dk:(h + 1) * dk]
        kt = mkt_ref[0, h * dk:(h + 1) * dk, :]
        v_ext = jnp.concatenate([mv_ref[0, :, h * dv:(h + 1) * dv], ones_col], axis=1)
        s = jnp.dot(q, kt, preferred_element_type=F32) * decay
        sv = jnp.dot(s.astype(BF16), v_ext, preferred_element_type=F32)
        c_old = c_ref[h]
        qc = jnp.dot(q, c_old.astype(BF16), preferred_element_type=F32)
        num = sv[:, 0:dv] + w_inter * qc[:, 0:dv]
        den = sv[:, dv:dv + 1] + w_inter * qc[:, dv:dv + 1]
        hh = num / jnp.maximum(jnp.abs(den), jnp.exp(-(b_col + big_m)))
        mu = jnp.mean(hh, axis=1, keepdims=True)
        xc = hh - mu
        var = jnp.mean(xc * xc, axis=1, keepdims=True)
        hn = xc * lax.rsqrt(var + GN_EPS) * gain_ref[:, h * dv:(h + 1) * dv]
        y_ref[0, :, h * dv:(h + 1) * dv] = (
            og_ref[0, :, h * dv:(h + 1) * dv].astype(F32) * hn).astype(y_ref.dtype)

        b_last = b_col[L - 1:L, :]
        g_row = b_last + a_row
        m_new = jnp.maximum(b_last + m, jnp.max(g_row, axis=1, keepdims=True))
        carry = jnp.exp(b_last + m - m_new)
        wkt = (kt.astype(F32) * jnp.exp(g_row - m_new)).astype(BF16)
        c_ref[h] = carry * c_old + jnp.dot(wkt, v_ext, preferred_element_type=F32)
        m_ref[h] = m_new


def mlstm_scan(mq, mkt, mv, og, ifc, ift, gain):
    B, S, qw = mq.shape
    vw = mv.shape[2]
    L = min(MLSTM_CHUNK, S)
    nh = ift.shape[1] // 2
    row = lambda w_: pl.BlockSpec((1, L, w_), lambda b_, c: (b_, c, 0))
    return pl.pallas_call(
        _mlstm_kernel,
        grid=(B, S // L),
        in_specs=[row(qw), pl.BlockSpec((1, qw, L), lambda b_, c: (b_, 0, c)),
                  row(vw), row(vw), row(LANES),
                  pl.BlockSpec((1, 2 * nh, L), lambda b_, c: (b_, 0, c)),
                  pl.BlockSpec(gain.shape, lambda b_, c: (0, 0))],
        out_specs=row(vw),
        out_shape=jax.ShapeDtypeStruct((B, S, vw), BF16),
        scratch_shapes=[pltpu.VMEM((nh, qw // nh, 2 * vw // nh), F32),
                        pltpu.VMEM((nh, 1, 1), F32)],
        compiler_params=pltpu.CompilerParams(
            dimension_semantics=("arbitrary", "arbitrary")),
        name="mlstm_scan",
    )(mq, mkt, mv, og, ifc, ift, gain)


LN_EPS = 1e-5


def _layer_norm(z, gain, bias):
    mu = jnp.mean(z, axis=1, keepdims=True)
    zc = z - mu
    var = jnp.mean(zc * zc, axis=1, keepdims=True)
    return zc * lax.rsqrt(var + LN_EPS) * gain + bias


def _merge_kernel(x_ref, ya_ref, ym_ref, wg_ref, bg_ref, wa_ref, wm_ref, wo_ref, g_ref, b_ref,
                  o_ref, *, alpha):
    x = x_ref[...]
    xb = x.astype(BF16)
    d = x.shape[1]
    ga = _sigmoid(jnp.dot(xb, wg_ref[:, 0:d], preferred_element_type=F32) + bg_ref[:, 0:d])
    merged = ga * jnp.dot(ya_ref[...], wa_ref[...], preferred_element_type=F32)
    gm = _sigmoid(jnp.dot(xb, wg_ref[:, d:2 * d], preferred_element_type=F32) + bg_ref[:, d:2 * d])
    merged = merged + gm * jnp.dot(ym_ref[...], wm_ref[...], preferred_element_type=F32)
    z = alpha * x + jnp.dot(merged.astype(BF16), wo_ref[...], preferred_element_type=F32)
    o_ref[...] = _layer_norm(z, g_ref[...], b_ref[...])


def merge_branches(x2, ya, ym, wg, bg, wa, wm, wo, g, b, *, alpha, tm=512):
    T, D = x2.shape
    tm = min(tm, T)
    full = lambda a: pl.BlockSpec(a.shape, lambda i: (0,) * a.ndim)
    row = lambda w_: pl.BlockSpec((tm, w_), lambda i: (i, 0))
    return pl.pallas_call(
        functools.partial(_merge_kernel, alpha=alpha),
        grid=(T // tm,),
        in_specs=[row(D), row(ya.shape[1]), row(ym.shape[1]), full(wg), full(bg), full(wa),
                  full(wm), full(wo), full(g), full(b)],
        out_specs=row(D),
        out_shape=jax.ShapeDtypeStruct((T, D), F32),
        compiler_params=pltpu.CompilerParams(
            dimension_semantics=("arbitrary",), vmem_limit_bytes=48 * 1024 * 1024),
        name="merge_branches",
    )(x2, ya, ym, wg, bg, wa, wm, wo, g, b)


N_GROUPS = 4
EXPERTS_PER_GROUP = 4
N_EXPERTS = N_GROUPS * EXPERTS_PER_GROUP


def _first_lane_of_max(vals, vmax, lane):
    return jnp.min(jnp.where(vals == vmax, lane, LANES), axis=1, keepdims=True)


def _moe_kernel(x_ref, wr_ref, br_ref, wg_ref, wu_ref, wd_ref, g_ref, b_ref, o_ref,
                comb_ref, acc_ref, *, alpha):
    e = pl.program_id(1)
    x = x_ref[...]
    tm = x.shape[0]

    @pl.when(e == 0)
    def _():
        logits = jnp.dot(x, wr_ref[...], preferred_element_type=F32,
                         precision=lax.Precision.HIGHEST) + br_ref[...]
        lane = lax.broadcasted_iota(jnp.int32, logits.shape, 1)
        g = jnp.where(lane < N_GROUPS, logits, NEG_INF)
        gmax = jnp.max(g, axis=1, keepdims=True)
        g_w = 1.0 / jnp.sum(jnp.exp(g - gmax), axis=1, keepdims=True)
        g_sel = _first_lane_of_max(g, gmax, lane)
        lo = N_GROUPS + EXPERTS_PER_GROUP * g_sel
        ev = jnp.where(jnp.logical_and(lane >= lo, lane < lo + EXPERTS_PER_GROUP), logits, NEG_INF)
        v1 = jnp.max(ev, axis=1, keepdims=True)
        i1 = _first_lane_of_max(ev, v1, lane)
        ev2 = jnp.where(lane == i1, NEG_INF, ev)
        v2 = jnp.max(ev2, axis=1, keepdims=True)
        i2 = _first_lane_of_max(ev2, v2, lane)
        r = jnp.exp(v2 - v1)
        p1 = 1.0 / (1.0 + r)
        p2 = r / (1.0 + r)
        comb_ref[...] = (jnp.where(lane == i1, g_w * p1, 0.0)
                         + jnp.where(lane == i2, g_w * p2, 0.0))
        acc_ref[...] = jnp.zeros(acc_ref.shape, F32)

    xb = x.astype(BF16)
    hg = jnp.dot(xb, wg_ref[0], preferred_element_type=F32)
    hu = jnp.dot(xb, wu_ref[0], preferred_element_type=F32)
    hdn = (_silu(hg) * hu).astype(BF16)
    lane = lax.broadcasted_iota(jnp.int32, (tm, LANES), 1)
    cw = jnp.sum(jnp.where(lane == N_GROUPS + e, comb_ref[...], 0.0), axis=1, keepdims=True)
    acc_ref[...] += cw * jnp.dot(hdn, wd_ref[0], preferred_element_type=F32)

    @pl.when(e == pl.num_programs(1) - 1)
    def _():
        o_ref[...] = _layer_norm(alpha * x + acc_ref[...], g_ref[...], b_ref[...])


def moe_layer(x2, wr, br, wg, wu, wd, g, b, *, alpha, tm=1024):
    T, D = x2.shape
    E, _, F = wg.shape
    tm = min(tm, T)
    full = lambda a: pl.BlockSpec(a.shape, lambda i, e: (0,) * a.ndim)
    return pl.pallas_call(
        functools.partial(_moe_kernel, alpha=alpha),
        grid=(T // tm, E),
        in_specs=[pl.BlockSpec((tm, D), lambda i, e: (i, 0)), full(wr), full(br),
                  pl.BlockSpec((1, D, F), lambda i, e: (e, 0, 0)),
                  pl.BlockSpec((1, D, F), lambda i, e: (e, 0, 0)),
                  pl.BlockSpec((1, F, D), lambda i, e: (e, 0, 0)),
                  full(g), full(b)],
        out_specs=pl.BlockSpec((tm, D), lambda i, e: (i, 0)),
        out_shape=jax.ShapeDtypeStruct((T, D), F32),
        scratch_shapes=[pltpu.VMEM((tm, LANES), F32), pltpu.VMEM((tm, D), F32)],
        compiler_params=pltpu.CompilerParams(
            dimension_semantics=("arbitrary", "arbitrary"), vmem_limit_bytes=48 * 1024 * 1024),
        name="moe_layer",
    )(x2, wr, br, wg, wu, wd, g, b)


DEPTH = 1
DEEPNORM_ALPHA = (2.0 * DEPTH) ** 0.25


def _pad_cols(a, width):
    return jnp.pad(a, ((0, 0), (0, width - a.shape[1])))


def kernel(x, w_in, b_in, conv_m, gn_m_gain, w_branch_attn, w_branch_mlstm, w_out, ln1_gain, ln1_bias, w_router_group, b_router_group, w_router_expert, b_router_expert, w_exp_gate, w_exp_up, w_exp_down, ln2_gain, ln2_bias):
    B, S, D = x.shape
    aw = ATT_HEADS * ATT_HEAD_DIM
    iw = IDX_HEADS * IDX_DIM
    qw = MLSTM_HEADS * MLSTM_QK_DIM
    vw = MLSTM_HEADS * MLSTM_V_DIM
    widths = (aw, aw, aw, iw, IDX_DIM, IDX_HEADS, qw, qw, vw, MLSTM_HEADS, MLSTM_HEADS, vw, D, D)
    offs = [0]
    for w_ in widths:
        offs.append(offs[-1] + w_)
    col = lambda k: w_in[:, offs[k]:offs[k + 1]]
    bia = lambda k: b_in[offs[k]:offs[k + 1]]
    (A_Q, A_K, A_V, I_Q, I_K, I_W, M_Q, M_K, M_V, M_I, M_F, M_O, G_A, G_M) = range(14)

    wr = jnp.concatenate([col(A_Q), col(A_V), col(I_Q), _pad_cols(col(I_W), LANES)], 1).astype(BF16)
    br = jnp.concatenate([bia(A_Q), bia(A_V), bia(I_Q),
                          jnp.pad(bia(I_W), (0, LANES - IDX_HEADS))])[None, :]
    wc = jnp.concatenate([col(A_K), col(I_K)], 1).T.astype(BF16)
    bc = jnp.concatenate([bia(A_K), bia(I_K)])[:, None]
    gpad = LANES - 2 * MLSTM_HEADS
    wm = jnp.concatenate([col(M_Q), col(M_K), col(M_V), col(M_O), col(M_I),
                          _pad_cols(col(M_F), MLSTM_HEADS + gpad)], 1).astype(BF16)
    bm = jnp.concatenate([bia(M_Q), bia(M_K), bia(M_V), bia(M_O), bia(M_I),
                          jnp.pad(bia(M_F), (0, gpad))])[None, :]
    wmt = jnp.concatenate([col(M_I), col(M_F)], 1).T.astype(BF16)
    bmt = jnp.concatenate([bia(M_I), bia(M_F)])[:, None]
    wgate = jnp.concatenate([col(G_A), col(G_M)], 1).astype(BF16)
    bgate = jnp.concatenate([bia(G_A), bia(G_M)])[None, :]

    half = ATT_HEAD_DIM // 2
    inv = ROPE_THETA ** (-jnp.arange(0, ATT_HEAD_DIM, 2, dtype=F32) / ATT_HEAD_DIM)
    ang = jnp.arange(S, dtype=F32)[:, None] * inv[None, :]
    cos, sin = jnp.cos(ang), jnp.sin(ang)
    cos_r = jnp.tile(cos, (1, LANES // half))
    sin_r = jnp.tile(jnp.concatenate([-sin, sin], 1), (1, LANES // ATT_HEAD_DIM))

    q, v, qi, wi, kt, kit = attn_projections(x, wr, br, wc, bc, cos_r, sin_r, cos.T, sin.T)
    y_attn = dsa_attention(q, kt, v, qi, kit, wi, topk=min(IDX_TOPK_MAX, S // 4))

    mq, mkt, mv, og, ifc, ift = mlstm_projections(x, wm, bm, wmt, bmt, conv_m)
    y_mlstm = mlstm_scan(mq, mkt, mv, og, ifc, ift, gn_m_gain[None, :])

    x1 = merge_branches(x.reshape(B * S, D), y_attn.reshape(B * S, aw), y_mlstm.reshape(B * S, vw),
                        wgate, bgate, w_branch_attn.astype(BF16), w_branch_mlstm.astype(BF16),
                        w_out.astype(BF16), ln1_gain[None, :], ln1_bias[None, :],
                        alpha=DEEPNORM_ALPHA)

    w_router = _pad_cols(jnp.concatenate([w_router_group, w_router_expert], 1), LANES)
    b_router = jnp.pad(jnp.concatenate([b_router_group, b_router_expert]),
                       (0, LANES - N_GROUPS - N_EXPERTS))[None, :]
    out = moe_layer(x1, w_router, b_router, w_exp_gate.astype(BF16), w_exp_up.astype(BF16),
                    w_exp_down.astype(BF16), ln2_gain[None, :], ln2_bias[None, :],
                    alpha=DEEPNORM_ALPHA)
    return out.reshape(B, S, D)
```

```python
import functools

import jax
import jax.numpy as jnp
from jax import lax
from jax.experimental import pallas as pl
from jax.experimental.pallas import tpu as pltpu

F32 = jnp.float32
BF16 = jnp.bfloat16
NEG_INF = float("-inf")

ATT_HEADS = 8
ATT_HEAD_DIM = 64
IDX_HEADS = 4
IDX_DIM = 64
IDX_TOPK_MAX = 256
Q_BLOCK = 128
ROPE_THETA = 10000.0

LANES = 128
BF16_ROWS = 16


def _key_to_f32(u):
    ks = u ^ jnp.int32(-2 ** 31)
    bits = ks ^ ((ks >> 31) & jnp.int32(0x7FFFFFFF))
    return lax.bitcast_convert_type(bits, F32)


def _dsa_kernel(qt_ref, k_ref, vt_ref, qit_ref, ki_ref, wit_ref, o_ref,
                sc_ref, qm_ref, m_ref, l_ref, acc_ref, *, topk, kchunk):
    qb = pl.program_id(1)
    tq = o_ref.shape[1]
    heads = qt_ref.shape[1] // ATT_HEAD_DIM
    step = 2 * LANES
    n_chunk = (qb * tq + tq + kchunk - 1) // kchunk
    n_step = (qb * tq + tq + step - 1) // step
    qpos = lax.broadcasted_iota(jnp.int32, (1, tq), 1) + qb * tq

    qit = qit_ref[0]
    zpad = jnp.zeros((LANES - IDX_DIM, tq), BF16)
    qi_pair = []
    for p in range(IDX_HEADS // 2):
        cols = [jnp.concatenate([qit[h * IDX_DIM:(h + 1) * IDX_DIM], zpad], axis=0)
                for h in (2 * p, 2 * p + 1)]
        qi_pair.append(jnp.concatenate(cols, axis=1))
    wit = wit_ref[0]

    def score_body(c, carry):
        off = pl.multiple_of(c * kchunk, kchunk)
        ki = ki_ref[0, pl.ds(off, kchunk), :]
        tot = None
        for p in range(IDX_HEADS // 2):
            s2 = jnp.dot(ki, qi_pair[p], preferred_element_type=F32)
            for j in range(2):
                h = 2 * p + j
                s = jnp.maximum(s2[:, j * tq:(j + 1) * tq], 0.0) * wit[h:h + 1, :]
                tot = s if tot is None else tot + s
        kpos = lax.broadcasted_iota(jnp.int32, (kchunk, tq), 0) + off
        sc_ref[pl.ds(off, kchunk), :] = jnp.where(kpos <= qpos, tot + 0.0, NEG_INF)
        return carry

    lax.fori_loop(0, n_chunk, score_body, 0)

    def count(pred):
        def body(j, acc):
            off = pl.multiple_of(j * step, step)
            hit = jnp.where(pred(sc_ref[pl.ds(off, step), :], off), 1.0, 0.0)
            return acc + hit[0:LANES] + hit[LANES:step]
        acc = lax.fori_loop(0, n_step, body, jnp.zeros((LANES, tq), F32))
        return jnp.sum(acc, axis=0, keepdims=True)

    def bit_body(i, carry):
        u, cge = carry
        cand = u | (jnp.int32(1) << (31 - i))
        thr = _key_to_f32(cand)
        cnt = count(lambda x, off: x >= thr)
        ok = cnt >= float(topk)
        return jnp.where(ok, cand, u), jnp.where(ok, cnt, cge)

    u, cge = lax.fori_loop(0, 32, bit_body,
                           (jnp.zeros((1, tq), jnp.int32), jnp.zeros((1, tq), F32)))
    short = qpos < topk
    tau = jnp.where(short, NEG_INF, _key_to_f32(u))

    has_tie = jnp.logical_and(jnp.logical_not(short), cge > float(topk))
    any_tie = jnp.max(jnp.where(has_tie, 1.0, 0.0)) > 0.0
    n_keys = sc_ref.shape[0]

    def tie_cut():
        need = float(topk) - count(lambda x, off: x > tau)
        nbits = max(1, (n_keys - 1).bit_length())

        def jbit(i, jlo):
            cand = jlo | (jnp.int32(1) << (nbits - 1 - i))
            def pred(x, off):
                kpos = lax.broadcasted_iota(jnp.int32, x.shape, 0) + off
                return jnp.logical_and(x == tau, kpos < cand)
            return jnp.where(count(pred) < need, cand, jlo)

        return lax.fori_loop(0, nbits, jbit, jnp.zeros((1, tq), jnp.int32))

    everything = jnp.full((1, tq), n_keys, jnp.int32)
    jcut = lax.cond(any_tie, tie_cut, lambda: everything)
    jcut = jnp.where(has_tie, jcut, everything)

    def bias_body(j, carry):
        off = pl.multiple_of(j * step, step)
        x = sc_ref[pl.ds(off, step), :]
        kpos = lax.broadcasted_iota(jnp.int32, x.shape, 0) + off
        keep = jnp.logical_or(x > tau, jnp.logical_and(x == tau, kpos <= jcut))
        keep = jnp.logical_and(keep, kpos <= qpos)
        sc_ref[pl.ds(off, step), :] = jnp.where(keep, 0.0, NEG_INF)
        return carry

    lax.fori_loop(0, n_step, bias_body, 0)

    qt = qt_ref[0]
    zrow = jnp.zeros((ATT_HEAD_DIM, tq), BF16)
    for p in range(heads // 2):
        a = qt[(2 * p) * ATT_HEAD_DIM:(2 * p + 1) * ATT_HEAD_DIM]
        b = qt[(2 * p + 1) * ATT_HEAD_DIM:(2 * p + 2) * ATT_HEAD_DIM]
        qm_ref[p] = jnp.concatenate([jnp.concatenate([a, zrow], axis=0),
                                     jnp.concatenate([zrow, b], axis=0)], axis=1)
    m_ref[...] = jnp.full(m_ref.shape, NEG_INF, F32)
    l_ref[...] = jnp.zeros(l_ref.shape, F32)
    acc_ref[...] = jnp.zeros(acc_ref.shape, F32)
    ones_rows = jnp.ones((BF16_ROWS, step), BF16)
    stages = [(sub, p) for sub in range(kchunk // step) for p in range(heads // 2)]

    def attn_body(c, carry):
        off = pl.multiple_of(c * kchunk, kchunk)

        def qk(stage):
            sub, p = stage
            kp = k_ref[0, pl.ds(off + sub * step, step), p * LANES:(p + 1) * LANES]
            return jnp.dot(kp, qm_ref[p], preferred_element_type=F32)

        s2 = qk(stages[0])
        for i, (sub, p) in enumerate(stages):
            s2_next = qk(stages[i + 1]) if i + 1 < len(stages) else None
            koff = off + sub * step
            bias = sc_ref[pl.ds(koff, step), :]
            for j in range(2):
                h = 2 * p + j
                s = s2[:, j * tq:(j + 1) * tq] + bias
                m_old = m_ref[h]
                m_new = jnp.maximum(m_old, jnp.max(s, axis=0, keepdims=True))
                m_use = jnp.where(m_new == NEG_INF, 0.0, m_new)
                pexp = jnp.exp(s - m_use).astype(BF16)
                alpha = jnp.exp(m_old - m_use)
                vt = jnp.concatenate(
                    [vt_ref[0, h * ATT_HEAD_DIM:(h + 1) * ATT_HEAD_DIM, pl.ds(koff, step)],
                     ones_rows], axis=0)
                pv = jnp.dot(vt, pexp, preferred_element_type=F32)
                acc_ref[h] = alpha * acc_ref[h] + pv[0:ATT_HEAD_DIM]
                l_ref[h] = alpha * l_ref[h] + pv[ATT_HEAD_DIM:ATT_HEAD_DIM + 1]
                m_ref[h] = m_new
            s2 = s2_next
        return carry

    lax.fori_loop(0, n_chunk, attn_body, 0)

    out_t = jnp.concatenate([acc_ref[h] / l_ref[h] for h in range(heads)], axis=0)
    o_ref[0] = jnp.transpose(out_t).astype(o_ref.dtype)


def dsa_attention(qt, k, vt, qit, ki, wit, *, topk, kchunk=512):
    B, W, S = qt.shape
    tq = Q_BLOCK
    kchunk = min(kchunk, S)
    heads = W // ATT_HEAD_DIM
    kern = functools.partial(_dsa_kernel, topk=topk, kchunk=kchunk)
    qcol = lambda a: pl.BlockSpec((1, a.shape[1], tq), lambda b, i: (b, 0, i))
    whole = lambda a: pl.BlockSpec((1,) + a.shape[1:], lambda b, i: (b, 0, 0))
    return pl.pallas_call(
        kern,
        grid=(B, S // tq),
        in_specs=[qcol(qt), whole(k), whole(vt), qcol(qit), whole(ki), qcol(wit)],
        out_specs=pl.BlockSpec((1, tq, W), lambda b, i: (b, i, 0)),
        out_shape=jax.ShapeDtypeStruct((B, S, W), BF16),
        scratch_shapes=[
            pltpu.VMEM((S, tq), F32),
            pltpu.VMEM((heads // 2, LANES, 2 * tq), BF16),
            pltpu.VMEM((heads, 1, tq), F32),
            pltpu.VMEM((heads, 1, tq), F32),
            pltpu.VMEM((heads, ATT_HEAD_DIM, tq), F32),
        ],
        compiler_params=pltpu.CompilerParams(
            dimension_semantics=("arbitrary", "arbitrary"),
            vmem_limit_bytes=56 * 1024 * 1024),
        name="dsa_attention",
    )(qt, k, vt, qit, ki, wit)


def _rope_rows(x, cos_t, sin_t):
    lane = lax.broadcasted_iota(jnp.int32, x.shape, 1)
    swapped = jnp.where((lane % ATT_HEAD_DIM) < ATT_HEAD_DIM // 2,
                        pltpu.roll(x, LANES - ATT_HEAD_DIM // 2, 1),
                        pltpu.roll(x, ATT_HEAD_DIM // 2, 1))
    return x * cos_t + swapped * sin_t


def _attn_proj_kernel(x_ref, wr_ref, br_ref, wc_ref, bc_ref, cos_ref, sin_ref, cost_ref, sint_ref,
                      qt_ref, k_ref, vt_ref, qit_ref, ki_ref, wit_ref, *, q_scale, wi_scale):
    xb = x_ref[0].astype(BF16)
    aw = k_ref.shape[2]
    iw = qit_ref.shape[1]
    half = ATT_HEAD_DIM // 2
    cos_t = cos_ref[...]
    sin_t = sin_ref[...]

    pk = jnp.dot(xb, wr_ref[...], preferred_element_type=F32) + br_ref[...]
    for j in range(aw // LANES):
        sl = slice(j * LANES, (j + 1) * LANES)
        k_ref[0, :, sl] = _rope_rows(pk[:, sl], cos_t, sin_t).astype(k_ref.dtype)
    ki_ref[0] = _rope_rows(pk[:, aw:aw + LANES], cos_t, sin_t).astype(ki_ref.dtype)

    pt = lax.dot_general(wc_ref[...], xb, (((1,), (1,)), ((), ())),
                         preferred_element_type=F32) + bc_ref[...]
    ct = cost_ref[...]
    st = sint_ref[...]

    def rope_cols(src0, dst_ref, nheads, scale):
        for h in range(nheads):
            r0 = src0 + h * ATT_HEAD_DIM
            x1 = pt[r0:r0 + half]
            x2 = pt[r0 + half:r0 + 2 * half]
            d0 = h * ATT_HEAD_DIM
            dst_ref[0, d0:d0 + half, :] = ((x1 * ct - x2 * st) * scale).astype(dst_ref.dtype)
            dst_ref[0, d0 + half:d0 + 2 * half, :] = ((x1 * st + x2 * ct) * scale).astype(dst_ref.dtype)

    rope_cols(0, qt_ref, aw // ATT_HEAD_DIM, q_scale)
    vt_ref[0] = pt[aw:2 * aw].astype(vt_ref.dtype)
    rope_cols(2 * aw, qit_ref, iw // IDX_DIM, 1.0)
    wit_ref[0] = pt[2 * aw + iw:] * wi_scale


def attn_projections(x, wr, br, wc, bc, cos_r, sin_r, cos_c, sin_c, *, tm=512):
    B, S, D = x.shape
    aw = ATT_HEADS * ATT_HEAD_DIM
    iw = IDX_HEADS * IDX_DIM
    tm = min(tm, S)
    kern = functools.partial(_attn_proj_kernel, q_scale=ATT_HEAD_DIM ** -0.5,
                             wi_scale=IDX_HEADS ** -0.5 * IDX_DIM ** -0.5)
    full = lambda a: pl.BlockSpec(a.shape, lambda b, i: (0,) * a.ndim)
    row = lambda w: pl.BlockSpec((1, tm, w), lambda b, i: (b, i, 0))
    colm = lambda r: pl.BlockSpec((1, r, tm), lambda b, i: (b, 0, i))
    return pl.pallas_call(
        kern,
        grid=(B, S // tm),
        in_specs=[row(D), full(wr), full(br), full(wc), full(bc),
                  pl.BlockSpec((tm, LANES), lambda b, i: (i, 0)),
                  pl.BlockSpec((tm, LANES), lambda b, i: (i, 0)),
                  pl.BlockSpec((ATT_HEAD_DIM // 2, tm), lambda b, i: (0, i)),
                  pl.BlockSpec((ATT_HEAD_DIM // 2, tm), lambda b, i: (0, i))],
        out_specs=[colm(aw), row(aw), colm(aw), colm(iw), row(LANES), colm(8)],
        out_shape=[jax.ShapeDtypeStruct((B, aw, S), BF16),
                   jax.ShapeDtypeStruct((B, S, aw), BF16),
                   jax.ShapeDtypeStruct((B, aw, S), BF16),
                   jax.ShapeDtypeStruct((B, iw, S), BF16),
                   jax.ShapeDtypeStruct((B, S, LANES), BF16),
                   jax.ShapeDtypeStruct((B, 8, S), F32)],
        compiler_params=pltpu.CompilerParams(
            dimension_semantics=("arbitrary", "arbitrary"),
            vmem_limit_bytes=48 * 1024 * 1024),
        name="attn_projections",
    )(x, wr, br, wc, bc, cos_r, sin_r, cos_c, sin_c)


MLSTM_HEADS = 8
MLSTM_QK_DIM = 64
MLSTM_V_DIM = 128
MLSTM_CHUNK = 128
CONV_WIDTH = 4
HALO = 8


def _silu(x):
    return x / (1.0 + jnp.exp(-x))


def _sigmoid(x):
    return 1.0 / (1.0 + jnp.exp(-x))


def _log_sigmoid(x):
    return jnp.minimum(x, 0.0) - jnp.log(1.0 + jnp.exp(-jnp.abs(x)))


def _mlstm_proj_kernel(x_ref, w_ref, b_ref, wt_ref, bt_ref, conv_ref,
                       mq_ref, mkt_ref, mv_ref, og_ref, ifc_ref, ift_ref, ext_ref, *, q_scale):
    i = pl.program_id(1)
    tm = x_ref.shape[1]
    qkw = 2 * mq_ref.shape[2]
    vw = mv_ref.shape[2]
    nh = ift_ref.shape[1] // 2
    xb = x_ref[0].astype(BF16)

    @pl.when(i == 0)
    def _():
        ext_ref[0:HALO, :] = jnp.zeros((HALO, qkw), F32)

    pqk = jnp.dot(xb, w_ref[:, 0:qkw], preferred_element_type=F32) + b_ref[:, 0:qkw]
    ext_ref[HALO:HALO + tm, :] = pqk
    acc = None
    for j in range(CONV_WIDTH):
        term = ext_ref[pl.ds(HALO - CONV_WIDTH + 1 + j, tm), :] * conv_ref[j:j + 1, :]
        acc = term if acc is None else acc + term
    ext_ref[0:HALO, :] = pqk[tm - HALO:tm, :]
    qk = _silu(acc)
    mq_ref[0] = (qk[:, 0:qkw // 2] * q_scale).astype(mq_ref.dtype)
    mkt_ref[0] = jnp.transpose(qk[:, qkw // 2:qkw]).astype(mkt_ref.dtype)

    pv = jnp.dot(xb, w_ref[:, qkw:qkw + vw], preferred_element_type=F32) + b_ref[:, qkw:qkw + vw]
    mv_ref[0] = pv.astype(mv_ref.dtype)
    po = (jnp.dot(xb, w_ref[:, qkw + vw:qkw + 2 * vw], preferred_element_type=F32)
          + b_ref[:, qkw + vw:qkw + 2 * vw])
    og_ref[0] = _sigmoid(po).astype(og_ref.dtype)

    pg = (jnp.dot(xb, w_ref[:, qkw + 2 * vw:], preferred_element_type=F32)
          + b_ref[:, qkw + 2 * vw:])
    lane = lax.broadcasted_iota(jnp.int32, pg.shape, 1)
    ifc_ref[0] = jnp.where(lane < nh, pg, _log_sigmoid(pg))
    pt = lax.dot_general(wt_ref[...], xb, (((1,), (1,)), ((), ())),
                         preferred_element_type=F32) + bt_ref[...]
    rowi = lax.broadcasted_iota(jnp.int32, pt.shape, 0)
    ift_ref[0] = jnp.where(rowi < nh, pt, _log_sigmoid(pt))


def mlstm_projections(x, w, b, wt, bt, conv, *, tm=512):
    B, S, D = x.shape
    qw = MLSTM_HEADS * MLSTM_QK_DIM
    vw = MLSTM_HEADS * MLSTM_V_DIM
    tm = min(tm, S)
    kern = functools.partial(_mlstm_proj_kernel, q_scale=MLSTM_QK_DIM ** -0.5)
    full = lambda a: pl.BlockSpec(a.shape, lambda b_, i: (0,) * a.ndim)
    row = lambda w_: pl.BlockSpec((1, tm, w_), lambda b_, i: (b_, i, 0))
    return pl.pallas_call(
        kern,
        grid=(B, S // tm),
        in_specs=[row(D), full(w), full(b), full(wt), full(bt), full(conv)],
        out_specs=[row(qw), pl.BlockSpec((1, qw, tm), lambda b_, i: (b_, 0, i)),
                   row(vw), row(vw), row(LANES),
                   pl.BlockSpec((1, 2 * MLSTM_HEADS, tm), lambda b_, i: (b_, 0, i))],
        out_shape=[jax.ShapeDtypeStruct((B, S, qw), BF16),
                   jax.ShapeDtypeStruct((B, qw, S), BF16),
                   jax.ShapeDtypeStruct((B, S, vw), BF16),
                   jax.ShapeDtypeStruct((B, S, vw), BF16),
                   jax.ShapeDtypeStruct((B, S, LANES), F32),
                   jax.ShapeDtypeStruct((B, 2 * MLSTM_HEADS, S), F32)],
        scratch_shapes=[pltpu.VMEM((HALO + tm, 2 * qw), F32)],
        compiler_params=pltpu.CompilerParams(
            dimension_semantics=("arbitrary", "arbitrary"),
            vmem_limit_bytes=56 * 1024 * 1024),
        name="mlstm_projections",
    )(x, w, b, wt, bt, conv)


GN_EPS = 1e-6


def _mlstm_kernel(mq_ref, mkt_ref, mv_ref, og_ref, ifc_ref, ift_ref, gain_ref, y_ref,
                  c_ref, m_ref):
    c = pl.program_id(1)
    L = mq_ref.shape[1]
    nh = ift_ref.shape[1] // 2
    dk = mq_ref.shape[2] // nh
    dv = mv_ref.shape[2] // nh

    @pl.when(c == 0)
    def _():
        c_ref[...] = jnp.zeros(c_ref.shape, F32)
        m_ref[...] = jnp.zeros(m_ref.shape, F32)

    r_i = lax.broadcasted_iota(jnp.int32, (L, L), 0)
    c_i = lax.broadcasted_iota(jnp.int32, (L, L), 1)
    causal = c_i <= r_i
    tril = jnp.where(causal, 1.0, 0.0)
    triu = jnp.where(r_i <= c_i, 1.0, 0.0)
    ifc = ifc_ref[0]
    ift = ift_ref[0]
    b_cols = jnp.dot(tril, ifc, preferred_element_type=F32, precision=lax.Precision.HIGHEST)
    b_rows = jnp.dot(ift, triu, preferred_element_type=F32, precision=lax.Precision.HIGHEST)
    ones_col = jnp.where(lax.broadcasted_iota(jnp.int32, (L, dv), 1) == 0, 1.0, 0.0).astype(BF16)

    for h in range(nh):
        m = m_ref[h]
        b_col = b_cols[:, nh + h:nh + h + 1]
        a_row = ift[h:h + 1, :] - b_rows[nh + h:nh + h + 1, :]
        amat = jnp.where(causal, a_row, NEG_INF)
        big_m = jnp.maximum(m, jnp.max(amat, axis=1, keepdims=True))
        decay = jnp.exp(amat - big_m)
        w_inter = jnp.exp(m - big_m)
        q = mq_ref[0, :, h * dk:(h + 1) * dk]
        kt = mkt_ref[0, h * dk:(h + 1) * dk, :]
        v_ext = jnp.concatenate([mv_ref[0, :, h * dv:(h + 1) * dv], ones_col], axis=1)
        s = jnp.dot(q, kt, preferred_element_type=F32) * decay
        sv = jnp.dot(s.astype(BF16), v_ext, preferred_element_type=F32)
        c_old = c_ref[h]
        qc = jnp.dot(q, c_old.astype(BF16), preferred_element_type=F32)
        num = sv[:, 0:dv] + w_inter * qc[:, 0:dv]
        den = sv[:, dv:dv + 1] + w_inter * qc[:, dv:dv + 1]
        hh = num / jnp.maximum(jnp.abs(den), jnp.exp(-(b_col + big_m)))
        mu = jnp.mean(hh, axis=1, keepdims=True)
        xc = hh - mu
        var = jnp.mean(xc * xc, axis=1, keepdims=True)
        hn = xc * lax.rsqrt(var + GN_EPS) * gain_ref[:, h * dv:(h + 1) * dv]
        y_ref[0, :, h * dv:(h + 1) * dv] = (
            og_ref[0, :, h * dv:(h + 1) * dv].astype(F32) * hn).astype(y_ref.dtype)

        b_last = b_col[L - 1:L, :]
        g_row = b_last + a_row
        m_new = jnp.maximum(b_last + m, jnp.max(g_row, axis=1, keepdims=True))
        carry = jnp.exp(b_last + m - m_new)
        wkt = (kt.astype(F32) * jnp.exp(g_row - m_new)).astype(BF16)
        c_ref[h] = carry * c_old + jnp.dot(wkt, v_ext, preferred_element_type=F32)
        m_ref[h] = m_new


def mlstm_scan(mq, mkt, mv, og, ifc, ift, gain):
    B, S, qw = mq.shape
    vw = mv.shape[2]
    L = min(MLSTM_CHUNK, S)
    nh = ift.shape[1] // 2
    row = lambda w_: pl.BlockSpec((1, L, w_), lambda b_, c: (b_, c, 0))
    return pl.pallas_call(
        _mlstm_kernel,
        grid=(B, S // L),
        in_specs=[row(qw), pl.BlockSpec((1, qw, L), lambda b_, c: (b_, 0, c)),
                  row(vw), row(vw), row(LANES),
                  pl.BlockSpec((1, 2 * nh, L), lambda b_, c: (b_, 0, c)),
                  pl.BlockSpec(gain.shape, lambda b_, c: (0, 0))],
        out_specs=row(vw),
        out_shape=jax.ShapeDtypeStruct((B, S, vw), BF16),
        scratch_shapes=[pltpu.VMEM((nh, qw // nh, 2 * vw // nh), F32),
                        pltpu.VMEM((nh, 1, 1), F32)],
        compiler_params=pltpu.CompilerParams(
            dimension_semantics=("arbitrary", "arbitrary")),
        name="mlstm_scan",
    )(mq, mkt, mv, og, ifc, ift, gain)


LN_EPS = 1e-5


def _layer_norm(z, gain, bias):
    mu = jnp.mean(z, axis=1, keepdims=True)
    zc = z - mu
    var = jnp.mean(zc * zc, axis=1, keepdims=True)
    return zc * lax.rsqrt(var + LN_EPS) * gain + bias


def _merge_kernel(x_ref, ya_ref, ym_ref, wg_ref, bg_ref, wa_ref, wm_ref, wo_ref, g_ref, b_ref,
                  o_ref, *, alpha):
    x = x_ref[...]
    xb = x.astype(BF16)
    d = x.shape[1]
    ga = _sigmoid(jnp.dot(xb, wg_ref[:, 0:d], preferred_element_type=F32) + bg_ref[:, 0:d])
    merged = ga * jnp.dot(ya_ref[...], wa_ref[...], preferred_element_type=F32)
    gm = _sigmoid(jnp.dot(xb, wg_ref[:, d:2 * d], preferred_element_type=F32) + bg_ref[:, d:2 * d])
    merged = merged + gm * jnp.dot(ym_ref[...], wm_ref[...], preferred_element_type=F32)
    z = alpha * x + jnp.dot(merged.astype(BF16), wo_ref[...], preferred_element_type=F32)
    o_ref[...] = _layer_norm(z, g_ref[...], b_ref[...])


def merge_branches(x2, ya, ym, wg, bg, wa, wm, wo, g, b, *, alpha, tm=512):
    T, D = x2.shape
    tm = min(tm, T)
    full = lambda a: pl.BlockSpec(a.shape, lambda i: (0,) * a.ndim)
    row = lambda w_: pl.BlockSpec((tm, w_), lambda i: (i, 0))
    return pl.pallas_call(
        functools.partial(_merge_kernel, alpha=alpha),
        grid=(T // tm,),
        in_specs=[row(D), row(ya.shape[1]), row(ym.shape[1]), full(wg), full(bg), full(wa),
                  full(wm), full(wo), full(g), full(b)],
        out_specs=row(D),
        out_shape=jax.ShapeDtypeStruct((T, D), F32),
        compiler_params=pltpu.CompilerParams(
            dimension_semantics=("arbitrary",), vmem_limit_bytes=48 * 1024 * 1024),
        name="merge_branches",
    )(x2, ya, ym, wg, bg, wa, wm, wo, g, b)


N_GROUPS = 4
EXPERTS_PER_GROUP = 4
N_EXPERTS = N_GROUPS * EXPERTS_PER_GROUP


def _first_lane_of_max(vals, vmax, lane):
    return jnp.min(jnp.where(vals == vmax, lane, LANES), axis=1, keepdims=True)


def _moe_kernel(x_ref, wr_ref, br_ref, wg_ref, wu_ref, wd_ref, g_ref, b_ref, o_ref,
                comb_ref, acc_ref, *, alpha):
    e = pl.program_id(1)
    x = x_ref[...]
    tm = x.shape[0]

    @pl.when(e == 0)
    def _():
        logits = jnp.dot(x, wr_ref[...], preferred_element_type=F32,
                         precision=lax.Precision.HIGHEST) + br_ref[...]
        lane = lax.broadcasted_iota(jnp.int32, logits.shape, 1)
        g = jnp.where(lane < N_GROUPS, logits, NEG_INF)
        gmax = jnp.max(g, axis=1, keepdims=True)
        g_w = 1.0 / jnp.sum(jnp.exp(g - gmax), axis=1, keepdims=True)
        g_sel = _first_lane_of_max(g, gmax, lane)
        lo = N_GROUPS + EXPERTS_PER_GROUP * g_sel
        ev = jnp.where(jnp.logical_and(lane >= lo, lane < lo + EXPERTS_PER_GROUP), logits, NEG_INF)
        v1 = jnp.max(ev, axis=1, keepdims=True)
        i1 = _first_lane_of_max(ev, v1, lane)
        ev2 = jnp.where(lane == i1, NEG_INF, ev)
        v2 = jnp.max(ev2, axis=1, keepdims=True)
        i2 = _first_lane_of_max(ev2, v2, lane)
        r = jnp.exp(v2 - v1)
        p1 = 1.0 / (1.0 + r)
        p2 = r / (1.0 + r)
        comb_ref[...] = (jnp.where(lane == i1, g_w * p1, 0.0)
                         + jnp.where(lane == i2, g_w * p2, 0.0))
        acc_ref[...] = jnp.zeros(acc_ref.shape, F32)

    xb = x.astype(BF16)
    hg = jnp.dot(xb, wg_ref[0], preferred_element_type=F32)
    hu = jnp.dot(xb, wu_ref[0], preferred_element_type=F32)
    hdn = (_silu(hg) * hu).astype(BF16)
    lane = lax.broadcasted_iota(jnp.int32, (tm, LANES), 1)
    cw = jnp.sum(jnp.where(lane == N_GROUPS + e, comb_ref[...], 0.0), axis=1, keepdims=True)
    acc_ref[...] += cw * jnp.dot(hdn, wd_ref[0], preferred_element_type=F32)

    @pl.when(e == pl.num_programs(1) - 1)
    def _():
        o_ref[...] = _layer_norm(alpha * x + acc_ref[...], g_ref[...], b_ref[...])


def moe_layer(x2, wr, br, wg, wu, wd, g, b, *, alpha, tm=1024):
    T, D = x2.shape
    E, _, F = wg.shape
    tm = min(tm, T)
    full = lambda a: pl.BlockSpec(a.shape, lambda i, e: (0,) * a.ndim)
    return pl.pallas_call(
        functools.partial(_moe_kernel, alpha=alpha),
        grid=(T // tm, E),
        in_specs=[pl.BlockSpec((tm, D), lambda i, e: (i, 0)), full(wr), full(br),
                  pl.BlockSpec((1, D, F), lambda i, e: (e, 0, 0)),
                  pl.BlockSpec((1, D, F), lambda i, e: (e, 0, 0)),
                  pl.BlockSpec((1, F, D), lambda i, e: (e, 0, 0)),
                  full(g), full(b)],
        out_specs=pl.BlockSpec((tm, D), lambda i, e: (i, 0)),
        out_shape=jax.ShapeDtypeStruct((T, D), F32),
        scratch_shapes=[pltpu.VMEM((tm, LANES), F32), pltpu.VMEM((tm, D), F32)],
        compiler_params=pltpu.CompilerParams(
            dimension_semantics=("arbitrary", "arbitrary"), vmem_limit_bytes=48 * 1024 * 1024),
        name="moe_layer",
    )(x2, wr, br, wg, wu, wd, g, b)


DEPTH = 1
DEEPNORM_ALPHA = (2.0 * DEPTH) ** 0.25


def _pad_cols(a, width):
    return jnp.pad(a, ((0, 0), (0, width - a.shape[1])))


def kernel(x, w_in, b_in, conv_m, gn_m_gain, w_branch_attn, w_branch_mlstm, w_out, ln1_gain, ln1_bias, w_router_group, b_router_group, w_router_expert, b_router_expert, w_exp_gate, w_exp_up, w_exp_down, ln2_gain, ln2_bias):
    B, S, D = x.shape
    aw = ATT_HEADS * ATT_HEAD_DIM
    iw = IDX_HEADS * IDX_DIM
    qw = MLSTM_HEADS * MLSTM_QK_DIM
    vw = MLSTM_HEADS * MLSTM_V_DIM
    widths = (aw, aw, aw, iw, IDX_DIM, IDX_HEADS, qw, qw, vw, MLSTM_HEADS, MLSTM_HEADS, vw, D, D)
    offs = [0]
    for w_ in widths:
        offs.append(offs[-1] + w_)
    col = lambda k: w_in[:, offs[k]:offs[k + 1]]
    bia = lambda k: b_in[offs[k]:offs[k + 1]]
    (A_Q, A_K, A_V, I_Q, I_K, I_W, M_Q, M_K, M_V, M_I, M_F, M_O, G_A, G_M) = range(14)

    wr = jnp.concatenate([col(A_K), _pad_cols(col(I_K), LANES)], 1).astype(BF16)
    br = jnp.concatenate([bia(A_K), jnp.pad(bia(I_K), (0, LANES - IDX_DIM))])[None, :]
    wc = jnp.concatenate([col(A_Q), col(A_V), col(I_Q), _pad_cols(col(I_W), 8)], 1).T.astype(BF16)
    bc = jnp.concatenate([bia(A_Q), bia(A_V), bia(I_Q), jnp.pad(bia(I_W), (0, 8 - IDX_HEADS))])[:, None]
    gpad = LANES - 2 * MLSTM_HEADS
    wm = jnp.concatenate([col(M_Q), col(M_K), col(M_V), col(M_O), col(M_I),
                          _pad_cols(col(M_F), MLSTM_HEADS + gpad)], 1).astype(BF16)
    bm = jnp.concatenate([bia(M_Q), bia(M_K), bia(M_V), bia(M_O), bia(M_I),
                          jnp.pad(bia(M_F), (0, gpad))])[None, :]
    wmt = jnp.concatenate([col(M_I), col(M_F)], 1).T.astype(BF16)
    bmt = jnp.concatenate([bia(M_I), bia(M_F)])[:, None]
    wgate = jnp.concatenate([col(G_A), col(G_M)], 1).astype(BF16)
    bgate = jnp.concatenate([bia(G_A), bia(G_M)])[None, :]

    half = ATT_HEAD_DIM // 2
    inv = ROPE_THETA ** (-jnp.arange(0, ATT_HEAD_DIM, 2, dtype=F32) / ATT_HEAD_DIM)
    ang = jnp.arange(S, dtype=F32)[:, None] * inv[None, :]
    cos, sin = jnp.cos(ang), jnp.sin(ang)
    cos_r = jnp.tile(cos, (1, LANES // half))
    sin_r = jnp.tile(jnp.concatenate([-sin, sin], 1), (1, LANES // ATT_HEAD_DIM))

    qt, k, vt, qit, ki, wit = attn_projections(x, wr, br, wc, bc, cos_r, sin_r, cos.T, sin.T)
    y_attn = dsa_attention(qt, k, vt, qit, ki, wit, topk=min(IDX_TOPK_MAX, S // 4))

    mq, mkt, mv, og, ifc, ift = mlstm_projections(x, wm, bm, wmt, bmt, conv_m)
    y_mlstm = mlstm_scan(mq, mkt, mv, og, ifc, ift, gn_m_gain[None, :])

    x1 = merge_branches(x.reshape(B * S, D), y_attn.reshape(B * S, aw), y_mlstm.reshape(B * S, vw),
                        wgate, bgate, w_branch_attn.astype(BF16), w_branch_mlstm.astype(BF16),
                        w_out.astype(BF16), ln1_gain[None, :], ln1_bias[None, :],
                        alpha=DEEPNORM_ALPHA)

    w_router = _pad_cols(jnp.concatenate([w_router_group, w_router_expert], 1), LANES)
    b_router = jnp.pad(jnp.concatenate([b_router_group, b_router_expert]),
                       (0, LANES - N_GROUPS - N_EXPERTS))[None, :]
    out = moe_layer(x1, w_router, b_router, w_exp_gate.astype(BF16), w_exp_up.astype(BF16),
                    w_exp_down.astype(BF16), ln2_gain[None, :], ln2_bias[None, :],
                    alpha=DEEPNORM_ALPHA)
    return out.reshape(B, S, D)
```

```python
import functools

import jax
import jax.numpy as jnp
from jax import lax
from jax.experimental import pallas as pl
from jax.experimental.pallas import tpu as pltpu

F32 = jnp.float32
BF16 = jnp.bfloat16
NEG_INF = float("-inf")

ATT_HEADS = 8
ATT_HEAD_DIM = 64
IDX_HEADS = 4
IDX_DIM = 64
IDX_TOPK_MAX = 256
Q_BLOCK = 128
ROPE_THETA = 10000.0

LANES = 128
BF16_ROWS = 16


def _key_to_f32(u):
    ks = u ^ jnp.int32(-2 ** 31)
    bits = ks ^ ((ks >> 31) & jnp.int32(0x7FFFFFFF))
    return lax.bitcast_convert_type(bits, F32)


def _dsa_kernel(qt_ref, k_ref, vt_ref, qit_ref, ki_ref, wit_ref, o_ref,
                sc_ref, qm_ref, m_ref, l_ref, acc_ref, *, topk, kchunk):
    qb = pl.program_id(1)
    tq = o_ref.shape[1]
    heads = qt_ref.shape[1] // ATT_HEAD_DIM
    step = 2 * LANES
    n_chunk = (qb * tq + tq + kchunk - 1) // kchunk
    qpos = lax.broadcasted_iota(jnp.int32, (1, tq), 1) + qb * tq

    qit = qit_ref[0]
    zpad = jnp.zeros((LANES - IDX_DIM, tq), BF16)
    qi_pair = []
    for p in range(IDX_HEADS // 2):
        cols = [jnp.concatenate([qit[h * IDX_DIM:(h + 1) * IDX_DIM], zpad], axis=0)
                for h in (2 * p, 2 * p + 1)]
        qi_pair.append(jnp.concatenate(cols, axis=1))
    wit = wit_ref[0]

    def score_body(c, carry):
        off = pl.multiple_of(c * kchunk, kchunk)
        ki = ki_ref[0, pl.ds(off, kchunk), :]
        tot = None
        for p in range(IDX_HEADS // 2):
            s2 = jnp.dot(ki, qi_pair[p], preferred_element_type=F32)
            for j in range(2):
                h = 2 * p + j
                s = jnp.maximum(s2[:, j * tq:(j + 1) * tq], 0.0) * wit[h:h + 1, :]
                tot = s if tot is None else tot + s
        kpos = lax.broadcasted_iota(jnp.int32, (kchunk, tq), 0) + off
        sc_ref[pl.ds(off, kchunk), :] = jnp.where(kpos <= qpos, tot + 0.0, NEG_INF)
        return carry

    lax.fori_loop(0, n_chunk, score_body, 0)

    def count(pred):
        def body(j, acc):
            off = pl.multiple_of(j * kchunk, kchunk)
            for t in range(kchunk // LANES):
                x = sc_ref[pl.ds(off + t * LANES, LANES), :]
                acc = acc + jnp.where(pred(x, off + t * LANES), 1.0, 0.0)
            return acc
        acc = lax.fori_loop(0, n_chunk, body, jnp.zeros((LANES, tq), F32))
        return jnp.sum(acc, axis=0, keepdims=True)

    def bit_body(i, carry):
        u, cgt = carry
        cand = u | (jnp.int32(1) << (31 - i))
        thr = _key_to_f32(cand)
        cnt = count(lambda x, off: x >= thr)
        ok = cnt >= float(topk)
        return jnp.where(ok, cand, u), jnp.where(ok, cgt, cnt)

    u, cgt = lax.fori_loop(0, 32, bit_body,
                           (jnp.zeros((1, tq), jnp.int32), jnp.zeros((1, tq), F32)))
    short = qpos < topk
    tau = jnp.where(short, NEG_INF, _key_to_f32(u))
    need = float(topk) - cgt

    r_i = lax.broadcasted_iota(jnp.int32, (LANES, LANES), 0)
    c_i = lax.broadcasted_iota(jnp.int32, (LANES, LANES), 1)
    tril = jnp.where(c_i <= r_i, 1.0, 0.0).astype(BF16)

    def bias_body(c, seen):
        base = pl.multiple_of(c * kchunk, kchunk)
        xs = [sc_ref[pl.ds(base + t * LANES, LANES), :] for t in range(kchunk // LANES)]
        ties = [x == tau for x in xs]
        ranks = [jnp.dot(tril, jnp.where(tie, 1.0, 0.0).astype(BF16),
                         preferred_element_type=F32) for tie in ties]
        for t, (x, tie, rank) in enumerate(zip(xs, ties, ranks)):
            keep = jnp.logical_or(x > tau, jnp.logical_and(tie, rank + seen <= need))
            kpos = lax.broadcasted_iota(jnp.int32, x.shape, 0) + (base + t * LANES)
            keep = jnp.logical_and(keep, kpos <= qpos)
            sc_ref[pl.ds(base + t * LANES, LANES), :] = jnp.where(keep, 0.0, NEG_INF)
            seen = seen + rank[LANES - 1:LANES, :]
        return seen

    lax.fori_loop(0, n_chunk, bias_body, jnp.zeros((1, tq), F32))

    qt = qt_ref[0]
    zrow = jnp.zeros((ATT_HEAD_DIM, tq), BF16)
    for p in range(heads // 2):
        a = qt[(2 * p) * ATT_HEAD_DIM:(2 * p + 1) * ATT_HEAD_DIM]
        b = qt[(2 * p + 1) * ATT_HEAD_DIM:(2 * p + 2) * ATT_HEAD_DIM]
        qm_ref[p] = jnp.concatenate([jnp.concatenate([a, zrow], axis=0),
                                     jnp.concatenate([zrow, b], axis=0)], axis=1)
    m_ref[...] = jnp.full(m_ref.shape, NEG_INF, F32)
    l_ref[...] = jnp.zeros(l_ref.shape, F32)
    acc_ref[...] = jnp.zeros(acc_ref.shape, F32)
    ones_rows = jnp.ones((BF16_ROWS, step), BF16)
    stages = [(sub, p) for sub in range(kchunk // step) for p in range(heads // 2)]

    def attn_body(c, carry):
        off = pl.multiple_of(c * kchunk, kchunk)

        def qk(stage):
            sub, p = stage
            kp = k_ref[0, pl.ds(off + sub * step, step), p * LANES:(p + 1) * LANES]
            return jnp.dot(kp, qm_ref[p], preferred_element_type=F32)

        s2 = qk(stages[0])
        for i, (sub, p) in enumerate(stages):
            s2_next = qk(stages[i + 1]) if i + 1 < len(stages) else None
            koff = off + sub * step
            bias = sc_ref[pl.ds(koff, step), :]
            for j in range(2):
                h = 2 * p + j
                s = s2[:, j * tq:(j + 1) * tq] + bias
                m_old = m_ref[h]
                m_new = jnp.maximum(m_old, jnp.max(s, axis=0, keepdims=True))
                m_use = jnp.where(m_new == NEG_INF, 0.0, m_new)
                pexp = jnp.exp(s - m_use).astype(BF16)
                alpha = jnp.exp(m_old - m_use)
                vt = jnp.concatenate(
                    [vt_ref[0, h * ATT_HEAD_DIM:(h + 1) * ATT_HEAD_DIM, pl.ds(koff, step)],
                     ones_rows], axis=0)
                pv = jnp.dot(vt, pexp, preferred_element_type=F32)
                acc_ref[h] = alpha * acc_ref[h] + pv[0:ATT_HEAD_DIM]
                l_ref[h] = alpha * l_ref[h] + pv[ATT_HEAD_DIM:ATT_HEAD_DIM + 1]
                m_ref[h] = m_new
            s2 = s2_next
        return carry

    lax.fori_loop(0, n_chunk, attn_body, 0)

    out_t = jnp.concatenate([acc_ref[h] / l_ref[h] for h in range(heads)], axis=0)
    o_ref[0] = jnp.transpose(out_t).astype(o_ref.dtype)


def dsa_attention(qt, k, vt, qit, ki, wit, *, topk, kchunk=512):
    B, W, S = qt.shape
    tq = Q_BLOCK
    kchunk = min(kchunk, S)
    heads = W // ATT_HEAD_DIM
    kern = functools.partial(_dsa_kernel, topk=topk, kchunk=kchunk)
    qcol = lambda a: pl.BlockSpec((1, a.shape[1], tq), lambda b, i: (b, 0, i))
    whole = lambda a: pl.BlockSpec((1,) + a.shape[1:], lambda b, i: (b, 0, 0))
    return pl.pallas_call(
        kern,
        grid=(B, S // tq),
        in_specs=[qcol(qt), whole(k), whole(vt), qcol(qit), whole(ki), qcol(wit)],
        out_specs=pl.BlockSpec((1, tq, W), lambda b, i: (b, i, 0)),
        out_shape=jax.ShapeDtypeStruct((B, S, W), BF16),
        scratch_shapes=[
            pltpu.VMEM((S, tq), F32),
            pltpu.VMEM((heads // 2, LANES, 2 * tq), BF16),
            pltpu.VMEM((heads, 1, tq), F32),
            pltpu.VMEM((heads, 1, tq), F32),
            pltpu.VMEM((heads, ATT_HEAD_DIM, tq), F32),
        ],
        compiler_params=pltpu.CompilerParams(
            dimension_semantics=("arbitrary", "arbitrary"),
            vmem_limit_bytes=56 * 1024 * 1024),
        name="dsa_attention",
    )(qt, k, vt, qit, ki, wit)


def _rope_rows(x, cos_t, sin_t):
    lane = lax.broadcasted_iota(jnp.int32, x.shape, 1)
    swapped = jnp.where((lane % ATT_HEAD_DIM) < ATT_HEAD_DIM // 2,
                        pltpu.roll(x, LANES - ATT_HEAD_DIM // 2, 1),
                        pltpu.roll(x, ATT_HEAD_DIM // 2, 1))
    return x * cos_t + swapped * sin_t


def _attn_proj_kernel(x_ref, wr_ref, br_ref, wc_ref, bc_ref, cos_ref, sin_ref, cost_ref, sint_ref,
                      qt_ref, k_ref, vt_ref, qit_ref, ki_ref, wit_ref, *, q_scale, wi_scale):
    xb = x_ref[0].astype(BF16)
    aw = k_ref.shape[2]
    iw = qit_ref.shape[1]
    half = ATT_HEAD_DIM // 2
    cos_t = cos_ref[...]
    sin_t = sin_ref[...]

    pk = jnp.dot(xb, wr_ref[...], preferred_element_type=F32) + br_ref[...]
    for j in range(aw // LANES):
        sl = slice(j * LANES, (j + 1) * LANES)
        k_ref[0, :, sl] = _rope_rows(pk[:, sl], cos_t, sin_t).astype(k_ref.dtype)
    ki_ref[0] = _rope_rows(pk[:, aw:aw + LANES], cos_t, sin_t).astype(ki_ref.dtype)

    pt = lax.dot_general(wc_ref[...], xb, (((1,), (1,)), ((), ())),
                         preferred_element_type=F32) + bc_ref[...]
    ct = cost_ref[...]
    st = sint_ref[...]

    def rope_cols(src0, dst_ref, nheads, scale):
        for h in range(nheads):
            r0 = src0 + h * ATT_HEAD_DIM
            x1 = pt[r0:r0 + half]
            x2 = pt[r0 + half:r0 + 2 * half]
            d0 = h * ATT_HEAD_DIM
            dst_ref[0, d0:d0 + half, :] = ((x1 * ct - x2 * st) * scale).astype(dst_ref.dtype)
            dst_ref[0, d0 + half:d0 + 2 * half, :] = ((x1 * st + x2 * ct) * scale).astype(dst_ref.dtype)

    rope_cols(0, qt_ref, aw // ATT_HEAD_DIM, q_scale)
    vt_ref[0] = pt[aw:2 * aw].astype(vt_ref.dtype)
    rope_cols(2 * aw, qit_ref, iw // IDX_DIM, 1.0)
    wit_ref[0] = pt[2 * aw + iw:] * wi_scale


def attn_projections(x, wr, br, wc, bc, cos_r, sin_r, cos_c, sin_c, *, tm=512):
    B, S, D = x.shape
    aw = ATT_HEADS * ATT_HEAD_DIM
    iw = IDX_HEADS * IDX_DIM
    tm = min(tm, S)
    kern = functools.partial(_attn_proj_kernel, q_scale=ATT_HEAD_DIM ** -0.5,
                             wi_scale=IDX_HEADS ** -0.5 * IDX_DIM ** -0.5)
    full = lambda a: pl.BlockSpec(a.shape, lambda b, i: (0,) * a.ndim)
    row = lambda w: pl.BlockSpec((1, tm, w), lambda b, i: (b, i, 0))
    colm = lambda r: pl.BlockSpec((1, r, tm), lambda b, i: (b, 0, i))
    return pl.pallas_call(
        kern,
        grid=(B, S // tm),
        in_specs=[row(D), full(wr), full(br), full(wc), full(bc),
                  pl.BlockSpec((tm, LANES), lambda b, i: (i, 0)),
                  pl.BlockSpec((tm, LANES), lambda b, i: (i, 0)),
                  pl.BlockSpec((ATT_HEAD_DIM // 2, tm), lambda b, i: (0, i)),
                  pl.BlockSpec((ATT_HEAD_DIM // 2, tm), lambda b, i: (0, i))],
        out_specs=[colm(aw), row(aw), colm(aw), colm(iw), row(LANES), colm(8)],
        out_shape=[jax.ShapeDtypeStruct((B, aw, S), BF16),
                   jax.ShapeDtypeStruct((B, S, aw), BF16),
                   jax.ShapeDtypeStruct((B, aw, S), BF16),
                   jax.ShapeDtypeStruct((B, iw, S), BF16),
                   jax.ShapeDtypeStruct((B, S, LANES), BF16),
                   jax.ShapeDtypeStruct((B, 8, S), F32)],
        compiler_params=pltpu.CompilerParams(
            dimension_semantics=("arbitrary", "arbitrary"),
            vmem_limit_bytes=48 * 1024 * 1024),
        name="attn_projections",
    )(x, wr, br, wc, bc, cos_r, sin_r, cos_c, sin_c)


MLSTM_HEADS = 8
MLSTM_QK_DIM = 64
MLSTM_V_DIM = 128
MLSTM_CHUNK = 128
CONV_WIDTH = 4
HALO = 8


def _silu(x):
    return x / (1.0 + jnp.exp(-x))


def _sigmoid(x):
    return 1.0 / (1.0 + jnp.exp(-x))


def _log_sigmoid(x):
    return jnp.minimum(x, 0.0) - jnp.log(1.0 + jnp.exp(-jnp.abs(x)))


def _mlstm_proj_kernel(x_ref, w_ref, b_ref, wt_ref, bt_ref, conv_ref,
                       mq_ref, mkt_ref, mv_ref, og_ref, ifc_ref, ift_ref, ext_ref, *, q_scale):
    i = pl.program_id(1)
    tm = x_ref.shape[1]
    qkw = 2 * mq_ref.shape[2]
    vw = mv_ref.shape[2]
    nh = ift_ref.shape[1] // 2
    xb = x_ref[0].astype(BF16)

    @pl.when(i == 0)
    def _():
        ext_ref[0:HALO, :] = jnp.zeros((HALO, qkw), F32)

    pqk = jnp.dot(xb, w_ref[:, 0:qkw], preferred_element_type=F32) + b_ref[:, 0:qkw]
    ext_ref[HALO:HALO + tm, :] = pqk
    acc = None
    for j in range(CONV_WIDTH):
        term = ext_ref[pl.ds(HALO - CONV_WIDTH + 1 + j, tm), :] * conv_ref[j:j + 1, :]
        acc = term if acc is None else acc + term
    ext_ref[0:HALO, :] = pqk[tm - HALO:tm, :]
    qk = _silu(acc)
    mq_ref[0] = (qk[:, 0:qkw // 2] * q_scale).astype(mq_ref.dtype)
    mkt_ref[0] = jnp.transpose(qk[:, qkw // 2:qkw]).astype(mkt_ref.dtype)

    pv = jnp.dot(xb, w_ref[:, qkw:qkw + vw], preferred_element_type=F32) + b_ref[:, qkw:qkw + vw]
    mv_ref[0] = pv.astype(mv_ref.dtype)
    po = (jnp.dot(xb, w_ref[:, qkw + vw:qkw + 2 * vw], preferred_element_type=F32)
          + b_ref[:, qkw + vw:qkw + 2 * vw])
    og_ref[0] = _sigmoid(po).astype(og_ref.dtype)

    pg = (jnp.dot(xb, w_ref[:, qkw + 2 * vw:], preferred_element_type=F32)
          + b_ref[:, qkw + 2 * vw:])
    lane = lax.broadcasted_iota(jnp.int32, pg.shape, 1)
    ifc_ref[0] = jnp.where(lane < nh, pg, _log_sigmoid(pg))
    pt = lax.dot_general(wt_ref[...], xb, (((1,), (1,)), ((), ())),
                         preferred_element_type=F32) + bt_ref[...]
    rowi = lax.broadcasted_iota(jnp.int32, pt.shape, 0)
    ift_ref[0] = jnp.where(rowi < nh, pt, _log_sigmoid(pt))


def mlstm_projections(x, w, b, wt, bt, conv, *, tm=512):
    B, S, D = x.shape
    qw = MLSTM_HEADS * MLSTM_QK_DIM
    vw = MLSTM_HEADS * MLSTM_V_DIM
    tm = min(tm, S)
    kern = functools.partial(_mlstm_proj_kernel, q_scale=MLSTM_QK_DIM ** -0.5)
    full = lambda a: pl.BlockSpec(a.shape, lambda b_, i: (0,) * a.ndim)
    row = lambda w_: pl.BlockSpec((1, tm, w_), lambda b_, i: (b_, i, 0))
    return pl.pallas_call(
        kern,
        grid=(B, S // tm),
        in_specs=[row(D), full(w), full(b), full(wt), full(bt), full(conv)],
        out_specs=[row(qw), pl.BlockSpec((1, qw, tm), lambda b_, i: (b_, 0, i)),
                   row(vw), row(vw), row(LANES),
                   pl.BlockSpec((1, 2 * MLSTM_HEADS, tm), lambda b_, i: (b_, 0, i))],
        out_shape=[jax.ShapeDtypeStruct((B, S, qw), BF16),
                   jax.ShapeDtypeStruct((B, qw, S), BF16),
                   jax.ShapeDtypeStruct((B, S, vw), BF16),
                   jax.ShapeDtypeStruct((B, S, vw), BF16),
                   jax.ShapeDtypeStruct((B, S, LANES), F32),
                   jax.ShapeDtypeStruct((B, 2 * MLSTM_HEADS, S), F32)],
        scratch_shapes=[pltpu.VMEM((HALO + tm, 2 * qw), F32)],
        compiler_params=pltpu.CompilerParams(
            dimension_semantics=("arbitrary", "arbitrary"),
            vmem_limit_bytes=56 * 1024 * 1024),
        name="mlstm_projections",
    )(x, w, b, wt, bt, conv)


GN_EPS = 1e-6


def _mlstm_kernel(mq_ref, mkt_ref, mv_ref, og_ref, ifc_ref, ift_ref, gain_ref, y_ref,
                  c_ref, m_ref):
    c = pl.program_id(1)
    L = mq_ref.shape[1]
    nh = ift_ref.shape[1] // 2
    dk = mq_ref.shape[2] // nh
    dv = mv_ref.shape[2] // nh

    @pl.when(c == 0)
    def _():
        c_ref[...] = jnp.zeros(c_ref.shape, F32)
        m_ref[...] = jnp.zeros(m_ref.shape, F32)

    r_i = lax.broadcasted_iota(jnp.int32, (L, L), 0)
    c_i = lax.broadcasted_iota(jnp.int32, (L, L), 1)
    causal = c_i <= r_i
    tril = jnp.where(causal, 1.0, 0.0)
    triu = jnp.where(r_i <= c_i, 1.0, 0.0)
    ifc = ifc_ref[0]
    ift = ift_ref[0]
    b_cols = jnp.dot(tril, ifc, preferred_element_type=F32, precision=lax.Precision.HIGHEST)
    b_rows = jnp.dot(ift, triu, preferred_element_type=F32, precision=lax.Precision.HIGHEST)
    ones_col = jnp.where(lax.broadcasted_iota(jnp.int32, (L, dv), 1) == 0, 1.0, 0.0).astype(BF16)

    for h in range(nh):
        m = m_ref[h]
        b_col = b_cols[:, nh + h:nh + h + 1]
        a_row = ift[h:h + 1, :] - b_rows[nh + h:nh + h + 1, :]
        amat = jnp.where(causal, a_row, NEG_INF)
        big_m = jnp.maximum(m, jnp.max(amat, axis=1, keepdims=True))
        decay = jnp.exp(amat - big_m)
        w_inter = jnp.exp(m - big_m)
        q = mq_ref[0, :, h * dk:(h + 1) * dk]
        kt = mkt_ref[0, h * dk:(h + 1) * dk, :]
        v_ext = jnp.concatenate([mv_ref[0, :, h * dv:(h + 1) * dv], ones_col], axis=1)
        s = jnp.dot(q, kt, preferred_element_type=F32) * decay
        sv = jnp.dot(s.astype(BF16), v_ext, preferred_element_type=F32)
        c_old = c_ref[h]
        qc = jnp.dot(q, c_old.astype(BF16), preferred_element_type=F32)
        num = sv[:, 0:dv] + w_inter * qc[:, 0:dv]
        den = sv[:, dv:dv + 1] + w_inter * qc[:, dv:dv + 1]
        hh = num / jnp.maximum(jnp.abs(den), jnp.exp(-(b_col + big_m)))
        mu = jnp.mean(hh, axis=1, keepdims=True)
        xc = hh - mu
        var = jnp.mean(xc * xc, axis=1, keepdims=True)
        hn = xc * lax.rsqrt(var + GN_EPS) * gain_ref[:, h * dv:(h + 1) * dv]
        y_ref[0, :, h * dv:(h + 1) * dv] = (
            og_ref[0, :, h * dv:(h + 1) * dv].astype(F32) * hn).astype(y_ref.dtype)

        b_last = b_col[L - 1:L, :]
        g_row = b_last + a_row
        m_new = jnp.maximum(b_last + m, jnp.max(g_row, axis=1, keepdims=True))
        carry = jnp.exp(b_last + m - m_new)
        wkt = (kt.astype(F32) * jnp.exp(g_row - m_new)).astype(BF16)
        c_ref[h] = carry * c_old + jnp.dot(wkt, v_ext, preferred_element_type=F32)
        m_ref[h] = m_new


def mlstm_scan(mq, mkt, mv, og, ifc, ift, gain):
    B, S, qw = mq.shape
    vw = mv.shape[2]
    L = min(MLSTM_CHUNK, S)
    nh = ift.shape[1] // 2
    row = lambda w_: pl.BlockSpec((1, L, w_), lambda b_, c: (b_, c, 0))
    return pl.pallas_call(
        _mlstm_kernel,
        grid=(B, S // L),
        in_specs=[row(qw), pl.BlockSpec((1, qw, L), lambda b_, c: (b_, 0, c)),
                  row(vw), row(vw), row(LANES),
                  pl.BlockSpec((1, 2 * nh, L), lambda b_, c: (b_, 0, c)),
                  pl.BlockSpec(gain.shape, lambda b_, c: (0, 0))],
        out_specs=row(vw),
        out_shape=jax.ShapeDtypeStruct((B, S, vw), BF16),
        scratch_shapes=[pltpu.VMEM((nh, qw // nh, 2 * vw // nh), F32),
                        pltpu.VMEM((nh, 1, 1), F32)],
        compiler_params=pltpu.CompilerParams(
            dimension_semantics=("arbitrary", "arbitrary")),
        name="mlstm_scan",
    )(mq, mkt, mv, og, ifc, ift, gain)


LN_EPS = 1e-5


def _layer_norm(z, gain, bias):
    mu = jnp.mean(z, axis=1, keepdims=True)
    zc = z - mu
    var = jnp.mean(zc * zc, axis=1, keepdims=True)
    return zc * lax.rsqrt(var + LN_EPS) * gain + bias


def _merge_kernel(x_ref, ya_ref, ym_ref, wg_ref, bg_ref, wa_ref, wm_ref, wo_ref, g_ref, b_ref,
                  o_ref, *, alpha):
    x = x_ref[...]
    xb = x.astype(BF16)
    d = x.shape[1]
    ga = _sigmoid(jnp.dot(xb, wg_ref[:, 0:d], preferred_element_type=F32) + bg_ref[:, 0:d])
    merged = ga * jnp.dot(ya_ref[...], wa_ref[...], preferred_element_type=F32)
    gm = _sigmoid(jnp.dot(xb, wg_ref[:, d:2 * d], preferred_element_type=F32) + bg_ref[:, d:2 * d])
    merged = merged + gm * jnp.dot(ym_ref[...], wm_ref[...], preferred_element_type=F32)
    z = alpha * x + jnp.dot(merged.astype(BF16), wo_ref[...], preferred_element_type=F32)
    o_ref[...] = _layer_norm(z, g_ref[...], b_ref[...])


def merge_branches(x2, ya, ym, wg, bg, wa, wm, wo, g, b, *, alpha, tm=512):
    T, D = x2.shape
    tm = min(tm, T)
    full = lambda a: pl.BlockSpec(a.shape, lambda i: (0,) * a.ndim)
    row = lambda w_: pl.BlockSpec((tm, w_), lambda i: (i, 0))
    return pl.pallas_call(
        functools.partial(_merge_kernel, alpha=alpha),
        grid=(T // tm,),
        in_specs=[row(D), row(ya.shape[1]), row(ym.shape[1]), full(wg), full(bg), full(wa),
                  full(wm), full(wo), full(g), full(b)],
        out_specs=row(D),
        out_shape=jax.ShapeDtypeStruct((T, D), F32),
        compiler_params=pltpu.CompilerParams(
            dimension_semantics=("arbitrary",), vmem_limit_bytes=48 * 1024 * 1024),
        name="merge_branches",
    )(x2, ya, ym, wg, bg, wa, wm, wo, g, b)


N_GROUPS = 4
EXPERTS_PER_GROUP = 4
N_EXPERTS = N_GROUPS * EXPERTS_PER_GROUP


def _first_lane_of_max(vals, vmax, lane):
    return jnp.min(jnp.where(vals == vmax, lane, LANES), axis=1, keepdims=True)


def _moe_kernel(x_ref, wr_ref, br_ref, wg_ref, wu_ref, wd_ref, g_ref, b_ref, o_ref,
                comb_ref, acc_ref, *, alpha):
    e = pl.program_id(1)
    x = x_ref[...]
    tm = x.shape[0]

    @pl.when(e == 0)
    def _():
        logits = jnp.dot(x, wr_ref[...], preferred_element_type=F32,
                         precision=lax.Precision.HIGHEST) + br_ref[...]
        lane = lax.broadcasted_iota(jnp.int32, logits.shape, 1)
        g = jnp.where(lane < N_GROUPS, logits, NEG_INF)
        gmax = jnp.max(g, axis=1, keepdims=True)
        g_w = 1.0 / jnp.sum(jnp.exp(g - gmax), axis=1, keepdims=True)
        g_sel = _first_lane_of_max(g, gmax, lane)
        lo = N_GROUPS + EXPERTS_PER_GROUP * g_sel
        ev = jnp.where(jnp.logical_and(lane >= lo, lane < lo + EXPERTS_PER_GROUP), logits, NEG_INF)
        v1 = jnp.max(ev, axis=1, keepdims=True)
        i1 = _first_lane_of_max(ev, v1, lane)
        ev2 = jnp.where(lane == i1, NEG_INF, ev)
        v2 = jnp.max(ev2, axis=1, keepdims=True)
        i2 = _first_lane_of_max(ev2, v2, lane)
        r = jnp.exp(v2 - v1)
        p1 = 1.0 / (1.0 + r)
        p2 = r / (1.0 + r)
        comb_ref[...] = (jnp.where(lane == i1, g_w * p1, 0.0)
                         + jnp.where(lane == i2, g_w * p2, 0.0))
        acc_ref[...] = jnp.zeros(acc_ref.shape, F32)

    xb = x.astype(BF16)
    hg = jnp.dot(xb, wg_ref[0], preferred_element_type=F32)
    hu = jnp.dot(xb, wu_ref[0], preferred_element_type=F32)
    hdn = (_silu(hg) * hu).astype(BF16)
    lane = lax.broadcasted_iota(jnp.int32, (tm, LANES), 1)
    cw = jnp.sum(jnp.where(lane == N_GROUPS + e, comb_ref[...], 0.0), axis=1, keepdims=True)
    acc_ref[...] += cw * jnp.dot(hdn, wd_ref[0], preferred_element_type=F32)

    @pl.when(e == pl.num_programs(1) - 1)
    def _():
        o_ref[...] = _layer_norm(alpha * x + acc_ref[...], g_ref[...], b_ref[...])


def moe_layer(x2, wr, br, wg, wu, wd, g, b, *, alpha, tm=1024):
    T, D = x2.shape
    E, _, F = wg.shape
    tm = min(tm, T)
    full = lambda a: pl.BlockSpec(a.shape, lambda i, e: (0,) * a.ndim)
    return pl.pallas_call(
        functools.partial(_moe_kernel, alpha=alpha),
        grid=(T // tm, E),
        in_specs=[pl.BlockSpec((tm, D), lambda i, e: (i, 0)), full(wr), full(br),
                  pl.BlockSpec((1, D, F), lambda i, e: (e, 0, 0)),
                  pl.BlockSpec((1, D, F), lambda i, e: (e, 0, 0)),
                  pl.BlockSpec((1, F, D), lambda i, e: (e, 0, 0)),
                  full(g), full(b)],
        out_specs=pl.BlockSpec((tm, D), lambda i, e: (i, 0)),
        out_shape=jax.ShapeDtypeStruct((T, D), F32),
        scratch_shapes=[pltpu.VMEM((tm, LANES), F32), pltpu.VMEM((tm, D), F32)],
        compiler_params=pltpu.CompilerParams(
            dimension_semantics=("arbitrary", "arbitrary"), vmem_limit_bytes=48 * 1024 * 1024),
        name="moe_layer",
    )(x2, wr, br, wg, wu, wd, g, b)


DEPTH = 1
DEEPNORM_ALPHA = (2.0 * DEPTH) ** 0.25


def _pad_cols(a, width):
    return jnp.pad(a, ((0, 0), (0, width - a.shape[1])))


def kernel(x, w_in, b_in, conv_m, gn_m_gain, w_branch_attn, w_branch_mlstm, w_out, ln1_gain, ln1_bias, w_router_group, b_router_group, w_router_expert, b_router_expert, w_exp_gate, w_exp_up, w_exp_down, ln2_gain, ln2_bias):
    B, S, D = x.shape
    aw = ATT_HEADS * ATT_HEAD_DIM
    iw = IDX_HEADS * IDX_DIM
    qw = MLSTM_HEADS * MLSTM_QK_DIM
    vw = MLSTM_HEADS * MLSTM_V_DIM
    widths = (aw, aw, aw, iw, IDX_DIM, IDX_HEADS, qw, qw, vw, MLSTM_HEADS, MLSTM_HEADS, vw, D, D)
    offs = [0]
    for w_ in widths:
        offs.append(offs[-1] + w_)
    col = lambda k: w_in[:, offs[k]:offs[k + 1]]
    bia = lambda k: b_in[offs[k]:offs[k + 1]]
    (A_Q, A_K, A_V, I_Q, I_K, I_W, M_Q, M_K, M_V, M_I, M_F, M_O, G_A, G_M) = range(14)

    wr = jnp.concatenate([col(A_K), _pad_cols(col(I_K), LANES)], 1).astype(BF16)
    br = jnp.concatenate([bia(A_K), jnp.pad(bia(I_K), (0, LANES - IDX_DIM))])[None, :]
    wc = jnp.concatenate([col(A_Q), col(A_V), col(I_Q), _pad_cols(col(I_W), 8)], 1).T.astype(BF16)
    bc = jnp.concatenate([bia(A_Q), bia(A_V), bia(I_Q), jnp.pad(bia(I_W), (0, 8 - IDX_HEADS))])[:, None]
    gpad = LANES - 2 * MLSTM_HEADS
    wm = jnp.concatenate([col(M_Q), col(M_K), col(M_V), col(M_O), col(M_I),
                          _pad_cols(col(M_F), MLSTM_HEADS + gpad)], 1).astype(BF16)
    bm = jnp.concatenate([bia(M_Q), bia(M_K), bia(M_V), bia(M_O), bia(M_I),
                          jnp.pad(bia(M_F), (0, gpad))])[None, :]
    wmt = jnp.concatenate([col(M_I), col(M_F)], 1).T.astype(BF16)
    bmt = jnp.concatenate([bia(M_I), bia(M_F)])[:, None]
    wgate = jnp.concatenate([col(G_A), col(G_M)], 1).astype(BF16)
    bgate = jnp.concatenate([bia(G_A), bia(G_M)])[None, :]

    half = ATT_HEAD_DIM // 2
    inv = ROPE_THETA ** (-jnp.arange(0, ATT_HEAD_DIM, 2, dtype=F32) / ATT_HEAD_DIM)
    ang = jnp.arange(S, dtype=F32)[:, None] * inv[None, :]
    cos, sin = jnp.cos(ang), jnp.sin(ang)
    cos_r = jnp.tile(cos, (1, LANES // half))
    sin_r = jnp.tile(jnp.concatenate([-sin, sin], 1), (1, LANES // ATT_HEAD_DIM))

    qt, k, vt, qit, ki, wit = attn_projections(x, wr, br, wc, bc, cos_r, sin_r, cos.T, sin.T)
    y_attn = dsa_attention(qt, k, vt, qit, ki, wit, topk=min(IDX_TOPK_MAX, S // 4))

    mq, mkt, mv, og, ifc, ift = mlstm_projections(x, wm, bm, wmt, bmt, conv_m)
    y_mlstm = mlstm_scan(mq, mkt, mv, og, ifc, ift, gn_m_gain[None, :])

    x1 = merge_branches(x.reshape(B * S, D), y_attn.reshape(B * S, aw), y_mlstm.reshape(B * S, vw),
                        wgate, bgate, w_branch_attn.astype(BF16), w_branch_mlstm.astype(BF16),
                        w_out.astype(BF16), ln1_gain[None, :], ln1_bias[None, :],
                        alpha=DEEPNORM_ALPHA)

    w_router = _pad_cols(jnp.concatenate([w_router_group, w_router_expert], 1), LANES)
    b_router = jnp.pad(jnp.concatenate([b_router_group, b_router_expert]),
                       (0, LANES - N_GROUPS - N_EXPERTS))[None, :]
    out = moe_layer(x1, w_router, b_router, w_exp_gate.astype(BF16), w_exp_up.astype(BF16),
                    w_exp_down.astype(BF16), ln2_gain[None, :], ln2_bias[None, :],
                    alpha=DEEPNORM_ALPHA)
    return out.reshape(B, S, D)
```

```python
import functools

import jax
import jax.numpy as jnp
from jax import lax
from jax.experimental import pallas as pl
from jax.experimental.pallas import tpu as pltpu

F32 = jnp.float32
BF16 = jnp.bfloat16
NEG_INF = float("-inf")
LOG2_E = 1.4426950408889634

ATT_HEADS = 8
ATT_HEAD_DIM = 64
IDX_HEADS = 4
IDX_DIM = 64
IDX_TOPK_MAX = 256
Q_BLOCK = 128
ROPE_THETA = 10000.0

LANES = 128
BF16_ROWS = 16


def _key_to_f32(u):
    ks = u ^ jnp.int32(-2 ** 31)
    bits = ks ^ ((ks >> 31) & jnp.int32(0x7FFFFFFF))
    return lax.bitcast_convert_type(bits, F32)


def _dsa_kernel(qt_ref, k_ref, vt_ref, qit_ref, ki_ref, wit_ref, o_ref,
                sc_ref, qm_ref, m_ref, l_ref, acc_ref, *, topk, kchunk):
    qb = pl.program_id(1)
    tq = o_ref.shape[1]
    heads = qt_ref.shape[1] // ATT_HEAD_DIM
    step = 2 * LANES
    n_chunk = (qb * tq + tq + kchunk - 1) // kchunk
    qpos = lax.broadcasted_iota(jnp.int32, (1, tq), 1) + qb * tq

    qit = qit_ref[0]
    zpad = jnp.zeros((LANES - IDX_DIM, tq), BF16)
    qi_pair = []
    for p in range(IDX_HEADS // 2):
        cols = [jnp.concatenate([qit[h * IDX_DIM:(h + 1) * IDX_DIM], zpad], axis=0)
                for h in (2 * p, 2 * p + 1)]
        qi_pair.append(jnp.concatenate(cols, axis=1))
    wit = wit_ref[0]

    def score_body(c, carry):
        off = pl.multiple_of(c * kchunk, kchunk)
        ki = ki_ref[0, pl.ds(off, kchunk), :]
        tot = None
        for p in range(IDX_HEADS // 2):
            s2 = jnp.dot(ki, qi_pair[p], preferred_element_type=F32)
            for j in range(2):
                h = 2 * p + j
                s = jnp.maximum(s2[:, j * tq:(j + 1) * tq], 0.0) * wit[h:h + 1, :]
                tot = s if tot is None else tot + s
        kpos = lax.broadcasted_iota(jnp.int32, (kchunk, tq), 0) + off
        sc_ref[pl.ds(off, kchunk), :] = jnp.where(kpos <= qpos, tot + 0.0, NEG_INF)
        return carry

    lax.fori_loop(0, n_chunk, score_body, 0)

    def count(pred):
        def body(j, acc):
            off = pl.multiple_of(j * kchunk, kchunk)
            for t in range(kchunk // LANES):
                x = sc_ref[pl.ds(off + t * LANES, LANES), :]
                acc = acc + jnp.where(pred(x, off + t * LANES), 1.0, 0.0)
            return acc
        acc = lax.fori_loop(0, n_chunk, body, jnp.zeros((LANES, tq), F32))
        return jnp.sum(acc, axis=0, keepdims=True)

    def bit_body(i, carry):
        u, cgt = carry
        cand = u | (jnp.int32(1) << (31 - i))
        thr = _key_to_f32(cand)
        cnt = count(lambda x, off: x >= thr)
        ok = cnt >= float(topk)
        return jnp.where(ok, cand, u), jnp.where(ok, cgt, cnt)

    u, cgt = lax.fori_loop(0, 32, bit_body,
                           (jnp.zeros((1, tq), jnp.int32), jnp.zeros((1, tq), F32)))
    short = qpos < topk
    tau = jnp.where(short, NEG_INF, _key_to_f32(u))
    need = float(topk) - cgt

    r_i = lax.broadcasted_iota(jnp.int32, (LANES, LANES), 0)
    c_i = lax.broadcasted_iota(jnp.int32, (LANES, LANES), 1)
    tril = jnp.where(c_i <= r_i, 1.0, 0.0).astype(BF16)

    def bias_body(c, seen):
        base = pl.multiple_of(c * kchunk, kchunk)
        xs = [sc_ref[pl.ds(base + t * LANES, LANES), :] for t in range(kchunk // LANES)]
        ties = [x == tau for x in xs]
        ranks = [jnp.dot(tril, jnp.where(tie, 1.0, 0.0).astype(BF16),
                         preferred_element_type=F32) for tie in ties]
        for t, (x, tie, rank) in enumerate(zip(xs, ties, ranks)):
            keep = jnp.logical_or(x > tau, jnp.logical_and(tie, rank + seen <= need))
            kpos = lax.broadcasted_iota(jnp.int32, x.shape, 0) + (base + t * LANES)
            keep = jnp.logical_and(keep, kpos <= qpos)
            sc_ref[pl.ds(base + t * LANES, LANES), :] = jnp.where(keep, 0.0, NEG_INF)
            seen = seen + rank[LANES - 1:LANES, :]
        return seen

    lax.fori_loop(0, n_chunk, bias_body, jnp.zeros((1, tq), F32))

    qt = qt_ref[0]
    zrow = jnp.zeros((ATT_HEAD_DIM, tq), BF16)
    for p in range(heads // 2):
        a = qt[(2 * p) * ATT_HEAD_DIM:(2 * p + 1) * ATT_HEAD_DIM]
        b = qt[(2 * p + 1) * ATT_HEAD_DIM:(2 * p + 2) * ATT_HEAD_DIM]
        qm_ref[p] = jnp.concatenate([jnp.concatenate([a, zrow], axis=0),
                                     jnp.concatenate([zrow, b], axis=0)], axis=1)
    m_ref[...] = jnp.full(m_ref.shape, NEG_INF, F32)
    l_ref[...] = jnp.zeros(l_ref.shape, F32)
    acc_ref[...] = jnp.zeros(acc_ref.shape, F32)
    ones_rows = jnp.ones((BF16_ROWS, step), BF16)
    achunk = min(2 * kchunk, sc_ref.shape[0])
    n_achunk = (qb * tq + tq + achunk - 1) // achunk

    @pl.when(n_achunk * achunk > n_chunk * kchunk)
    def _():
        fill = pl.multiple_of(n_chunk * kchunk, kchunk)
        sc_ref[pl.ds(fill, kchunk), :] = jnp.full((kchunk, tq), NEG_INF, F32)

    stages = [(sub, p) for sub in range(achunk // step) for p in range(heads // 2)]

    def attn_body(c, carry):
        off = pl.multiple_of(c * achunk, achunk)

        def qk(stage):
            sub, p = stage
            kp = k_ref[0, pl.ds(off + sub * step, step), p * LANES:(p + 1) * LANES]
            return jnp.dot(kp, qm_ref[p], preferred_element_type=F32)

        ahead = 4
        pending = [qk(st) for st in stages[:ahead]]
        for i, (sub, p) in enumerate(stages):
            if i + ahead < len(stages):
                pending.append(qk(stages[i + ahead]))
            s2 = pending.pop(0)
            koff = off + sub * step
            bias = sc_ref[pl.ds(koff, step), :]
            for j in range(2):
                h = 2 * p + j
                s = s2[:, j * tq:(j + 1) * tq] + bias
                m_old = m_ref[h]
                m_new = jnp.maximum(m_old, jnp.max(s, axis=0, keepdims=True))
                m_use = jnp.where(m_new == NEG_INF, 0.0, m_new)
                pexp = jnp.exp2(s - m_use).astype(BF16)
                alpha = jnp.exp2(m_old - m_use)
                vt = jnp.concatenate(
                    [vt_ref[0, h * ATT_HEAD_DIM:(h + 1) * ATT_HEAD_DIM, pl.ds(koff, step)],
                     ones_rows], axis=0)
                pv = jnp.dot(vt, pexp, preferred_element_type=F32)
                acc_ref[h] = alpha * acc_ref[h] + pv[0:ATT_HEAD_DIM]
                l_ref[h] = alpha * l_ref[h] + pv[ATT_HEAD_DIM:ATT_HEAD_DIM + 1]
                m_ref[h] = m_new
        return carry

    lax.fori_loop(0, n_achunk, attn_body, 0)

    out_t = jnp.concatenate([acc_ref[h] / l_ref[h] for h in range(heads)], axis=0)
    o_ref[0] = jnp.transpose(out_t).astype(o_ref.dtype)


def dsa_attention(qt, k, vt, qit, ki, wit, *, topk, kchunk=512):
    B, W, S = qt.shape
    tq = Q_BLOCK
    kchunk = min(kchunk, S)
    heads = W // ATT_HEAD_DIM
    kern = functools.partial(_dsa_kernel, topk=topk, kchunk=kchunk)
    qcol = lambda a: pl.BlockSpec((1, a.shape[1], tq), lambda b, i: (b, 0, i))
    whole = lambda a: pl.BlockSpec((1,) + a.shape[1:], lambda b, i: (b, 0, 0))
    return pl.pallas_call(
        kern,
        grid=(B, S // tq),
        in_specs=[qcol(qt), whole(k), whole(vt), qcol(qit), whole(ki), qcol(wit)],
        out_specs=pl.BlockSpec((1, tq, W), lambda b, i: (b, i, 0)),
        out_shape=jax.ShapeDtypeStruct((B, S, W), BF16),
        scratch_shapes=[
            pltpu.VMEM((S, tq), F32),
            pltpu.VMEM((heads // 2, LANES, 2 * tq), BF16),
            pltpu.VMEM((heads, 1, tq), F32),
            pltpu.VMEM((heads, 1, tq), F32),
            pltpu.VMEM((heads, ATT_HEAD_DIM, tq), F32),
        ],
        compiler_params=pltpu.CompilerParams(
            dimension_semantics=("arbitrary", "arbitrary"),
            vmem_limit_bytes=56 * 1024 * 1024),
        name="dsa_attention",
    )(qt, k, vt, qit, ki, wit)


def _rope_rows(x, cos_t, sin_t):
    lane = lax.broadcasted_iota(jnp.int32, x.shape, 1)
    swapped = jnp.where((lane % ATT_HEAD_DIM) < ATT_HEAD_DIM // 2,
                        pltpu.roll(x, LANES - ATT_HEAD_DIM // 2, 1),
                        pltpu.roll(x, ATT_HEAD_DIM // 2, 1))
    return x * cos_t + swapped * sin_t


def _attn_proj_kernel(x_ref, wr_ref, br_ref, wc_ref, bc_ref, cos_ref, sin_ref, cost_ref, sint_ref,
                      qt_ref, k_ref, vt_ref, qit_ref, ki_ref, wit_ref, *, q_scale, wi_scale):
    xb = x_ref[0].astype(BF16)
    aw = k_ref.shape[2]
    iw = qit_ref.shape[1]
    half = ATT_HEAD_DIM // 2
    cos_t = cos_ref[...]
    sin_t = sin_ref[...]

    pk = jnp.dot(xb, wr_ref[...], preferred_element_type=F32) + br_ref[...]
    for j in range(aw // LANES):
        sl = slice(j * LANES, (j + 1) * LANES)
        k_ref[0, :, sl] = _rope_rows(pk[:, sl], cos_t, sin_t).astype(k_ref.dtype)
    ki_ref[0] = _rope_rows(pk[:, aw:aw + LANES], cos_t, sin_t).astype(ki_ref.dtype)

    pt = lax.dot_general(wc_ref[...], xb, (((1,), (1,)), ((), ())),
                         preferred_element_type=F32) + bc_ref[...]
    ct = cost_ref[...]
    st = sint_ref[...]

    def rope_cols(src0, dst_ref, nheads, scale):
        for h in range(nheads):
            r0 = src0 + h * ATT_HEAD_DIM
            x1 = pt[r0:r0 + half]
            x2 = pt[r0 + half:r0 + 2 * half]
            d0 = h * ATT_HEAD_DIM
            dst_ref[0, d0:d0 + half, :] = ((x1 * ct - x2 * st) * scale).astype(dst_ref.dtype)
            dst_ref[0, d0 + half:d0 + 2 * half, :] = ((x1 * st + x2 * ct) * scale).astype(dst_ref.dtype)

    rope_cols(0, qt_ref, aw // ATT_HEAD_DIM, q_scale)
    vt_ref[0] = pt[aw:2 * aw].astype(vt_ref.dtype)
    rope_cols(2 * aw, qit_ref, iw // IDX_DIM, 1.0)
    wit_ref[0] = pt[2 * aw + iw:] * wi_scale


def attn_projections(x, wr, br, wc, bc, cos_r, sin_r, cos_c, sin_c, *, tm=512):
    B, S, D = x.shape
    aw = ATT_HEADS * ATT_HEAD_DIM
    iw = IDX_HEADS * IDX_DIM
    tm = min(tm, S)
    kern = functools.partial(_attn_proj_kernel, q_scale=ATT_HEAD_DIM ** -0.5 * LOG2_E,
                             wi_scale=IDX_HEADS ** -0.5 * IDX_DIM ** -0.5)
    full = lambda a: pl.BlockSpec(a.shape, lambda b, i: (0,) * a.ndim)
    row = lambda w: pl.BlockSpec((1, tm, w), lambda b, i: (b, i, 0))
    colm = lambda r: pl.BlockSpec((1, r, tm), lambda b, i: (b, 0, i))
    return pl.pallas_call(
        kern,
        grid=(B, S // tm),
        in_specs=[row(D), full(wr), full(br), full(wc), full(bc),
                  pl.BlockSpec((tm, LANES), lambda b, i: (i, 0)),
                  pl.BlockSpec((tm, LANES), lambda b, i: (i, 0)),
                  pl.BlockSpec((ATT_HEAD_DIM // 2, tm), lambda b, i: (0, i)),
                  pl.BlockSpec((ATT_HEAD_DIM // 2, tm), lambda b, i: (0, i))],
        out_specs=[colm(aw), row(aw), colm(aw), colm(iw), row(LANES), colm(8)],
        out_shape=[jax.ShapeDtypeStruct((B, aw, S), BF16),
                   jax.ShapeDtypeStruct((B, S, aw), BF16),
                   jax.ShapeDtypeStruct((B, aw, S), BF16),
                   jax.ShapeDtypeStruct((B, iw, S), BF16),
                   jax.ShapeDtypeStruct((B, S, LANES), BF16),
                   jax.ShapeDtypeStruct((B, 8, S), F32)],
        compiler_params=pltpu.CompilerParams(
            dimension_semantics=("arbitrary", "arbitrary"),
            vmem_limit_bytes=48 * 1024 * 1024),
        name="attn_projections",
    )(x, wr, br, wc, bc, cos_r, sin_r, cos_c, sin_c)


MLSTM_HEADS = 8
MLSTM_QK_DIM = 64
MLSTM_V_DIM = 128
MLSTM_CHUNK = 128
CONV_WIDTH = 4
HALO = 8


def _silu(x):
    return x / (1.0 + jnp.exp(-x))


def _sigmoid(x):
    return 1.0 / (1.0 + jnp.exp(-x))


def _log_sigmoid(x):
    return jnp.minimum(x, 0.0) - jnp.log(1.0 + jnp.exp(-jnp.abs(x)))


def _mlstm_proj_kernel(x_ref, w_ref, b_ref, wt_ref, bt_ref, conv_ref,
                       mq_ref, mkt_ref, mv_ref, og_ref, ifc_ref, ift_ref, ext_ref, *, q_scale):
    i = pl.program_id(1)
    tm = x_ref.shape[1]
    qkw = 2 * mq_ref.shape[2]
    vw = mv_ref.shape[2]
    nh = ift_ref.shape[1] // 2
    xb = x_ref[0].astype(BF16)

    @pl.when(i == 0)
    def _():
        ext_ref[0:HALO, :] = jnp.zeros((HALO, qkw), F32)

    pqk = jnp.dot(xb, w_ref[:, 0:qkw], preferred_element_type=F32) + b_ref[:, 0:qkw]
    ext_ref[HALO:HALO + tm, :] = pqk
    acc = None
    for j in range(CONV_WIDTH):
        term = ext_ref[pl.ds(HALO - CONV_WIDTH + 1 + j, tm), :] * conv_ref[j:j + 1, :]
        acc = term if acc is None else acc + term
    ext_ref[0:HALO, :] = pqk[tm - HALO:tm, :]
    qk = _silu(acc)
    mq_ref[0] = (qk[:, 0:qkw // 2] * q_scale).astype(mq_ref.dtype)
    mkt_ref[0] = jnp.transpose(qk[:, qkw // 2:qkw]).astype(mkt_ref.dtype)

    pv = jnp.dot(xb, w_ref[:, qkw:qkw + vw], preferred_element_type=F32) + b_ref[:, qkw:qkw + vw]
    mv_ref[0] = pv.astype(mv_ref.dtype)
    po = (jnp.dot(xb, w_ref[:, qkw + vw:qkw + 2 * vw], preferred_element_type=F32)
          + b_ref[:, qkw + vw:qkw + 2 * vw])
    og_ref[0] = _sigmoid(po).astype(og_ref.dtype)

    pg = (jnp.dot(xb, w_ref[:, qkw + 2 * vw:], preferred_element_type=F32)
          + b_ref[:, qkw + 2 * vw:])
    lane = lax.broadcasted_iota(jnp.int32, pg.shape, 1)
    ifc_ref[0] = jnp.where(lane < nh, pg, _log_sigmoid(pg))
    pt = lax.dot_general(wt_ref[...], xb, (((1,), (1,)), ((), ())),
                         preferred_element_type=F32) + bt_ref[...]
    rowi = lax.broadcasted_iota(jnp.int32, pt.shape, 0)
    ift_ref[0] = jnp.where(rowi < nh, pt, _log_sigmoid(pt))


def mlstm_projections(x, w, b, wt, bt, conv, *, tm=512):
    B, S, D = x.shape
    qw = MLSTM_HEADS * MLSTM_QK_DIM
    vw = MLSTM_HEADS * MLSTM_V_DIM
    tm = min(tm, S)
    kern = functools.partial(_mlstm_proj_kernel, q_scale=MLSTM_QK_DIM ** -0.5)
    full = lambda a: pl.BlockSpec(a.shape, lambda b_, i: (0,) * a.ndim)
    row = lambda w_: pl.BlockSpec((1, tm, w_), lambda b_, i: (b_, i, 0))
    return pl.pallas_call(
        kern,
        grid=(B, S // tm),
        in_specs=[row(D), full(w), full(b), full(wt), full(bt), full(conv)],
        out_specs=[row(qw), pl.BlockSpec((1, qw, tm), lambda b_, i: (b_, 0, i)),
                   row(vw), row(vw), row(LANES),
                   pl.BlockSpec((1, 2 * MLSTM_HEADS, tm), lambda b_, i: (b_, 0, i))],
        out_shape=[jax.ShapeDtypeStruct((B, S, qw), BF16),
                   jax.ShapeDtypeStruct((B, qw, S), BF16),
                   jax.ShapeDtypeStruct((B, S, vw), BF16),
                   jax.ShapeDtypeStruct((B, S, vw), BF16),
                   jax.ShapeDtypeStruct((B, S, LANES), F32),
                   jax.ShapeDtypeStruct((B, 2 * MLSTM_HEADS, S), F32)],
        scratch_shapes=[pltpu.VMEM((HALO + tm, 2 * qw), F32)],
        compiler_params=pltpu.CompilerParams(
            dimension_semantics=("arbitrary", "arbitrary"),
            vmem_limit_bytes=56 * 1024 * 1024),
        name="mlstm_projections",
    )(x, w, b, wt, bt, conv)


GN_EPS = 1e-6


def _mlstm_kernel(mq_ref, mkt_ref, mv_ref, og_ref, ifc_ref, ift_ref, gain_ref, y_ref,
                  c_ref, m_ref):
    c = pl.program_id(1)
    L = mq_ref.shape[1]
    nh = ift_ref.shape[1] // 2
    dk = mq_ref.shape[2] // nh
    dv = mv_ref.shape[2] // nh

    @pl.when(c == 0)
    def _():
        c_ref[...] = jnp.zeros(c_ref.shape, F32)
        m_ref[...] = jnp.zeros(m_ref.shape, F32)

    r_i = lax.broadcasted_iota(jnp.int32, (L, L), 0)
    c_i = lax.broadcasted_iota(jnp.int32, (L, L), 1)
    causal = c_i <= r_i
    tril = jnp.where(causal, 1.0, 0.0)
    triu = jnp.where(r_i <= c_i, 1.0, 0.0)
    ifc = ifc_ref[0]
    ift = ift_ref[0]
    b_cols = jnp.dot(tril, ifc, preferred_element_type=F32, precision=lax.Precision.HIGHEST)
    b_rows = jnp.dot(ift, triu, preferred_element_type=F32, precision=lax.Precision.HIGHEST)
    ones_col = jnp.where(lax.broadcasted_iota(jnp.int32, (L, dv), 1) == 0, 1.0, 0.0).astype(BF16)

    for h in range(nh):
        m = m_ref[h]
        b_col = b_cols[:, nh + h:nh + h + 1]
        a_row = ift[h:h + 1, :] - b_rows[nh + h:nh + h + 1, :]
        amat = jnp.where(causal, a_row, NEG_INF)
        big_m = jnp.maximum(m, jnp.max(amat, axis=1, keepdims=True))
        decay = jnp.exp(amat - big_m)
        w_inter = jnp.exp(m - big_m)
        q = mq_ref[0, :, h * dk:(h + 1) * dk]
        kt = mkt_ref[0, h * dk:(h + 1) * dk, :]
        v_ext = jnp.concatenate([mv_ref[0, :, h * dv:(h + 1) * dv], ones_col], axis=1)
        s = jnp.dot(q, kt, preferred_element_type=F32) * decay
        sv = jnp.dot(s.astype(BF16), v_ext, preferred_element_type=F32)
        c_old = c_ref[h]
        qc = jnp.dot(q, c_old.astype(BF16), preferred_element_type=F32)
        num = sv[:, 0:dv] + w_inter * qc[:, 0:dv]
        den = sv[:, dv:dv + 1] + w_inter * qc[:, dv:dv + 1]
        hh = num / jnp.maximum(jnp.abs(den), jnp.exp(-(b_col + big_m)))
        mu = jnp.mean(hh, axis=1, keepdims=True)
        xc = hh - mu
        var = jnp.mean(xc * xc, axis=1, keepdims=True)
        hn = xc * lax.rsqrt(var + GN_EPS) * gain_ref[:, h * dv:(h + 1) * dv]
        y_ref[0, :, h * dv:(h + 1) * dv] = (
            og_ref[0, :, h * dv:(h + 1) * dv].astype(F32) * hn).astype(y_ref.dtype)

        b_last = b_col[L - 1:L, :]
        g_row = b_last + a_row
        m_new = jnp.maximum(b_last + m, jnp.max(g_row, axis=1, keepdims=True))
        carry = jnp.exp(b_last + m - m_new)
        wkt = (kt.astype(F32) * jnp.exp(g_row - m_new)).astype(BF16)
        c_ref[h] = carry * c_old + jnp.dot(wkt, v_ext, preferred_element_type=F32)
        m_ref[h] = m_new


def mlstm_scan(mq, mkt, mv, og, ifc, ift, gain):
    B, S, qw = mq.shape
    vw = mv.shape[2]
    L = min(MLSTM_CHUNK, S)
    nh = ift.shape[1] // 2
    row = lambda w_: pl.BlockSpec((1, L, w_), lambda b_, c: (b_, c, 0))
    return pl.pallas_call(
        _mlstm_kernel,
        grid=(B, S // L),
        in_specs=[row(qw), pl.BlockSpec((1, qw, L), lambda b_, c: (b_, 0, c)),
                  row(vw), row(vw), row(LANES),
                  pl.BlockSpec((1, 2 * nh, L), lambda b_, c: (b_, 0, c)),
                  pl.BlockSpec(gain.shape, lambda b_, c: (0, 0))],
        out_specs=row(vw),
        out_shape=jax.ShapeDtypeStruct((B, S, vw), BF16),
        scratch_shapes=[pltpu.VMEM((nh, qw // nh, 2 * vw // nh), F32),
                        pltpu.VMEM((nh, 1, 1), F32)],
        compiler_params=pltpu.CompilerParams(
            dimension_semantics=("arbitrary", "arbitrary")),
        name="mlstm_scan",
    )(mq, mkt, mv, og, ifc, ift, gain)


LN_EPS = 1e-5


def _layer_norm(z, gain, bias):
    mu = jnp.mean(z, axis=1, keepdims=True)
    zc = z - mu
    var = jnp.mean(zc * zc, axis=1, keepdims=True)
    return zc * lax.rsqrt(var + LN_EPS) * gain + bias


def _merge_kernel(x_ref, ya_ref, ym_ref, wg_ref, bg_ref, wa_ref, wm_ref, wo_ref, g_ref, b_ref,
                  o_ref, *, alpha):
    x = x_ref[...]
    xb = x.astype(BF16)
    d = x.shape[1]
    ga = _sigmoid(jnp.dot(xb, wg_ref[:, 0:d], preferred_element_type=F32) + bg_ref[:, 0:d])
    merged = ga * jnp.dot(ya_ref[...], wa_ref[...], preferred_element_type=F32)
    gm = _sigmoid(jnp.dot(xb, wg_ref[:, d:2 * d], preferred_element_type=F32) + bg_ref[:, d:2 * d])
    merged = merged + gm * jnp.dot(ym_ref[...], wm_ref[...], preferred_element_type=F32)
    z = alpha * x + jnp.dot(merged.astype(BF16), wo_ref[...], preferred_element_type=F32)
    o_ref[...] = _layer_norm(z, g_ref[...], b_ref[...])


def merge_branches(x2, ya, ym, wg, bg, wa, wm, wo, g, b, *, alpha, tm=512):
    T, D = x2.shape
    tm = min(tm, T)
    full = lambda a: pl.BlockSpec(a.shape, lambda i: (0,) * a.ndim)
    row = lambda w_: pl.BlockSpec((tm, w_), lambda i: (i, 0))
    return pl.pallas_call(
        functools.partial(_merge_kernel, alpha=alpha),
        grid=(T // tm,),
        in_specs=[row(D), row(ya.shape[1]), row(ym.shape[1]), full(wg), full(bg), full(wa),
                  full(wm), full(wo), full(g), full(b)],
        out_specs=row(D),
        out_shape=jax.ShapeDtypeStruct((T, D), F32),
        compiler_params=pltpu.CompilerParams(
            dimension_semantics=("arbitrary",), vmem_limit_bytes=48 * 1024 * 1024),
        name="merge_branches",
    )(x2, ya, ym, wg, bg, wa, wm, wo, g, b)


N_GROUPS = 4
EXPERTS_PER_GROUP = 4
N_EXPERTS = N_GROUPS * EXPERTS_PER_GROUP


def _first_lane_of_max(vals, vmax, lane):
    return jnp.min(jnp.where(vals == vmax, lane, LANES), axis=1, keepdims=True)


def _moe_kernel(x_ref, wr_ref, br_ref, wg_ref, wu_ref, wd_ref, g_ref, b_ref, o_ref,
                comb_ref, acc_ref, *, alpha):
    e = pl.program_id(1)
    x = x_ref[...]
    tm = x.shape[0]

    @pl.when(e == 0)
    def _():
        logits = jnp.dot(x, wr_ref[...], preferred_element_type=F32,
                         precision=lax.Precision.HIGHEST) + br_ref[...]
        lane = lax.broadcasted_iota(jnp.int32, logits.shape, 1)
        g = jnp.where(lane < N_GROUPS, logits, NEG_INF)
        gmax = jnp.max(g, axis=1, keepdims=True)
        g_w = 1.0 / jnp.sum(jnp.exp(g - gmax), axis=1, keepdims=True)
        g_sel = _first_lane_of_max(g, gmax, lane)
        lo = N_GROUPS + EXPERTS_PER_GROUP * g_sel
        ev = jnp.where(jnp.logical_and(lane >= lo, lane < lo + EXPERTS_PER_GROUP), logits, NEG_INF)
        v1 = jnp.max(ev, axis=1, keepdims=True)
        i1 = _first_lane_of_max(ev, v1, lane)
        ev2 = jnp.where(lane == i1, NEG_INF, ev)
        v2 = jnp.max(ev2, axis=1, keepdims=True)
        i2 = _first_lane_of_max(ev2, v2, lane)
        r = jnp.exp(v2 - v1)
        p1 = 1.0 / (1.0 + r)
        p2 = r / (1.0 + r)
        comb_ref[...] = (jnp.where(lane == i1, g_w * p1, 0.0)
                         + jnp.where(lane == i2, g_w * p2, 0.0))
        acc_ref[...] = jnp.zeros(acc_ref.shape, F32)

    xb = x.astype(BF16)
    hg = jnp.dot(xb, wg_ref[0], preferred_element_type=F32)
    hu = jnp.dot(xb, wu_ref[0], preferred_element_type=F32)
    hdn = (_silu(hg) * hu).astype(BF16)
    lane = lax.broadcasted_iota(jnp.int32, (tm, LANES), 1)
    cw = jnp.sum(jnp.where(lane == N_GROUPS + e, comb_ref[...], 0.0), axis=1, keepdims=True)
    acc_ref[...] += cw * jnp.dot(hdn, wd_ref[0], preferred_element_type=F32)

    @pl.when(e == pl.num_programs(1) - 1)
    def _():
        o_ref[...] = _layer_norm(alpha * x + acc_ref[...], g_ref[...], b_ref[...])


def moe_layer(x2, wr, br, wg, wu, wd, g, b, *, alpha, tm=1024):
    T, D = x2.shape
    E, _, F = wg.shape
    tm = min(tm, T)
    full = lambda a: pl.BlockSpec(a.shape, lambda i, e: (0,) * a.ndim)
    return pl.pallas_call(
        functools.partial(_moe_kernel, alpha=alpha),
        grid=(T // tm, E),
        in_specs=[pl.BlockSpec((tm, D), lambda i, e: (i, 0)), full(wr), full(br),
                  pl.BlockSpec((1, D, F), lambda i, e: (e, 0, 0)),
                  pl.BlockSpec((1, D, F), lambda i, e: (e, 0, 0)),
                  pl.BlockSpec((1, F, D), lambda i, e: (e, 0, 0)),
                  full(g), full(b)],
        out_specs=pl.BlockSpec((tm, D), lambda i, e: (i, 0)),
        out_shape=jax.ShapeDtypeStruct((T, D), F32),
        scratch_shapes=[pltpu.VMEM((tm, LANES), F32), pltpu.VMEM((tm, D), F32)],
        compiler_params=pltpu.CompilerParams(
            dimension_semantics=("arbitrary", "arbitrary"), vmem_limit_bytes=48 * 1024 * 1024),
        name="moe_layer",
    )(x2, wr, br, wg, wu, wd, g, b)


DEPTH = 1
DEEPNORM_ALPHA = (2.0 * DEPTH) ** 0.25


def _pad_cols(a, width):
    return jnp.pad(a, ((0, 0), (0, width - a.shape[1])))


def kernel(x, w_in, b_in, conv_m, gn_m_gain, w_branch_attn, w_branch_mlstm, w_out, ln1_gain, ln1_bias, w_router_group, b_router_group, w_router_expert, b_router_expert, w_exp_gate, w_exp_up, w_exp_down, ln2_gain, ln2_bias):
    B, S, D = x.shape
    aw = ATT_HEADS * ATT_HEAD_DIM
    iw = IDX_HEADS * IDX_DIM
    qw = MLSTM_HEADS * MLSTM_QK_DIM
    vw = MLSTM_HEADS * MLSTM_V_DIM
    widths = (aw, aw, aw, iw, IDX_DIM, IDX_HEADS, qw, qw, vw, MLSTM_HEADS, MLSTM_HEADS, vw, D, D)
    offs = [0]
    for w_ in widths:
        offs.append(offs[-1] + w_)
    col = lambda k: w_in[:, offs[k]:offs[k + 1]]
    bia = lambda k: b_in[offs[k]:offs[k + 1]]
    (A_Q, A_K, A_V, I_Q, I_K, I_W, M_Q, M_K, M_V, M_I, M_F, M_O, G_A, G_M) = range(14)

    wr = jnp.concatenate([col(A_K), _pad_cols(col(I_K), LANES)], 1).astype(BF16)
    br = jnp.concatenate([bia(A_K), jnp.pad(bia(I_K), (0, LANES - IDX_DIM))])[None, :]
    wc = jnp.concatenate([col(A_Q), col(A_V), col(I_Q), _pad_cols(col(I_W), 8)], 1).T.astype(BF16)
    bc = jnp.concatenate([bia(A_Q), bia(A_V), bia(I_Q), jnp.pad(bia(I_W), (0, 8 - IDX_HEADS))])[:, None]
    gpad = LANES - 2 * MLSTM_HEADS
    wm = jnp.concatenate([col(M_Q), col(M_K), col(M_V), col(M_O), col(M_I),
                          _pad_cols(col(M_F), MLSTM_HEADS + gpad)], 1).astype(BF16)
    bm = jnp.concatenate([bia(M_Q), bia(M_K), bia(M_V), bia(M_O), bia(M_I),
                          jnp.pad(bia(M_F), (0, gpad))])[None, :]
    wmt = jnp.concatenate([col(M_I), col(M_F)], 1).T.astype(BF16)
    bmt = jnp.concatenate([bia(M_I), bia(M_F)])[:, None]
    wgate = jnp.concatenate([col(G_A), col(G_M)], 1).astype(BF16)
    bgate = jnp.concatenate([bia(G_A), bia(G_M)])[None, :]

    half = ATT_HEAD_DIM // 2
    inv = ROPE_THETA ** (-jnp.arange(0, ATT_HEAD_DIM, 2, dtype=F32) / ATT_HEAD_DIM)
    ang = jnp.arange(S, dtype=F32)[:, None] * inv[None, :]
    cos, sin = jnp.cos(ang), jnp.sin(ang)
    cos_r = jnp.tile(cos, (1, LANES // half))
    sin_r = jnp.tile(jnp.concatenate([-sin, sin], 1), (1, LANES // ATT_HEAD_DIM))

    qt, k, vt, qit, ki, wit = attn_projections(x, wr, br, wc, bc, cos_r, sin_r, cos.T, sin.T)
    y_attn = dsa_attention(qt, k, vt, qit, ki, wit, topk=min(IDX_TOPK_MAX, S // 4))

    mq, mkt, mv, og, ifc, ift = mlstm_projections(x, wm, bm, wmt, bmt, conv_m)
    y_mlstm = mlstm_scan(mq, mkt, mv, og, ifc, ift, gn_m_gain[None, :])

    x1 = merge_branches(x.reshape(B * S, D), y_attn.reshape(B * S, aw), y_mlstm.reshape(B * S, vw),
                        wgate, bgate, w_branch_attn.astype(BF16), w_branch_mlstm.astype(BF16),
                        w_out.astype(BF16), ln1_gain[None, :], ln1_bias[None, :],
                        alpha=DEEPNORM_ALPHA)

    w_router = _pad_cols(jnp.concatenate([w_router_group, w_router_expert], 1), LANES)
    b_router = jnp.pad(jnp.concatenate([b_router_group, b_router_expert]),
                       (0, LANES - N_GROUPS - N_EXPERTS))[None, :]
    out = moe_layer(x1, w_router, b_router, w_exp_gate.astype(BF16), w_exp_up.astype(BF16),
                    w_exp_down.astype(BF16), ln2_gain[None, :], ln2_bias[None, :],
                    alpha=DEEPNORM_ALPHA)
    return out.reshape(B, S, D)
```

```python
import functools

import jax
import jax.numpy as jnp
from jax import lax
from jax.experimental import pallas as pl
from jax.experimental.pallas import tpu as pltpu

F32 = jnp.float32
BF16 = jnp.bfloat16
NEG_INF = float("-inf")
LOG2_E = 1.4426950408889634

ATT_HEADS = 8
ATT_HEAD_DIM = 64
IDX_HEADS = 4
IDX_DIM = 64
IDX_TOPK_MAX = 256
Q_BLOCK = 128
ROPE_THETA = 10000.0

LANES = 128
BF16_ROWS = 16


def _key_to_f32(u):
    ks = u ^ jnp.int32(-2 ** 31)
    bits = ks ^ ((ks >> 31) & jnp.int32(0x7FFFFFFF))
    return lax.bitcast_convert_type(bits, F32)


def _dsa_kernel(qt_ref, k_ref, vt_ref, qit_ref, ki_ref, wit_ref, o_ref,
                sc_ref, qm_ref, m_ref, l_ref, acc_ref, *, topk, kchunk):
    qb = pl.program_id(1)
    tq = o_ref.shape[1]
    heads = qt_ref.shape[1] // ATT_HEAD_DIM
    step = 2 * LANES
    n_chunk = (qb * tq + tq + kchunk - 1) // kchunk
    qpos = lax.broadcasted_iota(jnp.int32, (1, tq), 1) + qb * tq

    qit = qit_ref[0]
    zpad = jnp.zeros((LANES - IDX_DIM, tq), BF16)
    qi_pair = []
    for p in range(IDX_HEADS // 2):
        cols = [jnp.concatenate([qit[h * IDX_DIM:(h + 1) * IDX_DIM], zpad], axis=0)
                for h in (2 * p, 2 * p + 1)]
        qi_pair.append(jnp.concatenate(cols, axis=1))
    wit = wit_ref[0]

    def score_body(c, carry):
        off = pl.multiple_of(c * kchunk, kchunk)
        ki = ki_ref[0, pl.ds(off, kchunk), :]
        tot = None
        for p in range(IDX_HEADS // 2):
            s2 = jnp.dot(ki, qi_pair[p], preferred_element_type=F32)
            for j in range(2):
                h = 2 * p + j
                s = jnp.maximum(s2[:, j * tq:(j + 1) * tq], 0.0) * wit[h:h + 1, :]
                tot = s if tot is None else tot + s
        kpos = lax.broadcasted_iota(jnp.int32, (kchunk, tq), 0) + off
        sc_ref[pl.ds(off, kchunk), :] = jnp.where(kpos <= qpos, tot + 0.0, NEG_INF)
        return carry

    lax.fori_loop(0, n_chunk, score_body, 0)

    def count(pred):
        def body(j, acc):
            off = pl.multiple_of(j * kchunk, kchunk)
            for t in range(kchunk // LANES):
                x = sc_ref[pl.ds(off + t * LANES, LANES), :]
                acc = acc + jnp.where(pred(x, off + t * LANES), 1.0, 0.0)
            return acc
        acc = lax.fori_loop(0, n_chunk, body, jnp.zeros((LANES, tq), F32))
        return jnp.sum(acc, axis=0, keepdims=True)

    def bit_body(i, carry):
        u, cgt = carry
        cand = u | (jnp.int32(1) << (31 - i))
        thr = _key_to_f32(cand)
        cnt = count(lambda x, off: x >= thr)
        ok = cnt >= float(topk)
        return jnp.where(ok, cand, u), jnp.where(ok, cgt, cnt)

    u, cgt = lax.fori_loop(0, 32, bit_body,
                           (jnp.zeros((1, tq), jnp.int32), jnp.zeros((1, tq), F32)))
    short = qpos < topk
    tau = jnp.where(short, NEG_INF, _key_to_f32(u))
    need = float(topk) - cgt

    r_i = lax.broadcasted_iota(jnp.int32, (LANES, LANES), 0)
    c_i = lax.broadcasted_iota(jnp.int32, (LANES, LANES), 1)
    tril = jnp.where(c_i <= r_i, 1.0, 0.0).astype(BF16)

    def bias_body(c, seen):
        base = pl.multiple_of(c * kchunk, kchunk)
        xs = [sc_ref[pl.ds(base + t * LANES, LANES), :] for t in range(kchunk // LANES)]
        ties = [x == tau for x in xs]
        ranks = [jnp.dot(tril, jnp.where(tie, 1.0, 0.0).astype(BF16),
                         preferred_element_type=F32) for tie in ties]
        for t, (x, tie, rank) in enumerate(zip(xs, ties, ranks)):
            keep = jnp.logical_or(x > tau, jnp.logical_and(tie, rank + seen <= need))
            kpos = lax.broadcasted_iota(jnp.int32, x.shape, 0) + (base + t * LANES)
            keep = jnp.logical_and(keep, kpos <= qpos)
            sc_ref[pl.ds(base + t * LANES, LANES), :] = jnp.where(keep, 0.0, NEG_INF)
            seen = seen + rank[LANES - 1:LANES, :]
        return seen

    lax.fori_loop(0, n_chunk, bias_body, jnp.zeros((1, tq), F32))

    qt = qt_ref[0]
    zrow = jnp.zeros((ATT_HEAD_DIM, tq), BF16)
    for p in range(heads // 2):
        a = qt[(2 * p) * ATT_HEAD_DIM:(2 * p + 1) * ATT_HEAD_DIM]
        b = qt[(2 * p + 1) * ATT_HEAD_DIM:(2 * p + 2) * ATT_HEAD_DIM]
        qm_ref[p] = jnp.concatenate([jnp.concatenate([a, zrow], axis=0),
                                     jnp.concatenate([zrow, b], axis=0)], axis=1)
    m_ref[...] = jnp.full(m_ref.shape, NEG_INF, F32)
    l_ref[...] = jnp.zeros(l_ref.shape, F32)
    acc_ref[...] = jnp.zeros(acc_ref.shape, F32)
    ones_rows = jnp.ones((BF16_ROWS, step), BF16)
    achunk = min(2 * kchunk, sc_ref.shape[0])
    n_achunk = (qb * tq + tq + achunk - 1) // achunk

    @pl.when(n_achunk * achunk > n_chunk * kchunk)
    def _():
        fill = pl.multiple_of(n_chunk * kchunk, kchunk)
        sc_ref[pl.ds(fill, kchunk), :] = jnp.full((kchunk, tq), NEG_INF, F32)

    stages = [(sub, p) for sub in range(achunk // step) for p in range(heads // 2)]

    def attn_body(c, carry):
        off = pl.multiple_of(c * achunk, achunk)

        def qk(stage):
            sub, p = stage
            kp = k_ref[0, pl.ds(off + sub * step, step), p * LANES:(p + 1) * LANES]
            return jnp.dot(kp, qm_ref[p], preferred_element_type=F32)

        ahead = 4
        pending = [qk(st) for st in stages[:ahead]]
        for i, (sub, p) in enumerate(stages):
            if i + ahead < len(stages):
                pending.append(qk(stages[i + ahead]))
            s2 = pending.pop(0)
            koff = off + sub * step
            bias = sc_ref[pl.ds(koff, step), :]
            for j in range(2):
                h = 2 * p + j
                s = s2[:, j * tq:(j + 1) * tq] + bias
                m_old = m_ref[h]
                m_new = jnp.maximum(m_old, jnp.max(s, axis=0, keepdims=True))
                m_use = jnp.where(m_new == NEG_INF, 0.0, m_new)
                pexp = jnp.exp2(s - m_use).astype(BF16)
                alpha = jnp.exp2(m_old - m_use)
                vt = jnp.concatenate(
                    [vt_ref[0, h * ATT_HEAD_DIM:(h + 1) * ATT_HEAD_DIM, pl.ds(koff, step)],
                     ones_rows], axis=0)
                pv = jnp.dot(vt, pexp, preferred_element_type=F32)
                acc_ref[h] = alpha * acc_ref[h] + pv[0:ATT_HEAD_DIM]
                l_ref[h] = alpha * l_ref[h] + pv[ATT_HEAD_DIM:ATT_HEAD_DIM + 1]
                m_ref[h] = m_new
        return carry

    lax.fori_loop(0, n_achunk, attn_body, 0)

    out_t = jnp.concatenate([acc_ref[h] / l_ref[h] for h in range(heads)], axis=0)
    o_ref[0] = jnp.transpose(out_t).astype(o_ref.dtype)


def dsa_attention(qt, k, vt, qit, ki, wit, *, topk, kchunk=512):
    B, W, S = qt.shape
    tq = Q_BLOCK
    kchunk = min(kchunk, S)
    heads = W // ATT_HEAD_DIM
    kern = functools.partial(_dsa_kernel, topk=topk, kchunk=kchunk)
    qcol = lambda a: pl.BlockSpec((1, a.shape[1], tq), lambda b, i: (b, 0, i))
    whole = lambda a: pl.BlockSpec((1,) + a.shape[1:], lambda b, i: (b, 0, 0))
    return pl.pallas_call(
        kern,
        grid=(B, S // tq),
        in_specs=[qcol(qt), whole(k), whole(vt), qcol(qit), whole(ki), qcol(wit)],
        out_specs=pl.BlockSpec((1, tq, W), lambda b, i: (b, i, 0)),
        out_shape=jax.ShapeDtypeStruct((B, S, W), BF16),
        scratch_shapes=[
            pltpu.VMEM((S, tq), F32),
            pltpu.VMEM((heads // 2, LANES, 2 * tq), BF16),
            pltpu.VMEM((heads, 1, tq), F32),
            pltpu.VMEM((heads, 1, tq), F32),
            pltpu.VMEM((heads, ATT_HEAD_DIM, tq), F32),
        ],
        compiler_params=pltpu.CompilerParams(
            dimension_semantics=("arbitrary", "arbitrary"),
            vmem_limit_bytes=56 * 1024 * 1024),
        name="dsa_attention",
    )(qt, k, vt, qit, ki, wit)


def _rope_rows(x, cos_t, sin_t):
    lane = lax.broadcasted_iota(jnp.int32, x.shape, 1)
    swapped = jnp.where((lane % ATT_HEAD_DIM) < ATT_HEAD_DIM // 2,
                        pltpu.roll(x, LANES - ATT_HEAD_DIM // 2, 1),
                        pltpu.roll(x, ATT_HEAD_DIM // 2, 1))
    return x * cos_t + swapped * sin_t


def _attn_proj_kernel(x_ref, wr_ref, br_ref, wc_ref, bc_ref, cos_ref, sin_ref, cost_ref, sint_ref,
                      qt_ref, k_ref, vt_ref, qit_ref, ki_ref, wit_ref, *, q_scale, wi_scale):
    xb = x_ref[0].astype(BF16)
    aw = k_ref.shape[2]
    iw = qit_ref.shape[1]
    half = ATT_HEAD_DIM // 2
    cos_t = cos_ref[...]
    sin_t = sin_ref[...]

    pk = jnp.dot(xb, wr_ref[...], preferred_element_type=F32) + br_ref[...]
    for j in range(aw // LANES):
        sl = slice(j * LANES, (j + 1) * LANES)
        k_ref[0, :, sl] = _rope_rows(pk[:, sl], cos_t, sin_t).astype(k_ref.dtype)
    ki_ref[0] = _rope_rows(pk[:, aw:aw + LANES], cos_t, sin_t).astype(ki_ref.dtype)

    pt = lax.dot_general(wc_ref[...], xb, (((1,), (1,)), ((), ())),
                         preferred_element_type=F32) + bc_ref[...]
    ct = cost_ref[...]
    st = sint_ref[...]

    def rope_cols(src0, dst_ref, nheads, scale):
        for h in range(nheads):
            r0 = src0 + h * ATT_HEAD_DIM
            x1 = pt[r0:r0 + half]
            x2 = pt[r0 + half:r0 + 2 * half]
            d0 = h * ATT_HEAD_DIM
            dst_ref[0, d0:d0 + half, :] = ((x1 * ct - x2 * st) * scale).astype(dst_ref.dtype)
            dst_ref[0, d0 + half:d0 + 2 * half, :] = ((x1 * st + x2 * ct) * scale).astype(dst_ref.dtype)

    rope_cols(0, qt_ref, aw // ATT_HEAD_DIM, q_scale)
    vt_ref[0] = pt[aw:2 * aw].astype(vt_ref.dtype)
    rope_cols(2 * aw, qit_ref, iw // IDX_DIM, 1.0)
    wit_ref[0] = pt[2 * aw + iw:] * wi_scale


def attn_projections(x, wr, br, wc, bc, cos_r, sin_r, cos_c, sin_c, *, tm=512):
    B, S, D = x.shape
    aw = ATT_HEADS * ATT_HEAD_DIM
    iw = IDX_HEADS * IDX_DIM
    tm = min(tm, S)
    kern = functools.partial(_attn_proj_kernel, q_scale=ATT_HEAD_DIM ** -0.5 * LOG2_E,
                             wi_scale=IDX_HEADS ** -0.5 * IDX_DIM ** -0.5)
    full = lambda a: pl.BlockSpec(a.shape, lambda b, i: (0,) * a.ndim)
    row = lambda w: pl.BlockSpec((1, tm, w), lambda b, i: (b, i, 0))
    colm = lambda r: pl.BlockSpec((1, r, tm), lambda b, i: (b, 0, i))
    return pl.pallas_call(
        kern,
        grid=(B, S // tm),
        in_specs=[row(D), full(wr), full(br), full(wc), full(bc),
                  pl.BlockSpec((tm, LANES), lambda b, i: (i, 0)),
                  pl.BlockSpec((tm, LANES), lambda b, i: (i, 0)),
                  pl.BlockSpec((ATT_HEAD_DIM // 2, tm), lambda b, i: (0, i)),
                  pl.BlockSpec((ATT_HEAD_DIM // 2, tm), lambda b, i: (0, i))],
        out_specs=[colm(aw), row(aw), colm(aw), colm(iw), row(LANES), colm(8)],
        out_shape=[jax.ShapeDtypeStruct((B, aw, S), BF16),
                   jax.ShapeDtypeStruct((B, S, aw), BF16),
                   jax.ShapeDtypeStruct((B, aw, S), BF16),
                   jax.ShapeDtypeStruct((B, iw, S), BF16),
                   jax.ShapeDtypeStruct((B, S, LANES), BF16),
                   jax.ShapeDtypeStruct((B, 8, S), F32)],
        compiler_params=pltpu.CompilerParams(
            dimension_semantics=("arbitrary", "arbitrary"),
            vmem_limit_bytes=48 * 1024 * 1024),
        name="attn_projections",
    )(x, wr, br, wc, bc, cos_r, sin_r, cos_c, sin_c)


MLSTM_HEADS = 8
MLSTM_QK_DIM = 64
MLSTM_V_DIM = 128
MLSTM_CHUNK = 128
CONV_WIDTH = 4
HALO = 8


def _silu(x):
    return x / (1.0 + jnp.exp(-x))


def _sigmoid(x):
    return 1.0 / (1.0 + jnp.exp(-x))


def _log_sigmoid(x):
    return jnp.minimum(x, 0.0) - jnp.log(1.0 + jnp.exp(-jnp.abs(x)))


def _mlstm_proj_kernel(x_ref, w_ref, b_ref, wt_ref, bt_ref, conv_ref,
                       mq_ref, mkt_ref, mv_ref, og_ref, ifc_ref, ift_ref, ext_ref, *, q_scale):
    i = pl.program_id(1)
    tm = x_ref.shape[1]
    qkw = 2 * mq_ref.shape[2]
    vw = mv_ref.shape[2]
    nh = ift_ref.shape[1] // 2
    xb = x_ref[0].astype(BF16)

    @pl.when(i == 0)
    def _():
        ext_ref[0:HALO, :] = jnp.zeros((HALO, qkw), F32)

    pqk = jnp.dot(xb, w_ref[:, 0:qkw], preferred_element_type=F32) + b_ref[:, 0:qkw]
    ext_ref[HALO:HALO + tm, :] = pqk
    acc = None
    for j in range(CONV_WIDTH):
        term = ext_ref[pl.ds(HALO - CONV_WIDTH + 1 + j, tm), :] * conv_ref[j:j + 1, :]
        acc = term if acc is None else acc + term
    ext_ref[0:HALO, :] = pqk[tm - HALO:tm, :]
    qk = _silu(acc)
    mq_ref[0] = (qk[:, 0:qkw // 2] * q_scale).astype(mq_ref.dtype)
    mkt_ref[0] = jnp.transpose(qk[:, qkw // 2:qkw]).astype(mkt_ref.dtype)

    pv = jnp.dot(xb, w_ref[:, qkw:qkw + vw], preferred_element_type=F32) + b_ref[:, qkw:qkw + vw]
    mv_ref[0] = pv.astype(mv_ref.dtype)
    po = (jnp.dot(xb, w_ref[:, qkw + vw:qkw + 2 * vw], preferred_element_type=F32)
          + b_ref[:, qkw + vw:qkw + 2 * vw])
    og_ref[0] = _sigmoid(po).astype(og_ref.dtype)

    pg = (jnp.dot(xb, w_ref[:, qkw + 2 * vw:], preferred_element_type=F32)
          + b_ref[:, qkw + 2 * vw:])
    lane = lax.broadcasted_iota(jnp.int32, pg.shape, 1)
    ifc_ref[0] = jnp.where(lane < nh, pg, _log_sigmoid(pg))
    pt = lax.dot_general(wt_ref[...], xb, (((1,), (1,)), ((), ())),
                         preferred_element_type=F32) + bt_ref[...]
    rowi = lax.broadcasted_iota(jnp.int32, pt.shape, 0)
    ift_ref[0] = jnp.where(rowi < nh, pt, _log_sigmoid(pt))


def mlstm_projections(x, w, b, wt, bt, conv, *, tm=512):
    B, S, D = x.shape
    qw = MLSTM_HEADS * MLSTM_QK_DIM
    vw = MLSTM_HEADS * MLSTM_V_DIM
    tm = min(tm, S)
    kern = functools.partial(_mlstm_proj_kernel, q_scale=MLSTM_QK_DIM ** -0.5)
    full = lambda a: pl.BlockSpec(a.shape, lambda b_, i: (0,) * a.ndim)
    row = lambda w_: pl.BlockSpec((1, tm, w_), lambda b_, i: (b_, i, 0))
    return pl.pallas_call(
        kern,
        grid=(B, S // tm),
        in_specs=[row(D), full(w), full(b), full(wt), full(bt), full(conv)],
        out_specs=[row(qw), pl.BlockSpec((1, qw, tm), lambda b_, i: (b_, 0, i)),
                   row(vw), row(vw), row(LANES),
                   pl.BlockSpec((1, 2 * MLSTM_HEADS, tm), lambda b_, i: (b_, 0, i))],
        out_shape=[jax.ShapeDtypeStruct((B, S, qw), BF16),
                   jax.ShapeDtypeStruct((B, qw, S), BF16),
                   jax.ShapeDtypeStruct((B, S, vw), BF16),
                   jax.ShapeDtypeStruct((B, S, vw), BF16),
                   jax.ShapeDtypeStruct((B, S, LANES), F32),
                   jax.ShapeDtypeStruct((B, 2 * MLSTM_HEADS, S), F32)],
        scratch_shapes=[pltpu.VMEM((HALO + tm, 2 * qw), F32)],
        compiler_params=pltpu.CompilerParams(
            dimension_semantics=("arbitrary", "arbitrary"),
            vmem_limit_bytes=56 * 1024 * 1024),
        name="mlstm_projections",
    )(x, w, b, wt, bt, conv)


GN_EPS = 1e-6


def _mlstm_kernel(mq_ref, mkt_ref, mv_ref, og_ref, ifc_ref, ift_ref, gain_ref, y_ref,
                  c_ref, m_ref):
    c = pl.program_id(1)
    L = mq_ref.shape[1]
    nh = ift_ref.shape[1] // 2
    dk = mq_ref.shape[2] // nh
    dv = mv_ref.shape[2] // nh

    @pl.when(c == 0)
    def _():
        c_ref[...] = jnp.zeros(c_ref.shape, F32)
        m_ref[...] = jnp.zeros(m_ref.shape, F32)

    r_i = lax.broadcasted_iota(jnp.int32, (L, L), 0)
    c_i = lax.broadcasted_iota(jnp.int32, (L, L), 1)
    causal = c_i <= r_i
    tril = jnp.where(causal, 1.0, 0.0)
    triu = jnp.where(r_i <= c_i, 1.0, 0.0)
    ifc = ifc_ref[0]
    ift = ift_ref[0]
    b_cols = jnp.dot(tril, ifc, preferred_element_type=F32, precision=lax.Precision.HIGHEST)
    b_rows = jnp.dot(ift, triu, preferred_element_type=F32, precision=lax.Precision.HIGHEST)
    ones_col = jnp.where(lax.broadcasted_iota(jnp.int32, (L, dv), 1) == 0, 1.0, 0.0).astype(BF16)

    for h in range(nh):
        m = m_ref[h]
        b_col = b_cols[:, nh + h:nh + h + 1]
        a_row = ift[h:h + 1, :] - b_rows[nh + h:nh + h + 1, :]
        amat = jnp.where(causal, a_row, NEG_INF)
        big_m = jnp.maximum(m, jnp.max(amat, axis=1, keepdims=True))
        decay = jnp.exp(amat - big_m)
        w_inter = jnp.exp(m - big_m)
        q = mq_ref[0, :, h * dk:(h + 1) * dk]
        kt = mkt_ref[0, h * dk:(h + 1) * dk, :]
        v_ext = jnp.concatenate([mv_ref[0, :, h * dv:(h + 1) * dv], ones_col], axis=1)
        s = jnp.dot(q, kt, preferred_element_type=F32) * decay
        sv = jnp.dot(s.astype(BF16), v_ext, preferred_element_type=F32)
        c_old = c_ref[h]
        qc = jnp.dot(q, c_old.astype(BF16), preferred_element_type=F32)
        num = sv[:, 0:dv] + w_inter * qc[:, 0:dv]
        den = sv[:, dv:dv + 1] + w_inter * qc[:, dv:dv + 1]
        hh = num / jnp.maximum(jnp.abs(den), jnp.exp(-(b_col + big_m)))
        mu = jnp.mean(hh, axis=1, keepdims=True)
        xc = hh - mu
        var = jnp.mean(xc * xc, axis=1, keepdims=True)
        hn = xc * lax.rsqrt(var + GN_EPS) * gain_ref[:, h * dv:(h + 1) * dv]
        y_ref[0, :, h * dv:(h + 1) * dv] = (
            og_ref[0, :, h * dv:(h + 1) * dv].astype(F32) * hn).astype(y_ref.dtype)

        b_last = b_col[L - 1:L, :]
        g_row = b_last + a_row
        m_new = jnp.maximum(b_last + m, jnp.max(g_row, axis=1, keepdims=True))
        carry = jnp.exp(b_last + m - m_new)
        wkt = (kt.astype(F32) * jnp.exp(g_row - m_new)).astype(BF16)
        c_ref[h] = carry * c_old + jnp.dot(wkt, v_ext, preferred_element_type=F32)
        m_ref[h] = m_new


def mlstm_scan(mq, mkt, mv, og, ifc, ift, gain):
    B, S, qw = mq.shape
    vw = mv.shape[2]
    L = min(MLSTM_CHUNK, S)
    nh = ift.shape[1] // 2
    row = lambda w_: pl.BlockSpec((1, L, w_), lambda b_, c: (b_, c, 0))
    return pl.pallas_call(
        _mlstm_kernel,
        grid=(B, S // L),
        in_specs=[row(qw), pl.BlockSpec((1, qw, L), lambda b_, c: (b_, 0, c)),
                  row(vw), row(vw), row(LANES),
                  pl.BlockSpec((1, 2 * nh, L), lambda b_, c: (b_, 0, c)),
                  pl.BlockSpec(gain.shape, lambda b_, c: (0, 0))],
        out_specs=row(vw),
        out_shape=jax.ShapeDtypeStruct((B, S, vw), BF16),
        scratch_shapes=[pltpu.VMEM((nh, qw // nh, 2 * vw // nh), F32),
                        pltpu.VMEM((nh, 1, 1), F32)],
        compiler_params=pltpu.CompilerParams(
            dimension_semantics=("arbitrary", "arbitrary")),
        name="mlstm_scan",
    )(mq, mkt, mv, og, ifc, ift, gain)


LN_EPS = 1e-5


def _layer_norm(z, gain, bias):
    mu = jnp.mean(z, axis=1, keepdims=True)
    zc = z - mu
    var = jnp.mean(zc * zc, axis=1, keepdims=True)
    return zc * lax.rsqrt(var + LN_EPS) * gain + bias


def _merge_kernel(x_ref, ya_ref, ym_ref, wg_ref, bg_ref, wa_ref, wm_ref, wo_ref, g_ref, b_ref,
                  o_ref, *, alpha):
    x = x_ref[...]
    xb = x.astype(BF16)
    d = x.shape[1]
    ga = _sigmoid(jnp.dot(xb, wg_ref[:, 0:d], preferred_element_type=F32) + bg_ref[:, 0:d])
    merged = ga * jnp.dot(ya_ref[...], wa_ref[...], preferred_element_type=F32)
    gm = _sigmoid(jnp.dot(xb, wg_ref[:, d:2 * d], preferred_element_type=F32) + bg_ref[:, d:2 * d])
    merged = merged + gm * jnp.dot(ym_ref[...], wm_ref[...], preferred_element_type=F32)
    z = alpha * x + jnp.dot(merged.astype(BF16), wo_ref[...], preferred_element_type=F32)
    o_ref[...] = _layer_norm(z, g_ref[...], b_ref[...])


def merge_branches(x2, ya, ym, wg, bg, wa, wm, wo, g, b, *, alpha, tm=512):
    T, D = x2.shape
    tm = min(tm, T)
    full = lambda a: pl.BlockSpec(a.shape, lambda i: (0,) * a.ndim)
    row = lambda w_: pl.BlockSpec((tm, w_), lambda i: (i, 0))
    return pl.pallas_call(
        functools.partial(_merge_kernel, alpha=alpha),
        grid=(T // tm,),
        in_specs=[row(D), row(ya.shape[1]), row(ym.shape[1]), full(wg), full(bg), full(wa),
                  full(wm), full(wo), full(g), full(b)],
        out_specs=row(D),
        out_shape=jax.ShapeDtypeStruct((T, D), F32),
        compiler_params=pltpu.CompilerParams(
            dimension_semantics=("arbitrary",), vmem_limit_bytes=48 * 1024 * 1024),
        name="merge_branches",
    )(x2, ya, ym, wg, bg, wa, wm, wo, g, b)


N_GROUPS = 4
EXPERTS_PER_GROUP = 4
N_EXPERTS = N_GROUPS * EXPERTS_PER_GROUP


def _first_lane_of_max(vals, vmax, lane):
    return jnp.min(jnp.where(vals == vmax, lane, LANES), axis=1, keepdims=True)


MOE_SUB = 128


def _moe_kernel(x_ref, wr_ref, br_ref, wg_ref, wu_ref, wd_ref, g_ref, b_ref, o_ref,
                tril_ref, slot_ref, xg_ref, cw_ref, y_ref, tiles_ref, *, alpha):
    w = pl.program_id(0)
    e = pl.program_id(1)
    tm = x_ref.shape[0]
    slots = xg_ref.shape[0]
    lane = lax.broadcasted_iota(jnp.int32, (tm, LANES), 1)

    @pl.when(jnp.logical_and(w == 0, e == 0))
    def _():
        r_i = lax.broadcasted_iota(jnp.int32, (tm, tm), 0)
        c_i = lax.broadcasted_iota(jnp.int32, (tm, tm), 1)
        tril_ref[...] = jnp.where(c_i < r_i, 1.0, 0.0).astype(BF16)

    @pl.when(e == 0)
    def _():
        x = x_ref[...]
        logits = jnp.dot(x, wr_ref[...], preferred_element_type=F32,
                         precision=lax.Precision.HIGHEST) + br_ref[...]
        g = jnp.where(lane < N_GROUPS, logits, NEG_INF)
        gmax = jnp.max(g, axis=1, keepdims=True)
        g_w = 1.0 / jnp.sum(jnp.exp(g - gmax), axis=1, keepdims=True)
        g_sel = _first_lane_of_max(g, gmax, lane)
        lo = N_GROUPS + EXPERTS_PER_GROUP * g_sel
        ev = jnp.where(jnp.logical_and(lane >= lo, lane < lo + EXPERTS_PER_GROUP), logits, NEG_INF)
        v1 = jnp.max(ev, axis=1, keepdims=True)
        i1 = _first_lane_of_max(ev, v1, lane)
        ev2 = jnp.where(lane == i1, NEG_INF, ev)
        v2 = jnp.max(ev2, axis=1, keepdims=True)
        i2 = _first_lane_of_max(ev2, v2, lane)
        r = jnp.exp(v2 - v1)
        p1 = 1.0 / (1.0 + r)
        p2 = r / (1.0 + r)
        comb = jnp.where(lane == i1, g_w * p1, 0.0) + jnp.where(lane == i2, g_w * p2, 0.0)
        comb_hi = comb.astype(BF16)
        comb_lo = (comb - comb_hi.astype(F32)).astype(BF16)

        onehot = jnp.where(lane == g_sel, 1.0, 0.0)
        before = jnp.dot(tril_ref[...], onehot.astype(BF16), preferred_element_type=F32)
        rank = jnp.sum(jnp.where(lane == g_sel, before, 0.0), axis=1, keepdims=True)
        total = jnp.sum(onehot, axis=0, keepdims=True)
        lane1 = lax.broadcasted_iota(jnp.int32, (1, LANES), 1)
        start = jnp.zeros((tm, 1), F32)
        first_tile = jnp.int32(0)
        for grp in range(N_GROUPS):
            n_tok = jnp.sum(jnp.where(lane1 == grp, total, 0.0)).astype(jnp.int32)
            n_tile = (n_tok + MOE_SUB - 1) // MOE_SUB
            tiles_ref[grp] = first_tile
            tiles_ref[N_GROUPS + grp] = n_tile
            start = jnp.where(g_sel == grp, (first_tile * MOE_SUB).astype(F32), start)
            first_tile = first_tile + n_tile
        slot = start + rank
        slot_ref[...] = slot
        slot_row = jnp.transpose(jnp.broadcast_to(slot, (tm, LANES)))[0:1]
        pick = slot_row == lax.broadcasted_iota(jnp.int32, (slots, tm), 0).astype(F32)
        pmat = jnp.where(pick, 1.0, 0.0).astype(BF16)
        xg_ref[...] = jnp.dot(pmat, x.astype(BF16), preferred_element_type=F32).astype(BF16)
        cw_ref[...] = (jnp.dot(pmat, comb_hi, preferred_element_type=F32)
                       + jnp.dot(pmat, comb_lo, preferred_element_type=F32))
        y_ref[...] = jnp.zeros(y_ref.shape, F32)

    grp = e // EXPERTS_PER_GROUP
    first = tiles_ref[grp]

    def expert(j, carry):
        base = pl.multiple_of((first + j) * MOE_SUB, MOE_SUB)
        xg = xg_ref[pl.ds(base, MOE_SUB), :]
        hg = jnp.dot(xg, wg_ref[0], preferred_element_type=F32)
        hu = jnp.dot(xg, wu_ref[0], preferred_element_type=F32)
        hdn = (_silu(hg) * hu).astype(BF16)
        lane_s = lax.broadcasted_iota(jnp.int32, (MOE_SUB, LANES), 1)
        cw = jnp.sum(jnp.where(lane_s == N_GROUPS + e, cw_ref[pl.ds(base, MOE_SUB), :], 0.0),
                     axis=1, keepdims=True)
        y_ref[pl.ds(base, MOE_SUB), :] += cw * jnp.dot(hdn, wd_ref[0], preferred_element_type=F32)
        return carry
    lax.fori_loop(0, tiles_ref[N_GROUPS + grp], expert, 0)

    @pl.when(e == pl.num_programs(1) - 1)
    def _():
        pick = slot_ref[...] == lax.broadcasted_iota(jnp.int32, (tm, slots), 1).astype(F32)
        pmat_t = jnp.where(pick, 1.0, 0.0).astype(BF16)
        moe = jnp.dot(pmat_t, y_ref[...].astype(BF16), preferred_element_type=F32)
        o_ref[...] = _layer_norm(alpha * x_ref[...] + moe, g_ref[...], b_ref[...])


def moe_layer(x2, wr, br, wg, wu, wd, g, b, *, alpha, tm=1024):
    T, D = x2.shape
    E, _, F = wg.shape
    tm = min(tm, T)
    slots = tm + N_GROUPS * MOE_SUB
    full = lambda a: pl.BlockSpec(a.shape, lambda i, e: (0,) * a.ndim)
    return pl.pallas_call(
        functools.partial(_moe_kernel, alpha=alpha),
        grid=(T // tm, E),
        in_specs=[pl.BlockSpec((tm, D), lambda i, e: (i, 0)), full(wr), full(br),
                  pl.BlockSpec((1, D, F), lambda i, e: (e, 0, 0)),
                  pl.BlockSpec((1, D, F), lambda i, e: (e, 0, 0)),
                  pl.BlockSpec((1, F, D), lambda i, e: (e, 0, 0)),
                  full(g), full(b)],
        out_specs=pl.BlockSpec((tm, D), lambda i, e: (i, 0)),
        out_shape=jax.ShapeDtypeStruct((T, D), F32),
        scratch_shapes=[pltpu.VMEM((tm, tm), BF16),
                        pltpu.VMEM((tm, 1), F32),
                        pltpu.VMEM((slots, D), BF16),
                        pltpu.VMEM((slots, LANES), F32),
                        pltpu.VMEM((slots, D), F32),
                        pltpu.SMEM((2 * N_GROUPS,), jnp.int32)],
        compiler_params=pltpu.CompilerParams(
            dimension_semantics=("arbitrary", "arbitrary"), vmem_limit_bytes=56 * 1024 * 1024),
        name="moe_layer",
    )(x2, wr, br, wg, wu, wd, g, b)


DEPTH = 1
DEEPNORM_ALPHA = (2.0 * DEPTH) ** 0.25


def _pad_cols(a, width):
    return jnp.pad(a, ((0, 0), (0, width - a.shape[1])))


def kernel(x, w_in, b_in, conv_m, gn_m_gain, w_branch_attn, w_branch_mlstm, w_out, ln1_gain, ln1_bias, w_router_group, b_router_group, w_router_expert, b_router_expert, w_exp_gate, w_exp_up, w_exp_down, ln2_gain, ln2_bias):
    B, S, D = x.shape
    aw = ATT_HEADS * ATT_HEAD_DIM
    iw = IDX_HEADS * IDX_DIM
    qw = MLSTM_HEADS * MLSTM_QK_DIM
    vw = MLSTM_HEADS * MLSTM_V_DIM
    widths = (aw, aw, aw, iw, IDX_DIM, IDX_HEADS, qw, qw, vw, MLSTM_HEADS, MLSTM_HEADS, vw, D, D)
    offs = [0]
    for w_ in widths:
        offs.append(offs[-1] + w_)
    col = lambda k: w_in[:, offs[k]:offs[k + 1]]
    bia = lambda k: b_in[offs[k]:offs[k + 1]]
    (A_Q, A_K, A_V, I_Q, I_K, I_W, M_Q, M_K, M_V, M_I, M_F, M_O, G_A, G_M) = range(14)

    wr = jnp.concatenate([col(A_K), _pad_cols(col(I_K), LANES)], 1).astype(BF16)
    br = jnp.concatenate([bia(A_K), jnp.pad(bia(I_K), (0, LANES - IDX_DIM))])[None, :]
    wc = jnp.concatenate([col(A_Q), col(A_V), col(I_Q), _pad_cols(col(I_W), 8)], 1).T.astype(BF16)
    bc = jnp.concatenate([bia(A_Q), bia(A_V), bia(I_Q), jnp.pad(bia(I_W), (0, 8 - IDX_HEADS))])[:, None]
    gpad = LANES - 2 * MLSTM_HEADS
    wm = jnp.concatenate([col(M_Q), col(M_K), col(M_V), col(M_O), col(M_I),
                          _pad_cols(col(M_F), MLSTM_HEADS + gpad)], 1).astype(BF16)
    bm = jnp.concatenate([bia(M_Q), bia(M_K), bia(M_V), bia(M_O), bia(M_I),
                          jnp.pad(bia(M_F), (0, gpad))])[None, :]
    wmt = jnp.concatenate([col(M_I), col(M_F)], 1).T.astype(BF16)
    bmt = jnp.concatenate([bia(M_I), bia(M_F)])[:, None]
    wgate = jnp.concatenate([col(G_A), col(G_M)], 1).astype(BF16)
    bgate = jnp.concatenate([bia(G_A), bia(G_M)])[None, :]

    half = ATT_HEAD_DIM // 2
    inv = ROPE_THETA ** (-jnp.arange(0, ATT_HEAD_DIM, 2, dtype=F32) / ATT_HEAD_DIM)
    ang = jnp.arange(S, dtype=F32)[:, None] * inv[None, :]
    cos, sin = jnp.cos(ang), jnp.sin(ang)
    cos_r = jnp.tile(cos, (1, LANES // half))
    sin_r = jnp.tile(jnp.concatenate([-sin, sin], 1), (1, LANES // ATT_HEAD_DIM))

    qt, k, vt, qit, ki, wit = attn_projections(x, wr, br, wc, bc, cos_r, sin_r, cos.T, sin.T)
    y_attn = dsa_attention(qt, k, vt, qit, ki, wit, topk=min(IDX_TOPK_MAX, S // 4))

    mq, mkt, mv, og, ifc, ift = mlstm_projections(x, wm, bm, wmt, bmt, conv_m)
    y_mlstm = mlstm_scan(mq, mkt, mv, og, ifc, ift, gn_m_gain[None, :])

    x1 = merge_branches(x.reshape(B * S, D), y_attn.reshape(B * S, aw), y_mlstm.reshape(B * S, vw),
                        wgate, bgate, w_branch_attn.astype(BF16), w_branch_mlstm.astype(BF16),
                        w_out.astype(BF16), ln1_gain[None, :], ln1_bias[None, :],
                        alpha=DEEPNORM_ALPHA)

    w_router = _pad_cols(jnp.concatenate([w_router_group, w_router_expert], 1), LANES)
    b_router = jnp.pad(jnp.concatenate([b_router_group, b_router_expert]),
                       (0, LANES - N_GROUPS - N_EXPERTS))[None, :]
    out = moe_layer(x1, w_router, b_router, w_exp_gate.astype(BF16), w_exp_up.astype(BF16),
                    w_exp_down.astype(BF16), ln2_gain[None, :], ln2_bias[None, :],
                    alpha=DEEPNORM_ALPHA)
    return out.reshape(B, S, D)
```

```python
import functools

import jax
import jax.numpy as jnp
from jax import lax
from jax.experimental import pallas as pl
from jax.experimental.pallas import tpu as pltpu

F32 = jnp.float32
BF16 = jnp.bfloat16
NEG_INF = float("-inf")
LOG2_E = 1.4426950408889634

ATT_HEADS = 8
ATT_HEAD_DIM = 64
IDX_HEADS = 4
IDX_DIM = 64
IDX_TOPK_MAX = 256
Q_BLOCK = 128
ROPE_THETA = 10000.0

LANES = 128
BF16_ROWS = 16


def _key_to_f32(u):
    ks = u ^ jnp.int32(-2 ** 31)
    bits = ks ^ ((ks >> 31) & jnp.int32(0x7FFFFFFF))
    return lax.bitcast_convert_type(bits, F32)


def _dsa_kernel(qt_ref, k_ref, vt_ref, qit_ref, ki_ref, wit_ref, o_ref,
                sc_ref, hi_ref, lo_ref, qm_ref, m_ref, l_ref, acc_ref, *, topk, kchunk):
    qb = pl.program_id(1)
    tq = o_ref.shape[1]
    heads = qt_ref.shape[1] // ATT_HEAD_DIM
    step = 2 * LANES
    n_chunk = (qb * tq + tq + kchunk - 1) // kchunk
    qpos = lax.broadcasted_iota(jnp.int32, (1, tq), 1) + qb * tq

    qit = qit_ref[0]
    zpad = jnp.zeros((LANES - IDX_DIM, tq), BF16)
    qi_pair = []
    for p in range(IDX_HEADS // 2):
        cols = [jnp.concatenate([qit[h * IDX_DIM:(h + 1) * IDX_DIM], zpad], axis=0)
                for h in (2 * p, 2 * p + 1)]
        qi_pair.append(jnp.concatenate(cols, axis=1))
    wit = wit_ref[0]

    def score_body(c, carry):
        off = pl.multiple_of(c * kchunk, kchunk)
        ki = ki_ref[0, pl.ds(off, kchunk), :]
        tot = None
        for p in range(IDX_HEADS // 2):
            s2 = jnp.dot(ki, qi_pair[p], preferred_element_type=F32)
            for j in range(2):
                h = 2 * p + j
                s = jnp.maximum(s2[:, j * tq:(j + 1) * tq], 0.0) * wit[h:h + 1, :]
                tot = s if tot is None else tot + s
        kpos = lax.broadcasted_iota(jnp.int32, (kchunk, tq), 0) + off
        tot = jnp.where(kpos <= qpos, tot + 0.0, NEG_INF)
        sc_ref[pl.ds(off, kchunk), :] = tot
        bits = lax.bitcast_convert_type(tot, jnp.int32)
        hi_ref[pl.ds(off, kchunk), :] = lax.bitcast_convert_type(
            bits & jnp.int32(-65536), F32).astype(BF16)
        skey = bits ^ ((bits >> 31) & jnp.int32(0x7FFFFFFF))
        lo_ref[pl.ds(off, kchunk), :] = ((skey & jnp.int32(0xFFFF)) - 32768).astype(jnp.int16)
        return carry

    lax.fori_loop(0, n_chunk, score_body, 0)

    def count16(ref, pred, one, zero):
        def body(j, acc):
            off = pl.multiple_of(j * kchunk, kchunk)
            for t in range(kchunk // (2 * LANES)):
                x = ref[pl.ds(off + t * 2 * LANES, 2 * LANES), :]
                acc = acc + jnp.where(pred(x), one, zero)
            return acc
        acc = lax.fori_loop(0, n_chunk, body, jnp.zeros((2 * LANES, tq), one.dtype))
        return jnp.sum(acc.astype(F32), axis=0, keepdims=True)

    def top16_bf16(key):
        fbits = lax.bitcast_convert_type(_key_to_f32(key), jnp.int32)
        return lax.bitcast_convert_type(fbits & jnp.int32(-65536), F32).astype(BF16)

    one_b, zero_b = jnp.ones((), BF16), jnp.zeros((), BF16)
    one_i, zero_i = jnp.ones((), jnp.int16), jnp.zeros((), jnp.int16)

    def hi_body(i, carry):
        u, cgt = carry
        cand = u | (jnp.int32(1) << (31 - i))
        thr = top16_bf16(cand)
        cnt = count16(hi_ref, lambda x: x >= thr, one_b, zero_b)
        ok = cnt >= float(topk)
        return jnp.where(ok, cand, u), jnp.where(ok, cgt, cnt)

    u, cgt = lax.fori_loop(0, 16, hi_body,
                           (jnp.zeros((1, tq), jnp.int32), jnp.zeros((1, tq), F32)))

    bucket = top16_bf16(u)
    lowest = jnp.full((), -32768, jnp.int16)

    def prep_body(j, acc):
        off = pl.multiple_of(j * kchunk, kchunk)
        for t in range(kchunk // (2 * LANES)):
            sl = pl.ds(off + t * 2 * LANES, 2 * LANES)
            h = hi_ref[sl, :]
            lo_ref[sl, :] = jnp.where(h == bucket, lo_ref[sl, :], lowest)
            acc = acc + jnp.where(h > bucket, one_b, zero_b)
        return acc

    above = lax.fori_loop(0, n_chunk, prep_body, jnp.zeros((2 * LANES, tq), BF16))
    above = jnp.sum(above.astype(F32), axis=0, keepdims=True)

    def lo_body(i, carry):
        u, cgt = carry
        cand = u | (jnp.int32(1) << (15 - i))
        thr = ((cand & jnp.int32(0xFFFF)) - 32768).astype(jnp.int16)
        cnt = above + count16(lo_ref, lambda x: x >= thr, one_i, zero_i)
        ok = cnt >= float(topk)
        return jnp.where(ok, cand, u), jnp.where(ok, cgt, cnt)

    u, cgt = lax.fori_loop(0, 16, lo_body, (u, cgt))
    short = qpos < topk
    tau = jnp.where(short, NEG_INF, _key_to_f32(u))
    need = float(topk) - cgt

    r_i = lax.broadcasted_iota(jnp.int32, (LANES, LANES), 0)
    c_i = lax.broadcasted_iota(jnp.int32, (LANES, LANES), 1)
    tril = jnp.where(c_i <= r_i, 1.0, 0.0).astype(BF16)

    def bias_body(c, seen):
        base = pl.multiple_of(c * kchunk, kchunk)
        xs = [sc_ref[pl.ds(base + t * LANES, LANES), :] for t in range(kchunk // LANES)]
        ties = [x == tau for x in xs]
        ranks = [jnp.dot(tril, jnp.where(tie, 1.0, 0.0).astype(BF16),
                         preferred_element_type=F32) for tie in ties]
        for t, (x, tie, rank) in enumerate(zip(xs, ties, ranks)):
            keep = jnp.logical_or(x > tau, jnp.logical_and(tie, rank + seen <= need))
            kpos = lax.broadcasted_iota(jnp.int32, x.shape, 0) + (base + t * LANES)
            keep = jnp.logical_and(keep, kpos <= qpos)
            sc_ref[pl.ds(base + t * LANES, LANES), :] = jnp.where(keep, 0.0, NEG_INF)
            seen = seen + rank[LANES - 1:LANES, :]
        return seen

    lax.fori_loop(0, n_chunk, bias_body, jnp.zeros((1, tq), F32))

    qt = qt_ref[0]
    zrow = jnp.zeros((ATT_HEAD_DIM, tq), BF16)
    for p in range(heads // 2):
        a = qt[(2 * p) * ATT_HEAD_DIM:(2 * p + 1) * ATT_HEAD_DIM]
        b = qt[(2 * p + 1) * ATT_HEAD_DIM:(2 * p + 2) * ATT_HEAD_DIM]
        qm_ref[p] = jnp.concatenate([jnp.concatenate([a, zrow], axis=0),
                                     jnp.concatenate([zrow, b], axis=0)], axis=1)
    m_ref[...] = jnp.full(m_ref.shape, NEG_INF, F32)
    l_ref[...] = jnp.zeros(l_ref.shape, F32)
    acc_ref[...] = jnp.zeros(acc_ref.shape, F32)
    ones_rows = jnp.ones((BF16_ROWS, step), BF16)
    achunk = min(2 * kchunk, sc_ref.shape[0])
    n_achunk = (qb * tq + tq + achunk - 1) // achunk

    @pl.when(n_achunk * achunk > n_chunk * kchunk)
    def _():
        fill = pl.multiple_of(n_chunk * kchunk, kchunk)
        sc_ref[pl.ds(fill, kchunk), :] = jnp.full((kchunk, tq), NEG_INF, F32)

    stages = [(sub, p) for sub in range(achunk // step) for p in range(heads // 2)]

    def attn_body(c, carry):
        off = pl.multiple_of(c * achunk, achunk)

        def qk(stage):
            sub, p = stage
            kp = k_ref[0, pl.ds(off + sub * step, step), p * LANES:(p + 1) * LANES]
            return jnp.dot(kp, qm_ref[p], preferred_element_type=F32)

        ahead = 4
        pending = [qk(st) for st in stages[:ahead]]
        for i, (sub, p) in enumerate(stages):
            if i + ahead < len(stages):
                pending.append(qk(stages[i + ahead]))
            s2 = pending.pop(0)
            koff = off + sub * step
            bias = sc_ref[pl.ds(koff, step), :]
            for j in range(2):
                h = 2 * p + j
                s = s2[:, j * tq:(j + 1) * tq] + bias
                m_old = m_ref[h]
                m_new = jnp.maximum(m_old, jnp.max(s, axis=0, keepdims=True))
                m_use = jnp.where(m_new == NEG_INF, 0.0, m_new)
                pexp = jnp.exp2(s - m_use).astype(BF16)
                alpha = jnp.exp2(m_old - m_use)
                vt = jnp.concatenate(
                    [vt_ref[0, h * ATT_HEAD_DIM:(h + 1) * ATT_HEAD_DIM, pl.ds(koff, step)],
                     ones_rows], axis=0)
                pv = jnp.dot(vt, pexp, preferred_element_type=F32)
                acc_ref[h] = alpha * acc_ref[h] + pv[0:ATT_HEAD_DIM]
                l_ref[h] = alpha * l_ref[h] + pv[ATT_HEAD_DIM:ATT_HEAD_DIM + 1]
                m_ref[h] = m_new
        return carry

    lax.fori_loop(0, n_achunk, attn_body, 0)

    out_t = jnp.concatenate([acc_ref[h] / l_ref[h] for h in range(heads)], axis=0)
    o_ref[0] = jnp.transpose(out_t).astype(o_ref.dtype)


def dsa_attention(qt, k, vt, qit, ki, wit, *, topk, kchunk=512):
    B, W, S = qt.shape
    tq = Q_BLOCK
    kchunk = min(kchunk, S)
    heads = W // ATT_HEAD_DIM
    kern = functools.partial(_dsa_kernel, topk=topk, kchunk=kchunk)
    qcol = lambda a: pl.BlockSpec((1, a.shape[1], tq), lambda b, i: (b, 0, i))
    whole = lambda a: pl.BlockSpec((1,) + a.shape[1:], lambda b, i: (b, 0, 0))
    return pl.pallas_call(
        kern,
        grid=(B, S // tq),
        in_specs=[qcol(qt), whole(k), whole(vt), qcol(qit), whole(ki), qcol(wit)],
        out_specs=pl.BlockSpec((1, tq, W), lambda b, i: (b, i, 0)),
        out_shape=jax.ShapeDtypeStruct((B, S, W), BF16),
        scratch_shapes=[
            pltpu.VMEM((S, tq), F32),
            pltpu.VMEM((S, tq), BF16),
            pltpu.VMEM((S, tq), jnp.int16),
            pltpu.VMEM((heads // 2, LANES, 2 * tq), BF16),
            pltpu.VMEM((heads, 1, tq), F32),
            pltpu.VMEM((heads, 1, tq), F32),
            pltpu.VMEM((heads, ATT_HEAD_DIM, tq), F32),
        ],
        compiler_params=pltpu.CompilerParams(
            dimension_semantics=("arbitrary", "arbitrary"),
            vmem_limit_bytes=56 * 1024 * 1024),
        name="dsa_attention",
    )(qt, k, vt, qit, ki, wit)


def _rope_rows(x, cos_t, sin_t):
    lane = lax.broadcasted_iota(jnp.int32, x.shape, 1)
    swapped = jnp.where((lane % ATT_HEAD_DIM) < ATT_HEAD_DIM // 2,
                        pltpu.roll(x, LANES - ATT_HEAD_DIM // 2, 1),
                        pltpu.roll(x, ATT_HEAD_DIM // 2, 1))
    return x * cos_t + swapped * sin_t


def _attn_proj_kernel(x_ref, wr_ref, br_ref, wc_ref, bc_ref, cos_ref, sin_ref, cost_ref, sint_ref,
                      qt_ref, k_ref, vt_ref, qit_ref, ki_ref, wit_ref, *, q_scale, wi_scale):
    xb = x_ref[0].astype(BF16)
    aw = k_ref.shape[2]
    iw = qit_ref.shape[1]
    half = ATT_HEAD_DIM // 2
    cos_t = cos_ref[...]
    sin_t = sin_ref[...]

    pk = jnp.dot(xb, wr_ref[...], preferred_element_type=F32) + br_ref[...]
    for j in range(aw // LANES):
        sl = slice(j * LANES, (j + 1) * LANES)
        k_ref[0, :, sl] = _rope_rows(pk[:, sl], cos_t, sin_t).astype(k_ref.dtype)
    ki_ref[0] = _rope_rows(pk[:, aw:aw + LANES], cos_t, sin_t).astype(ki_ref.dtype)

    pt = lax.dot_general(wc_ref[...], xb, (((1,), (1,)), ((), ())),
                         preferred_element_type=F32) + bc_ref[...]
    ct = cost_ref[...]
    st = sint_ref[...]

    def rope_cols(src0, dst_ref, nheads, scale):
        for h in range(nheads):
            r0 = src0 + h * ATT_HEAD_DIM
            x1 = pt[r0:r0 + half]
            x2 = pt[r0 + half:r0 + 2 * half]
            d0 = h * ATT_HEAD_DIM
            dst_ref[0, d0:d0 + half, :] = ((x1 * ct - x2 * st) * scale).astype(dst_ref.dtype)
            dst_ref[0, d0 + half:d0 + 2 * half, :] = ((x1 * st + x2 * ct) * scale).astype(dst_ref.dtype)

    rope_cols(0, qt_ref, aw // ATT_HEAD_DIM, q_scale)
    vt_ref[0] = pt[aw:2 * aw].astype(vt_ref.dtype)
    rope_cols(2 * aw, qit_ref, iw // IDX_DIM, 1.0)
    wit_ref[0] = pt[2 * aw + iw:] * wi_scale


def attn_projections(x, wr, br, wc, bc, cos_r, sin_r, cos_c, sin_c, *, tm=512):
    B, S, D = x.shape
    aw = ATT_HEADS * ATT_HEAD_DIM
    iw = IDX_HEADS * IDX_DIM
    tm = min(tm, S)
    kern = functools.partial(_attn_proj_kernel, q_scale=ATT_HEAD_DIM ** -0.5 * LOG2_E,
                             wi_scale=IDX_HEADS ** -0.5 * IDX_DIM ** -0.5)
    full = lambda a: pl.BlockSpec(a.shape, lambda b, i: (0,) * a.ndim)
    row = lambda w: pl.BlockSpec((1, tm, w), lambda b, i: (b, i, 0))
    colm = lambda r: pl.BlockSpec((1, r, tm), lambda b, i: (b, 0, i))
    return pl.pallas_call(
        kern,
        grid=(B, S // tm),
        in_specs=[row(D), full(wr), full(br), full(wc), full(bc),
                  pl.BlockSpec((tm, LANES), lambda b, i: (i, 0)),
                  pl.BlockSpec((tm, LANES), lambda b, i: (i, 0)),
                  pl.BlockSpec((ATT_HEAD_DIM // 2, tm), lambda b, i: (0, i)),
                  pl.BlockSpec((ATT_HEAD_DIM // 2, tm), lambda b, i: (0, i))],
        out_specs=[colm(aw), row(aw), colm(aw), colm(iw), row(LANES), colm(8)],
        out_shape=[jax.ShapeDtypeStruct((B, aw, S), BF16),
                   jax.ShapeDtypeStruct((B, S, aw), BF16),
                   jax.ShapeDtypeStruct((B, aw, S), BF16),
                   jax.ShapeDtypeStruct((B, iw, S), BF16),
                   jax.ShapeDtypeStruct((B, S, LANES), BF16),
                   jax.ShapeDtypeStruct((B, 8, S), F32)],
        compiler_params=pltpu.CompilerParams(
            dimension_semantics=("arbitrary", "arbitrary"),
            vmem_limit_bytes=48 * 1024 * 1024),
        name="attn_projections",
    )(x, wr, br, wc, bc, cos_r, sin_r, cos_c, sin_c)


MLSTM_HEADS = 8
MLSTM_QK_DIM = 64
MLSTM_V_DIM = 128
MLSTM_CHUNK = 128
CONV_WIDTH = 4
HALO = 8


def _silu(x):
    return x / (1.0 + jnp.exp(-x))


def _sigmoid(x):
    return 1.0 / (1.0 + jnp.exp(-x))


def _log_sigmoid(x):
    return jnp.minimum(x, 0.0) - jnp.log(1.0 + jnp.exp(-jnp.abs(x)))


def _mlstm_proj_kernel(x_ref, w_ref, b_ref, wt_ref, bt_ref, conv_ref,
                       mq_ref, mkt_ref, mv_ref, og_ref, ifc_ref, ift_ref, ext_ref, *, q_scale):
    i = pl.program_id(1)
    tm = x_ref.shape[1]
    qkw = 2 * mq_ref.shape[2]
    vw = mv_ref.shape[2]
    nh = ift_ref.shape[1] // 2
    xb = x_ref[0].astype(BF16)

    @pl.when(i == 0)
    def _():
        ext_ref[0:HALO, :] = jnp.zeros((HALO, qkw), F32)

    pqk = jnp.dot(xb, w_ref[:, 0:qkw], preferred_element_type=F32) + b_ref[:, 0:qkw]
    ext_ref[HALO:HALO + tm, :] = pqk
    acc = None
    for j in range(CONV_WIDTH):
        term = ext_ref[pl.ds(HALO - CONV_WIDTH + 1 + j, tm), :] * conv_ref[j:j + 1, :]
        acc = term if acc is None else acc + term
    ext_ref[0:HALO, :] = pqk[tm - HALO:tm, :]
    qk = _silu(acc)
    mq_ref[0] = (qk[:, 0:qkw // 2] * q_scale).astype(mq_ref.dtype)
    mkt_ref[0] = jnp.transpose(qk[:, qkw // 2:qkw]).astype(mkt_ref.dtype)

    pv = jnp.dot(xb, w_ref[:, qkw:qkw + vw], preferred_element_type=F32) + b_ref[:, qkw:qkw + vw]
    mv_ref[0] = pv.astype(mv_ref.dtype)
    po = (jnp.dot(xb, w_ref[:, qkw + vw:qkw + 2 * vw], preferred_element_type=F32)
          + b_ref[:, qkw + vw:qkw + 2 * vw])
    og_ref[0] = _sigmoid(po).astype(og_ref.dtype)

    pg = (jnp.dot(xb, w_ref[:, qkw + 2 * vw:], preferred_element_type=F32)
          + b_ref[:, qkw + 2 * vw:])
    lane = lax.broadcasted_iota(jnp.int32, pg.shape, 1)
    ifc_ref[0] = jnp.where(lane < nh, pg, _log_sigmoid(pg))
    pt = lax.dot_general(wt_ref[...], xb, (((1,), (1,)), ((), ())),
                         preferred_element_type=F32) + bt_ref[...]
    rowi = lax.broadcasted_iota(jnp.int32, pt.shape, 0)
    ift_ref[0] = jnp.where(rowi < nh, pt, _log_sigmoid(pt))


def mlstm_projections(x, w, b, wt, bt, conv, *, tm=512):
    B, S, D = x.shape
    qw = MLSTM_HEADS * MLSTM_QK_DIM
    vw = MLSTM_HEADS * MLSTM_V_DIM
    tm = min(tm, S)
    kern = functools.partial(_mlstm_proj_kernel, q_scale=MLSTM_QK_DIM ** -0.5)
    full = lambda a: pl.BlockSpec(a.shape, lambda b_, i: (0,) * a.ndim)
    row = lambda w_: pl.BlockSpec((1, tm, w_), lambda b_, i: (b_, i, 0))
    return pl.pallas_call(
        kern,
        grid=(B, S // tm),
        in_specs=[row(D), full(w), full(b), full(wt), full(bt), full(conv)],
        out_specs=[row(qw), pl.BlockSpec((1, qw, tm), lambda b_, i: (b_, 0, i)),
                   row(vw), row(vw), row(LANES),
                   pl.BlockSpec((1, 2 * MLSTM_HEADS, tm), lambda b_, i: (b_, 0, i))],
        out_shape=[jax.ShapeDtypeStruct((B, S, qw), BF16),
                   jax.ShapeDtypeStruct((B, qw, S), BF16),
                   jax.ShapeDtypeStruct((B, S, vw), BF16),
                   jax.ShapeDtypeStruct((B, S, vw), BF16),
                   jax.ShapeDtypeStruct((B, S, LANES), F32),
                   jax.ShapeDtypeStruct((B, 2 * MLSTM_HEADS, S), F32)],
        scratch_shapes=[pltpu.VMEM((HALO + tm, 2 * qw), F32)],
        compiler_params=pltpu.CompilerParams(
            dimension_semantics=("arbitrary", "arbitrary"),
            vmem_limit_bytes=56 * 1024 * 1024),
        name="mlstm_projections",
    )(x, w, b, wt, bt, conv)


GN_EPS = 1e-6


def _mlstm_kernel(mq_ref, mkt_ref, mv_ref, og_ref, ifc_ref, ift_ref, gain_ref, y_ref,
                  c_ref, m_ref):
    c = pl.program_id(1)
    L = mq_ref.shape[1]
    nh = ift_ref.shape[1] // 2
    dk = mq_ref.shape[2] // nh
    dv = mv_ref.shape[2] // nh

    @pl.when(c == 0)
    def _():
        c_ref[...] = jnp.zeros(c_ref.shape, F32)
        m_ref[...] = jnp.zeros(m_ref.shape, F32)

    r_i = lax.broadcasted_iota(jnp.int32, (L, L), 0)
    c_i = lax.broadcasted_iota(jnp.int32, (L, L), 1)
    causal = c_i <= r_i
    tril = jnp.where(causal, 1.0, 0.0)
    triu = jnp.where(r_i <= c_i, 1.0, 0.0)
    ifc = ifc_ref[0]
    ift = ift_ref[0]
    b_cols = jnp.dot(tril, ifc, preferred_element_type=F32, precision=lax.Precision.HIGHEST)
    b_rows = jnp.dot(ift, triu, preferred_element_type=F32, precision=lax.Precision.HIGHEST)
    ones_col = jnp.where(lax.broadcasted_iota(jnp.int32, (L, dv), 1) == 0, 1.0, 0.0).astype(BF16)

    for h in range(nh):
        m = m_ref[h]
        b_col = b_cols[:, nh + h:nh + h + 1]
        a_row = ift[h:h + 1, :] - b_rows[nh + h:nh + h + 1, :]
        amat = jnp.where(causal, a_row, NEG_INF)
        big_m = jnp.maximum(m, jnp.max(amat, axis=1, keepdims=True))
        decay = jnp.exp(amat - big_m)
        w_inter = jnp.exp(m - big_m)
        q = mq_ref[0, :, h * dk:(h + 1) * dk]
        kt = mkt_ref[0, h * dk:(h + 1) * dk, :]
        v_ext = jnp.concatenate([mv_ref[0, :, h * dv:(h + 1) * dv], ones_col], axis=1)
        s = jnp.dot(q, kt, preferred_element_type=F32) * decay
        sv = jnp.dot(s.astype(BF16), v_ext, preferred_element_type=F32)
        c_old = c_ref[h]
        qc = jnp.dot(q, c_old.astype(BF16), preferred_element_type=F32)
        num = sv[:, 0:dv] + w_inter * qc[:, 0:dv]
        den = sv[:, dv:dv + 1] + w_inter * qc[:, dv:dv + 1]
        hh = num / jnp.maximum(jnp.abs(den), jnp.exp(-(b_col + big_m)))
        mu = jnp.mean(hh, axis=1, keepdims=True)
        xc = hh - mu
        var = jnp.mean(xc * xc, axis=1, keepdims=True)
        hn = xc * lax.rsqrt(var + GN_EPS) * gain_ref[:, h * dv:(h + 1) * dv]
        y_ref[0, :, h * dv:(h + 1) * dv] = (
            og_ref[0, :, h * dv:(h + 1) * dv].astype(F32) * hn).astype(y_ref.dtype)

        b_last = b_col[L - 1:L, :]
        g_row = b_last + a_row
        m_new = jnp.maximum(b_last + m, jnp.max(g_row, axis=1, keepdims=True))
        carry = jnp.exp(b_last + m - m_new)
        wkt = (kt.astype(F32) * jnp.exp(g_row - m_new)).astype(BF16)
        c_ref[h] = carry * c_old + jnp.dot(wkt, v_ext, preferred_element_type=F32)
        m_ref[h] = m_new


def mlstm_scan(mq, mkt, mv, og, ifc, ift, gain):
    B, S, qw = mq.shape
    vw = mv.shape[2]
    L = min(MLSTM_CHUNK, S)
    nh = ift.shape[1] // 2
    row = lambda w_: pl.BlockSpec((1, L, w_), lambda b_, c: (b_, c, 0))
    return pl.pallas_call(
        _mlstm_kernel,
        grid=(B, S // L),
        in_specs=[row(qw), pl.BlockSpec((1, qw, L), lambda b_, c: (b_, 0, c)),
                  row(vw), row(vw), row(LANES),
                  pl.BlockSpec((1, 2 * nh, L), lambda b_, c: (b_, 0, c)),
                  pl.BlockSpec(gain.shape, lambda b_, c: (0, 0))],
        out_specs=row(vw),
        out_shape=jax.ShapeDtypeStruct((B, S, vw), BF16),
        scratch_shapes=[pltpu.VMEM((nh, qw // nh, 2 * vw // nh), F32),
                        pltpu.VMEM((nh, 1, 1), F32)],
        compiler_params=pltpu.CompilerParams(
            dimension_semantics=("arbitrary", "arbitrary")),
        name="mlstm_scan",
    )(mq, mkt, mv, og, ifc, ift, gain)


LN_EPS = 1e-5


def _layer_norm(z, gain, bias):
    mu = jnp.mean(z, axis=1, keepdims=True)
    zc = z - mu
    var = jnp.mean(zc * zc, axis=1, keepdims=True)
    return zc * lax.rsqrt(var + LN_EPS) * gain + bias


def _merge_kernel(x_ref, ya_ref, ym_ref, wg_ref, bg_ref, wa_ref, wm_ref, wo_ref, g_ref, b_ref,
                  o_ref, *, alpha):
    x = x_ref[...]
    xb = x.astype(BF16)
    d = x.shape[1]
    ga = _sigmoid(jnp.dot(xb, wg_ref[:, 0:d], preferred_element_type=F32) + bg_ref[:, 0:d])
    merged = ga * jnp.dot(ya_ref[...], wa_ref[...], preferred_element_type=F32)
    gm = _sigmoid(jnp.dot(xb, wg_ref[:, d:2 * d], preferred_element_type=F32) + bg_ref[:, d:2 * d])
    merged = merged + gm * jnp.dot(ym_ref[...], wm_ref[...], preferred_element_type=F32)
    z = alpha * x + jnp.dot(merged.astype(BF16), wo_ref[...], preferred_element_type=F32)
    o_ref[...] = _layer_norm(z, g_ref[...], b_ref[...])


def merge_branches(x2, ya, ym, wg, bg, wa, wm, wo, g, b, *, alpha, tm=512):
    T, D = x2.shape
    tm = min(tm, T)
    full = lambda a: pl.BlockSpec(a.shape, lambda i: (0,) * a.ndim)
    row = lambda w_: pl.BlockSpec((tm, w_), lambda i: (i, 0))
    return pl.pallas_call(
        functools.partial(_merge_kernel, alpha=alpha),
        grid=(T // tm,),
        in_specs=[row(D), row(ya.shape[1]), row(ym.shape[1]), full(wg), full(bg), full(wa),
                  full(wm), full(wo), full(g), full(b)],
        out_specs=row(D),
        out_shape=jax.ShapeDtypeStruct((T, D), F32),
        compiler_params=pltpu.CompilerParams(
            dimension_semantics=("arbitrary",), vmem_limit_bytes=48 * 1024 * 1024),
        name="merge_branches",
    )(x2, ya, ym, wg, bg, wa, wm, wo, g, b)


N_GROUPS = 4
EXPERTS_PER_GROUP = 4
N_EXPERTS = N_GROUPS * EXPERTS_PER_GROUP


def _first_lane_of_max(vals, vmax, lane):
    return jnp.min(jnp.where(vals == vmax, lane, LANES), axis=1, keepdims=True)


MOE_SUB = 128


def _moe_kernel(x_ref, wr_ref, br_ref, wg_ref, wu_ref, wd_ref, g_ref, b_ref, o_ref,
                tril_ref, slot_ref, xg_ref, cw_ref, y_ref, tiles_ref, *, alpha):
    w = pl.program_id(0)
    e = pl.program_id(1)
    tm = x_ref.shape[0]
    slots = xg_ref.shape[0]
    lane = lax.broadcasted_iota(jnp.int32, (tm, LANES), 1)

    @pl.when(jnp.logical_and(w == 0, e == 0))
    def _():
        r_i = lax.broadcasted_iota(jnp.int32, (tm, tm), 0)
        c_i = lax.broadcasted_iota(jnp.int32, (tm, tm), 1)
        tril_ref[...] = jnp.where(c_i < r_i, 1.0, 0.0).astype(BF16)

    @pl.when(e == 0)
    def _():
        x = x_ref[...]
        logits = jnp.dot(x, wr_ref[...], preferred_element_type=F32,
                         precision=lax.Precision.HIGHEST) + br_ref[...]
        g = jnp.where(lane < N_GROUPS, logits, NEG_INF)
        gmax = jnp.max(g, axis=1, keepdims=True)
        g_w = 1.0 / jnp.sum(jnp.exp(g - gmax), axis=1, keepdims=True)
        g_sel = _first_lane_of_max(g, gmax, lane)
        lo = N_GROUPS + EXPERTS_PER_GROUP * g_sel
        ev = jnp.where(jnp.logical_and(lane >= lo, lane < lo + EXPERTS_PER_GROUP), logits, NEG_INF)
        v1 = jnp.max(ev, axis=1, keepdims=True)
        i1 = _first_lane_of_max(ev, v1, lane)
        ev2 = jnp.where(lane == i1, NEG_INF, ev)
        v2 = jnp.max(ev2, axis=1, keepdims=True)
        i2 = _first_lane_of_max(ev2, v2, lane)
        r = jnp.exp(v2 - v1)
        p1 = 1.0 / (1.0 + r)
        p2 = r / (1.0 + r)
        comb = jnp.where(lane == i1, g_w * p1, 0.0) + jnp.where(lane == i2, g_w * p2, 0.0)
        comb_hi = comb.astype(BF16)
        comb_lo = (comb - comb_hi.astype(F32)).astype(BF16)

        onehot = jnp.where(lane == g_sel, 1.0, 0.0)
        before = jnp.dot(tril_ref[...], onehot.astype(BF16), preferred_element_type=F32)
        rank = jnp.sum(jnp.where(lane == g_sel, before, 0.0), axis=1, keepdims=True)
        total = jnp.sum(onehot, axis=0, keepdims=True)
        lane1 = lax.broadcasted_iota(jnp.int32, (1, LANES), 1)
        start = jnp.zeros((tm, 1), F32)
        first_tile = jnp.int32(0)
        for grp in range(N_GROUPS):
            n_tok = jnp.sum(jnp.where(lane1 == grp, total, 0.0)).astype(jnp.int32)
            n_tile = (n_tok + MOE_SUB - 1) // MOE_SUB
            tiles_ref[grp] = first_tile
            tiles_ref[N_GROUPS + grp] = n_tile
            start = jnp.where(g_sel == grp, (first_tile * MOE_SUB).astype(F32), start)
            first_tile = first_tile + n_tile
        slot = start + rank
        slot_ref[...] = slot
        slot_row = jnp.transpose(jnp.broadcast_to(slot, (tm, LANES)))[0:1]
        pick = slot_row == lax.broadcasted_iota(jnp.int32, (slots, tm), 0).astype(F32)
        pmat = jnp.where(pick, 1.0, 0.0).astype(BF16)
        xg_ref[...] = jnp.dot(pmat, x.astype(BF16), preferred_element_type=F32).astype(BF16)
        cw_ref[...] = (jnp.dot(pmat, comb_hi, preferred_element_type=F32)
                       + jnp.dot(pmat, comb_lo, preferred_element_type=F32))
        y_ref[...] = jnp.zeros(y_ref.shape, F32)

    grp = e // EXPERTS_PER_GROUP
    first = tiles_ref[grp]

    def expert(j, carry):
        base = pl.multiple_of((first + j) * MOE_SUB, MOE_SUB)
        xg = xg_ref[pl.ds(base, MOE_SUB), :]
        hg = jnp.dot(xg, wg_ref[0], preferred_element_type=F32)
        hu = jnp.dot(xg, wu_ref[0], preferred_element_type=F32)
        hdn = (_silu(hg) * hu).astype(BF16)
        lane_s = lax.broadcasted_iota(jnp.int32, (MOE_SUB, LANES), 1)
        cw = jnp.sum(jnp.where(lane_s == N_GROUPS + e, cw_ref[pl.ds(base, MOE_SUB), :], 0.0),
                     axis=1, keepdims=True)
        y_ref[pl.ds(base, MOE_SUB), :] += cw * jnp.dot(hdn, wd_ref[0], preferred_element_type=F32)
        return carry
    lax.fori_loop(0, tiles_ref[N_GROUPS + grp], expert, 0)

    @pl.when(e == pl.num_programs(1) - 1)
    def _():
        pick = slot_ref[...] == lax.broadcasted_iota(jnp.int32, (tm, slots), 1).astype(F32)
        pmat_t = jnp.where(pick, 1.0, 0.0).astype(BF16)
        moe = jnp.dot(pmat_t, y_ref[...].astype(BF16), preferred_element_type=F32)
        o_ref[...] = _layer_norm(alpha * x_ref[...] + moe, g_ref[...], b_ref[...])


def moe_layer(x2, wr, br, wg, wu, wd, g, b, *, alpha, tm=1024):
    T, D = x2.shape
    E, _, F = wg.shape
    tm = min(tm, T)
    slots = tm + N_GROUPS * MOE_SUB
    full = lambda a: pl.BlockSpec(a.shape, lambda i, e: (0,) * a.ndim)
    return pl.pallas_call(
        functools.partial(_moe_kernel, alpha=alpha),
        grid=(T // tm, E),
        in_specs=[pl.BlockSpec((tm, D), lambda i, e: (i, 0)), full(wr), full(br),
                  pl.BlockSpec((1, D, F), lambda i, e: (e, 0, 0)),
                  pl.BlockSpec((1, D, F), lambda i, e: (e, 0, 0)),
                  pl.BlockSpec((1, F, D), lambda i, e: (e, 0, 0)),
                  full(g), full(b)],
        out_specs=pl.BlockSpec((tm, D), lambda i, e: (i, 0)),
        out_shape=jax.ShapeDtypeStruct((T, D), F32),
        scratch_shapes=[pltpu.VMEM((tm, tm), BF16),
                        pltpu.VMEM((tm, 1), F32),
                        pltpu.VMEM((slots, D), BF16),
                        pltpu.VMEM((slots, LANES), F32),
                        pltpu.VMEM((slots, D), F32),
                        pltpu.SMEM((2 * N_GROUPS,), jnp.int32)],
        compiler_params=pltpu.CompilerParams(
            dimension_semantics=("arbitrary", "arbitrary"), vmem_limit_bytes=56 * 1024 * 1024),
        name="moe_layer",
    )(x2, wr, br, wg, wu, wd, g, b)


DEPTH = 1
DEEPNORM_ALPHA = (2.0 * DEPTH) ** 0.25


def _pad_cols(a, width):
    return jnp.pad(a, ((0, 0), (0, width - a.shape[1])))


def kernel(x, w_in, b_in, conv_m, gn_m_gain, w_branch_attn, w_branch_mlstm, w_out, ln1_gain, ln1_bias, w_router_group, b_router_group, w_router_expert, b_router_expert, w_exp_gate, w_exp_up, w_exp_down, ln2_gain, ln2_bias):
    B, S, D = x.shape
    aw = ATT_HEADS * ATT_HEAD_DIM
    iw = IDX_HEADS * IDX_DIM
    qw = MLSTM_HEADS * MLSTM_QK_DIM
    vw = MLSTM_HEADS * MLSTM_V_DIM
    widths = (aw, aw, aw, iw, IDX_DIM, IDX_HEADS, qw, qw, vw, MLSTM_HEADS, MLSTM_HEADS, vw, D, D)
    offs = [0]
    for w_ in widths:
        offs.append(offs[-1] + w_)
    col = lambda k: w_in[:, offs[k]:offs[k + 1]]
    bia = lambda k: b_in[offs[k]:offs[k + 1]]
    (A_Q, A_K, A_V, I_Q, I_K, I_W, M_Q, M_K, M_V, M_I, M_F, M_O, G_A, G_M) = range(14)

    wr = jnp.concatenate([col(A_K), _pad_cols(col(I_K), LANES)], 1).astype(BF16)
    br = jnp.concatenate([bia(A_K), jnp.pad(bia(I_K), (0, LANES - IDX_DIM))])[None, :]
    wc = jnp.concatenate([col(A_Q), col(A_V), col(I_Q), _pad_cols(col(I_W), 8)], 1).T.astype(BF16)
    bc = jnp.concatenate([bia(A_Q), bia(A_V), bia(I_Q), jnp.pad(bia(I_W), (0, 8 - IDX_HEADS))])[:, None]
    gpad = LANES - 2 * MLSTM_HEADS
    wm = jnp.concatenate([col(M_Q), col(M_K), col(M_V), col(M_O), col(M_I),
                          _pad_cols(col(M_F), MLSTM_HEADS + gpad)], 1).astype(BF16)
    bm = jnp.concatenate([bia(M_Q), bia(M_K), bia(M_V), bia(M_O), bia(M_I),
                          jnp.pad(bia(M_F), (0, gpad))])[None, :]
    wmt = jnp.concatenate([col(M_I), col(M_F)], 1).T.astype(BF16)
    bmt = jnp.concatenate([bia(M_I), bia(M_F)])[:, None]
    wgate = jnp.concatenate([col(G_A), col(G_M)], 1).astype(BF16)
    bgate = jnp.concatenate([bia(G_A), bia(G_M)])[None, :]

    half = ATT_HEAD_DIM // 2
    inv = ROPE_THETA ** (-jnp.arange(0, ATT_HEAD_DIM, 2, dtype=F32) / ATT_HEAD_DIM)
    ang = jnp.arange(S, dtype=F32)[:, None] * inv[None, :]
    cos, sin = jnp.cos(ang), jnp.sin(ang)
    cos_r = jnp.tile(cos, (1, LANES // half))
    sin_r = jnp.tile(jnp.concatenate([-sin, sin], 1), (1, LANES // ATT_HEAD_DIM))

    qt, k, vt, qit, ki, wit = attn_projections(x, wr, br, wc, bc, cos_r, sin_r, cos.T, sin.T)
    y_attn = dsa_attention(qt, k, vt, qit, ki, wit, topk=min(IDX_TOPK_MAX, S // 4))

    mq, mkt, mv, og, ifc, ift = mlstm_projections(x, wm, bm, wmt, bmt, conv_m)
    y_mlstm = mlstm_scan(mq, mkt, mv, og, ifc, ift, gn_m_gain[None, :])

    x1 = merge_branches(x.reshape(B * S, D), y_attn.reshape(B * S, aw), y_mlstm.reshape(B * S, vw),
                        wgate, bgate, w_branch_attn.astype(BF16), w_branch_mlstm.astype(BF16),
                        w_out.astype(BF16), ln1_gain[None, :], ln1_bias[None, :],
                        alpha=DEEPNORM_ALPHA)

    w_router = _pad_cols(jnp.concatenate([w_router_group, w_router_expert], 1), LANES)
    b_router = jnp.pad(jnp.concatenate([b_router_group, b_router_expert]),
                       (0, LANES - N_GROUPS - N_EXPERTS))[None, :]
    out = moe_layer(x1, w_router, b_router, w_exp_gate.astype(BF16), w_exp_up.astype(BF16),
                    w_exp_down.astype(BF16), ln2_gain[None, :], ln2_bias[None, :],
                    alpha=DEEPNORM_ALPHA)
    return out.reshape(B, S, D)
```

```python
import functools

import jax
import jax.numpy as jnp
from jax import lax
from jax.experimental import pallas as pl
from jax.experimental.pallas import tpu as pltpu

F32 = jnp.float32
BF16 = jnp.bfloat16
NEG_INF = float("-inf")
LOG2_E = 1.4426950408889634

ATT_HEADS = 8
ATT_HEAD_DIM = 64
IDX_HEADS = 4
IDX_DIM = 64
IDX_TOPK_MAX = 256
Q_BLOCK = 128
ROPE_THETA = 10000.0

LANES = 128
BF16_ROWS = 16


def _key_to_f32(u):
    ks = u ^ jnp.int32(-2 ** 31)
    bits = ks ^ ((ks >> 31) & jnp.int32(0x7FFFFFFF))
    return lax.bitcast_convert_type(bits, F32)


def _dsa_kernel(qt_ref, k_ref, vt_ref, qit_ref, ki_ref, wit_ref, o_ref,
                sc_ref, qm_ref, m_ref, l_ref, acc_ref, *, topk, kchunk):
    qb = pl.program_id(1)
    tq = o_ref.shape[1]
    heads = qt_ref.shape[1] // ATT_HEAD_DIM
    step = 2 * LANES
    n_chunk = (qb * tq + tq + kchunk - 1) // kchunk
    qpos = lax.broadcasted_iota(jnp.int32, (1, tq), 1) + qb * tq

    qit = qit_ref[0]
    zpad = jnp.zeros((LANES - IDX_DIM, tq), BF16)
    qi_pair = []
    for p in range(IDX_HEADS // 2):
        cols = [jnp.concatenate([qit[h * IDX_DIM:(h + 1) * IDX_DIM], zpad], axis=0)
                for h in (2 * p, 2 * p + 1)]
        qi_pair.append(jnp.concatenate(cols, axis=1))
    wit = wit_ref[0]

    def score_body(c, carry):
        off = pl.multiple_of(c * kchunk, kchunk)
        ki = ki_ref[0, pl.ds(off, kchunk), :]
        tot = None
        for p in range(IDX_HEADS // 2):
            s2 = jnp.dot(ki, qi_pair[p], preferred_element_type=F32)
            for j in range(2):
                h = 2 * p + j
                s = jnp.maximum(s2[:, j * tq:(j + 1) * tq], 0.0) * wit[h:h + 1, :]
                tot = s if tot is None else tot + s
        kpos = lax.broadcasted_iota(jnp.int32, (kchunk, tq), 0) + off
        sc_ref[pl.ds(off, kchunk), :] = jnp.where(kpos <= qpos, tot + 0.0, NEG_INF)
        return carry

    lax.fori_loop(0, n_chunk, score_body, 0)

    def count(pred):
        def body(j, acc):
            off = pl.multiple_of(j * kchunk, kchunk)
            for t in range(kchunk // LANES):
                x = sc_ref[pl.ds(off + t * LANES, LANES), :]
                acc = acc + jnp.where(pred(x, off + t * LANES), 1.0, 0.0)
            return acc
        acc = lax.fori_loop(0, n_chunk, body, jnp.zeros((LANES, tq), F32))
        return jnp.sum(acc, axis=0, keepdims=True)

    def bit_body(i, carry):
        u, cgt = carry
        cand = u | (jnp.int32(1) << (31 - i))
        thr = _key_to_f32(cand)
        cnt = count(lambda x, off: x >= thr)
        ok = cnt >= float(topk)
        return jnp.where(ok, cand, u), jnp.where(ok, cgt, cnt)

    u, cgt = lax.fori_loop(0, 32, bit_body,
                           (jnp.zeros((1, tq), jnp.int32), jnp.zeros((1, tq), F32)))
    short = qpos < topk
    tau = jnp.where(short, NEG_INF, _key_to_f32(u))
    need = float(topk) - cgt

    r_i = lax.broadcasted_iota(jnp.int32, (LANES, LANES), 0)
    c_i = lax.broadcasted_iota(jnp.int32, (LANES, LANES), 1)
    tril = jnp.where(c_i <= r_i, 1.0, 0.0).astype(BF16)

    def bias_body(c, seen):
        base = pl.multiple_of(c * kchunk, kchunk)
        xs = [sc_ref[pl.ds(base + t * LANES, LANES), :] for t in range(kchunk // LANES)]
        ties = [x == tau for x in xs]
        ranks = [jnp.dot(tril, jnp.where(tie, 1.0, 0.0).astype(BF16),
                         preferred_element_type=F32) for tie in ties]
        for t, (x, tie, rank) in enumerate(zip(xs, ties, ranks)):
            keep = jnp.logical_or(x > tau, jnp.logical_and(tie, rank + seen <= need))
            kpos = lax.broadcasted_iota(jnp.int32, x.shape, 0) + (base + t * LANES)
            keep = jnp.logical_and(keep, kpos <= qpos)
            sc_ref[pl.ds(base + t * LANES, LANES), :] = jnp.where(keep, 0.0, NEG_INF)
            seen = seen + rank[LANES - 1:LANES, :]
        return seen

    lax.fori_loop(0, n_chunk, bias_body, jnp.zeros((1, tq), F32))

    qt = qt_ref[0]
    zrow = jnp.zeros((ATT_HEAD_DIM, tq), BF16)
    for p in range(heads // 2):
        a = qt[(2 * p) * ATT_HEAD_DIM:(2 * p + 1) * ATT_HEAD_DIM]
        b = qt[(2 * p + 1) * ATT_HEAD_DIM:(2 * p + 2) * ATT_HEAD_DIM]
        qm_ref[p] = jnp.concatenate([jnp.concatenate([a, zrow], axis=0),
                                     jnp.concatenate([zrow, b], axis=0)], axis=1)
    m_ref[...] = jnp.full(m_ref.shape, NEG_INF, F32)
    l_ref[...] = jnp.zeros(l_ref.shape, F32)
    acc_ref[...] = jnp.zeros(acc_ref.shape, F32)
    ones_rows = jnp.ones((BF16_ROWS, step), BF16)
    achunk = min(2 * kchunk, sc_ref.shape[0])
    n_achunk = (qb * tq + tq + achunk - 1) // achunk

    @pl.when(n_achunk * achunk > n_chunk * kchunk)
    def _():
        fill = pl.multiple_of(n_chunk * kchunk, kchunk)
        sc_ref[pl.ds(fill, kchunk), :] = jnp.full((kchunk, tq), NEG_INF, F32)

    stages = [(sub, p) for sub in range(achunk // step) for p in range(heads // 2)]

    def attn_body(c, carry):
        off = pl.multiple_of(c * achunk, achunk)

        def qk(stage):
            sub, p = stage
            kp = k_ref[0, pl.ds(off + sub * step, step), p * LANES:(p + 1) * LANES]
            return jnp.dot(kp, qm_ref[p], preferred_element_type=F32)

        ahead = 4
        pending = [qk(st) for st in stages[:ahead]]
        for i, (sub, p) in enumerate(stages):
            if i + ahead < len(stages):
                pending.append(qk(stages[i + ahead]))
            s2 = pending.pop(0)
            koff = off + sub * step
            bias = sc_ref[pl.ds(koff, step), :]
            for j in range(2):
                h = 2 * p + j
                s = s2[:, j * tq:(j + 1) * tq] + bias
                m_old = m_ref[h]
                m_new = jnp.maximum(m_old, jnp.max(s, axis=0, keepdims=True))
                m_use = jnp.where(m_new == NEG_INF, 0.0, m_new)
                pexp = jnp.exp2(s - m_use).astype(BF16)
                alpha = jnp.exp2(m_old - m_use)
                vt = jnp.concatenate(
                    [vt_ref[0, h * ATT_HEAD_DIM:(h + 1) * ATT_HEAD_DIM, pl.ds(koff, step)],
                     ones_rows], axis=0)
                pv = jnp.dot(vt, pexp, preferred_element_type=F32)
                acc_ref[h] = alpha * acc_ref[h] + pv[0:ATT_HEAD_DIM]
                l_ref[h] = alpha * l_ref[h] + pv[ATT_HEAD_DIM:ATT_HEAD_DIM + 1]
                m_ref[h] = m_new
        return carry

    lax.fori_loop(0, n_achunk, attn_body, 0)

    out_t = jnp.concatenate([acc_ref[h] / l_ref[h] for h in range(heads)], axis=0)
    o_ref[0] = jnp.transpose(out_t).astype(o_ref.dtype)


def dsa_attention(qt, k, vt, qit, ki, wit, *, topk, kchunk=512):
    B, W, S = qt.shape
    tq = Q_BLOCK
    kchunk = min(kchunk, S)
    heads = W // ATT_HEAD_DIM
    kern = functools.partial(_dsa_kernel, topk=topk, kchunk=kchunk)
    qcol = lambda a: pl.BlockSpec((1, a.shape[1], tq), lambda b, i: (b, 0, i))
    whole = lambda a: pl.BlockSpec((1,) + a.shape[1:], lambda b, i: (b, 0, 0))
    return pl.pallas_call(
        kern,
        grid=(B, S // tq),
        in_specs=[qcol(qt), whole(k), whole(vt), qcol(qit), whole(ki), qcol(wit)],
        out_specs=pl.BlockSpec((1, tq, W), lambda b, i: (b, i, 0)),
        out_shape=jax.ShapeDtypeStruct((B, S, W), BF16),
        scratch_shapes=[
            pltpu.VMEM((S, tq), F32),
            pltpu.VMEM((heads // 2, LANES, 2 * tq), BF16),
            pltpu.VMEM((heads, 1, tq), F32),
            pltpu.VMEM((heads, 1, tq), F32),
            pltpu.VMEM((heads, ATT_HEAD_DIM, tq), F32),
        ],
        compiler_params=pltpu.CompilerParams(
            dimension_semantics=("arbitrary", "arbitrary"),
            vmem_limit_bytes=56 * 1024 * 1024),
        name="dsa_attention",
    )(qt, k, vt, qit, ki, wit)


def _rope_rows(x, cos_t, sin_t):
    lane = lax.broadcasted_iota(jnp.int32, x.shape, 1)
    swapped = jnp.where((lane % ATT_HEAD_DIM) < ATT_HEAD_DIM // 2,
                        pltpu.roll(x, LANES - ATT_HEAD_DIM // 2, 1),
                        pltpu.roll(x, ATT_HEAD_DIM // 2, 1))
    return x * cos_t + swapped * sin_t


def _attn_proj_kernel(x_ref, wr_ref, br_ref, wc_ref, bc_ref, cos_ref, sin_ref, cost_ref, sint_ref,
                      qt_ref, k_ref, vt_ref, qit_ref, ki_ref, wit_ref, *, q_scale, wi_scale):
    xb = x_ref[0].astype(BF16)
    aw = k_ref.shape[2]
    iw = qit_ref.shape[1]
    half = ATT_HEAD_DIM // 2
    cos_t = cos_ref[...]
    sin_t = sin_ref[...]

    pk = jnp.dot(xb, wr_ref[...], preferred_element_type=F32) + br_ref[...]
    for j in range(aw // LANES):
        sl = slice(j * LANES, (j + 1) * LANES)
        k_ref[0, :, sl] = _rope_rows(pk[:, sl], cos_t, sin_t).astype(k_ref.dtype)
    ki_ref[0] = _rope_rows(pk[:, aw:aw + LANES], cos_t, sin_t).astype(ki_ref.dtype)

    pt = lax.dot_general(wc_ref[...], xb, (((1,), (1,)), ((), ())),
                         preferred_element_type=F32) + bc_ref[...]
    ct = cost_ref[...]
    st = sint_ref[...]

    def rope_cols(src0, dst_ref, nheads, scale):
        for h in range(nheads):
            r0 = src0 + h * ATT_HEAD_DIM
            x1 = pt[r0:r0 + half]
            x2 = pt[r0 + half:r0 + 2 * half]
            d0 = h * ATT_HEAD_DIM
            dst_ref[0, d0:d0 + half, :] = ((x1 * ct - x2 * st) * scale).astype(dst_ref.dtype)
            dst_ref[0, d0 + half:d0 + 2 * half, :] = ((x1 * st + x2 * ct) * scale).astype(dst_ref.dtype)

    rope_cols(0, qt_ref, aw // ATT_HEAD_DIM, q_scale)
    vt_ref[0] = pt[aw:2 * aw].astype(vt_ref.dtype)
    rope_cols(2 * aw, qit_ref, iw // IDX_DIM, 1.0)
    wit_ref[0] = pt[2 * aw + iw:] * wi_scale


def attn_projections(x, wr, br, wc, bc, cos_r, sin_r, cos_c, sin_c, *, tm=512):
    B, S, D = x.shape
    aw = ATT_HEADS * ATT_HEAD_DIM
    iw = IDX_HEADS * IDX_DIM
    tm = min(tm, S)
    kern = functools.partial(_attn_proj_kernel, q_scale=ATT_HEAD_DIM ** -0.5 * LOG2_E,
                             wi_scale=IDX_HEADS ** -0.5 * IDX_DIM ** -0.5)
    full = lambda a: pl.BlockSpec(a.shape, lambda b, i: (0,) * a.ndim)
    row = lambda w: pl.BlockSpec((1, tm, w), lambda b, i: (b, i, 0))
    colm = lambda r: pl.BlockSpec((1, r, tm), lambda b, i: (b, 0, i))
    return pl.pallas_call(
        kern,
        grid=(B, S // tm),
        in_specs=[row(D), full(wr), full(br), full(wc), full(bc),
                  pl.BlockSpec((tm, LANES), lambda b, i: (i, 0)),
                  pl.BlockSpec((tm, LANES), lambda b, i: (i, 0)),
                  pl.BlockSpec((ATT_HEAD_DIM // 2, tm), lambda b, i: (0, i)),
                  pl.BlockSpec((ATT_HEAD_DIM // 2, tm), lambda b, i: (0, i))],
        out_specs=[colm(aw), row(aw), colm(aw), colm(iw), row(LANES), colm(8)],
        out_shape=[jax.ShapeDtypeStruct((B, aw, S), BF16),
                   jax.ShapeDtypeStruct((B, S, aw), BF16),
                   jax.ShapeDtypeStruct((B, aw, S), BF16),
                   jax.ShapeDtypeStruct((B, iw, S), BF16),
                   jax.ShapeDtypeStruct((B, S, LANES), BF16),
                   jax.ShapeDtypeStruct((B, 8, S), F32)],
        compiler_params=pltpu.CompilerParams(
            dimension_semantics=("arbitrary", "arbitrary"),
            vmem_limit_bytes=48 * 1024 * 1024),
        name="attn_projections",
    )(x, wr, br, wc, bc, cos_r, sin_r, cos_c, sin_c)


MLSTM_HEADS = 8
MLSTM_QK_DIM = 64
MLSTM_V_DIM = 128
MLSTM_CHUNK = 128
CONV_WIDTH = 4
HALO = 8


def _silu(x):
    return x / (1.0 + jnp.exp(-x))


def _sigmoid(x):
    return 1.0 / (1.0 + jnp.exp(-x))


def _log_sigmoid(x):
    return jnp.minimum(x, 0.0) - jnp.log(1.0 + jnp.exp(-jnp.abs(x)))


def _mlstm_proj_kernel(x_ref, w_ref, b_ref, wt_ref, bt_ref, conv_ref,
                       mqt_ref, mk_ref, mvt_ref, ogt_ref, ifc_ref, ift_ref, ext_ref, *, q_scale):
    i = pl.program_id(1)
    tm = x_ref.shape[1]
    qkw = 2 * mk_ref.shape[2]
    vw = mvt_ref.shape[1]
    nh = ift_ref.shape[1] // 2
    xb = x_ref[0].astype(BF16)

    @pl.when(i == 0)
    def _():
        ext_ref[0:HALO, :] = jnp.zeros((HALO, qkw), F32)

    pqk = jnp.dot(xb, w_ref[:, 0:qkw], preferred_element_type=F32) + b_ref[:, 0:qkw]
    ext_ref[HALO:HALO + tm, :] = pqk
    acc = None
    for j in range(CONV_WIDTH):
        term = ext_ref[pl.ds(HALO - CONV_WIDTH + 1 + j, tm), :] * conv_ref[j:j + 1, :]
        acc = term if acc is None else acc + term
    ext_ref[0:HALO, :] = pqk[tm - HALO:tm, :]
    qk = _silu(acc)
    mqt_ref[0] = jnp.transpose(qk[:, 0:qkw // 2] * q_scale).astype(mqt_ref.dtype)
    mk_ref[0] = qk[:, qkw // 2:qkw].astype(mk_ref.dtype)

    pg = jnp.dot(xb, w_ref[:, qkw:], preferred_element_type=F32) + b_ref[:, qkw:]
    lane = lax.broadcasted_iota(jnp.int32, pg.shape, 1)
    ifc_ref[0] = jnp.where(lane < nh, pg, _log_sigmoid(pg))

    pt = lax.dot_general(wt_ref[...], xb, (((1,), (1,)), ((), ())),
                         preferred_element_type=F32) + bt_ref[...]
    mvt_ref[0] = pt[0:vw].astype(mvt_ref.dtype)
    ogt_ref[0] = _sigmoid(pt[vw:2 * vw]).astype(ogt_ref.dtype)
    gates = pt[2 * vw:]
    rowi = lax.broadcasted_iota(jnp.int32, gates.shape, 0)
    ift_ref[0] = jnp.where(rowi < nh, gates, _log_sigmoid(gates))


def mlstm_projections(x, w, b, wt, bt, conv, *, tm=512):
    B, S, D = x.shape
    qw = MLSTM_HEADS * MLSTM_QK_DIM
    vw = MLSTM_HEADS * MLSTM_V_DIM
    tm = min(tm, S)
    kern = functools.partial(_mlstm_proj_kernel, q_scale=MLSTM_QK_DIM ** -0.5)
    full = lambda a: pl.BlockSpec(a.shape, lambda b_, i: (0,) * a.ndim)
    row = lambda w_: pl.BlockSpec((1, tm, w_), lambda b_, i: (b_, i, 0))
    colm = lambda r: pl.BlockSpec((1, r, tm), lambda b_, i: (b_, 0, i))
    return pl.pallas_call(
        kern,
        grid=(B, S // tm),
        in_specs=[row(D), full(w), full(b), full(wt), full(bt), full(conv)],
        out_specs=[colm(qw), row(qw), colm(vw), colm(vw), row(LANES), colm(2 * MLSTM_HEADS)],
        out_shape=[jax.ShapeDtypeStruct((B, qw, S), BF16),
                   jax.ShapeDtypeStruct((B, S, qw), BF16),
                   jax.ShapeDtypeStruct((B, vw, S), BF16),
                   jax.ShapeDtypeStruct((B, vw, S), BF16),
                   jax.ShapeDtypeStruct((B, S, LANES), F32),
                   jax.ShapeDtypeStruct((B, 2 * MLSTM_HEADS, S), F32)],
        scratch_shapes=[pltpu.VMEM((HALO + tm, 2 * qw), F32)],
        compiler_params=pltpu.CompilerParams(
            dimension_semantics=("arbitrary", "arbitrary"),
            vmem_limit_bytes=56 * 1024 * 1024),
        name="mlstm_projections",
    )(x, w, b, wt, bt, conv)


GN_EPS = 1e-6


def _mlstm_kernel(mqt_ref, mk_ref, mvt_ref, ogt_ref, ifc_ref, ift_ref, gain_ref, y_ref,
                  ct_ref, m_ref):
    c = pl.program_id(1)
    L = mk_ref.shape[1]
    nh = ift_ref.shape[1] // 2
    dk = mk_ref.shape[2] // nh
    dv = mvt_ref.shape[1] // nh

    @pl.when(c == 0)
    def _():
        ct_ref[...] = jnp.zeros(ct_ref.shape, F32)
        m_ref[...] = jnp.zeros(m_ref.shape, F32)

    r_i = lax.broadcasted_iota(jnp.int32, (L, L), 0)
    c_i = lax.broadcasted_iota(jnp.int32, (L, L), 1)
    causal_t = r_i <= c_i
    tril = jnp.where(c_i <= r_i, 1.0, 0.0)
    triu = jnp.where(causal_t, 1.0, 0.0)
    ifc = ifc_ref[0]
    ift = ift_ref[0]
    b_cols = jnp.dot(tril, ifc, preferred_element_type=F32, precision=lax.Precision.HIGHEST)
    b_rows = jnp.dot(ift, triu, preferred_element_type=F32, precision=lax.Precision.HIGHEST)
    ones_rows = jnp.where(lax.broadcasted_iota(jnp.int32, (BF16_ROWS, L), 0) == 0,
                          1.0, 0.0).astype(BF16)
    first_half_rows = lax.broadcasted_iota(jnp.int32, (2 * dk, L), 0) < dk
    first_half_lanes = lax.broadcasted_iota(jnp.int32, (L, 2 * dk), 1) < dk

    k_pair, q_heads, s_heads, inter_heads = [], [], [], []
    for p in range(nh // 2):
        k_pair.append(mk_ref[0, :, 2 * p * dk:2 * (p + 1) * dk])
        qt_pair = mqt_ref[0, 2 * p * dk:2 * (p + 1) * dk, :]
        zero = jnp.zeros_like(qt_pair)
        q_heads.append(jnp.where(first_half_rows, qt_pair, zero))
        q_heads.append(jnp.where(first_half_rows, zero, qt_pair))
    for h in range(nh):
        s_heads.append(jnp.dot(k_pair[h // 2], q_heads[h], preferred_element_type=F32))
        inter_heads.append(jnp.dot(ct_ref[h].astype(BF16), q_heads[h],
                                   preferred_element_type=F32))

    for h in range(nh):
        m = m_ref[h]
        a_col = ifc[:, h:h + 1] - b_cols[:, nh + h:nh + h + 1]
        amat = jnp.where(causal_t, a_col, NEG_INF)
        big_m = jnp.maximum(m, jnp.max(amat, axis=0, keepdims=True))
        decay = jnp.exp(amat - big_m)
        w_inter = jnp.exp(m - big_m)
        s = (s_heads[h] * decay).astype(BF16)
        vt_ext = jnp.concatenate([mvt_ref[0, h * dv:(h + 1) * dv, :], ones_rows], axis=0)
        tot = jnp.dot(vt_ext, s, preferred_element_type=F32) + w_inter * inter_heads[h]
        den = tot[dv:dv + 1]
        b_row = b_rows[nh + h:nh + h + 1, :]
        hh = tot[0:dv] / jnp.maximum(jnp.abs(den), jnp.exp(-(b_row + big_m)))
        mu = jnp.mean(hh, axis=0, keepdims=True)
        xc = hh - mu
        var = jnp.mean(xc * xc, axis=0, keepdims=True)
        hn = (xc * lax.rsqrt(var + GN_EPS) * gain_ref[h * dv:(h + 1) * dv, :]
              * ogt_ref[0, h * dv:(h + 1) * dv, :].astype(F32))
        y_ref[0, :, h * dv:(h + 1) * dv] = jnp.transpose(hn).astype(y_ref.dtype)

        b_last = b_cols[L - 1:L, nh + h:nh + h + 1]
        g_col = b_last + a_col
        m_new = jnp.maximum(b_last + m, jnp.max(g_col, axis=0, keepdims=True))
        carry = jnp.exp(b_last + m - m_new)
        head_lanes = first_half_lanes if h % 2 == 0 else jnp.logical_not(first_half_lanes)
        wk = jnp.where(head_lanes, k_pair[h // 2].astype(F32) * jnp.exp(g_col - m_new),
                       0.0).astype(BF16)
        ct_ref[h] = carry * ct_ref[h] + jnp.dot(vt_ext, wk, preferred_element_type=F32)
        m_ref[h] = m_new


def mlstm_scan(mqt, mk, mvt, ogt, ifc, ift, gain_b):
    B, S, qw = mk.shape
    vw = mvt.shape[1]
    L = min(MLSTM_CHUNK, S)
    nh = ift.shape[1] // 2
    row = lambda w_: pl.BlockSpec((1, L, w_), lambda b_, c: (b_, c, 0))
    colm = lambda r: pl.BlockSpec((1, r, L), lambda b_, c: (b_, 0, c))
    return pl.pallas_call(
        _mlstm_kernel,
        grid=(B, S // L),
        in_specs=[colm(qw), row(qw), colm(vw), colm(vw), row(LANES), colm(2 * nh),
                  pl.BlockSpec(gain_b.shape, lambda b_, c: (0, 0))],
        out_specs=row(vw),
        out_shape=jax.ShapeDtypeStruct((B, S, vw), BF16),
        scratch_shapes=[pltpu.VMEM((nh, vw // nh + BF16_ROWS, 2 * qw // nh), F32),
                        pltpu.VMEM((nh, 1, 1), F32)],
        compiler_params=pltpu.CompilerParams(
            dimension_semantics=("arbitrary", "arbitrary")),
        name="mlstm_scan",
    )(mqt, mk, mvt, ogt, ifc, ift, gain_b)


LN_EPS = 1e-5


def _layer_norm(z, gain, bias):
    mu = jnp.mean(z, axis=1, keepdims=True)
    zc = z - mu
    var = jnp.mean(zc * zc, axis=1, keepdims=True)
    return zc * lax.rsqrt(var + LN_EPS) * gain + bias


def _merge_kernel(x_ref, ya_ref, ym_ref, wg_ref, bg_ref, wa_ref, wm_ref, wo_ref, g_ref, b_ref,
                  o_ref, *, alpha):
    x = x_ref[...]
    xb = x.astype(BF16)
    d = x.shape[1]
    ga = _sigmoid(jnp.dot(xb, wg_ref[:, 0:d], preferred_element_type=F32) + bg_ref[:, 0:d])
    merged = ga * jnp.dot(ya_ref[...], wa_ref[...], preferred_element_type=F32)
    gm = _sigmoid(jnp.dot(xb, wg_ref[:, d:2 * d], preferred_element_type=F32) + bg_ref[:, d:2 * d])
    merged = merged + gm * jnp.dot(ym_ref[...], wm_ref[...], preferred_element_type=F32)
    z = alpha * x + jnp.dot(merged.astype(BF16), wo_ref[...], preferred_element_type=F32)
    o_ref[...] = _layer_norm(z, g_ref[...], b_ref[...])


def merge_branches(x2, ya, ym, wg, bg, wa, wm, wo, g, b, *, alpha, tm=512):
    T, D = x2.shape
    tm = min(tm, T)
    full = lambda a: pl.BlockSpec(a.shape, lambda i: (0,) * a.ndim)
    row = lambda w_: pl.BlockSpec((tm, w_), lambda i: (i, 0))
    return pl.pallas_call(
        functools.partial(_merge_kernel, alpha=alpha),
        grid=(T // tm,),
        in_specs=[row(D), row(ya.shape[1]), row(ym.shape[1]), full(wg), full(bg), full(wa),
                  full(wm), full(wo), full(g), full(b)],
        out_specs=row(D),
        out_shape=jax.ShapeDtypeStruct((T, D), F32),
        compiler_params=pltpu.CompilerParams(
            dimension_semantics=("arbitrary",), vmem_limit_bytes=48 * 1024 * 1024),
        name="merge_branches",
    )(x2, ya, ym, wg, bg, wa, wm, wo, g, b)


N_GROUPS = 4
EXPERTS_PER_GROUP = 4
N_EXPERTS = N_GROUPS * EXPERTS_PER_GROUP


def _first_lane_of_max(vals, vmax, lane):
    return jnp.min(jnp.where(vals == vmax, lane, LANES), axis=1, keepdims=True)


MOE_SUB = 128


def _moe_kernel(x_ref, wr_ref, br_ref, wg_ref, wu_ref, wd_ref, g_ref, b_ref, o_ref,
                tril_ref, slot_ref, xg_ref, cw_ref, y_ref, tiles_ref, *, alpha):
    w = pl.program_id(0)
    e = pl.program_id(1)
    tm = x_ref.shape[0]
    slots = xg_ref.shape[0]
    lane = lax.broadcasted_iota(jnp.int32, (tm, LANES), 1)

    @pl.when(jnp.logical_and(w == 0, e == 0))
    def _():
        r_i = lax.broadcasted_iota(jnp.int32, (tm, tm), 0)
        c_i = lax.broadcasted_iota(jnp.int32, (tm, tm), 1)
        tril_ref[...] = jnp.where(c_i < r_i, 1.0, 0.0).astype(BF16)

    @pl.when(e == 0)
    def _():
        x = x_ref[...]
        logits = jnp.dot(x, wr_ref[...], preferred_element_type=F32,
                         precision=lax.Precision.HIGHEST) + br_ref[...]
        g = jnp.where(lane < N_GROUPS, logits, NEG_INF)
        gmax = jnp.max(g, axis=1, keepdims=True)
        g_w = 1.0 / jnp.sum(jnp.exp(g - gmax), axis=1, keepdims=True)
        g_sel = _first_lane_of_max(g, gmax, lane)
        lo = N_GROUPS + EXPERTS_PER_GROUP * g_sel
        ev = jnp.where(jnp.logical_and(lane >= lo, lane < lo + EXPERTS_PER_GROUP), logits, NEG_INF)
        v1 = jnp.max(ev, axis=1, keepdims=True)
        i1 = _first_lane_of_max(ev, v1, lane)
        ev2 = jnp.where(lane == i1, NEG_INF, ev)
        v2 = jnp.max(ev2, axis=1, keepdims=True)
        i2 = _first_lane_of_max(ev2, v2, lane)
        r = jnp.exp(v2 - v1)
        p1 = 1.0 / (1.0 + r)
        p2 = r / (1.0 + r)
        comb = jnp.where(lane == i1, g_w * p1, 0.0) + jnp.where(lane == i2, g_w * p2, 0.0)
        comb_hi = comb.astype(BF16)
        comb_lo = (comb - comb_hi.astype(F32)).astype(BF16)

        onehot = jnp.where(lane == g_sel, 1.0, 0.0)
        before = jnp.dot(tril_ref[...], onehot.astype(BF16), preferred_element_type=F32)
        rank = jnp.sum(jnp.where(lane == g_sel, before, 0.0), axis=1, keepdims=True)
        total = jnp.sum(onehot, axis=0, keepdims=True)
        lane1 = lax.broadcasted_iota(jnp.int32, (1, LANES), 1)
        start = jnp.zeros((tm, 1), F32)
        first_tile = jnp.int32(0)
        for grp in range(N_GROUPS):
            n_tok = jnp.sum(jnp.where(lane1 == grp, total, 0.0)).astype(jnp.int32)
            n_tile = (n_tok + MOE_SUB - 1) // MOE_SUB
            tiles_ref[grp] = first_tile
            tiles_ref[N_GROUPS + grp] = n_tile
            start = jnp.where(g_sel == grp, (first_tile * MOE_SUB).astype(F32), start)
            first_tile = first_tile + n_tile
        slot = start + rank
        slot_ref[...] = slot
        slot_row = jnp.transpose(jnp.broadcast_to(slot, (tm, LANES)))[0:1]
        pick = slot_row == lax.broadcasted_iota(jnp.int32, (slots, tm), 0).astype(F32)
        pmat = jnp.where(pick, 1.0, 0.0).astype(BF16)
        xg_ref[...] = jnp.dot(pmat, x.astype(BF16), preferred_element_type=F32).astype(BF16)
        cw_ref[...] = (jnp.dot(pmat, comb_hi, preferred_element_type=F32)
                       + jnp.dot(pmat, comb_lo, preferred_element_type=F32))
        y_ref[...] = jnp.zeros(y_ref.shape, F32)

    grp = e // EXPERTS_PER_GROUP
    first = tiles_ref[grp]

    def expert(j, carry):
        base = pl.multiple_of((first + j) * MOE_SUB, MOE_SUB)
        xg = xg_ref[pl.ds(base, MOE_SUB), :]
        hg = jnp.dot(xg, wg_ref[0], preferred_element_type=F32)
        hu = jnp.dot(xg, wu_ref[0], preferred_element_type=F32)
        hdn = (_silu(hg) * hu).astype(BF16)
        lane_s = lax.broadcasted_iota(jnp.int32, (MOE_SUB, LANES), 1)
        cw = jnp.sum(jnp.where(lane_s == N_GROUPS + e, cw_ref[pl.ds(base, MOE_SUB), :], 0.0),
                     axis=1, keepdims=True)
        y_ref[pl.ds(base, MOE_SUB), :] += cw * jnp.dot(hdn, wd_ref[0], preferred_element_type=F32)
        return carry
    lax.fori_loop(0, tiles_ref[N_GROUPS + grp], expert, 0)

    @pl.when(e == pl.num_programs(1) - 1)
    def _():
        pick = slot_ref[...] == lax.broadcasted_iota(jnp.int32, (tm, slots), 1).astype(F32)
        pmat_t = jnp.where(pick, 1.0, 0.0).astype(BF16)
        moe = jnp.dot(pmat_t, y_ref[...].astype(BF16), preferred_element_type=F32)
        o_ref[...] = _layer_norm(alpha * x_ref[...] + moe, g_ref[...], b_ref[...])


def moe_layer(x2, wr, br, wg, wu, wd, g, b, *, alpha, tm=1024):
    T, D = x2.shape
    E, _, F = wg.shape
    tm = min(tm, T)
    slots = tm + N_GROUPS * MOE_SUB
    full = lambda a: pl.BlockSpec(a.shape, lambda i, e: (0,) * a.ndim)
    return pl.pallas_call(
        functools.partial(_moe_kernel, alpha=alpha),
        grid=(T // tm, E),
        in_specs=[pl.BlockSpec((tm, D), lambda i, e: (i, 0)), full(wr), full(br),
                  pl.BlockSpec((1, D, F), lambda i, e: (e, 0, 0)),
                  pl.BlockSpec((1, D, F), lambda i, e: (e, 0, 0)),
                  pl.BlockSpec((1, F, D), lambda i, e: (e, 0, 0)),
                  full(g), full(b)],
        out_specs=pl.BlockSpec((tm, D), lambda i, e: (i, 0)),
        out_shape=jax.ShapeDtypeStruct((T, D), F32),
        scratch_shapes=[pltpu.VMEM((tm, tm), BF16),
                        pltpu.VMEM((tm, 1), F32),
                        pltpu.VMEM((slots, D), BF16),
                        pltpu.VMEM((slots, LANES), F32),
                        pltpu.VMEM((slots, D), F32),
                        pltpu.SMEM((2 * N_GROUPS,), jnp.int32)],
        compiler_params=pltpu.CompilerParams(
            dimension_semantics=("arbitrary", "arbitrary"), vmem_limit_bytes=56 * 1024 * 1024),
        name="moe_layer",
    )(x2, wr, br, wg, wu, wd, g, b)


DEPTH = 1
DEEPNORM_ALPHA = (2.0 * DEPTH) ** 0.25


def _pad_cols(a, width):
    return jnp.pad(a, ((0, 0), (0, width - a.shape[1])))


def kernel(x, w_in, b_in, conv_m, gn_m_gain, w_branch_attn, w_branch_mlstm, w_out, ln1_gain, ln1_bias, w_router_group, b_router_group, w_router_expert, b_router_expert, w_exp_gate, w_exp_up, w_exp_down, ln2_gain, ln2_bias):
    B, S, D = x.shape
    aw = ATT_HEADS * ATT_HEAD_DIM
    iw = IDX_HEADS * IDX_DIM
    qw = MLSTM_HEADS * MLSTM_QK_DIM
    vw = MLSTM_HEADS * MLSTM_V_DIM
    widths = (aw, aw, aw, iw, IDX_DIM, IDX_HEADS, qw, qw, vw, MLSTM_HEADS, MLSTM_HEADS, vw, D, D)
    offs = [0]
    for w_ in widths:
        offs.append(offs[-1] + w_)
    col = lambda k: w_in[:, offs[k]:offs[k + 1]]
    bia = lambda k: b_in[offs[k]:offs[k + 1]]
    (A_Q, A_K, A_V, I_Q, I_K, I_W, M_Q, M_K, M_V, M_I, M_F, M_O, G_A, G_M) = range(14)

    wr = jnp.concatenate([col(A_K), _pad_cols(col(I_K), LANES)], 1).astype(BF16)
    br = jnp.concatenate([bia(A_K), jnp.pad(bia(I_K), (0, LANES - IDX_DIM))])[None, :]
    wc = jnp.concatenate([col(A_Q), col(A_V), col(I_Q), _pad_cols(col(I_W), 8)], 1).T.astype(BF16)
    bc = jnp.concatenate([bia(A_Q), bia(A_V), bia(I_Q), jnp.pad(bia(I_W), (0, 8 - IDX_HEADS))])[:, None]
    gpad = LANES - 2 * MLSTM_HEADS
    wm = jnp.concatenate([col(M_Q), col(M_K), col(M_I),
                          _pad_cols(col(M_F), MLSTM_HEADS + gpad)], 1).astype(BF16)
    bm = jnp.concatenate([bia(M_Q), bia(M_K), bia(M_I), jnp.pad(bia(M_F), (0, gpad))])[None, :]
    wmt = jnp.concatenate([col(M_V), col(M_O), col(M_I), col(M_F)], 1).T.astype(BF16)
    bmt = jnp.concatenate([bia(M_V), bia(M_O), bia(M_I), bia(M_F)])[:, None]
    wgate = jnp.concatenate([col(G_A), col(G_M)], 1).astype(BF16)
    bgate = jnp.concatenate([bia(G_A), bia(G_M)])[None, :]

    half = ATT_HEAD_DIM // 2
    inv = ROPE_THETA ** (-jnp.arange(0, ATT_HEAD_DIM, 2, dtype=F32) / ATT_HEAD_DIM)
    ang = jnp.arange(S, dtype=F32)[:, None] * inv[None, :]
    cos, sin = jnp.cos(ang), jnp.sin(ang)
    cos_r = jnp.tile(cos, (1, LANES // half))
    sin_r = jnp.tile(jnp.concatenate([-sin, sin], 1), (1, LANES // ATT_HEAD_DIM))

    qt, k, vt, qit, ki, wit = attn_projections(x, wr, br, wc, bc, cos_r, sin_r, cos.T, sin.T)
    y_attn = dsa_attention(qt, k, vt, qit, ki, wit, topk=min(IDX_TOPK_MAX, S // 4))

    mqt, mk, mvt, ogt, ifc, ift = mlstm_projections(x, wm, bm, wmt, bmt, conv_m)
    y_mlstm = mlstm_scan(mqt, mk, mvt, ogt, ifc, ift,
                         jnp.broadcast_to(gn_m_gain[:, None], (vw, LANES)))

    x1 = merge_branches(x.reshape(B * S, D), y_attn.reshape(B * S, aw), y_mlstm.reshape(B * S, vw),
                        wgate, bgate, w_branch_attn.astype(BF16), w_branch_mlstm.astype(BF16),
                        w_out.astype(BF16), ln1_gain[None, :], ln1_bias[None, :],
                        alpha=DEEPNORM_ALPHA)

    w_router = _pad_cols(jnp.concatenate([w_router_group, w_router_expert], 1), LANES)
    b_router = jnp.pad(jnp.concatenate([b_router_group, b_router_expert]),
                       (0, LANES - N_GROUPS - N_EXPERTS))[None, :]
    out = moe_layer(x1, w_router, b_router, w_exp_gate.astype(BF16), w_exp_up.astype(BF16),
                    w_exp_down.astype(BF16), ln2_gain[None, :], ln2_bias[None, :],
                    alpha=DEEPNORM_ALPHA)
    return out.reshape(B, S, D)
```

```python
import functools

import jax
import jax.numpy as jnp
from jax import lax
from jax.experimental import pallas as pl
from jax.experimental.pallas import tpu as pltpu

F32 = jnp.float32
BF16 = jnp.bfloat16
NEG_INF = float("-inf")
LOG2_E = 1.4426950408889634

ATT_HEADS = 8
ATT_HEAD_DIM = 64
IDX_HEADS = 4
IDX_DIM = 64
IDX_TOPK_MAX = 256
Q_BLOCK = 128
ROPE_THETA = 10000.0

LANES = 128
BF16_ROWS = 16


def _key_to_f32(u):
    ks = u ^ jnp.int32(-2 ** 31)
    bits = ks ^ ((ks >> 31) & jnp.int32(0x7FFFFFFF))
    return lax.bitcast_convert_type(bits, F32)


def _dsa_kernel(qt_ref, k_ref, vt_ref, qit_ref, ki_ref, wit_ref, o_ref,
                sc_ref, qm_ref, m_ref, l_ref, acc_ref, *, topk, kchunk):
    qb = pl.program_id(1)
    tq = o_ref.shape[1]
    heads = qt_ref.shape[1] // ATT_HEAD_DIM
    step = 2 * LANES
    n_chunk = (qb * tq + tq + kchunk - 1) // kchunk
    qpos = lax.broadcasted_iota(jnp.int32, (1, tq), 1) + qb * tq

    qit = qit_ref[0]
    zpad = jnp.zeros((LANES - IDX_DIM, tq), BF16)
    qi_pair = []
    for p in range(IDX_HEADS // 2):
        cols = [jnp.concatenate([qit[h * IDX_DIM:(h + 1) * IDX_DIM], zpad], axis=0)
                for h in (2 * p, 2 * p + 1)]
        qi_pair.append(jnp.concatenate(cols, axis=1))
    wit = wit_ref[0]

    achunk = min(2 * kchunk, sc_ref.shape[0])
    n_achunk = (qb * tq + tq + achunk - 1) // achunk
    pairs = IDX_HEADS // 2

    def score_body(c, carry):
        off = pl.multiple_of(c * achunk, achunk)

        def qk(i):
            ki = ki_ref[0, pl.ds(off + (i // pairs) * step, step), :]
            return jnp.dot(ki, qi_pair[i % pairs], preferred_element_type=F32)

        n_dots = (achunk // step) * pairs
        ahead = 3
        pending = [qk(i) for i in range(ahead)]
        for t in range(achunk // step):
            tot = None
            for p in range(pairs):
                i = t * pairs + p
                if i + ahead < n_dots:
                    pending.append(qk(i + ahead))
                s2 = pending.pop(0)
                for j in range(2):
                    h = 2 * p + j
                    s = jnp.maximum(s2[:, j * tq:(j + 1) * tq], 0.0) * wit[h:h + 1, :]
                    tot = s if tot is None else tot + s
            kpos = lax.broadcasted_iota(jnp.int32, (step, tq), 0) + (off + t * step)
            sc_ref[pl.ds(off + t * step, step), :] = jnp.where(kpos <= qpos, tot + 0.0, NEG_INF)
        return carry

    lax.fori_loop(0, n_achunk, score_body, 0)

    def count(pred):
        def body(j, acc):
            off = pl.multiple_of(j * kchunk, kchunk)
            for t in range(kchunk // LANES):
                x = sc_ref[pl.ds(off + t * LANES, LANES), :]
                acc = acc + jnp.where(pred(x, off + t * LANES), 1.0, 0.0)
            return acc
        acc = lax.fori_loop(0, n_chunk, body, jnp.zeros((LANES, tq), F32))
        return jnp.sum(acc, axis=0, keepdims=True)

    def bit_body(i, carry):
        u, cgt = carry
        cand = u | (jnp.int32(1) << (31 - i))
        thr = _key_to_f32(cand)
        cnt = count(lambda x, off: x >= thr)
        ok = cnt >= float(topk)
        return jnp.where(ok, cand, u), jnp.where(ok, cgt, cnt)

    u, cgt = lax.fori_loop(0, 32, bit_body,
                           (jnp.zeros((1, tq), jnp.int32), jnp.zeros((1, tq), F32)))
    short = qpos < topk
    tau = jnp.where(short, NEG_INF, _key_to_f32(u))
    need = float(topk) - cgt

    r_i = lax.broadcasted_iota(jnp.int32, (LANES, LANES), 0)
    c_i = lax.broadcasted_iota(jnp.int32, (LANES, LANES), 1)
    tril = jnp.where(c_i <= r_i, 1.0, 0.0).astype(BF16)

    def bias_body(c, seen):
        base = pl.multiple_of(c * achunk, achunk)
        ranks = [jnp.dot(tril, jnp.where(sc_ref[pl.ds(base + t * LANES, LANES), :] == tau,
                                         1.0, 0.0).astype(BF16), preferred_element_type=F32)
                 for t in range(achunk // LANES)]
        for t, rank in enumerate(ranks):
            x = sc_ref[pl.ds(base + t * LANES, LANES), :]
            keep = jnp.logical_or(x > tau, jnp.logical_and(x == tau, rank + seen <= need))
            kpos = lax.broadcasted_iota(jnp.int32, x.shape, 0) + (base + t * LANES)
            keep = jnp.logical_and(keep, kpos <= qpos)
            sc_ref[pl.ds(base + t * LANES, LANES), :] = jnp.where(keep, 0.0, NEG_INF)
            seen = seen + rank[LANES - 1:LANES, :]
        return seen

    lax.fori_loop(0, n_achunk, bias_body, jnp.zeros((1, tq), F32))

    qt = qt_ref[0]
    zrow = jnp.zeros((ATT_HEAD_DIM, tq), BF16)
    for p in range(heads // 2):
        a = qt[(2 * p) * ATT_HEAD_DIM:(2 * p + 1) * ATT_HEAD_DIM]
        b = qt[(2 * p + 1) * ATT_HEAD_DIM:(2 * p + 2) * ATT_HEAD_DIM]
        qm_ref[p] = jnp.concatenate([jnp.concatenate([a, zrow], axis=0),
                                     jnp.concatenate([zrow, b], axis=0)], axis=1)
    m_ref[...] = jnp.full(m_ref.shape, NEG_INF, F32)
    l_ref[...] = jnp.zeros(l_ref.shape, F32)
    acc_ref[...] = jnp.zeros(acc_ref.shape, F32)
    ones_rows = jnp.ones((BF16_ROWS, step), BF16)
    stages = [(sub, p) for sub in range(achunk // step) for p in range(heads // 2)]

    def attn_body(c, carry):
        off = pl.multiple_of(c * achunk, achunk)

        def qk(stage):
            sub, p = stage
            kp = k_ref[0, pl.ds(off + sub * step, step), p * LANES:(p + 1) * LANES]
            return jnp.dot(kp, qm_ref[p], preferred_element_type=F32)

        ahead = 4
        pending = [qk(st) for st in stages[:ahead]]
        for i, (sub, p) in enumerate(stages):
            if i + ahead < len(stages):
                pending.append(qk(stages[i + ahead]))
            s2 = pending.pop(0)
            koff = off + sub * step
            bias = sc_ref[pl.ds(koff, step), :]
            for j in range(2):
                h = 2 * p + j
                s = s2[:, j * tq:(j + 1) * tq] + bias
                m_old = m_ref[h]
                m_new = jnp.maximum(m_old, jnp.max(s, axis=0, keepdims=True))
                m_use = jnp.where(m_new == NEG_INF, 0.0, m_new)
                pexp = jnp.exp2(s - m_use).astype(BF16)
                alpha = jnp.exp2(m_old - m_use)
                vt = jnp.concatenate(
                    [vt_ref[0, h * ATT_HEAD_DIM:(h + 1) * ATT_HEAD_DIM, pl.ds(koff, step)],
                     ones_rows], axis=0)
                pv = jnp.dot(vt, pexp, preferred_element_type=F32)
                acc_ref[h] = alpha * acc_ref[h] + pv[0:ATT_HEAD_DIM]
                l_ref[h] = alpha * l_ref[h] + pv[ATT_HEAD_DIM:ATT_HEAD_DIM + 1]
                m_ref[h] = m_new
        return carry

    lax.fori_loop(0, n_achunk, attn_body, 0)

    out_t = jnp.concatenate([acc_ref[h] / l_ref[h] for h in range(heads)], axis=0)
    o_ref[0] = jnp.transpose(out_t).astype(o_ref.dtype)


def dsa_attention(qt, k, vt, qit, ki, wit, *, topk, kchunk=512):
    B, W, S = qt.shape
    tq = Q_BLOCK
    kchunk = min(kchunk, S)
    heads = W // ATT_HEAD_DIM
    kern = functools.partial(_dsa_kernel, topk=topk, kchunk=kchunk)
    qcol = lambda a: pl.BlockSpec((1, a.shape[1], tq), lambda b, i: (b, 0, i))
    whole = lambda a: pl.BlockSpec((1,) + a.shape[1:], lambda b, i: (b, 0, 0))
    return pl.pallas_call(
        kern,
        grid=(B, S // tq),
        in_specs=[qcol(qt), whole(k), whole(vt), qcol(qit), whole(ki), qcol(wit)],
        out_specs=pl.BlockSpec((1, tq, W), lambda b, i: (b, i, 0)),
        out_shape=jax.ShapeDtypeStruct((B, S, W), BF16),
        scratch_shapes=[
            pltpu.VMEM((S, tq), F32),
            pltpu.VMEM((heads // 2, LANES, 2 * tq), BF16),
            pltpu.VMEM((heads, 1, tq), F32),
            pltpu.VMEM((heads, 1, tq), F32),
            pltpu.VMEM((heads, ATT_HEAD_DIM, tq), F32),
        ],
        compiler_params=pltpu.CompilerParams(
            dimension_semantics=("arbitrary", "arbitrary"),
            vmem_limit_bytes=56 * 1024 * 1024),
        name="dsa_attention",
    )(qt, k, vt, qit, ki, wit)


def _rope_rows(x, cos_t, sin_t):
    lane = lax.broadcasted_iota(jnp.int32, x.shape, 1)
    swapped = jnp.where((lane % ATT_HEAD_DIM) < ATT_HEAD_DIM // 2,
                        pltpu.roll(x, LANES - ATT_HEAD_DIM // 2, 1),
                        pltpu.roll(x, ATT_HEAD_DIM // 2, 1))
    return x * cos_t + swapped * sin_t


def _attn_proj_kernel(x_ref, wr_ref, br_ref, wc_ref, bc_ref, cos_ref, sin_ref, cost_ref, sint_ref,
                      qt_ref, k_ref, vt_ref, qit_ref, ki_ref, wit_ref, *, q_scale, wi_scale):
    xb = x_ref[0].astype(BF16)
    aw = k_ref.shape[2]
    iw = qit_ref.shape[1]
    half = ATT_HEAD_DIM // 2
    cos_t = cos_ref[...]
    sin_t = sin_ref[...]

    pk = jnp.dot(xb, wr_ref[...], preferred_element_type=F32) + br_ref[...]
    for j in range(aw // LANES):
        sl = slice(j * LANES, (j + 1) * LANES)
        k_ref[0, :, sl] = _rope_rows(pk[:, sl], cos_t, sin_t).astype(k_ref.dtype)
    ki_ref[0] = _rope_rows(pk[:, aw:aw + LANES], cos_t, sin_t).astype(ki_ref.dtype)

    pt = lax.dot_general(wc_ref[...], xb, (((1,), (1,)), ((), ())),
                         preferred_element_type=F32) + bc_ref[...]
    ct = cost_ref[...]
    st = sint_ref[...]

    def rope_cols(src0, dst_ref, nheads, scale):
        for h in range(nheads):
            r0 = src0 + h * ATT_HEAD_DIM
            x1 = pt[r0:r0 + half]
            x2 = pt[r0 + half:r0 + 2 * half]
            d0 = h * ATT_HEAD_DIM
            dst_ref[0, d0:d0 + half, :] = ((x1 * ct - x2 * st) * scale).astype(dst_ref.dtype)
            dst_ref[0, d0 + half:d0 + 2 * half, :] = ((x1 * st + x2 * ct) * scale).astype(dst_ref.dtype)

    rope_cols(0, qt_ref, aw // ATT_HEAD_DIM, q_scale)
    vt_ref[0] = pt[aw:2 * aw].astype(vt_ref.dtype)
    rope_cols(2 * aw, qit_ref, iw // IDX_DIM, 1.0)
    wit_ref[0] = pt[2 * aw + iw:] * wi_scale


def attn_projections(x, wr, br, wc, bc, cos_r, sin_r, cos_c, sin_c, *, tm=512):
    B, S, D = x.shape
    aw = ATT_HEADS * ATT_HEAD_DIM
    iw = IDX_HEADS * IDX_DIM
    tm = min(tm, S)
    kern = functools.partial(_attn_proj_kernel, q_scale=ATT_HEAD_DIM ** -0.5 * LOG2_E,
                             wi_scale=IDX_HEADS ** -0.5 * IDX_DIM ** -0.5)
    full = lambda a: pl.BlockSpec(a.shape, lambda b, i: (0,) * a.ndim)
    row = lambda w: pl.BlockSpec((1, tm, w), lambda b, i: (b, i, 0))
    colm = lambda r: pl.BlockSpec((1, r, tm), lambda b, i: (b, 0, i))
    return pl.pallas_call(
        kern,
        grid=(B, S // tm),
        in_specs=[row(D), full(wr), full(br), full(wc), full(bc),
                  pl.BlockSpec((tm, LANES), lambda b, i: (i, 0)),
                  pl.BlockSpec((tm, LANES), lambda b, i: (i, 0)),
                  pl.BlockSpec((ATT_HEAD_DIM // 2, tm), lambda b, i: (0, i)),
                  pl.BlockSpec((ATT_HEAD_DIM // 2, tm), lambda b, i: (0, i))],
        out_specs=[colm(aw), row(aw), colm(aw), colm(iw), row(LANES), colm(8)],
        out_shape=[jax.ShapeDtypeStruct((B, aw, S), BF16),
                   jax.ShapeDtypeStruct((B, S, aw), BF16),
                   jax.ShapeDtypeStruct((B, aw, S), BF16),
                   jax.ShapeDtypeStruct((B, iw, S), BF16),
                   jax.ShapeDtypeStruct((B, S, LANES), BF16),
                   jax.ShapeDtypeStruct((B, 8, S), F32)],
        compiler_params=pltpu.CompilerParams(
            dimension_semantics=("arbitrary", "arbitrary"),
            vmem_limit_bytes=48 * 1024 * 1024),
        name="attn_projections",
    )(x, wr, br, wc, bc, cos_r, sin_r, cos_c, sin_c)


MLSTM_HEADS = 8
MLSTM_QK_DIM = 64
MLSTM_V_DIM = 128
MLSTM_CHUNK = 128
CONV_WIDTH = 4
HALO = 8


def _silu(x):
    return x / (1.0 + jnp.exp(-x))


def _sigmoid(x):
    return 1.0 / (1.0 + jnp.exp(-x))


def _log_sigmoid(x):
    return jnp.minimum(x, 0.0) - jnp.log(1.0 + jnp.exp(-jnp.abs(x)))


def _mlstm_proj_kernel(x_ref, w_ref, b_ref, wt_ref, bt_ref, conv_ref,
                       mqt_ref, mk_ref, mvt_ref, ogt_ref, ifc_ref, ift_ref, ext_ref, *, q_scale):
    i = pl.program_id(1)
    tm = x_ref.shape[1]
    qkw = 2 * mk_ref.shape[2]
    vw = mvt_ref.shape[1]
    nh = ift_ref.shape[1] // 2
    xb = x_ref[0].astype(BF16)

    @pl.when(i == 0)
    def _():
        ext_ref[0:HALO, :] = jnp.zeros((HALO, qkw), F32)

    pqk = jnp.dot(xb, w_ref[:, 0:qkw], preferred_element_type=F32) + b_ref[:, 0:qkw]
    ext_ref[HALO:HALO + tm, :] = pqk
    acc = None
    for j in range(CONV_WIDTH):
        term = ext_ref[pl.ds(HALO - CONV_WIDTH + 1 + j, tm), :] * conv_ref[j:j + 1, :]
        acc = term if acc is None else acc + term
    ext_ref[0:HALO, :] = pqk[tm - HALO:tm, :]
    qk = _silu(acc)
    mqt_ref[0] = jnp.transpose(qk[:, 0:qkw // 2] * q_scale).astype(mqt_ref.dtype)
    mk_ref[0] = qk[:, qkw // 2:qkw].astype(mk_ref.dtype)

    pg = jnp.dot(xb, w_ref[:, qkw:], preferred_element_type=F32) + b_ref[:, qkw:]
    lane = lax.broadcasted_iota(jnp.int32, pg.shape, 1)
    ifc_ref[0] = jnp.where(lane < nh, pg, _log_sigmoid(pg))

    pt = lax.dot_general(wt_ref[...], xb, (((1,), (1,)), ((), ())),
                         preferred_element_type=F32) + bt_ref[...]
    mvt_ref[0] = pt[0:vw].astype(mvt_ref.dtype)
    ogt_ref[0] = _sigmoid(pt[vw:2 * vw]).astype(ogt_ref.dtype)
    gates = pt[2 * vw:]
    rowi = lax.broadcasted_iota(jnp.int32, gates.shape, 0)
    ift_ref[0] = jnp.where(rowi < nh, gates, _log_sigmoid(gates))


def mlstm_projections(x, w, b, wt, bt, conv, *, tm=512):
    B, S, D = x.shape
    qw = MLSTM_HEADS * MLSTM_QK_DIM
    vw = MLSTM_HEADS * MLSTM_V_DIM
    tm = min(tm, S)
    kern = functools.partial(_mlstm_proj_kernel, q_scale=MLSTM_QK_DIM ** -0.5)
    full = lambda a: pl.BlockSpec(a.shape, lambda b_, i: (0,) * a.ndim)
    row = lambda w_: pl.BlockSpec((1, tm, w_), lambda b_, i: (b_, i, 0))
    colm = lambda r: pl.BlockSpec((1, r, tm), lambda b_, i: (b_, 0, i))
    return pl.pallas_call(
        kern,
        grid=(B, S // tm),
        in_specs=[row(D), full(w), full(b), full(wt), full(bt), full(conv)],
        out_specs=[colm(qw), row(qw), colm(vw), colm(vw), row(LANES), colm(2 * MLSTM_HEADS)],
        out_shape=[jax.ShapeDtypeStruct((B, qw, S), BF16),
                   jax.ShapeDtypeStruct((B, S, qw), BF16),
                   jax.ShapeDtypeStruct((B, vw, S), BF16),
                   jax.ShapeDtypeStruct((B, vw, S), BF16),
                   jax.ShapeDtypeStruct((B, S, LANES), F32),
                   jax.ShapeDtypeStruct((B, 2 * MLSTM_HEADS, S), F32)],
        scratch_shapes=[pltpu.VMEM((HALO + tm, 2 * qw), F32)],
        compiler_params=pltpu.CompilerParams(
            dimension_semantics=("arbitrary", "arbitrary"),
            vmem_limit_bytes=56 * 1024 * 1024),
        name="mlstm_projections",
    )(x, w, b, wt, bt, conv)


GN_EPS = 1e-6


def _mlstm_kernel(mqt_ref, mk_ref, mvt_ref, ogt_ref, ifc_ref, ift_ref, gain_ref, y_ref,
                  ct_ref, m_ref):
    c = pl.program_id(1)
    L = mk_ref.shape[1]
    nh = ift_ref.shape[1] // 2
    dk = mk_ref.shape[2] // nh
    dv = mvt_ref.shape[1] // nh

    @pl.when(c == 0)
    def _():
        ct_ref[...] = jnp.zeros(ct_ref.shape, F32)
        m_ref[...] = jnp.zeros(m_ref.shape, F32)

    r_i = lax.broadcasted_iota(jnp.int32, (L, L), 0)
    c_i = lax.broadcasted_iota(jnp.int32, (L, L), 1)
    causal_t = r_i <= c_i
    tril = jnp.where(c_i <= r_i, 1.0, 0.0)
    triu = jnp.where(causal_t, 1.0, 0.0)
    ifc = ifc_ref[0]
    ift = ift_ref[0]
    b_cols = jnp.dot(tril, ifc, preferred_element_type=F32, precision=lax.Precision.HIGHEST)
    b_rows = jnp.dot(ift, triu, preferred_element_type=F32, precision=lax.Precision.HIGHEST)
    ones_rows = jnp.where(lax.broadcasted_iota(jnp.int32, (BF16_ROWS, L), 0) == 0,
                          1.0, 0.0).astype(BF16)
    first_half_rows = lax.broadcasted_iota(jnp.int32, (2 * dk, L), 0) < dk
    first_half_lanes = lax.broadcasted_iota(jnp.int32, (L, 2 * dk), 1) < dk

    k_pair, q_heads, s_heads, inter_heads = [], [], [], []
    for p in range(nh // 2):
        k_pair.append(mk_ref[0, :, 2 * p * dk:2 * (p + 1) * dk])
        qt_pair = mqt_ref[0, 2 * p * dk:2 * (p + 1) * dk, :]
        zero = jnp.zeros_like(qt_pair)
        q_heads.append(jnp.where(first_half_rows, qt_pair, zero))
        q_heads.append(jnp.where(first_half_rows, zero, qt_pair))
    for h in range(nh):
        s_heads.append(jnp.dot(k_pair[h // 2], q_heads[h], preferred_element_type=F32))
        inter_heads.append(jnp.dot(ct_ref[h].astype(BF16), q_heads[h],
                                   preferred_element_type=F32))

    for h in range(nh):
        m = m_ref[h]
        a_col = ifc[:, h:h + 1] - b_cols[:, nh + h:nh + h + 1]
        amat = jnp.where(causal_t, a_col, NEG_INF)
        big_m = jnp.maximum(m, jnp.max(amat, axis=0, keepdims=True))
        decay = jnp.exp(amat - big_m)
        w_inter = jnp.exp(m - big_m)
        s = (s_heads[h] * decay).astype(BF16)
        vt_ext = jnp.concatenate([mvt_ref[0, h * dv:(h + 1) * dv, :], ones_rows], axis=0)
        tot = jnp.dot(vt_ext, s, preferred_element_type=F32) + w_inter * inter_heads[h]
        den = tot[dv:dv + 1]
        b_row = b_rows[nh + h:nh + h + 1, :]
        hh = tot[0:dv] / jnp.maximum(jnp.abs(den), jnp.exp(-(b_row + big_m)))
        mu = jnp.mean(hh, axis=0, keepdims=True)
        xc = hh - mu
        var = jnp.mean(xc * xc, axis=0, keepdims=True)
        hn = (xc * lax.rsqrt(var + GN_EPS) * gain_ref[h * dv:(h + 1) * dv, :]
              * ogt_ref[0, h * dv:(h + 1) * dv, :].astype(F32))
        y_ref[0, :, h * dv:(h + 1) * dv] = jnp.transpose(hn).astype(y_ref.dtype)

        b_last = b_cols[L - 1:L, nh + h:nh + h + 1]
        g_col = b_last + a_col
        m_new = jnp.maximum(b_last + m, jnp.max(g_col, axis=0, keepdims=True))
        carry = jnp.exp(b_last + m - m_new)
        head_lanes = first_half_lanes if h % 2 == 0 else jnp.logical_not(first_half_lanes)
        wk = jnp.where(head_lanes, k_pair[h // 2].astype(F32) * jnp.exp(g_col - m_new),
                       0.0).astype(BF16)
        ct_ref[h] = carry * ct_ref[h] + jnp.dot(vt_ext, wk, preferred_element_type=F32)
        m_ref[h] = m_new


def mlstm_scan(mqt, mk, mvt, ogt, ifc, ift, gain_b):
    B, S, qw = mk.shape
    vw = mvt.shape[1]
    L = min(MLSTM_CHUNK, S)
    nh = ift.shape[1] // 2
    row = lambda w_: pl.BlockSpec((1, L, w_), lambda b_, c: (b_, c, 0))
    colm = lambda r: pl.BlockSpec((1, r, L), lambda b_, c: (b_, 0, c))
    return pl.pallas_call(
        _mlstm_kernel,
        grid=(B, S // L),
        in_specs=[colm(qw), row(qw), colm(vw), colm(vw), row(LANES), colm(2 * nh),
                  pl.BlockSpec(gain_b.shape, lambda b_, c: (0, 0))],
        out_specs=row(vw),
        out_shape=jax.ShapeDtypeStruct((B, S, vw), BF16),
        scratch_shapes=[pltpu.VMEM((nh, vw // nh + BF16_ROWS, 2 * qw // nh), F32),
                        pltpu.VMEM((nh, 1, 1), F32)],
        compiler_params=pltpu.CompilerParams(
            dimension_semantics=("arbitrary", "arbitrary")),
        name="mlstm_scan",
    )(mqt, mk, mvt, ogt, ifc, ift, gain_b)


LN_EPS = 1e-5


def _layer_norm(z, gain, bias):
    mu = jnp.mean(z, axis=1, keepdims=True)
    zc = z - mu
    var = jnp.mean(zc * zc, axis=1, keepdims=True)
    return zc * lax.rsqrt(var + LN_EPS) * gain + bias


def _merge_kernel(x_ref, ya_ref, ym_ref, wg_ref, bg_ref, wa_ref, wm_ref, wo_ref, g_ref, b_ref,
                  o_ref, *, alpha):
    x = x_ref[...]
    xb = x.astype(BF16)
    d = x.shape[1]
    ga = _sigmoid(jnp.dot(xb, wg_ref[:, 0:d], preferred_element_type=F32) + bg_ref[:, 0:d])
    merged = ga * jnp.dot(ya_ref[...], wa_ref[...], preferred_element_type=F32)
    gm = _sigmoid(jnp.dot(xb, wg_ref[:, d:2 * d], preferred_element_type=F32) + bg_ref[:, d:2 * d])
    merged = merged + gm * jnp.dot(ym_ref[...], wm_ref[...], preferred_element_type=F32)
    z = alpha * x + jnp.dot(merged.astype(BF16), wo_ref[...], preferred_element_type=F32)
    o_ref[...] = _layer_norm(z, g_ref[...], b_ref[...])


def merge_branches(x2, ya, ym, wg, bg, wa, wm, wo, g, b, *, alpha, tm=512):
    T, D = x2.shape
    tm = min(tm, T)
    full = lambda a: pl.BlockSpec(a.shape, lambda i: (0,) * a.ndim)
    row = lambda w_: pl.BlockSpec((tm, w_), lambda i: (i, 0))
    return pl.pallas_call(
        functools.partial(_merge_kernel, alpha=alpha),
        grid=(T // tm,),
        in_specs=[row(D), row(ya.shape[1]), row(ym.shape[1]), full(wg), full(bg), full(wa),
                  full(wm), full(wo), full(g), full(b)],
        out_specs=row(D),
        out_shape=jax.ShapeDtypeStruct((T, D), F32),
        compiler_params=pltpu.CompilerParams(
            dimension_semantics=("arbitrary",), vmem_limit_bytes=48 * 1024 * 1024),
        name="merge_branches",
    )(x2, ya, ym, wg, bg, wa, wm, wo, g, b)


N_GROUPS = 4
EXPERTS_PER_GROUP = 4
N_EXPERTS = N_GROUPS * EXPERTS_PER_GROUP


def _first_lane_of_max(vals, vmax, lane):
    return jnp.min(jnp.where(vals == vmax, lane, LANES), axis=1, keepdims=True)


MOE_SUB = 128


def _moe_kernel(x_ref, wr_ref, br_ref, wg_ref, wu_ref, wd_ref, g_ref, b_ref, o_ref,
                tril_ref, slot_ref, xg_ref, cw_ref, y_ref, tiles_ref, *, alpha):
    w = pl.program_id(0)
    e = pl.program_id(1)
    tm = x_ref.shape[0]
    slots = xg_ref.shape[0]
    lane = lax.broadcasted_iota(jnp.int32, (tm, LANES), 1)

    @pl.when(jnp.logical_and(w == 0, e == 0))
    def _():
        r_i = lax.broadcasted_iota(jnp.int32, (tm, tm), 0)
        c_i = lax.broadcasted_iota(jnp.int32, (tm, tm), 1)
        tril_ref[...] = jnp.where(c_i < r_i, 1.0, 0.0).astype(BF16)

    @pl.when(e == 0)
    def _():
        x = x_ref[...]
        logits = jnp.dot(x, wr_ref[...], preferred_element_type=F32,
                         precision=lax.Precision.HIGHEST) + br_ref[...]
        g = jnp.where(lane < N_GROUPS, logits, NEG_INF)
        gmax = jnp.max(g, axis=1, keepdims=True)
        g_w = 1.0 / jnp.sum(jnp.exp(g - gmax), axis=1, keepdims=True)
        g_sel = _first_lane_of_max(g, gmax, lane)
        lo = N_GROUPS + EXPERTS_PER_GROUP * g_sel
        ev = jnp.where(jnp.logical_and(lane >= lo, lane < lo + EXPERTS_PER_GROUP), logits, NEG_INF)
        v1 = jnp.max(ev, axis=1, keepdims=True)
        i1 = _first_lane_of_max(ev, v1, lane)
        ev2 = jnp.where(lane == i1, NEG_INF, ev)
        v2 = jnp.max(ev2, axis=1, keepdims=True)
        i2 = _first_lane_of_max(ev2, v2, lane)
        r = jnp.exp(v2 - v1)
        p1 = 1.0 / (1.0 + r)
        p2 = r / (1.0 + r)
        comb = jnp.where(lane == i1, g_w * p1, 0.0) + jnp.where(lane == i2, g_w * p2, 0.0)
        comb_hi = comb.astype(BF16)
        comb_lo = (comb - comb_hi.astype(F32)).astype(BF16)

        onehot = jnp.where(lane == g_sel, 1.0, 0.0)
        before = jnp.dot(tril_ref[...], onehot.astype(BF16), preferred_element_type=F32)
        rank = jnp.sum(jnp.where(lane == g_sel, before, 0.0), axis=1, keepdims=True)
        total = jnp.sum(onehot, axis=0, keepdims=True)
        lane1 = lax.broadcasted_iota(jnp.int32, (1, LANES), 1)
        start = jnp.zeros((tm, 1), F32)
        first_tile = jnp.int32(0)
        for grp in range(N_GROUPS):
            n_tok = jnp.sum(jnp.where(lane1 == grp, total, 0.0)).astype(jnp.int32)
            n_tile = (n_tok + MOE_SUB - 1) // MOE_SUB
            tiles_ref[grp] = first_tile
            tiles_ref[N_GROUPS + grp] = n_tile
            start = jnp.where(g_sel == grp, (first_tile * MOE_SUB).astype(F32), start)
            first_tile = first_tile + n_tile
        slot = start + rank
        slot_ref[...] = slot
        slot_row = jnp.transpose(jnp.broadcast_to(slot, (tm, LANES)))[0:1]
        pick = slot_row == lax.broadcasted_iota(jnp.int32, (slots, tm), 0).astype(F32)
        pmat = jnp.where(pick, 1.0, 0.0).astype(BF16)
        xg_ref[...] = jnp.dot(pmat, x.astype(BF16), preferred_element_type=F32).astype(BF16)
        cw_ref[...] = (jnp.dot(pmat, comb_hi, preferred_element_type=F32)
                       + jnp.dot(pmat, comb_lo, preferred_element_type=F32))
        y_ref[...] = jnp.zeros(y_ref.shape, F32)

    grp = e // EXPERTS_PER_GROUP
    first = tiles_ref[grp]

    def expert(j, carry):
        base = pl.multiple_of((first + j) * MOE_SUB, MOE_SUB)
        xg = xg_ref[pl.ds(base, MOE_SUB), :]
        hg = jnp.dot(xg, wg_ref[0], preferred_element_type=F32)
        hu = jnp.dot(xg, wu_ref[0], preferred_element_type=F32)
        hdn = (_silu(hg) * hu).astype(BF16)
        lane_s = lax.broadcasted_iota(jnp.int32, (MOE_SUB, LANES), 1)
        cw = jnp.sum(jnp.where(lane_s == N_GROUPS + e, cw_ref[pl.ds(base, MOE_SUB), :], 0.0),
                     axis=1, keepdims=True)
        y_ref[pl.ds(base, MOE_SUB), :] += cw * jnp.dot(hdn, wd_ref[0], preferred_element_type=F32)
        return carry
    lax.fori_loop(0, tiles_ref[N_GROUPS + grp], expert, 0)

    @pl.when(e == pl.num_programs(1) - 1)
    def _():
        pick = slot_ref[...] == lax.broadcasted_iota(jnp.int32, (tm, slots), 1).astype(F32)
        pmat_t = jnp.where(pick, 1.0, 0.0).astype(BF16)
        moe = jnp.dot(pmat_t, y_ref[...].astype(BF16), preferred_element_type=F32)
        o_ref[...] = _layer_norm(alpha * x_ref[...] + moe, g_ref[...], b_ref[...])


def moe_layer(x2, wr, br, wg, wu, wd, g, b, *, alpha, tm=1024):
    T, D = x2.shape
    E, _, F = wg.shape
    tm = min(tm, T)
    slots = tm + N_GROUPS * MOE_SUB
    full = lambda a: pl.BlockSpec(a.shape, lambda i, e: (0,) * a.ndim)
    return pl.pallas_call(
        functools.partial(_moe_kernel, alpha=alpha),
        grid=(T // tm, E),
        in_specs=[pl.BlockSpec((tm, D), lambda i, e: (i, 0)), full(wr), full(br),
                  pl.BlockSpec((1, D, F), lambda i, e: (e, 0, 0)),
                  pl.BlockSpec((1, D, F), lambda i, e: (e, 0, 0)),
                  pl.BlockSpec((1, F, D), lambda i, e: (e, 0, 0)),
                  full(g), full(b)],
        out_specs=pl.BlockSpec((tm, D), lambda i, e: (i, 0)),
        out_shape=jax.ShapeDtypeStruct((T, D), F32),
        scratch_shapes=[pltpu.VMEM((tm, tm), BF16),
                        pltpu.VMEM((tm, 1), F32),
                        pltpu.VMEM((slots, D), BF16),
                        pltpu.VMEM((slots, LANES), F32),
                        pltpu.VMEM((slots, D), F32),
                        pltpu.SMEM((2 * N_GROUPS,), jnp.int32)],
        compiler_params=pltpu.CompilerParams(
            dimension_semantics=("arbitrary", "arbitrary"), vmem_limit_bytes=56 * 1024 * 1024),
        name="moe_layer",
    )(x2, wr, br, wg, wu, wd, g, b)


DEPTH = 1
DEEPNORM_ALPHA = (2.0 * DEPTH) ** 0.25


def _pad_cols(a, width):
    return jnp.pad(a, ((0, 0), (0, width - a.shape[1])))


def kernel(x, w_in, b_in, conv_m, gn_m_gain, w_branch_attn, w_branch_mlstm, w_out, ln1_gain, ln1_bias, w_router_group, b_router_group, w_router_expert, b_router_expert, w_exp_gate, w_exp_up, w_exp_down, ln2_gain, ln2_bias):
    B, S, D = x.shape
    aw = ATT_HEADS * ATT_HEAD_DIM
    iw = IDX_HEADS * IDX_DIM
    qw = MLSTM_HEADS * MLSTM_QK_DIM
    vw = MLSTM_HEADS * MLSTM_V_DIM
    widths = (aw, aw, aw, iw, IDX_DIM, IDX_HEADS, qw, qw, vw, MLSTM_HEADS, MLSTM_HEADS, vw, D, D)
    offs = [0]
    for w_ in widths:
        offs.append(offs[-1] + w_)
    col = lambda k: w_in[:, offs[k]:offs[k + 1]]
    bia = lambda k: b_in[offs[k]:offs[k + 1]]
    (A_Q, A_K, A_V, I_Q, I_K, I_W, M_Q, M_K, M_V, M_I, M_F, M_O, G_A, G_M) = range(14)

    wr = jnp.concatenate([col(A_K), _pad_cols(col(I_K), LANES)], 1).astype(BF16)
    br = jnp.concatenate([bia(A_K), jnp.pad(bia(I_K), (0, LANES - IDX_DIM))])[None, :]
    wc = jnp.concatenate([col(A_Q), col(A_V), col(I_Q), _pad_cols(col(I_W), 8)], 1).T.astype(BF16)
    bc = jnp.concatenate([bia(A_Q), bia(A_V), bia(I_Q), jnp.pad(bia(I_W), (0, 8 - IDX_HEADS))])[:, None]
    gpad = LANES - 2 * MLSTM_HEADS
    wm = jnp.concatenate([col(M_Q), col(M_K), col(M_I),
                          _pad_cols(col(M_F), MLSTM_HEADS + gpad)], 1).astype(BF16)
    bm = jnp.concatenate([bia(M_Q), bia(M_K), bia(M_I), jnp.pad(bia(M_F), (0, gpad))])[None, :]
    wmt = jnp.concatenate([col(M_V), col(M_O), col(M_I), col(M_F)], 1).T.astype(BF16)
    bmt = jnp.concatenate([bia(M_V), bia(M_O), bia(M_I), bia(M_F)])[:, None]
    wgate = jnp.concatenate([col(G_A), col(G_M)], 1).astype(BF16)
    bgate = jnp.concatenate([bia(G_A), bia(G_M)])[None, :]

    half = ATT_HEAD_DIM // 2
    inv = ROPE_THETA ** (-jnp.arange(0, ATT_HEAD_DIM, 2, dtype=F32) / ATT_HEAD_DIM)
    ang = jnp.arange(S, dtype=F32)[:, None] * inv[None, :]
    cos, sin = jnp.cos(ang), jnp.sin(ang)
    cos_r = jnp.tile(cos, (1, LANES // half))
    sin_r = jnp.tile(jnp.concatenate([-sin, sin], 1), (1, LANES // ATT_HEAD_DIM))

    qt, k, vt, qit, ki, wit = attn_projections(x, wr, br, wc, bc, cos_r, sin_r, cos.T, sin.T)
    y_attn = dsa_attention(qt, k, vt, qit, ki, wit, topk=min(IDX_TOPK_MAX, S // 4))

    mqt, mk, mvt, ogt, ifc, ift = mlstm_projections(x, wm, bm, wmt, bmt, conv_m)
    y_mlstm = mlstm_scan(mqt, mk, mvt, ogt, ifc, ift,
                         jnp.broadcast_to(gn_m_gain[:, None], (vw, LANES)))

    x1 = merge_branches(x.reshape(B * S, D), y_attn.reshape(B * S, aw), y_mlstm.reshape(B * S, vw),
                        wgate, bgate, w_branch_attn.astype(BF16), w_branch_mlstm.astype(BF16),
                        w_out.astype(BF16), ln1_gain[None, :], ln1_bias[None, :],
                        alpha=DEEPNORM_ALPHA)

    w_router = _pad_cols(jnp.concatenate([w_router_group, w_router_expert], 1), LANES)
    b_router = jnp.pad(jnp.concatenate([b_router_group, b_router_expert]),
                       (0, LANES - N_GROUPS - N_EXPERTS))[None, :]
    out = moe_layer(x1, w_router, b_router, w_exp_gate.astype(BF16), w_exp_up.astype(BF16),
                    w_exp_down.astype(BF16), ln2_gain[None, :], ln2_bias[None, :],
                    alpha=DEEPNORM_ALPHA)
    return out.reshape(B, S, D)
```

```python
import functools

import jax
import jax.numpy as jnp
from jax import lax
from jax.experimental import pallas as pl
from jax.experimental.pallas import tpu as pltpu

F32 = jnp.float32
BF16 = jnp.bfloat16
NEG_INF = float("-inf")
LOG2_E = 1.4426950408889634

ATT_HEADS = 8
ATT_HEAD_DIM = 64
IDX_HEADS = 4
IDX_DIM = 64
IDX_TOPK_MAX = 256
Q_BLOCK = 128
ROPE_THETA = 10000.0

LANES = 128
BF16_ROWS = 16


def _key_to_f32(u):
    ks = u ^ jnp.int32(-2 ** 31)
    bits = ks ^ ((ks >> 31) & jnp.int32(0x7FFFFFFF))
    return lax.bitcast_convert_type(bits, F32)


def _dsa_kernel(qt_ref, k_ref, vt_ref, qit_ref, ki_ref, wit_ref, o_ref,
                sc_ref, qm_ref, m_ref, l_ref, acc_ref, *, topk, kchunk):
    qb = pl.program_id(1)
    tq = o_ref.shape[1]
    heads = qt_ref.shape[1] // ATT_HEAD_DIM
    step = 2 * LANES
    n_chunk = (qb * tq + tq + kchunk - 1) // kchunk
    qpos = lax.broadcasted_iota(jnp.int32, (1, tq), 1) + qb * tq

    qit = qit_ref[0]
    zpad = jnp.zeros((LANES - IDX_DIM, tq), BF16)
    qi_pair = []
    for p in range(IDX_HEADS // 2):
        cols = [jnp.concatenate([qit[h * IDX_DIM:(h + 1) * IDX_DIM], zpad], axis=0)
                for h in (2 * p, 2 * p + 1)]
        qi_pair.append(jnp.concatenate(cols, axis=1))
    wit = wit_ref[0]

    achunk = min(2 * kchunk, sc_ref.shape[0])
    n_achunk = (qb * tq + tq + achunk - 1) // achunk
    pairs = IDX_HEADS // 2

    def score_body(c, carry):
        off = pl.multiple_of(c * achunk, achunk)

        def qk(i):
            ki = ki_ref[0, pl.ds(off + (i // pairs) * step, step), :]
            return jnp.dot(ki, qi_pair[i % pairs], preferred_element_type=F32)

        n_dots = (achunk // step) * pairs
        ahead = 3
        pending = [qk(i) for i in range(ahead)]
        for t in range(achunk // step):
            tot = None
            for p in range(pairs):
                i = t * pairs + p
                if i + ahead < n_dots:
                    pending.append(qk(i + ahead))
                s2 = pending.pop(0)
                for j in range(2):
                    h = 2 * p + j
                    s = jnp.maximum(s2[:, j * tq:(j + 1) * tq], 0.0) * wit[h:h + 1, :]
                    tot = s if tot is None else tot + s
            kpos = lax.broadcasted_iota(jnp.int32, (step, tq), 0) + (off + t * step)
            sc_ref[pl.ds(off + t * step, step), :] = jnp.where(kpos <= qpos, tot + 0.0, NEG_INF)
        return carry

    lax.fori_loop(0, n_achunk, score_body, 0)

    def count(pred):
        def body(j, acc):
            off = pl.multiple_of(j * kchunk, kchunk)
            for t in range(kchunk // LANES):
                x = sc_ref[pl.ds(off + t * LANES, LANES), :]
                acc = acc + jnp.where(pred(x, off + t * LANES), 1.0, 0.0)
            return acc
        acc = lax.fori_loop(0, n_chunk, body, jnp.zeros((LANES, tq), F32))
        return jnp.sum(acc, axis=0, keepdims=True)

    kf = float(topk)
    short = qpos < topk

    def bit_pass(i, state, frozen):
        u, cge, cgt = state
        cand = u | (jnp.int32(1) << (31 - i))
        thr = _key_to_f32(cand)
        cnt = count(lambda x, off: x >= thr)
        ok = jnp.logical_and(cnt >= kf, jnp.logical_not(frozen))
        fail = jnp.logical_and(cnt < kf, jnp.logical_not(frozen))
        return (jnp.where(ok, cand, u), jnp.where(ok, cnt, cge), jnp.where(fail, cnt, cgt))

    c_pos = count(lambda x, off: x > 0.0)
    never = jnp.zeros((1, tq), jnp.bool_)
    state = bit_pass(0, (jnp.zeros((1, tq), jnp.int32), jnp.zeros((1, tq), F32),
                         jnp.zeros((1, tq), F32)), never)
    frozen = jnp.logical_and(c_pos < kf, state[1] >= kf)
    state = (state[0], state[1], jnp.where(frozen, c_pos, state[2]))

    fixed_bits = 20
    state = lax.fori_loop(1, fixed_bits, lambda i, st: bit_pass(i, st, frozen), state)

    def all_settled(st):
        done = jnp.logical_or(jnp.logical_or(short, frozen), st[1] == kf)
        return jnp.min(jnp.where(done, 1.0, 0.0)) > 0.0

    def refine(carry):
        i, st, _ = carry
        st = bit_pass(i, st, frozen)
        st = bit_pass(i + 1, st, frozen)
        return i + 2, st, all_settled(st)

    _, (u, cge, cgt), _ = lax.while_loop(
        lambda c: jnp.logical_and(c[0] < 32, jnp.logical_not(c[2])),
        refine, (jnp.int32(fixed_bits), state, all_settled(state)))
    tau = jnp.where(short, NEG_INF, _key_to_f32(u))
    need = jnp.where(cge == kf, kf, kf - cgt)

    r_i = lax.broadcasted_iota(jnp.int32, (LANES, LANES), 0)
    c_i = lax.broadcasted_iota(jnp.int32, (LANES, LANES), 1)
    tril = jnp.where(c_i <= r_i, 1.0, 0.0).astype(BF16)

    def bias_body(c, seen):
        base = pl.multiple_of(c * achunk, achunk)
        ranks = [jnp.dot(tril, jnp.where(sc_ref[pl.ds(base + t * LANES, LANES), :] == tau,
                                         1.0, 0.0).astype(BF16), preferred_element_type=F32)
                 for t in range(achunk // LANES)]
        for t, rank in enumerate(ranks):
            x = sc_ref[pl.ds(base + t * LANES, LANES), :]
            keep = jnp.logical_or(x > tau, jnp.logical_and(x == tau, rank + seen <= need))
            kpos = lax.broadcasted_iota(jnp.int32, x.shape, 0) + (base + t * LANES)
            keep = jnp.logical_and(keep, kpos <= qpos)
            sc_ref[pl.ds(base + t * LANES, LANES), :] = jnp.where(keep, 0.0, NEG_INF)
            seen = seen + rank[LANES - 1:LANES, :]
        return seen

    lax.fori_loop(0, n_achunk, bias_body, jnp.zeros((1, tq), F32))

    qt = qt_ref[0]
    zrow = jnp.zeros((ATT_HEAD_DIM, tq), BF16)
    for p in range(heads // 2):
        a = qt[(2 * p) * ATT_HEAD_DIM:(2 * p + 1) * ATT_HEAD_DIM]
        b = qt[(2 * p + 1) * ATT_HEAD_DIM:(2 * p + 2) * ATT_HEAD_DIM]
        qm_ref[p] = jnp.concatenate([jnp.concatenate([a, zrow], axis=0),
                                     jnp.concatenate([zrow, b], axis=0)], axis=1)
    m_ref[...] = jnp.full(m_ref.shape, NEG_INF, F32)
    l_ref[...] = jnp.zeros(l_ref.shape, F32)
    acc_ref[...] = jnp.zeros(acc_ref.shape, F32)
    ones_rows = jnp.ones((BF16_ROWS, step), BF16)
    stages = [(sub, p) for sub in range(achunk // step) for p in range(heads // 2)]

    def attn_body(c, carry):
        off = pl.multiple_of(c * achunk, achunk)

        def qk(stage):
            sub, p = stage
            kp = k_ref[0, pl.ds(off + sub * step, step), p * LANES:(p + 1) * LANES]
            return jnp.dot(kp, qm_ref[p], preferred_element_type=F32)

        ahead = 4
        pending = [qk(st) for st in stages[:ahead]]
        for i, (sub, p) in enumerate(stages):
            if i + ahead < len(stages):
                pending.append(qk(stages[i + ahead]))
            s2 = pending.pop(0)
            koff = off + sub * step
            bias = sc_ref[pl.ds(koff, step), :]
            for j in range(2):
                h = 2 * p + j
                s = s2[:, j * tq:(j + 1) * tq] + bias
                m_old = m_ref[h]
                m_new = jnp.maximum(m_old, jnp.max(s, axis=0, keepdims=True))
                m_use = jnp.where(m_new == NEG_INF, 0.0, m_new)
                pexp = jnp.exp2(s - m_use).astype(BF16)
                alpha = jnp.exp2(m_old - m_use)
                vt = jnp.concatenate(
                    [vt_ref[0, h * ATT_HEAD_DIM:(h + 1) * ATT_HEAD_DIM, pl.ds(koff, step)],
                     ones_rows], axis=0)
                pv = jnp.dot(vt, pexp, preferred_element_type=F32)
                acc_ref[h] = alpha * acc_ref[h] + pv[0:ATT_HEAD_DIM]
                l_ref[h] = alpha * l_ref[h] + pv[ATT_HEAD_DIM:ATT_HEAD_DIM + 1]
                m_ref[h] = m_new
        return carry

    lax.fori_loop(0, n_achunk, attn_body, 0)

    out_t = jnp.concatenate([acc_ref[h] / l_ref[h] for h in range(heads)], axis=0)
    o_ref[0] = jnp.transpose(out_t).astype(o_ref.dtype)


def dsa_attention(qt, k, vt, qit, ki, wit, *, topk, kchunk=512):
    B, W, S = qt.shape
    tq = Q_BLOCK
    kchunk = min(kchunk, S)
    heads = W // ATT_HEAD_DIM
    kern = functools.partial(_dsa_kernel, topk=topk, kchunk=kchunk)
    qcol = lambda a: pl.BlockSpec((1, a.shape[1], tq), lambda b, i: (b, 0, i))
    whole = lambda a: pl.BlockSpec((1,) + a.shape[1:], lambda b, i: (b, 0, 0))
    return pl.pallas_call(
        kern,
        grid=(B, S // tq),
        in_specs=[qcol(qt), whole(k), whole(vt), qcol(qit), whole(ki), qcol(wit)],
        out_specs=pl.BlockSpec((1, tq, W), lambda b, i: (b, i, 0)),
        out_shape=jax.ShapeDtypeStruct((B, S, W), BF16),
        scratch_shapes=[
            pltpu.VMEM((S, tq), F32),
            pltpu.VMEM((heads // 2, LANES, 2 * tq), BF16),
            pltpu.VMEM((heads, 1, tq), F32),
            pltpu.VMEM((heads, 1, tq), F32),
            pltpu.VMEM((heads, ATT_HEAD_DIM, tq), F32),
        ],
        compiler_params=pltpu.CompilerParams(
            dimension_semantics=("arbitrary", "arbitrary"),
            vmem_limit_bytes=56 * 1024 * 1024),
        name="dsa_attention",
    )(qt, k, vt, qit, ki, wit)


def _rope_rows(x, cos_t, sin_t):
    lane = lax.broadcasted_iota(jnp.int32, x.shape, 1)
    swapped = jnp.where((lane % ATT_HEAD_DIM) < ATT_HEAD_DIM // 2,
                        pltpu.roll(x, LANES - ATT_HEAD_DIM // 2, 1),
                        pltpu.roll(x, ATT_HEAD_DIM // 2, 1))
    return x * cos_t + swapped * sin_t


def _attn_proj_kernel(x_ref, wr_ref, br_ref, wc_ref, bc_ref, cos_ref, sin_ref, cost_ref, sint_ref,
                      qt_ref, k_ref, vt_ref, qit_ref, ki_ref, wit_ref, *, q_scale, wi_scale):
    xb = x_ref[0].astype(BF16)
    aw = k_ref.shape[2]
    iw = qit_ref.shape[1]
    half = ATT_HEAD_DIM // 2
    cos_t = cos_ref[...]
    sin_t = sin_ref[...]

    pk = jnp.dot(xb, wr_ref[...], preferred_element_type=F32) + br_ref[...]
    for j in range(aw // LANES):
        sl = slice(j * LANES, (j + 1) * LANES)
        k_ref[0, :, sl] = _rope_rows(pk[:, sl], cos_t, sin_t).astype(k_ref.dtype)
    ki_ref[0] = _rope_rows(pk[:, aw:aw + LANES], cos_t, sin_t).astype(ki_ref.dtype)

    pt = lax.dot_general(wc_ref[...], xb, (((1,), (1,)), ((), ())),
                         preferred_element_type=F32) + bc_ref[...]
    ct = cost_ref[...]
    st = sint_ref[...]

    def rope_cols(src0, dst_ref, nheads, scale):
        for h in range(nheads):
            r0 = src0 + h * ATT_HEAD_DIM
            x1 = pt[r0:r0 + half]
            x2 = pt[r0 + half:r0 + 2 * half]
            d0 = h * ATT_HEAD_DIM
            dst_ref[0, d0:d0 + half, :] = ((x1 * ct - x2 * st) * scale).astype(dst_ref.dtype)
            dst_ref[0, d0 + half:d0 + 2 * half, :] = ((x1 * st + x2 * ct) * scale).astype(dst_ref.dtype)

    rope_cols(0, qt_ref, aw // ATT_HEAD_DIM, q_scale)
    vt_ref[0] = pt[aw:2 * aw].astype(vt_ref.dtype)
    rope_cols(2 * aw, qit_ref, iw // IDX_DIM, 1.0)
    wit_ref[0] = pt[2 * aw + iw:] * wi_scale


def attn_projections(x, wr, br, wc, bc, cos_r, sin_r, cos_c, sin_c, *, tm=512):
    B, S, D = x.shape
    aw = ATT_HEADS * ATT_HEAD_DIM
    iw = IDX_HEADS * IDX_DIM
    tm = min(tm, S)
    kern = functools.partial(_attn_proj_kernel, q_scale=ATT_HEAD_DIM ** -0.5 * LOG2_E,
                             wi_scale=IDX_HEADS ** -0.5 * IDX_DIM ** -0.5)
    full = lambda a: pl.BlockSpec(a.shape, lambda b, i: (0,) * a.ndim)
    row = lambda w: pl.BlockSpec((1, tm, w), lambda b, i: (b, i, 0))
    colm = lambda r: pl.BlockSpec((1, r, tm), lambda b, i: (b, 0, i))
    return pl.pallas_call(
        kern,
        grid=(B, S // tm),
        in_specs=[row(D), full(wr), full(br), full(wc), full(bc),
                  pl.BlockSpec((tm, LANES), lambda b, i: (i, 0)),
                  pl.BlockSpec((tm, LANES), lambda b, i: (i, 0)),
                  pl.BlockSpec((ATT_HEAD_DIM // 2, tm), lambda b, i: (0, i)),
                  pl.BlockSpec((ATT_HEAD_DIM // 2, tm), lambda b, i: (0, i))],
        out_specs=[colm(aw), row(aw), colm(aw), colm(iw), row(LANES), colm(8)],
        out_shape=[jax.ShapeDtypeStruct((B, aw, S), BF16),
                   jax.ShapeDtypeStruct((B, S, aw), BF16),
                   jax.ShapeDtypeStruct((B, aw, S), BF16),
                   jax.ShapeDtypeStruct((B, iw, S), BF16),
                   jax.ShapeDtypeStruct((B, S, LANES), BF16),
                   jax.ShapeDtypeStruct((B, 8, S), F32)],
        compiler_params=pltpu.CompilerParams(
            dimension_semantics=("arbitrary", "arbitrary"),
            vmem_limit_bytes=48 * 1024 * 1024),
        name="attn_projections",
    )(x, wr, br, wc, bc, cos_r, sin_r, cos_c, sin_c)


MLSTM_HEADS = 8
MLSTM_QK_DIM = 64
MLSTM_V_DIM = 128
MLSTM_CHUNK = 128
CONV_WIDTH = 4
HALO = 8


def _silu(x):
    return x / (1.0 + jnp.exp(-x))


def _sigmoid(x):
    return 1.0 / (1.0 + jnp.exp(-x))


def _log_sigmoid(x):
    return jnp.minimum(x, 0.0) - jnp.log(1.0 + jnp.exp(-jnp.abs(x)))


def _mlstm_proj_kernel(x_ref, w_ref, b_ref, wt_ref, bt_ref, conv_ref,
                       mqt_ref, mk_ref, mvt_ref, ogt_ref, ifc_ref, ift_ref, ext_ref, *, q_scale):
    i = pl.program_id(1)
    tm = x_ref.shape[1]
    qkw = 2 * mk_ref.shape[2]
    vw = mvt_ref.shape[1]
    nh = ift_ref.shape[1] // 2
    xb = x_ref[0].astype(BF16)

    @pl.when(i == 0)
    def _():
        ext_ref[0:HALO, :] = jnp.zeros((HALO, qkw), F32)

    pqk = jnp.dot(xb, w_ref[:, 0:qkw], preferred_element_type=F32) + b_ref[:, 0:qkw]
    ext_ref[HALO:HALO + tm, :] = pqk
    acc = None
    for j in range(CONV_WIDTH):
        term = ext_ref[pl.ds(HALO - CONV_WIDTH + 1 + j, tm), :] * conv_ref[j:j + 1, :]
        acc = term if acc is None else acc + term
    ext_ref[0:HALO, :] = pqk[tm - HALO:tm, :]
    qk = _silu(acc)
    mqt_ref[0] = jnp.transpose(qk[:, 0:qkw // 2] * q_scale).astype(mqt_ref.dtype)
    mk_ref[0] = qk[:, qkw // 2:qkw].astype(mk_ref.dtype)

    pg = jnp.dot(xb, w_ref[:, qkw:], preferred_element_type=F32) + b_ref[:, qkw:]
    lane = lax.broadcasted_iota(jnp.int32, pg.shape, 1)
    ifc_ref[0] = jnp.where(lane < nh, pg, _log_sigmoid(pg))

    pt = lax.dot_general(wt_ref[...], xb, (((1,), (1,)), ((), ())),
                         preferred_element_type=F32) + bt_ref[...]
    mvt_ref[0] = pt[0:vw].astype(mvt_ref.dtype)
    ogt_ref[0] = _sigmoid(pt[vw:2 * vw]).astype(ogt_ref.dtype)
    gates = pt[2 * vw:]
    rowi = lax.broadcasted_iota(jnp.int32, gates.shape, 0)
    ift_ref[0] = jnp.where(rowi < nh, gates, _log_sigmoid(gates))


def mlstm_projections(x, w, b, wt, bt, conv, *, tm=512):
    B, S, D = x.shape
    qw = MLSTM_HEADS * MLSTM_QK_DIM
    vw = MLSTM_HEADS * MLSTM_V_DIM
    tm = min(tm, S)
    kern = functools.partial(_mlstm_proj_kernel, q_scale=MLSTM_QK_DIM ** -0.5)
    full = lambda a: pl.BlockSpec(a.shape, lambda b_, i: (0,) * a.ndim)
    row = lambda w_: pl.BlockSpec((1, tm, w_), lambda b_, i: (b_, i, 0))
    colm = lambda r: pl.BlockSpec((1, r, tm), lambda b_, i: (b_, 0, i))
    return pl.pallas_call(
        kern,
        grid=(B, S // tm),
        in_specs=[row(D), full(w), full(b), full(wt), full(bt), full(conv)],
        out_specs=[colm(qw), row(qw), colm(vw), colm(vw), row(LANES), colm(2 * MLSTM_HEADS)],
        out_shape=[jax.ShapeDtypeStruct((B, qw, S), BF16),
                   jax.ShapeDtypeStruct((B, S, qw), BF16),
                   jax.ShapeDtypeStruct((B, vw, S), BF16),
                   jax.ShapeDtypeStruct((B, vw, S), BF16),
                   jax.ShapeDtypeStruct((B, S, LANES), F32),
                   jax.ShapeDtypeStruct((B, 2 * MLSTM_HEADS, S), F32)],
        scratch_shapes=[pltpu.VMEM((HALO + tm, 2 * qw), F32)],
        compiler_params=pltpu.CompilerParams(
            dimension_semantics=("arbitrary", "arbitrary"),
            vmem_limit_bytes=56 * 1024 * 1024),
        name="mlstm_projections",
    )(x, w, b, wt, bt, conv)


GN_EPS = 1e-6


def _mlstm_kernel(mqt_ref, mk_ref, mvt_ref, ogt_ref, ifc_ref, ift_ref, gain_ref, y_ref,
                  ct_ref, m_ref):
    c = pl.program_id(1)
    L = mk_ref.shape[1]
    nh = ift_ref.shape[1] // 2
    dk = mk_ref.shape[2] // nh
    dv = mvt_ref.shape[1] // nh

    @pl.when(c == 0)
    def _():
        ct_ref[...] = jnp.zeros(ct_ref.shape, F32)
        m_ref[...] = jnp.zeros(m_ref.shape, F32)

    r_i = lax.broadcasted_iota(jnp.int32, (L, L), 0)
    c_i = lax.broadcasted_iota(jnp.int32, (L, L), 1)
    causal_t = r_i <= c_i
    tril = jnp.where(c_i <= r_i, 1.0, 0.0)
    triu = jnp.where(causal_t, 1.0, 0.0)
    ifc = ifc_ref[0]
    ift = ift_ref[0]
    b_cols = jnp.dot(tril, ifc, preferred_element_type=F32, precision=lax.Precision.HIGHEST)
    b_rows = jnp.dot(ift, triu, preferred_element_type=F32, precision=lax.Precision.HIGHEST)
    ones_rows = jnp.where(lax.broadcasted_iota(jnp.int32, (BF16_ROWS, L), 0) == 0,
                          1.0, 0.0).astype(BF16)
    first_half_rows = lax.broadcasted_iota(jnp.int32, (2 * dk, L), 0) < dk
    first_half_lanes = lax.broadcasted_iota(jnp.int32, (L, 2 * dk), 1) < dk

    k_pair, q_heads, s_heads, inter_heads = [], [], [], []
    for p in range(nh // 2):
        k_pair.append(mk_ref[0, :, 2 * p * dk:2 * (p + 1) * dk])
        qt_pair = mqt_ref[0, 2 * p * dk:2 * (p + 1) * dk, :]
        zero = jnp.zeros_like(qt_pair)
        q_heads.append(jnp.where(first_half_rows, qt_pair, zero))
        q_heads.append(jnp.where(first_half_rows, zero, qt_pair))
    for h in range(nh):
        s_heads.append(jnp.dot(k_pair[h // 2], q_heads[h], preferred_element_type=F32))
        inter_heads.append(jnp.dot(ct_ref[h].astype(BF16), q_heads[h],
                                   preferred_element_type=F32))

    for h in range(nh):
        m = m_ref[h]
        a_col = ifc[:, h:h + 1] - b_cols[:, nh + h:nh + h + 1]
        amat = jnp.where(causal_t, a_col, NEG_INF)
        big_m = jnp.maximum(m, jnp.max(amat, axis=0, keepdims=True))
        decay = jnp.exp(amat - big_m)
        w_inter = jnp.exp(m - big_m)
        s = (s_heads[h] * decay).astype(BF16)
        vt_ext = jnp.concatenate([mvt_ref[0, h * dv:(h + 1) * dv, :], ones_rows], axis=0)
        tot = jnp.dot(vt_ext, s, preferred_element_type=F32) + w_inter * inter_heads[h]
        den = tot[dv:dv + 1]
        b_row = b_rows[nh + h:nh + h + 1, :]
        hh = tot[0:dv] / jnp.maximum(jnp.abs(den), jnp.exp(-(b_row + big_m)))
        mu = jnp.mean(hh, axis=0, keepdims=True)
        xc = hh - mu
        var = jnp.mean(xc * xc, axis=0, keepdims=True)
        hn = (xc * lax.rsqrt(var + GN_EPS) * gain_ref[h * dv:(h + 1) * dv, :]
              * ogt_ref[0, h * dv:(h + 1) * dv, :].astype(F32))
        y_ref[0, :, h * dv:(h + 1) * dv] = jnp.transpose(hn).astype(y_ref.dtype)

        b_last = b_cols[L - 1:L, nh + h:nh + h + 1]
        g_col = b_last + a_col
        m_new = jnp.maximum(b_last + m, jnp.max(g_col, axis=0, keepdims=True))
        carry = jnp.exp(b_last + m - m_new)
        head_lanes = first_half_lanes if h % 2 == 0 else jnp.logical_not(first_half_lanes)
        wk = jnp.where(head_lanes, k_pair[h // 2].astype(F32) * jnp.exp(g_col - m_new),
                       0.0).astype(BF16)
        ct_ref[h] = carry * ct_ref[h] + jnp.dot(vt_ext, wk, preferred_element_type=F32)
        m_ref[h] = m_new


def mlstm_scan(mqt, mk, mvt, ogt, ifc, ift, gain_b):
    B, S, qw = mk.shape
    vw = mvt.shape[1]
    L = min(MLSTM_CHUNK, S)
    nh = ift.shape[1] // 2
    row = lambda w_: pl.BlockSpec((1, L, w_), lambda b_, c: (b_, c, 0))
    colm = lambda r: pl.BlockSpec((1, r, L), lambda b_, c: (b_, 0, c))
    return pl.pallas_call(
        _mlstm_kernel,
        grid=(B, S // L),
        in_specs=[colm(qw), row(qw), colm(vw), colm(vw), row(LANES), colm(2 * nh),
                  pl.BlockSpec(gain_b.shape, lambda b_, c: (0, 0))],
        out_specs=row(vw),
        out_shape=jax.ShapeDtypeStruct((B, S, vw), BF16),
        scratch_shapes=[pltpu.VMEM((nh, vw // nh + BF16_ROWS, 2 * qw // nh), F32),
                        pltpu.VMEM((nh, 1, 1), F32)],
        compiler_params=pltpu.CompilerParams(
            dimension_semantics=("arbitrary", "arbitrary")),
        name="mlstm_scan",
    )(mqt, mk, mvt, ogt, ifc, ift, gain_b)


LN_EPS = 1e-5


def _layer_norm(z, gain, bias):
    mu = jnp.mean(z, axis=1, keepdims=True)
    zc = z - mu
    var = jnp.mean(zc * zc, axis=1, keepdims=True)
    return zc * lax.rsqrt(var + LN_EPS) * gain + bias


def _merge_kernel(x_ref, ya_ref, ym_ref, wg_ref, bg_ref, wa_ref, wm_ref, wo_ref, g_ref, b_ref,
                  o_ref, *, alpha):
    x = x_ref[...]
    xb = x.astype(BF16)
    d = x.shape[1]
    ga = _sigmoid(jnp.dot(xb, wg_ref[:, 0:d], preferred_element_type=F32) + bg_ref[:, 0:d])
    merged = ga * jnp.dot(ya_ref[...], wa_ref[...], preferred_element_type=F32)
    gm = _sigmoid(jnp.dot(xb, wg_ref[:, d:2 * d], preferred_element_type=F32) + bg_ref[:, d:2 * d])
    merged = merged + gm * jnp.dot(ym_ref[...], wm_ref[...], preferred_element_type=F32)
    z = alpha * x + jnp.dot(merged.astype(BF16), wo_ref[...], preferred_element_type=F32)
    o_ref[...] = _layer_norm(z, g_ref[...], b_ref[...])


def merge_branches(x2, ya, ym, wg, bg, wa, wm, wo, g, b, *, alpha, tm=512):
    T, D = x2.shape
    tm = min(tm, T)
    full = lambda a: pl.BlockSpec(a.shape, lambda i: (0,) * a.ndim)
    row = lambda w_: pl.BlockSpec((tm, w_), lambda i: (i, 0))
    return pl.pallas_call(
        functools.partial(_merge_kernel, alpha=alpha),
        grid=(T // tm,),
        in_specs=[row(D), row(ya.shape[1]), row(ym.shape[1]), full(wg), full(bg), full(wa),
                  full(wm), full(wo), full(g), full(b)],
        out_specs=row(D),
        out_shape=jax.ShapeDtypeStruct((T, D), F32),
        compiler_params=pltpu.CompilerParams(
            dimension_semantics=("arbitrary",), vmem_limit_bytes=48 * 1024 * 1024),
        name="merge_branches",
    )(x2, ya, ym, wg, bg, wa, wm, wo, g, b)


N_GROUPS = 4
EXPERTS_PER_GROUP = 4
N_EXPERTS = N_GROUPS * EXPERTS_PER_GROUP


def _first_lane_of_max(vals, vmax, lane):
    return jnp.min(jnp.where(vals == vmax, lane, LANES), axis=1, keepdims=True)


MOE_SUB = 128


def _moe_kernel(x_ref, wr_ref, br_ref, wg_ref, wu_ref, wd_ref, g_ref, b_ref, o_ref,
                tril_ref, slot_ref, xg_ref, cw_ref, y_ref, tiles_ref, *, alpha):
    w = pl.program_id(0)
    e = pl.program_id(1)
    tm = x_ref.shape[0]
    slots = xg_ref.shape[0]
    lane = lax.broadcasted_iota(jnp.int32, (tm, LANES), 1)

    @pl.when(jnp.logical_and(w == 0, e == 0))
    def _():
        r_i = lax.broadcasted_iota(jnp.int32, (tm, tm), 0)
        c_i = lax.broadcasted_iota(jnp.int32, (tm, tm), 1)
        tril_ref[...] = jnp.where(c_i < r_i, 1.0, 0.0).astype(BF16)

    @pl.when(e == 0)
    def _():
        x = x_ref[...]
        logits = jnp.dot(x, wr_ref[...], preferred_element_type=F32,
                         precision=lax.Precision.HIGHEST) + br_ref[...]
        g = jnp.where(lane < N_GROUPS, logits, NEG_INF)
        gmax = jnp.max(g, axis=1, keepdims=True)
        g_w = 1.0 / jnp.sum(jnp.exp(g - gmax), axis=1, keepdims=True)
        g_sel = _first_lane_of_max(g, gmax, lane)
        lo = N_GROUPS + EXPERTS_PER_GROUP * g_sel
        ev = jnp.where(jnp.logical_and(lane >= lo, lane < lo + EXPERTS_PER_GROUP), logits, NEG_INF)
        v1 = jnp.max(ev, axis=1, keepdims=True)
        i1 = _first_lane_of_max(ev, v1, lane)
        ev2 = jnp.where(lane == i1, NEG_INF, ev)
        v2 = jnp.max(ev2, axis=1, keepdims=True)
        i2 = _first_lane_of_max(ev2, v2, lane)
        r = jnp.exp(v2 - v1)
        p1 = 1.0 / (1.0 + r)
        p2 = r / (1.0 + r)
        comb = jnp.where(lane == i1, g_w * p1, 0.0) + jnp.where(lane == i2, g_w * p2, 0.0)
        comb_hi = comb.astype(BF16)
        comb_lo = (comb - comb_hi.astype(F32)).astype(BF16)

        onehot = jnp.where(lane == g_sel, 1.0, 0.0)
        before = jnp.dot(tril_ref[...], onehot.astype(BF16), preferred_element_type=F32)
        rank = jnp.sum(jnp.where(lane == g_sel, before, 0.0), axis=1, keepdims=True)
        total = jnp.sum(onehot, axis=0, keepdims=True)
        lane1 = lax.broadcasted_iota(jnp.int32, (1, LANES), 1)
        start = jnp.zeros((tm, 1), F32)
        first_tile = jnp.int32(0)
        for grp in range(N_GROUPS):
            n_tok = jnp.sum(jnp.where(lane1 == grp, total, 0.0)).astype(jnp.int32)
            n_tile = (n_tok + MOE_SUB - 1) // MOE_SUB
            tiles_ref[grp] = first_tile
            tiles_ref[N_GROUPS + grp] = n_tile
            start = jnp.where(g_sel == grp, (first_tile * MOE_SUB).astype(F32), start)
            first_tile = first_tile + n_tile
        slot = start + rank
        slot_ref[...] = slot
        slot_row = jnp.transpose(jnp.broadcast_to(slot, (tm, LANES)))[0:1]
        pick = slot_row == lax.broadcasted_iota(jnp.int32, (slots, tm), 0).astype(F32)
        pmat = jnp.where(pick, 1.0, 0.0).astype(BF16)
        xg_ref[...] = jnp.dot(pmat, x.astype(BF16), preferred_element_type=F32).astype(BF16)
        cw_ref[...] = (jnp.dot(pmat, comb_hi, preferred_element_type=F32)
                       + jnp.dot(pmat, comb_lo, preferred_element_type=F32))
        y_ref[...] = jnp.zeros(y_ref.shape, F32)

    grp = e // EXPERTS_PER_GROUP
    first = tiles_ref[grp]

    def expert(j, carry):
        base = pl.multiple_of((first + j) * MOE_SUB, MOE_SUB)
        xg = xg_ref[pl.ds(base, MOE_SUB), :]
        hg = jnp.dot(xg, wg_ref[0], preferred_element_type=F32)
        hu = jnp.dot(xg, wu_ref[0], preferred_element_type=F32)
        hdn = (_silu(hg) * hu).astype(BF16)
        lane_s = lax.broadcasted_iota(jnp.int32, (MOE_SUB, LANES), 1)
        cw = jnp.sum(jnp.where(lane_s == N_GROUPS + e, cw_ref[pl.ds(base, MOE_SUB), :], 0.0),
                     axis=1, keepdims=True)
        y_ref[pl.ds(base, MOE_SUB), :] += cw * jnp.dot(hdn, wd_ref[0], preferred_element_type=F32)
        return carry
    lax.fori_loop(0, tiles_ref[N_GROUPS + grp], expert, 0)

    @pl.when(e == pl.num_programs(1) - 1)
    def _():
        pick = slot_ref[...] == lax.broadcasted_iota(jnp.int32, (tm, slots), 1).astype(F32)
        pmat_t = jnp.where(pick, 1.0, 0.0).astype(BF16)
        moe = jnp.dot(pmat_t, y_ref[...].astype(BF16), preferred_element_type=F32)
        o_ref[...] = _layer_norm(alpha * x_ref[...] + moe, g_ref[...], b_ref[...])


def moe_layer(x2, wr, br, wg, wu, wd, g, b, *, alpha, tm=1024):
    T, D = x2.shape
    E, _, F = wg.shape
    tm = min(tm, T)
    slots = tm + N_GROUPS * MOE_SUB
    full = lambda a: pl.BlockSpec(a.shape, lambda i, e: (0,) * a.ndim)
    return pl.pallas_call(
        functools.partial(_moe_kernel, alpha=alpha),
        grid=(T // tm, E),
        in_specs=[pl.BlockSpec((tm, D), lambda i, e: (i, 0)), full(wr), full(br),
                  pl.BlockSpec((1, D, F), lambda i, e: (e, 0, 0)),
                  pl.BlockSpec((1, D, F), lambda i, e: (e, 0, 0)),
                  pl.BlockSpec((1, F, D), lambda i, e: (e, 0, 0)),
                  full(g), full(b)],
        out_specs=pl.BlockSpec((tm, D), lambda i, e: (i, 0)),
        out_shape=jax.ShapeDtypeStruct((T, D), F32),
        scratch_shapes=[pltpu.VMEM((tm, tm), BF16),
                        pltpu.VMEM((tm, 1), F32),
                        pltpu.VMEM((slots, D), BF16),
                        pltpu.VMEM((slots, LANES), F32),
                        pltpu.VMEM((slots, D), F32),
                        pltpu.SMEM((2 * N_GROUPS,), jnp.int32)],
        compiler_params=pltpu.CompilerParams(
            dimension_semantics=("arbitrary", "arbitrary"), vmem_limit_bytes=56 * 1024 * 1024),
        name="moe_layer",
    )(x2, wr, br, wg, wu, wd, g, b)


DEPTH = 1
DEEPNORM_ALPHA = (2.0 * DEPTH) ** 0.25


def _pad_cols(a, width):
    return jnp.pad(a, ((0, 0), (0, width - a.shape[1])))


def kernel(x, w_in, b_in, conv_m, gn_m_gain, w_branch_attn, w_branch_mlstm, w_out, ln1_gain, ln1_bias, w_router_group, b_router_group, w_router_expert, b_router_expert, w_exp_gate, w_exp_up, w_exp_down, ln2_gain, ln2_bias):
    B, S, D = x.shape
    aw = ATT_HEADS * ATT_HEAD_DIM
    iw = IDX_HEADS * IDX_DIM
    qw = MLSTM_HEADS * MLSTM_QK_DIM
    vw = MLSTM_HEADS * MLSTM_V_DIM
    widths = (aw, aw, aw, iw, IDX_DIM, IDX_HEADS, qw, qw, vw, MLSTM_HEADS, MLSTM_HEADS, vw, D, D)
    offs = [0]
    for w_ in widths:
        offs.append(offs[-1] + w_)
    col = lambda k: w_in[:, offs[k]:offs[k + 1]]
    bia = lambda k: b_in[offs[k]:offs[k + 1]]
    (A_Q, A_K, A_V, I_Q, I_K, I_W, M_Q, M_K, M_V, M_I, M_F, M_O, G_A, G_M) = range(14)

    wr = jnp.concatenate([col(A_K), _pad_cols(col(I_K), LANES)], 1).astype(BF16)
    br = jnp.concatenate([bia(A_K), jnp.pad(bia(I_K), (0, LANES - IDX_DIM))])[None, :]
    wc = jnp.concatenate([col(A_Q), col(A_V), col(I_Q), _pad_cols(col(I_W), 8)], 1).T.astype(BF16)
    bc = jnp.concatenate([bia(A_Q), bia(A_V), bia(I_Q), jnp.pad(bia(I_W), (0, 8 - IDX_HEADS))])[:, None]
    gpad = LANES - 2 * MLSTM_HEADS
    wm = jnp.concatenate([col(M_Q), col(M_K), col(M_I),
                          _pad_cols(col(M_F), MLSTM_HEADS + gpad)], 1).astype(BF16)
    bm = jnp.concatenate([bia(M_Q), bia(M_K), bia(M_I), jnp.pad(bia(M_F), (0, gpad))])[None, :]
    wmt = jnp.concatenate([col(M_V), col(M_O), col(M_I), col(M_F)], 1).T.astype(BF16)
    bmt = jnp.concatenate([bia(M_V), bia(M_O), bia(M_I), bia(M_F)])[:, None]
    wgate = jnp.concatenate([col(G_A), col(G_M)], 1).astype(BF16)
    bgate = jnp.concatenate([bia(G_A), bia(G_M)])[None, :]

    half = ATT_HEAD_DIM // 2
    inv = ROPE_THETA ** (-jnp.arange(0, ATT_HEAD_DIM, 2, dtype=F32) / ATT_HEAD_DIM)
    ang = jnp.arange(S, dtype=F32)[:, None] * inv[None, :]
    cos, sin = jnp.cos(ang), jnp.sin(ang)
    cos_r = jnp.tile(cos, (1, LANES // half))
    sin_r = jnp.tile(jnp.concatenate([-sin, sin], 1), (1, LANES // ATT_HEAD_DIM))

    qt, k, vt, qit, ki, wit = attn_projections(x, wr, br, wc, bc, cos_r, sin_r, cos.T, sin.T)
    y_attn = dsa_attention(qt, k, vt, qit, ki, wit, topk=min(IDX_TOPK_MAX, S // 4))

    mqt, mk, mvt, ogt, ifc, ift = mlstm_projections(x, wm, bm, wmt, bmt, conv_m)
    y_mlstm = mlstm_scan(mqt, mk, mvt, ogt, ifc, ift,
                         jnp.broadcast_to(gn_m_gain[:, None], (vw, LANES)))

    x1 = merge_branches(x.reshape(B * S, D), y_attn.reshape(B * S, aw), y_mlstm.reshape(B * S, vw),
                        wgate, bgate, w_branch_attn.astype(BF16), w_branch_mlstm.astype(BF16),
                        w_out.astype(BF16), ln1_gain[None, :], ln1_bias[None, :],
                        alpha=DEEPNORM_ALPHA)

    w_router = _pad_cols(jnp.concatenate([w_router_group, w_router_expert], 1), LANES)
    b_router = jnp.pad(jnp.concatenate([b_router_group, b_router_expert]),
                       (0, LANES - N_GROUPS - N_EXPERTS))[None, :]
    out = moe_layer(x1, w_router, b_router, w_exp_gate.astype(BF16), w_exp_up.astype(BF16),
                    w_exp_down.astype(BF16), ln2_gain[None, :], ln2_bias[None, :],
                    alpha=DEEPNORM_ALPHA)
    return out.reshape(B, S, D)
```

```python
import functools

import jax
import jax.numpy as jnp
from jax import lax
from jax.experimental import pallas as pl
from jax.experimental.pallas import tpu as pltpu

F32 = jnp.float32
BF16 = jnp.bfloat16
NEG_INF = float("-inf")
LOG2_E = 1.4426950408889634

ATT_HEADS = 8
ATT_HEAD_DIM = 64
IDX_HEADS = 4
IDX_DIM = 64
IDX_TOPK_MAX = 256
Q_BLOCK = 128
ROPE_THETA = 10000.0

LANES = 128
BF16_ROWS = 16


def _key_to_f32(u):
    ks = u ^ jnp.int32(-2 ** 31)
    bits = ks ^ ((ks >> 31) & jnp.int32(0x7FFFFFFF))
    return lax.bitcast_convert_type(bits, F32)


def _dsa_kernel(qt_ref, k_ref, vt_ref, qit_ref, ki_ref, wit_ref, o_ref,
                sc_ref, qm_ref, m_ref, l_ref, acc_ref, *, topk, kchunk):
    qb = pl.program_id(1)
    tq = o_ref.shape[1]
    heads = qt_ref.shape[1] // ATT_HEAD_DIM
    step = 2 * LANES
    n_chunk = (qb * tq + tq + kchunk - 1) // kchunk
    qpos = lax.broadcasted_iota(jnp.int32, (1, tq), 1) + qb * tq

    qit = qit_ref[0]
    zpad = jnp.zeros((LANES - IDX_DIM, tq), BF16)
    qi_pair = []
    for p in range(IDX_HEADS // 2):
        cols = [jnp.concatenate([qit[h * IDX_DIM:(h + 1) * IDX_DIM], zpad], axis=0)
                for h in (2 * p, 2 * p + 1)]
        qi_pair.append(jnp.concatenate(cols, axis=1))
    wit = wit_ref[0]

    achunk = min(2 * kchunk, sc_ref.shape[0])
    n_achunk = (qb * tq + tq + achunk - 1) // achunk
    pairs = IDX_HEADS // 2

    def score_body(c, carry):
        off = pl.multiple_of(c * achunk, achunk)

        def qk(i):
            ki = ki_ref[0, pl.ds(off + (i // pairs) * step, step), :]
            return jnp.dot(ki, qi_pair[i % pairs], preferred_element_type=F32)

        n_dots = (achunk // step) * pairs
        ahead = 3
        pending = [qk(i) for i in range(ahead)]
        for t in range(achunk // step):
            tot = None
            for p in range(pairs):
                i = t * pairs + p
                if i + ahead < n_dots:
                    pending.append(qk(i + ahead))
                s2 = pending.pop(0)
                for j in range(2):
                    h = 2 * p + j
                    s = jnp.maximum(s2[:, j * tq:(j + 1) * tq], 0.0) * wit[h:h + 1, :]
                    tot = s if tot is None else tot + s
            kpos = lax.broadcasted_iota(jnp.int32, (step, tq), 0) + (off + t * step)
            sc_ref[pl.ds(off + t * step, step), :] = jnp.where(kpos <= qpos, tot + 0.0, NEG_INF)
        return carry

    lax.fori_loop(0, n_achunk, score_body, 0)

    def count(pred):
        def body(j, acc):
            off = pl.multiple_of(j * kchunk, kchunk)
            for t in range(kchunk // LANES):
                x = sc_ref[pl.ds(off + t * LANES, LANES), :]
                acc = acc + jnp.where(pred(x, off + t * LANES), 1.0, 0.0)
            return acc
        acc = lax.fori_loop(0, n_chunk, body, jnp.zeros((LANES, tq), F32))
        return jnp.sum(acc, axis=0, keepdims=True)

    kf = float(topk)
    short = qpos < topk

    def bit_pass(i, state, frozen):
        u, cge, cgt = state
        cand = u | (jnp.int32(1) << (31 - i))
        thr = _key_to_f32(cand)
        cnt = count(lambda x, off: x >= thr)
        ok = jnp.logical_and(cnt >= kf, jnp.logical_not(frozen))
        fail = jnp.logical_and(cnt < kf, jnp.logical_not(frozen))
        return (jnp.where(ok, cand, u), jnp.where(ok, cnt, cge), jnp.where(fail, cnt, cgt))

    c_pos = count(lambda x, off: x > 0.0)
    never = jnp.zeros((1, tq), jnp.bool_)
    state = bit_pass(0, (jnp.zeros((1, tq), jnp.int32), jnp.zeros((1, tq), F32),
                         jnp.zeros((1, tq), F32)), never)
    frozen = jnp.logical_and(c_pos < kf, state[1] >= kf)
    state = (state[0], state[1], jnp.where(frozen, c_pos, state[2]))

    fixed_bits = 20
    state = lax.fori_loop(1, fixed_bits, lambda i, st: bit_pass(i, st, frozen), state)

    def all_settled(st):
        done = jnp.logical_or(jnp.logical_or(short, frozen), st[1] == kf)
        return jnp.min(jnp.where(done, 1.0, 0.0)) > 0.0

    def refine(carry):
        i, st, _ = carry
        st = bit_pass(i, st, frozen)
        st = bit_pass(i + 1, st, frozen)
        return i + 2, st, all_settled(st)

    _, (u, cge, cgt), _ = lax.while_loop(
        lambda c: jnp.logical_and(c[0] < 32, jnp.logical_not(c[2])),
        refine, (jnp.int32(fixed_bits), state, all_settled(state)))
    tau = jnp.where(short, NEG_INF, _key_to_f32(u))
    need = jnp.where(cge == kf, kf, kf - cgt)

    r_i = lax.broadcasted_iota(jnp.int32, (LANES, LANES), 0)
    c_i = lax.broadcasted_iota(jnp.int32, (LANES, LANES), 1)
    tril = jnp.where(c_i <= r_i, 1.0, 0.0).astype(BF16)

    def bias_body(c, seen):
        base = pl.multiple_of(c * achunk, achunk)
        ranks = [jnp.dot(tril, jnp.where(sc_ref[pl.ds(base + t * LANES, LANES), :] == tau,
                                         1.0, 0.0).astype(BF16), preferred_element_type=F32)
                 for t in range(achunk // LANES)]
        for t, rank in enumerate(ranks):
            x = sc_ref[pl.ds(base + t * LANES, LANES), :]
            keep = jnp.logical_or(x > tau, jnp.logical_and(x == tau, rank + seen <= need))
            kpos = lax.broadcasted_iota(jnp.int32, x.shape, 0) + (base + t * LANES)
            keep = jnp.logical_and(keep, kpos <= qpos)
            sc_ref[pl.ds(base + t * LANES, LANES), :] = jnp.where(keep, 0.0, NEG_INF)
            seen = seen + rank[LANES - 1:LANES, :]
        return seen

    lax.fori_loop(0, n_achunk, bias_body, jnp.zeros((1, tq), F32))

    qt = qt_ref[0]
    zrow = jnp.zeros((ATT_HEAD_DIM, tq), BF16)
    for p in range(heads // 2):
        a = qt[(2 * p) * ATT_HEAD_DIM:(2 * p + 1) * ATT_HEAD_DIM]
        b = qt[(2 * p + 1) * ATT_HEAD_DIM:(2 * p + 2) * ATT_HEAD_DIM]
        qm_ref[p] = jnp.concatenate([jnp.concatenate([a, zrow], axis=0),
                                     jnp.concatenate([zrow, b], axis=0)], axis=1)
    m_ref[...] = jnp.full(m_ref.shape, NEG_INF, F32)
    l_ref[...] = jnp.zeros(l_ref.shape, F32)
    acc_ref[...] = jnp.zeros(acc_ref.shape, F32)
    ones_rows = jnp.ones((BF16_ROWS, step), BF16)
    stages = [(sub, p) for sub in range(achunk // step) for p in range(heads // 2)]

    def attn_body(c, carry):
        off = pl.multiple_of(c * achunk, achunk)

        def qk(stage):
            sub, p = stage
            kp = k_ref[0, pl.ds(off + sub * step, step), p * LANES:(p + 1) * LANES]
            return jnp.dot(kp, qm_ref[p], preferred_element_type=F32)

        ahead = 4
        pending = [qk(st) for st in stages[:ahead]]
        for i, (sub, p) in enumerate(stages):
            if i + ahead < len(stages):
                pending.append(qk(stages[i + ahead]))
            s2 = pending.pop(0)
            koff = off + sub * step
            bias = sc_ref[pl.ds(koff, step), :]
            for j in range(2):
                h = 2 * p + j
                s = s2[:, j * tq:(j + 1) * tq] + bias
                m_old = m_ref[h]
                m_new = jnp.maximum(m_old, jnp.max(s, axis=0, keepdims=True))
                m_use = jnp.where(m_new == NEG_INF, 0.0, m_new)
                pexp = jnp.exp2(s - m_use).astype(BF16)
                alpha = jnp.exp2(m_old - m_use)
                vt = jnp.concatenate(
                    [vt_ref[0, h * ATT_HEAD_DIM:(h + 1) * ATT_HEAD_DIM, pl.ds(koff, step)],
                     ones_rows], axis=0)
                pv = jnp.dot(vt, pexp, preferred_element_type=F32)
                acc_ref[h] = alpha * acc_ref[h] + pv[0:ATT_HEAD_DIM]
                l_ref[h] = alpha * l_ref[h] + pv[ATT_HEAD_DIM:ATT_HEAD_DIM + 1]
                m_ref[h] = m_new
        return carry

    lax.fori_loop(0, n_achunk, attn_body, 0)

    out_t = jnp.concatenate([acc_ref[h] / l_ref[h] for h in range(heads)], axis=0)
    o_ref[0] = jnp.transpose(out_t).astype(o_ref.dtype)


def dsa_attention(qt, k, vt, qit, ki, wit, *, topk, kchunk=512):
    B, W, S = qt.shape
    tq = Q_BLOCK
    kchunk = min(kchunk, S)
    heads = W // ATT_HEAD_DIM
    kern = functools.partial(_dsa_kernel, topk=topk, kchunk=kchunk)
    qcol = lambda a: pl.BlockSpec((1, a.shape[1], tq), lambda b, i: (b, 0, i))
    whole = lambda a: pl.BlockSpec((1,) + a.shape[1:], lambda b, i: (b, 0, 0))
    return pl.pallas_call(
        kern,
        grid=(B, S // tq),
        in_specs=[qcol(qt), whole(k), whole(vt), qcol(qit), whole(ki), qcol(wit)],
        out_specs=pl.BlockSpec((1, tq, W), lambda b, i: (b, i, 0)),
        out_shape=jax.ShapeDtypeStruct((B, S, W), BF16),
        scratch_shapes=[
            pltpu.VMEM((S, tq), F32),
            pltpu.VMEM((heads // 2, LANES, 2 * tq), BF16),
            pltpu.VMEM((heads, 1, tq), F32),
            pltpu.VMEM((heads, 1, tq), F32),
            pltpu.VMEM((heads, ATT_HEAD_DIM, tq), F32),
        ],
        compiler_params=pltpu.CompilerParams(
            dimension_semantics=("arbitrary", "arbitrary"),
            vmem_limit_bytes=56 * 1024 * 1024),
        name="dsa_attention",
    )(qt, k, vt, qit, ki, wit)


def _rope_rows(x, cos_t, sin_t):
    lane = lax.broadcasted_iota(jnp.int32, x.shape, 1)
    swapped = jnp.where((lane % ATT_HEAD_DIM) < ATT_HEAD_DIM // 2,
                        pltpu.roll(x, LANES - ATT_HEAD_DIM // 2, 1),
                        pltpu.roll(x, ATT_HEAD_DIM // 2, 1))
    return x * cos_t + swapped * sin_t


def _attn_proj_kernel(x_ref, wr_ref, br_ref, wc_ref, bc_ref, cos_ref, sin_ref, cost_ref, sint_ref,
                      qt_ref, k_ref, vt_ref, qit_ref, ki_ref, wit_ref, *, q_scale, wi_scale):
    xb = x_ref[0].astype(BF16)
    aw = k_ref.shape[2]
    iw = qit_ref.shape[1]
    half = ATT_HEAD_DIM // 2
    cos_t = cos_ref[...]
    sin_t = sin_ref[...]

    pk = jnp.dot(xb, wr_ref[...], preferred_element_type=F32) + br_ref[...]
    for j in range(aw // LANES):
        sl = slice(j * LANES, (j + 1) * LANES)
        k_ref[0, :, sl] = _rope_rows(pk[:, sl], cos_t, sin_t).astype(k_ref.dtype)
    ki_ref[0] = _rope_rows(pk[:, aw:aw + LANES], cos_t, sin_t).astype(ki_ref.dtype)

    pt = lax.dot_general(wc_ref[...], xb, (((1,), (1,)), ((), ())),
                         preferred_element_type=F32) + bc_ref[...]
    ct = cost_ref[...]
    st = sint_ref[...]

    def rope_cols(src0, dst_ref, nheads, scale):
        for h in range(nheads):
            r0 = src0 + h * ATT_HEAD_DIM
            x1 = pt[r0:r0 + half]
            x2 = pt[r0 + half:r0 + 2 * half]
            d0 = h * ATT_HEAD_DIM
            dst_ref[0, d0:d0 + half, :] = ((x1 * ct - x2 * st) * scale).astype(dst_ref.dtype)
            dst_ref[0, d0 + half:d0 + 2 * half, :] = ((x1 * st + x2 * ct) * scale).astype(dst_ref.dtype)

    rope_cols(0, qt_ref, aw // ATT_HEAD_DIM, q_scale)
    vt_ref[0] = pt[aw:2 * aw].astype(vt_ref.dtype)
    rope_cols(2 * aw, qit_ref, iw // IDX_DIM, 1.0)
    wit_ref[0] = pt[2 * aw + iw:] * wi_scale


def attn_projections(x, wr, br, wc, bc, cos_r, sin_r, cos_c, sin_c, *, tm=512):
    B, S, D = x.shape
    aw = ATT_HEADS * ATT_HEAD_DIM
    iw = IDX_HEADS * IDX_DIM
    tm = min(tm, S)
    kern = functools.partial(_attn_proj_kernel, q_scale=ATT_HEAD_DIM ** -0.5 * LOG2_E,
                             wi_scale=IDX_HEADS ** -0.5 * IDX_DIM ** -0.5)
    full = lambda a: pl.BlockSpec(a.shape, lambda b, i: (0,) * a.ndim)
    row = lambda w: pl.BlockSpec((1, tm, w), lambda b, i: (b, i, 0))
    colm = lambda r: pl.BlockSpec((1, r, tm), lambda b, i: (b, 0, i))
    return pl.pallas_call(
        kern,
        grid=(B, S // tm),
        in_specs=[row(D), full(wr), full(br), full(wc), full(bc),
                  pl.BlockSpec((tm, LANES), lambda b, i: (i, 0)),
                  pl.BlockSpec((tm, LANES), lambda b, i: (i, 0)),
                  pl.BlockSpec((ATT_HEAD_DIM // 2, tm), lambda b, i: (0, i)),
                  pl.BlockSpec((ATT_HEAD_DIM // 2, tm), lambda b, i: (0, i))],
        out_specs=[colm(aw), row(aw), colm(aw), colm(iw), row(LANES), colm(8)],
        out_shape=[jax.ShapeDtypeStruct((B, aw, S), BF16),
                   jax.ShapeDtypeStruct((B, S, aw), BF16),
                   jax.ShapeDtypeStruct((B, aw, S), BF16),
                   jax.ShapeDtypeStruct((B, iw, S), BF16),
                   jax.ShapeDtypeStruct((B, S, LANES), BF16),
                   jax.ShapeDtypeStruct((B, 8, S), F32)],
        compiler_params=pltpu.CompilerParams(
            dimension_semantics=("arbitrary", "arbitrary"),
            vmem_limit_bytes=48 * 1024 * 1024),
        name="attn_projections",
    )(x, wr, br, wc, bc, cos_r, sin_r, cos_c, sin_c)


MLSTM_HEADS = 8
MLSTM_QK_DIM = 64
MLSTM_V_DIM = 128
MLSTM_CHUNK = 128
CONV_WIDTH = 4
HALO = 8


def _silu(x):
    return x / (1.0 + jnp.exp(-x))


def _sigmoid(x):
    return 1.0 / (1.0 + jnp.exp(-x))


def _log_sigmoid(x):
    return jnp.minimum(x, 0.0) - jnp.log(1.0 + jnp.exp(-jnp.abs(x)))


def _mlstm_proj_kernel(x_ref, w_ref, b_ref, wt_ref, bt_ref, conv_ref,
                       mqt_ref, mk_ref, mvt_ref, ogt_ref, ifc_ref, ift_ref, ext_ref, *, q_scale):
    i = pl.program_id(1)
    tm = x_ref.shape[1]
    qkw = 2 * mk_ref.shape[2]
    vw = mvt_ref.shape[1]
    nh = ift_ref.shape[1] // 2
    xb = x_ref[0].astype(BF16)

    @pl.when(i == 0)
    def _():
        ext_ref[0:HALO, :] = jnp.zeros((HALO, qkw), F32)

    pqk = jnp.dot(xb, w_ref[:, 0:qkw], preferred_element_type=F32) + b_ref[:, 0:qkw]
    ext_ref[HALO:HALO + tm, :] = pqk
    acc = None
    for j in range(CONV_WIDTH):
        term = ext_ref[pl.ds(HALO - CONV_WIDTH + 1 + j, tm), :] * conv_ref[j:j + 1, :]
        acc = term if acc is None else acc + term
    ext_ref[0:HALO, :] = pqk[tm - HALO:tm, :]
    qk = _silu(acc)
    mqt_ref[0] = jnp.transpose(qk[:, 0:qkw // 2] * q_scale).astype(mqt_ref.dtype)
    mk_ref[0] = qk[:, qkw // 2:qkw].astype(mk_ref.dtype)

    pg = jnp.dot(xb, w_ref[:, qkw:], preferred_element_type=F32) + b_ref[:, qkw:]
    lane = lax.broadcasted_iota(jnp.int32, pg.shape, 1)
    ifc_ref[0] = jnp.where(lane < nh, pg, _log_sigmoid(pg))

    pt = lax.dot_general(wt_ref[...], xb, (((1,), (1,)), ((), ())),
                         preferred_element_type=F32) + bt_ref[...]
    mvt_ref[0] = pt[0:vw].astype(mvt_ref.dtype)
    ogt_ref[0] = _sigmoid(pt[vw:2 * vw]).astype(ogt_ref.dtype)
    gates = pt[2 * vw:]
    rowi = lax.broadcasted_iota(jnp.int32, gates.shape, 0)
    ift_ref[0] = jnp.where(rowi < nh, gates, _log_sigmoid(gates))


def mlstm_projections(x, w, b, wt, bt, conv, *, tm=512):
    B, S, D = x.shape
    qw = MLSTM_HEADS * MLSTM_QK_DIM
    vw = MLSTM_HEADS * MLSTM_V_DIM
    tm = min(tm, S)
    kern = functools.partial(_mlstm_proj_kernel, q_scale=MLSTM_QK_DIM ** -0.5)
    full = lambda a: pl.BlockSpec(a.shape, lambda b_, i: (0,) * a.ndim)
    row = lambda w_: pl.BlockSpec((1, tm, w_), lambda b_, i: (b_, i, 0))
    colm = lambda r: pl.BlockSpec((1, r, tm), lambda b_, i: (b_, 0, i))
    return pl.pallas_call(
        kern,
        grid=(B, S // tm),
        in_specs=[row(D), full(w), full(b), full(wt), full(bt), full(conv)],
        out_specs=[colm(qw), row(qw), colm(vw), colm(vw), row(LANES), colm(2 * MLSTM_HEADS)],
        out_shape=[jax.ShapeDtypeStruct((B, qw, S), BF16),
                   jax.ShapeDtypeStruct((B, S, qw), BF16),
                   jax.ShapeDtypeStruct((B, vw, S), BF16),
                   jax.ShapeDtypeStruct((B, vw, S), BF16),
                   jax.ShapeDtypeStruct((B, S, LANES), F32),
                   jax.ShapeDtypeStruct((B, 2 * MLSTM_HEADS, S), F32)],
        scratch_shapes=[pltpu.VMEM((HALO + tm, 2 * qw), F32)],
        compiler_params=pltpu.CompilerParams(
            dimension_semantics=("arbitrary", "arbitrary"),
            vmem_limit_bytes=56 * 1024 * 1024),
        name="mlstm_projections",
    )(x, w, b, wt, bt, conv)


GN_EPS = 1e-6


def _mlstm_kernel(mqt_ref, mk_ref, mvt_ref, ogt_ref, ifc_ref, ift_ref, gain_ref, y_ref,
                  ct_ref, m_ref):
    c = pl.program_id(1)
    L = mk_ref.shape[1]
    nh = ift_ref.shape[1] // 2
    dk = mk_ref.shape[2] // nh
    dv = mvt_ref.shape[1] // nh

    @pl.when(c == 0)
    def _():
        ct_ref[...] = jnp.zeros(ct_ref.shape, F32)
        m_ref[...] = jnp.zeros(m_ref.shape, F32)

    r_i = lax.broadcasted_iota(jnp.int32, (L, L), 0)
    c_i = lax.broadcasted_iota(jnp.int32, (L, L), 1)
    causal_t = r_i <= c_i
    tril = jnp.where(c_i <= r_i, 1.0, 0.0)
    triu = jnp.where(causal_t, 1.0, 0.0)
    ifc = ifc_ref[0]
    ift = ift_ref[0]
    b_cols = jnp.dot(tril, ifc, preferred_element_type=F32, precision=lax.Precision.HIGHEST)
    b_rows = jnp.dot(ift, triu, preferred_element_type=F32, precision=lax.Precision.HIGHEST)
    ones_rows = jnp.where(lax.broadcasted_iota(jnp.int32, (BF16_ROWS, L), 0) == 0,
                          1.0, 0.0).astype(BF16)
    first_half_rows = lax.broadcasted_iota(jnp.int32, (2 * dk, L), 0) < dk
    first_half_lanes = lax.broadcasted_iota(jnp.int32, (L, 2 * dk), 1) < dk

    k_pair, q_heads, s_heads, inter_heads = [], [], [], []
    for p in range(nh // 2):
        k_pair.append(mk_ref[0, :, 2 * p * dk:2 * (p + 1) * dk])
        qt_pair = mqt_ref[0, 2 * p * dk:2 * (p + 1) * dk, :]
        zero = jnp.zeros_like(qt_pair)
        q_heads.append(jnp.where(first_half_rows, qt_pair, zero))
        q_heads.append(jnp.where(first_half_rows, zero, qt_pair))
    for h in range(nh):
        s_heads.append(jnp.dot(k_pair[h // 2], q_heads[h], preferred_element_type=F32))
        inter_heads.append(jnp.dot(ct_ref[h].astype(BF16), q_heads[h],
                                   preferred_element_type=F32))

    for h in range(nh):
        m = m_ref[h]
        a_col = ifc[:, h:h + 1] - b_cols[:, nh + h:nh + h + 1]
        amat = jnp.where(causal_t, a_col, NEG_INF)
        big_m = jnp.maximum(m, jnp.max(amat, axis=0, keepdims=True))
        decay = jnp.exp(amat - big_m)
        w_inter = jnp.exp(m - big_m)
        s = (s_heads[h] * decay).astype(BF16)
        vt_ext = jnp.concatenate([mvt_ref[0, h * dv:(h + 1) * dv, :], ones_rows], axis=0)
        tot = jnp.dot(vt_ext, s, preferred_element_type=F32) + w_inter * inter_heads[h]
        den = tot[dv:dv + 1]
        b_row = b_rows[nh + h:nh + h + 1, :]
        hh = tot[0:dv] / jnp.maximum(jnp.abs(den), jnp.exp(-(b_row + big_m)))
        mu = jnp.mean(hh, axis=0, keepdims=True)
        xc = hh - mu
        var = jnp.mean(xc * xc, axis=0, keepdims=True)
        hn = (xc * lax.rsqrt(var + GN_EPS) * gain_ref[h * dv:(h + 1) * dv, :]
              * ogt_ref[0, h * dv:(h + 1) * dv, :].astype(F32))
        y_ref[0, :, h * dv:(h + 1) * dv] = jnp.transpose(hn).astype(y_ref.dtype)

        b_last = b_cols[L - 1:L, nh + h:nh + h + 1]
        g_col = b_last + a_col
        m_new = jnp.maximum(b_last + m, jnp.max(g_col, axis=0, keepdims=True))
        carry = jnp.exp(b_last + m - m_new)
        head_lanes = first_half_lanes if h % 2 == 0 else jnp.logical_not(first_half_lanes)
        wk = jnp.where(head_lanes, k_pair[h // 2].astype(F32) * jnp.exp(g_col - m_new),
                       0.0).astype(BF16)
        ct_ref[h] = carry * ct_ref[h] + jnp.dot(vt_ext, wk, preferred_element_type=F32)
        m_ref[h] = m_new


def mlstm_scan(mqt, mk, mvt, ogt, ifc, ift, gain_b):
    B, S, qw = mk.shape
    vw = mvt.shape[1]
    L = min(MLSTM_CHUNK, S)
    nh = ift.shape[1] // 2
    row = lambda w_: pl.BlockSpec((1, L, w_), lambda b_, c: (b_, c, 0))
    colm = lambda r: pl.BlockSpec((1, r, L), lambda b_, c: (b_, 0, c))
    return pl.pallas_call(
        _mlstm_kernel,
        grid=(B, S // L),
        in_specs=[colm(qw), row(qw), colm(vw), colm(vw), row(LANES), colm(2 * nh),
                  pl.BlockSpec(gain_b.shape, lambda b_, c: (0, 0))],
        out_specs=row(vw),
        out_shape=jax.ShapeDtypeStruct((B, S, vw), BF16),
        scratch_shapes=[pltpu.VMEM((nh, vw // nh + BF16_ROWS, 2 * qw // nh), F32),
                        pltpu.VMEM((nh, 1, 1), F32)],
        compiler_params=pltpu.CompilerParams(
            dimension_semantics=("arbitrary", "arbitrary")),
        name="mlstm_scan",
    )(mqt, mk, mvt, ogt, ifc, ift, gain_b)


LN_EPS = 1e-5


def _layer_norm(z, gain, bias):
    mu = jnp.mean(z, axis=1, keepdims=True)
    zc = z - mu
    var = jnp.mean(zc * zc, axis=1, keepdims=True)
    return zc * lax.rsqrt(var + LN_EPS) * gain + bias


def _merge_kernel(x_ref, ya_ref, ym_ref, wg_ref, bg_ref, wa_ref, wm_ref, wo_ref, g_ref, b_ref,
                  o_ref, *, alpha):
    x = x_ref[...]
    xb = x.astype(BF16)
    d = x.shape[1]
    ga = _sigmoid(jnp.dot(xb, wg_ref[:, 0:d], preferred_element_type=F32) + bg_ref[:, 0:d])
    merged = ga * jnp.dot(ya_ref[...], wa_ref[...], preferred_element_type=F32)
    gm = _sigmoid(jnp.dot(xb, wg_ref[:, d:2 * d], preferred_element_type=F32) + bg_ref[:, d:2 * d])
    merged = merged + gm * jnp.dot(ym_ref[...], wm_ref[...], preferred_element_type=F32)
    z = alpha * x + jnp.dot(merged.astype(BF16), wo_ref[...], preferred_element_type=F32)
    o_ref[...] = _layer_norm(z, g_ref[...], b_ref[...])


def merge_branches(x2, ya, ym, wg, bg, wa, wm, wo, g, b, *, alpha, tm=512):
    T, D = x2.shape
    tm = min(tm, T)
    full = lambda a: pl.BlockSpec(a.shape, lambda i: (0,) * a.ndim)
    row = lambda w_: pl.BlockSpec((tm, w_), lambda i: (i, 0))
    return pl.pallas_call(
        functools.partial(_merge_kernel, alpha=alpha),
        grid=(T // tm,),
        in_specs=[row(D), row(ya.shape[1]), row(ym.shape[1]), full(wg), full(bg), full(wa),
                  full(wm), full(wo), full(g), full(b)],
        out_specs=row(D),
        out_shape=jax.ShapeDtypeStruct((T, D), F32),
        compiler_params=pltpu.CompilerParams(
            dimension_semantics=("arbitrary",), vmem_limit_bytes=48 * 1024 * 1024),
        name="merge_branches",
    )(x2, ya, ym, wg, bg, wa, wm, wo, g, b)


N_GROUPS = 4
EXPERTS_PER_GROUP = 4
N_EXPERTS = N_GROUPS * EXPERTS_PER_GROUP


def _first_lane_of_max(vals, vmax, lane):
    return jnp.min(jnp.where(vals == vmax, lane, LANES), axis=1, keepdims=True)


MOE_SUB = 128


def _moe_kernel(x_ref, wr_ref, br_ref, wg_ref, wu_ref, wd_ref, g_ref, b_ref, o_ref,
                tril_ref, slot_ref, xg_ref, cw_ref, y_ref, tiles_ref, *, alpha):
    w = pl.program_id(0)
    e = pl.program_id(1)
    tm = x_ref.shape[0]
    slots = xg_ref.shape[0]
    lane = lax.broadcasted_iota(jnp.int32, (tm, LANES), 1)

    @pl.when(jnp.logical_and(w == 0, e == 0))
    def _():
        r_i = lax.broadcasted_iota(jnp.int32, (tm, tm), 0)
        c_i = lax.broadcasted_iota(jnp.int32, (tm, tm), 1)
        tril_ref[...] = jnp.where(c_i < r_i, 1.0, 0.0).astype(BF16)

    @pl.when(e == 0)
    def _():
        x = x_ref[...]
        x_hi = x.astype(BF16)
        x_lo = (x - x_hi.astype(F32)).astype(BF16)
        w_hi = wr_ref[...].astype(BF16)
        w_lo = (wr_ref[...] - w_hi.astype(F32)).astype(BF16)
        logits = (jnp.dot(x_hi, w_hi, preferred_element_type=F32)
                  + jnp.dot(x_lo, w_hi, preferred_element_type=F32)
                  + jnp.dot(x_hi, w_lo, preferred_element_type=F32)
                  + br_ref[...])
        g = jnp.where(lane < N_GROUPS, logits, NEG_INF)
        gmax = jnp.max(g, axis=1, keepdims=True)
        g_w = 1.0 / jnp.sum(jnp.exp(g - gmax), axis=1, keepdims=True)
        g_sel = _first_lane_of_max(g, gmax, lane)
        lo = N_GROUPS + EXPERTS_PER_GROUP * g_sel
        ev = jnp.where(jnp.logical_and(lane >= lo, lane < lo + EXPERTS_PER_GROUP), logits, NEG_INF)
        v1 = jnp.max(ev, axis=1, keepdims=True)
        i1 = _first_lane_of_max(ev, v1, lane)
        ev2 = jnp.where(lane == i1, NEG_INF, ev)
        v2 = jnp.max(ev2, axis=1, keepdims=True)
        i2 = _first_lane_of_max(ev2, v2, lane)
        r = jnp.exp(v2 - v1)
        p1 = 1.0 / (1.0 + r)
        p2 = r / (1.0 + r)
        comb = jnp.where(lane == i1, g_w * p1, 0.0) + jnp.where(lane == i2, g_w * p2, 0.0)
        comb_hi = comb.astype(BF16)
        comb_lo = (comb - comb_hi.astype(F32)).astype(BF16)

        onehot = jnp.where(lane == g_sel, 1.0, 0.0)
        before = jnp.dot(tril_ref[...], onehot.astype(BF16), preferred_element_type=F32)
        rank = jnp.sum(jnp.where(lane == g_sel, before, 0.0), axis=1, keepdims=True)
        total = jnp.sum(onehot, axis=0, keepdims=True)
        lane1 = lax.broadcasted_iota(jnp.int32, (1, LANES), 1)
        start = jnp.zeros((tm, 1), F32)
        first_tile = jnp.int32(0)
        for grp in range(N_GROUPS):
            n_tok = jnp.sum(jnp.where(lane1 == grp, total, 0.0)).astype(jnp.int32)
            n_tile = (n_tok + MOE_SUB - 1) // MOE_SUB
            tiles_ref[grp] = first_tile
            tiles_ref[N_GROUPS + grp] = n_tile
            start = jnp.where(g_sel == grp, (first_tile * MOE_SUB).astype(F32), start)
            first_tile = first_tile + n_tile
        slot = start + rank
        slot_ref[...] = slot
        slot_row = jnp.transpose(jnp.broadcast_to(slot, (tm, LANES)))[0:1]
        pick = slot_row == lax.broadcasted_iota(jnp.int32, (slots, tm), 0).astype(F32)
        pmat = jnp.where(pick, 1.0, 0.0).astype(BF16)
        xg_ref[...] = jnp.dot(pmat, x_hi, preferred_element_type=F32).astype(BF16)
        cw_ref[...] = (jnp.dot(pmat, comb_hi, preferred_element_type=F32)
                       + jnp.dot(pmat, comb_lo, preferred_element_type=F32))
        y_ref[...] = jnp.zeros(y_ref.shape, F32)

    grp = e // EXPERTS_PER_GROUP
    first = tiles_ref[grp]

    n_tile = tiles_ref[N_GROUPS + grp]

    def expert(tile, rows):
        base = pl.multiple_of(tile * MOE_SUB, MOE_SUB)
        xg = xg_ref[pl.ds(base, rows), :]
        hg = jnp.dot(xg, wg_ref[0], preferred_element_type=F32)
        hu = jnp.dot(xg, wu_ref[0], preferred_element_type=F32)
        hdn = (_silu(hg) * hu).astype(BF16)
        lane_s = lax.broadcasted_iota(jnp.int32, (rows, LANES), 1)
        cw = jnp.sum(jnp.where(lane_s == N_GROUPS + e, cw_ref[pl.ds(base, rows), :], 0.0),
                     axis=1, keepdims=True)
        y_ref[pl.ds(base, rows), :] += cw * jnp.dot(hdn, wd_ref[0], preferred_element_type=F32)

    def expert_pair(j, carry):
        expert(first + 2 * j, 2 * MOE_SUB)
        return carry
    lax.fori_loop(0, n_tile // 2, expert_pair, 0)

    @pl.when(n_tile % 2 == 1)
    def _():
        expert(first + n_tile - 1, MOE_SUB)

    @pl.when(e == pl.num_programs(1) - 1)
    def _():
        pick = slot_ref[...] == lax.broadcasted_iota(jnp.int32, (tm, slots), 1).astype(F32)
        pmat_t = jnp.where(pick, 1.0, 0.0).astype(BF16)
        moe = jnp.dot(pmat_t, y_ref[...].astype(BF16), preferred_element_type=F32)
        o_ref[...] = _layer_norm(alpha * x_ref[...] + moe, g_ref[...], b_ref[...])


def moe_layer(x2, wr, br, wg, wu, wd, g, b, *, alpha, tm=1024):
    T, D = x2.shape
    E, _, F = wg.shape
    tm = min(tm, T)
    slots = tm + N_GROUPS * MOE_SUB
    full = lambda a: pl.BlockSpec(a.shape, lambda i, e: (0,) * a.ndim)
    return pl.pallas_call(
        functools.partial(_moe_kernel, alpha=alpha),
        grid=(T // tm, E),
        in_specs=[pl.BlockSpec((tm, D), lambda i, e: (i, 0)), full(wr), full(br),
                  pl.BlockSpec((1, D, F), lambda i, e: (e, 0, 0)),
                  pl.BlockSpec((1, D, F), lambda i, e: (e, 0, 0)),
                  pl.BlockSpec((1, F, D), lambda i, e: (e, 0, 0)),
                  full(g), full(b)],
        out_specs=pl.BlockSpec((tm, D), lambda i, e: (i, 0)),
        out_shape=jax.ShapeDtypeStruct((T, D), F32),
        scratch_shapes=[pltpu.VMEM((tm, tm), BF16),
                        pltpu.VMEM((tm, 1), F32),
                        pltpu.VMEM((slots, D), BF16),
                        pltpu.VMEM((slots, LANES), F32),
                        pltpu.VMEM((slots, D), F32),
                        pltpu.SMEM((2 * N_GROUPS,), jnp.int32)],
        compiler_params=pltpu.CompilerParams(
            dimension_semantics=("arbitrary", "arbitrary"), vmem_limit_bytes=56 * 1024 * 1024),
        name="moe_layer",
    )(x2, wr, br, wg, wu, wd, g, b)


DEPTH = 1
DEEPNORM_ALPHA = (2.0 * DEPTH) ** 0.25


def _pad_cols(a, width):
    return jnp.pad(a, ((0, 0), (0, width - a.shape[1])))


def kernel(x, w_in, b_in, conv_m, gn_m_gain, w_branch_attn, w_branch_mlstm, w_out, ln1_gain, ln1_bias, w_router_group, b_router_group, w_router_expert, b_router_expert, w_exp_gate, w_exp_up, w_exp_down, ln2_gain, ln2_bias):
    B, S, D = x.shape
    aw = ATT_HEADS * ATT_HEAD_DIM
    iw = IDX_HEADS * IDX_DIM
    qw = MLSTM_HEADS * MLSTM_QK_DIM
    vw = MLSTM_HEADS * MLSTM_V_DIM
    widths = (aw, aw, aw, iw, IDX_DIM, IDX_HEADS, qw, qw, vw, MLSTM_HEADS, MLSTM_HEADS, vw, D, D)
    offs = [0]
    for w_ in widths:
        offs.append(offs[-1] + w_)
    col = lambda k: w_in[:, offs[k]:offs[k + 1]]
    bia = lambda k: b_in[offs[k]:offs[k + 1]]
    (A_Q, A_K, A_V, I_Q, I_K, I_W, M_Q, M_K, M_V, M_I, M_F, M_O, G_A, G_M) = range(14)

    wr = jnp.concatenate([col(A_K), _pad_cols(col(I_K), LANES)], 1).astype(BF16)
    br = jnp.concatenate([bia(A_K), jnp.pad(bia(I_K), (0, LANES - IDX_DIM))])[None, :]
    wc = jnp.concatenate([col(A_Q), col(A_V), col(I_Q), _pad_cols(col(I_W), 8)], 1).T.astype(BF16)
    bc = jnp.concatenate([bia(A_Q), bia(A_V), bia(I_Q), jnp.pad(bia(I_W), (0, 8 - IDX_HEADS))])[:, None]
    gpad = LANES - 2 * MLSTM_HEADS
    wm = jnp.concatenate([col(M_Q), col(M_K), col(M_I),
                          _pad_cols(col(M_F), MLSTM_HEADS + gpad)], 1).astype(BF16)
    bm = jnp.concatenate([bia(M_Q), bia(M_K), bia(M_I), jnp.pad(bia(M_F), (0, gpad))])[None, :]
    wmt = jnp.concatenate([col(M_V), col(M_O), col(M_I), col(M_F)], 1).T.astype(BF16)
    bmt = jnp.concatenate([bia(M_V), bia(M_O), bia(M_I), bia(M_F)])[:, None]
    wgate = jnp.concatenate([col(G_A), col(G_M)], 1).astype(BF16)
    bgate = jnp.concatenate([bia(G_A), bia(G_M)])[None, :]

    half = ATT_HEAD_DIM // 2
    inv = ROPE_THETA ** (-jnp.arange(0, ATT_HEAD_DIM, 2, dtype=F32) / ATT_HEAD_DIM)
    ang = jnp.arange(S, dtype=F32)[:, None] * inv[None, :]
    cos, sin = jnp.cos(ang), jnp.sin(ang)
    cos_r = jnp.tile(cos, (1, LANES // half))
    sin_r = jnp.tile(jnp.concatenate([-sin, sin], 1), (1, LANES // ATT_HEAD_DIM))

    qt, k, vt, qit, ki, wit = attn_projections(x, wr, br, wc, bc, cos_r, sin_r, cos.T, sin.T)
    y_attn = dsa_attention(qt, k, vt, qit, ki, wit, topk=min(IDX_TOPK_MAX, S // 4))

    mqt, mk, mvt, ogt, ifc, ift = mlstm_projections(x, wm, bm, wmt, bmt, conv_m)
    y_mlstm = mlstm_scan(mqt, mk, mvt, ogt, ifc, ift,
                         jnp.broadcast_to(gn_m_gain[:, None], (vw, LANES)))

    x1 = merge_branches(x.reshape(B * S, D), y_attn.reshape(B * S, aw), y_mlstm.reshape(B * S, vw),
                        wgate, bgate, w_branch_attn.astype(BF16), w_branch_mlstm.astype(BF16),
                        w_out.astype(BF16), ln1_gain[None, :], ln1_bias[None, :],
                        alpha=DEEPNORM_ALPHA)

    w_router = _pad_cols(jnp.concatenate([w_router_group, w_router_expert], 1), LANES)
    b_router = jnp.pad(jnp.concatenate([b_router_group, b_router_expert]),
                       (0, LANES - N_GROUPS - N_EXPERTS))[None, :]
    out = moe_layer(x1, w_router, b_router, w_exp_gate.astype(BF16), w_exp_up.astype(BF16),
                    w_exp_down.astype(BF16), ln2_gain[None, :], ln2_bias[None, :],
                    alpha=DEEPNORM_ALPHA)
    return out.reshape(B, S, D)
```

```python
import functools

import jax
import jax.numpy as jnp
from jax import lax
from jax.experimental import pallas as pl
from jax.experimental.pallas import tpu as pltpu

F32 = jnp.float32
BF16 = jnp.bfloat16
NEG_INF = float("-inf")
LOG2_E = 1.4426950408889634

ATT_HEADS = 8
ATT_HEAD_DIM = 64
IDX_HEADS = 4
IDX_DIM = 64
IDX_TOPK_MAX = 256
Q_BLOCK = 128
ROPE_THETA = 10000.0

LANES = 128
BF16_ROWS = 16


def _key_to_f32(u):
    ks = u ^ jnp.int32(-2 ** 31)
    bits = ks ^ ((ks >> 31) & jnp.int32(0x7FFFFFFF))
    return lax.bitcast_convert_type(bits, F32)


def _dsa_kernel(qt_ref, k_ref, vt_ref, qit_ref, ki_ref, wit_ref, o_ref,
                sc_ref, qm_ref, m_ref, l_ref, acc_ref, *, topk, kchunk):
    qb = pl.program_id(1)
    tq = o_ref.shape[1]
    heads = qt_ref.shape[1] // ATT_HEAD_DIM
    step = 2 * LANES
    n_chunk = (qb * tq + tq + kchunk - 1) // kchunk
    qpos = lax.broadcasted_iota(jnp.int32, (1, tq), 1) + qb * tq

    qit = qit_ref[0]
    zpad = jnp.zeros((LANES - IDX_DIM, tq), BF16)
    qi_pair = []
    for p in range(IDX_HEADS // 2):
        cols = [jnp.concatenate([qit[h * IDX_DIM:(h + 1) * IDX_DIM], zpad], axis=0)
                for h in (2 * p, 2 * p + 1)]
        qi_pair.append(jnp.concatenate(cols, axis=1))
    wit = wit_ref[0]

    achunk = min(2 * kchunk, sc_ref.shape[0])
    n_achunk = (qb * tq + tq + achunk - 1) // achunk
    pairs = IDX_HEADS // 2

    def score_body(c, carry):
        off = pl.multiple_of(c * achunk, achunk)

        def qk(i):
            ki = ki_ref[0, pl.ds(off + (i // pairs) * step, step), :]
            return jnp.dot(ki, qi_pair[i % pairs], preferred_element_type=F32)

        n_dots = (achunk // step) * pairs
        ahead = 3
        pending = [qk(i) for i in range(ahead)]
        for t in range(achunk // step):
            tot = None
            for p in range(pairs):
                i = t * pairs + p
                if i + ahead < n_dots:
                    pending.append(qk(i + ahead))
                s2 = pending.pop(0)
                for j in range(2):
                    h = 2 * p + j
                    s = jnp.maximum(s2[:, j * tq:(j + 1) * tq], 0.0) * wit[h:h + 1, :]
                    tot = s if tot is None else tot + s
            kpos = lax.broadcasted_iota(jnp.int32, (step, tq), 0) + (off + t * step)
            sc_ref[pl.ds(off + t * step, step), :] = jnp.where(kpos <= qpos, tot + 0.0, NEG_INF)
        return carry

    lax.fori_loop(0, n_achunk, score_body, 0)

    def count(pred):
        def body(j, acc):
            off = pl.multiple_of(j * kchunk, kchunk)
            for t in range(kchunk // LANES):
                x = sc_ref[pl.ds(off + t * LANES, LANES), :]
                acc = acc + jnp.where(pred(x, off + t * LANES), 1.0, 0.0)
            return acc
        acc = lax.fori_loop(0, n_chunk, body, jnp.zeros((LANES, tq), F32))
        return jnp.sum(acc, axis=0, keepdims=True)

    kf = float(topk)
    short = qpos < topk

    def bit_pass(i, state, frozen):
        u, cge, cgt = state
        cand = u | (jnp.int32(1) << (31 - i))
        thr = _key_to_f32(cand)
        cnt = count(lambda x, off: x >= thr)
        ok = jnp.logical_and(cnt >= kf, jnp.logical_not(frozen))
        fail = jnp.logical_and(cnt < kf, jnp.logical_not(frozen))
        return (jnp.where(ok, cand, u), jnp.where(ok, cnt, cge), jnp.where(fail, cnt, cgt))

    c_pos = count(lambda x, off: x > 0.0)
    never = jnp.zeros((1, tq), jnp.bool_)
    state = bit_pass(0, (jnp.zeros((1, tq), jnp.int32), jnp.zeros((1, tq), F32),
                         jnp.zeros((1, tq), F32)), never)
    frozen = jnp.logical_and(c_pos < kf, state[1] >= kf)
    state = (state[0], state[1], jnp.where(frozen, c_pos, state[2]))

    fixed_bits = 20
    state = lax.fori_loop(1, fixed_bits, lambda i, st: bit_pass(i, st, frozen), state)

    def all_settled(st):
        done = jnp.logical_or(jnp.logical_or(short, frozen), st[1] == kf)
        return jnp.min(jnp.where(done, 1.0, 0.0)) > 0.0

    def refine(carry):
        i, st, _ = carry
        st = bit_pass(i, st, frozen)
        st = bit_pass(i + 1, st, frozen)
        return i + 2, st, all_settled(st)

    _, (u, cge, cgt), _ = lax.while_loop(
        lambda c: jnp.logical_and(c[0] < 32, jnp.logical_not(c[2])),
        refine, (jnp.int32(fixed_bits), state, all_settled(state)))
    tau = jnp.where(short, NEG_INF, _key_to_f32(u))
    need = jnp.where(cge == kf, kf, kf - cgt)

    r_i = lax.broadcasted_iota(jnp.int32, (LANES, LANES), 0)
    c_i = lax.broadcasted_iota(jnp.int32, (LANES, LANES), 1)
    tril = jnp.where(c_i <= r_i, 1.0, 0.0).astype(BF16)

    def bias_body(c, seen):
        base = pl.multiple_of(c * achunk, achunk)
        ranks = [jnp.dot(tril, jnp.where(sc_ref[pl.ds(base + t * LANES, LANES), :] == tau,
                                         1.0, 0.0).astype(BF16), preferred_element_type=F32)
                 for t in range(achunk // LANES)]
        for t, rank in enumerate(ranks):
            x = sc_ref[pl.ds(base + t * LANES, LANES), :]
            keep = jnp.logical_or(x > tau, jnp.logical_and(x == tau, rank + seen <= need))
            kpos = lax.broadcasted_iota(jnp.int32, x.shape, 0) + (base + t * LANES)
            keep = jnp.logical_and(keep, kpos <= qpos)
            sc_ref[pl.ds(base + t * LANES, LANES), :] = jnp.where(keep, 0.0, NEG_INF)
            seen = seen + rank[LANES - 1:LANES, :]
        return seen

    lax.fori_loop(0, n_achunk, bias_body, jnp.zeros((1, tq), F32))

    qt = qt_ref[0]
    zrow = jnp.zeros((ATT_HEAD_DIM, tq), BF16)
    for p in range(heads // 2):
        a = qt[(2 * p) * ATT_HEAD_DIM:(2 * p + 1) * ATT_HEAD_DIM]
        b = qt[(2 * p + 1) * ATT_HEAD_DIM:(2 * p + 2) * ATT_HEAD_DIM]
        qm_ref[p] = jnp.concatenate([jnp.concatenate([a, zrow], axis=0),
                                     jnp.concatenate([zrow, b], axis=0)], axis=1)
    m_ref[...] = jnp.full(m_ref.shape, NEG_INF, F32)
    l_ref[...] = jnp.zeros(l_ref.shape, F32)
    acc_ref[...] = jnp.zeros(acc_ref.shape, F32)
    ones_rows = jnp.ones((BF16_ROWS, step), BF16)
    bchunk = min(2 * achunk, sc_ref.shape[0])
    n_bchunk = (qb * tq + tq + bchunk - 1) // bchunk

    @pl.when(n_bchunk * bchunk > n_achunk * achunk)
    def _():
        fill = pl.multiple_of(n_achunk * achunk, achunk)
        sc_ref[pl.ds(fill, achunk), :] = jnp.full((achunk, tq), NEG_INF, F32)

    stages = [(sub, p) for sub in range(bchunk // step) for p in range(heads // 2)]

    def attn_body(c, carry):
        off = pl.multiple_of(c * bchunk, bchunk)

        def qk(stage):
            sub, p = stage
            kp = k_ref[0, pl.ds(off + sub * step, step), p * LANES:(p + 1) * LANES]
            return jnp.dot(kp, qm_ref[p], preferred_element_type=F32)

        ahead = 4
        pending = [qk(st) for st in stages[:ahead]]
        for i, (sub, p) in enumerate(stages):
            if i + ahead < len(stages):
                pending.append(qk(stages[i + ahead]))
            s2 = pending.pop(0)
            koff = off + sub * step
            bias = sc_ref[pl.ds(koff, step), :]
            for j in range(2):
                h = 2 * p + j
                s = s2[:, j * tq:(j + 1) * tq] + bias
                m_old = m_ref[h]
                m_new = jnp.maximum(m_old, jnp.max(s, axis=0, keepdims=True))
                m_use = jnp.where(m_new == NEG_INF, 0.0, m_new)
                pexp = jnp.exp2(s - m_use).astype(BF16)
                alpha = jnp.exp2(m_old - m_use)
                vt = jnp.concatenate(
                    [vt_ref[0, h * ATT_HEAD_DIM:(h + 1) * ATT_HEAD_DIM, pl.ds(koff, step)],
                     ones_rows], axis=0)
                pv = jnp.dot(vt, pexp, preferred_element_type=F32)
                acc_ref[h] = alpha * acc_ref[h] + pv[0:ATT_HEAD_DIM]
                l_ref[h] = alpha * l_ref[h] + pv[ATT_HEAD_DIM:ATT_HEAD_DIM + 1]
                m_ref[h] = m_new
        return carry

    lax.fori_loop(0, n_bchunk, attn_body, 0)

    out_t = jnp.concatenate([acc_ref[h] / l_ref[h] for h in range(heads)], axis=0)
    o_ref[0] = jnp.transpose(out_t).astype(o_ref.dtype)


def dsa_attention(qt, k, vt, qit, ki, wit, *, topk, kchunk=512):
    B, W, S = qt.shape
    tq = Q_BLOCK
    kchunk = min(kchunk, S)
    heads = W // ATT_HEAD_DIM
    kern = functools.partial(_dsa_kernel, topk=topk, kchunk=kchunk)
    qcol = lambda a: pl.BlockSpec((1, a.shape[1], tq), lambda b, i: (b, 0, i))
    whole = lambda a: pl.BlockSpec((1,) + a.shape[1:], lambda b, i: (b, 0, 0))
    return pl.pallas_call(
        kern,
        grid=(B, S // tq),
        in_specs=[qcol(qt), whole(k), whole(vt), qcol(qit), whole(ki), qcol(wit)],
        out_specs=pl.BlockSpec((1, tq, W), lambda b, i: (b, i, 0)),
        out_shape=jax.ShapeDtypeStruct((B, S, W), BF16),
        scratch_shapes=[
            pltpu.VMEM((S, tq), F32),
            pltpu.VMEM((heads // 2, LANES, 2 * tq), BF16),
            pltpu.VMEM((heads, 1, tq), F32),
            pltpu.VMEM((heads, 1, tq), F32),
            pltpu.VMEM((heads, ATT_HEAD_DIM, tq), F32),
        ],
        compiler_params=pltpu.CompilerParams(
            dimension_semantics=("arbitrary", "arbitrary"),
            vmem_limit_bytes=56 * 1024 * 1024),
        name="dsa_attention",
    )(qt, k, vt, qit, ki, wit)


def _rope_rows(x, cos_t, sin_t):
    lane = lax.broadcasted_iota(jnp.int32, x.shape, 1)
    swapped = jnp.where((lane % ATT_HEAD_DIM) < ATT_HEAD_DIM // 2,
                        pltpu.roll(x, LANES - ATT_HEAD_DIM // 2, 1),
                        pltpu.roll(x, ATT_HEAD_DIM // 2, 1))
    return x * cos_t + swapped * sin_t


def _attn_proj_kernel(x_ref, wr_ref, br_ref, wc_ref, bc_ref, cos_ref, sin_ref, cost_ref, sint_ref,
                      qt_ref, k_ref, vt_ref, qit_ref, ki_ref, wit_ref, *, q_scale, wi_scale):
    xb = x_ref[0].astype(BF16)
    aw = k_ref.shape[2]
    iw = qit_ref.shape[1]
    half = ATT_HEAD_DIM // 2
    cos_t = cos_ref[...]
    sin_t = sin_ref[...]

    pk = jnp.dot(xb, wr_ref[...], preferred_element_type=F32) + br_ref[...]
    for j in range(aw // LANES):
        sl = slice(j * LANES, (j + 1) * LANES)
        k_ref[0, :, sl] = _rope_rows(pk[:, sl], cos_t, sin_t).astype(k_ref.dtype)
    ki_ref[0] = _rope_rows(pk[:, aw:aw + LANES], cos_t, sin_t).astype(ki_ref.dtype)

    pt = lax.dot_general(wc_ref[...], xb, (((1,), (1,)), ((), ())),
                         preferred_element_type=F32) + bc_ref[...]
    ct = cost_ref[...]
    st = sint_ref[...]

    def rope_cols(src0, dst_ref, nheads, scale):
        for h in range(nheads):
            r0 = src0 + h * ATT_HEAD_DIM
            x1 = pt[r0:r0 + half]
            x2 = pt[r0 + half:r0 + 2 * half]
            d0 = h * ATT_HEAD_DIM
            dst_ref[0, d0:d0 + half, :] = ((x1 * ct - x2 * st) * scale).astype(dst_ref.dtype)
            dst_ref[0, d0 + half:d0 + 2 * half, :] = ((x1 * st + x2 * ct) * scale).astype(dst_ref.dtype)

    rope_cols(0, qt_ref, aw // ATT_HEAD_DIM, q_scale)
    vt_ref[0] = pt[aw:2 * aw].astype(vt_ref.dtype)
    rope_cols(2 * aw, qit_ref, iw // IDX_DIM, 1.0)
    wit_ref[0] = pt[2 * aw + iw:] * wi_scale


def attn_projections(x, wr, br, wc, bc, cos_r, sin_r, cos_c, sin_c, *, tm=512):
    B, S, D = x.shape
    aw = ATT_HEADS * ATT_HEAD_DIM
    iw = IDX_HEADS * IDX_DIM
    tm = min(tm, S)
    kern = functools.partial(_attn_proj_kernel, q_scale=ATT_HEAD_DIM ** -0.5 * LOG2_E,
                             wi_scale=IDX_HEADS ** -0.5 * IDX_DIM ** -0.5)
    full = lambda a: pl.BlockSpec(a.shape, lambda b, i: (0,) * a.ndim)
    row = lambda w: pl.BlockSpec((1, tm, w), lambda b, i: (b, i, 0))
    colm = lambda r: pl.BlockSpec((1, r, tm), lambda b, i: (b, 0, i))
    return pl.pallas_call(
        kern,
        grid=(B, S // tm),
        in_specs=[row(D), full(wr), full(br), full(wc), full(bc),
                  pl.BlockSpec((tm, LANES), lambda b, i: (i, 0)),
                  pl.BlockSpec((tm, LANES), lambda b, i: (i, 0)),
                  pl.BlockSpec((ATT_HEAD_DIM // 2, tm), lambda b, i: (0, i)),
                  pl.BlockSpec((ATT_HEAD_DIM // 2, tm), lambda b, i: (0, i))],
        out_specs=[colm(aw), row(aw), colm(aw), colm(iw), row(LANES), colm(8)],
        out_shape=[jax.ShapeDtypeStruct((B, aw, S), BF16),
                   jax.ShapeDtypeStruct((B, S, aw), BF16),
                   jax.ShapeDtypeStruct((B, aw, S), BF16),
                   jax.ShapeDtypeStruct((B, iw, S), BF16),
                   jax.ShapeDtypeStruct((B, S, LANES), BF16),
                   jax.ShapeDtypeStruct((B, 8, S), F32)],
        compiler_params=pltpu.CompilerParams(
            dimension_semantics=("arbitrary", "arbitrary"),
            vmem_limit_bytes=48 * 1024 * 1024),
        name="attn_projections",
    )(x, wr, br, wc, bc, cos_r, sin_r, cos_c, sin_c)


MLSTM_HEADS = 8
MLSTM_QK_DIM = 64
MLSTM_V_DIM = 128
MLSTM_CHUNK = 128
CONV_WIDTH = 4
HALO = 8


def _silu(x):
    return x / (1.0 + jnp.exp(-x))


def _sigmoid(x):
    return 1.0 / (1.0 + jnp.exp(-x))


def _log_sigmoid(x):
    return jnp.minimum(x, 0.0) - jnp.log(1.0 + jnp.exp(-jnp.abs(x)))


def _mlstm_proj_kernel(x_ref, w_ref, b_ref, wt_ref, bt_ref, conv_ref,
                       mqt_ref, mk_ref, mvt_ref, ogt_ref, ifc_ref, ift_ref, ext_ref, *, q_scale):
    i = pl.program_id(1)
    tm = x_ref.shape[1]
    qkw = 2 * mk_ref.shape[2]
    vw = mvt_ref.shape[1]
    nh = ift_ref.shape[1] // 2
    xb = x_ref[0].astype(BF16)

    @pl.when(i == 0)
    def _():
        ext_ref[0:HALO, :] = jnp.zeros((HALO, qkw), F32)

    pqk = jnp.dot(xb, w_ref[:, 0:qkw], preferred_element_type=F32) + b_ref[:, 0:qkw]
    ext_ref[HALO:HALO + tm, :] = pqk
    acc = None
    for j in range(CONV_WIDTH):
        term = ext_ref[pl.ds(HALO - CONV_WIDTH + 1 + j, tm), :] * conv_ref[j:j + 1, :]
        acc = term if acc is None else acc + term
    ext_ref[0:HALO, :] = pqk[tm - HALO:tm, :]
    qk = _silu(acc)
    mqt_ref[0] = jnp.transpose(qk[:, 0:qkw // 2] * q_scale).astype(mqt_ref.dtype)
    mk_ref[0] = qk[:, qkw // 2:qkw].astype(mk_ref.dtype)

    pg = jnp.dot(xb, w_ref[:, qkw:], preferred_element_type=F32) + b_ref[:, qkw:]
    lane = lax.broadcasted_iota(jnp.int32, pg.shape, 1)
    ifc_ref[0] = jnp.where(lane < nh, pg, _log_sigmoid(pg))

    pt = lax.dot_general(wt_ref[...], xb, (((1,), (1,)), ((), ())),
                         preferred_element_type=F32) + bt_ref[...]
    mvt_ref[0] = pt[0:vw].astype(mvt_ref.dtype)
    ogt_ref[0] = _sigmoid(pt[vw:2 * vw]).astype(ogt_ref.dtype)
    gates = pt[2 * vw:]
    rowi = lax.broadcasted_iota(jnp.int32, gates.shape, 0)
    ift_ref[0] = jnp.where(rowi < nh, gates, _log_sigmoid(gates))


def mlstm_projections(x, w, b, wt, bt, conv, *, tm=512):
    B, S, D = x.shape
    qw = MLSTM_HEADS * MLSTM_QK_DIM
    vw = MLSTM_HEADS * MLSTM_V_DIM
    tm = min(tm, S)
    kern = functools.partial(_mlstm_proj_kernel, q_scale=MLSTM_QK_DIM ** -0.5)
    full = lambda a: pl.BlockSpec(a.shape, lambda b_, i: (0,) * a.ndim)
    row = lambda w_: pl.BlockSpec((1, tm, w_), lambda b_, i: (b_, i, 0))
    colm = lambda r: pl.BlockSpec((1, r, tm), lambda b_, i: (b_, 0, i))
    return pl.pallas_call(
        kern,
        grid=(B, S // tm),
        in_specs=[row(D), full(w), full(b), full(wt), full(bt), full(conv)],
        out_specs=[colm(qw), row(qw), colm(vw), colm(vw), row(LANES), colm(2 * MLSTM_HEADS)],
        out_shape=[jax.ShapeDtypeStruct((B, qw, S), BF16),
                   jax.ShapeDtypeStruct((B, S, qw), BF16),
                   jax.ShapeDtypeStruct((B, vw, S), BF16),
                   jax.ShapeDtypeStruct((B, vw, S), BF16),
                   jax.ShapeDtypeStruct((B, S, LANES), F32),
                   jax.ShapeDtypeStruct((B, 2 * MLSTM_HEADS, S), F32)],
        scratch_shapes=[pltpu.VMEM((HALO + tm, 2 * qw), F32)],
        compiler_params=pltpu.CompilerParams(
            dimension_semantics=("arbitrary", "arbitrary"),
            vmem_limit_bytes=56 * 1024 * 1024),
        name="mlstm_projections",
    )(x, w, b, wt, bt, conv)


GN_EPS = 1e-6


def _mlstm_kernel(mqt_ref, mk_ref, mvt_ref, ogt_ref, ifc_ref, ift_ref, gain_ref, y_ref,
                  ct_ref, m_ref):
    c = pl.program_id(1)
    L = mk_ref.shape[1]
    nh = ift_ref.shape[1] // 2
    dk = mk_ref.shape[2] // nh
    dv = mvt_ref.shape[1] // nh

    @pl.when(c == 0)
    def _():
        ct_ref[...] = jnp.zeros(ct_ref.shape, F32)
        m_ref[...] = jnp.zeros(m_ref.shape, F32)

    r_i = lax.broadcasted_iota(jnp.int32, (L, L), 0)
    c_i = lax.broadcasted_iota(jnp.int32, (L, L), 1)
    causal_t = r_i <= c_i
    tril = jnp.where(c_i <= r_i, 1.0, 0.0)
    triu = jnp.where(causal_t, 1.0, 0.0)
    ifc = ifc_ref[0]
    ift = ift_ref[0]
    b_cols = jnp.dot(tril, ifc, preferred_element_type=F32, precision=lax.Precision.HIGHEST)
    b_rows = jnp.dot(ift, triu, preferred_element_type=F32, precision=lax.Precision.HIGHEST)
    ones_rows = jnp.where(lax.broadcasted_iota(jnp.int32, (BF16_ROWS, L), 0) == 0,
                          1.0, 0.0).astype(BF16)
    first_half_rows = lax.broadcasted_iota(jnp.int32, (2 * dk, L), 0) < dk
    first_half_lanes = lax.broadcasted_iota(jnp.int32, (L, 2 * dk), 1) < dk

    k_pair, q_heads, s_heads, inter_heads = [], [], [], []
    for p in range(nh // 2):
        k_pair.append(mk_ref[0, :, 2 * p * dk:2 * (p + 1) * dk])
        qt_pair = mqt_ref[0, 2 * p * dk:2 * (p + 1) * dk, :]
        zero = jnp.zeros_like(qt_pair)
        q_heads.append(jnp.where(first_half_rows, qt_pair, zero))
        q_heads.append(jnp.where(first_half_rows, zero, qt_pair))
    for h in range(nh):
        s_heads.append(jnp.dot(k_pair[h // 2], q_heads[h], preferred_element_type=F32))
        inter_heads.append(jnp.dot(ct_ref[h].astype(BF16), q_heads[h],
                                   preferred_element_type=F32))

    for h in range(nh):
        m = m_ref[h]
        a_col = ifc[:, h:h + 1] - b_cols[:, nh + h:nh + h + 1]
        amat = jnp.where(causal_t, a_col, NEG_INF)
        big_m = jnp.maximum(m, jnp.max(amat, axis=0, keepdims=True))
        decay = jnp.exp(amat - big_m)
        w_inter = jnp.exp(m - big_m)
        s = (s_heads[h] * decay).astype(BF16)
        vt_ext = jnp.concatenate([mvt_ref[0, h * dv:(h + 1) * dv, :], ones_rows], axis=0)
        tot = jnp.dot(vt_ext, s, preferred_element_type=F32) + w_inter * inter_heads[h]
        den = tot[dv:dv + 1]
        b_row = b_rows[nh + h:nh + h + 1, :]
        hh = tot[0:dv] / jnp.maximum(jnp.abs(den), jnp.exp(-(b_row + big_m)))
        mu = jnp.mean(hh, axis=0, keepdims=True)
        xc = hh - mu
        var = jnp.mean(xc * xc, axis=0, keepdims=True)
        hn = (xc * lax.rsqrt(var + GN_EPS) * gain_ref[h * dv:(h + 1) * dv, :]
              * ogt_ref[0, h * dv:(h + 1) * dv, :].astype(F32))
        y_ref[0, :, h * dv:(h + 1) * dv] = jnp.transpose(hn).astype(y_ref.dtype)

        b_last = b_cols[L - 1:L, nh + h:nh + h + 1]
        g_col = b_last + a_col
        m_new = jnp.maximum(b_last + m, jnp.max(g_col, axis=0, keepdims=True))
        carry = jnp.exp(b_last + m - m_new)
        head_lanes = first_half_lanes if h % 2 == 0 else jnp.logical_not(first_half_lanes)
        wk = jnp.where(head_lanes, k_pair[h // 2].astype(F32) * jnp.exp(g_col - m_new),
                       0.0).astype(BF16)
        ct_ref[h] = carry * ct_ref[h] + jnp.dot(vt_ext, wk, preferred_element_type=F32)
        m_ref[h] = m_new


def mlstm_scan(mqt, mk, mvt, ogt, ifc, ift, gain_b):
    B, S, qw = mk.shape
    vw = mvt.shape[1]
    L = min(MLSTM_CHUNK, S)
    nh = ift.shape[1] // 2
    row = lambda w_: pl.BlockSpec((1, L, w_), lambda b_, c: (b_, c, 0))
    colm = lambda r: pl.BlockSpec((1, r, L), lambda b_, c: (b_, 0, c))
    return pl.pallas_call(
        _mlstm_kernel,
        grid=(B, S // L),
        in_specs=[colm(qw), row(qw), colm(vw), colm(vw), row(LANES), colm(2 * nh),
                  pl.BlockSpec(gain_b.shape, lambda b_, c: (0, 0))],
        out_specs=row(vw),
        out_shape=jax.ShapeDtypeStruct((B, S, vw), BF16),
        scratch_shapes=[pltpu.VMEM((nh, vw // nh + BF16_ROWS, 2 * qw // nh), F32),
                        pltpu.VMEM((nh, 1, 1), F32)],
        compiler_params=pltpu.CompilerParams(
            dimension_semantics=("arbitrary", "arbitrary")),
        name="mlstm_scan",
    )(mqt, mk, mvt, ogt, ifc, ift, gain_b)


LN_EPS = 1e-5


def _layer_norm(z, gain, bias):
    mu = jnp.mean(z, axis=1, keepdims=True)
    zc = z - mu
    var = jnp.mean(zc * zc, axis=1, keepdims=True)
    return zc * lax.rsqrt(var + LN_EPS) * gain + bias


def _merge_kernel(x_ref, ya_ref, ym_ref, wg_ref, bg_ref, wa_ref, wm_ref, wo_ref, g_ref, b_ref,
                  o_ref, *, alpha):
    x = x_ref[...]
    xb = x.astype(BF16)
    d = x.shape[1]
    ga = _sigmoid(jnp.dot(xb, wg_ref[:, 0:d], preferred_element_type=F32) + bg_ref[:, 0:d])
    merged = ga * jnp.dot(ya_ref[...], wa_ref[...], preferred_element_type=F32)
    gm = _sigmoid(jnp.dot(xb, wg_ref[:, d:2 * d], preferred_element_type=F32) + bg_ref[:, d:2 * d])
    merged = merged + gm * jnp.dot(ym_ref[...], wm_ref[...], preferred_element_type=F32)
    z = alpha * x + jnp.dot(merged.astype(BF16), wo_ref[...], preferred_element_type=F32)
    o_ref[...] = _layer_norm(z, g_ref[...], b_ref[...])


def merge_branches(x2, ya, ym, wg, bg, wa, wm, wo, g, b, *, alpha, tm=512):
    T, D = x2.shape
    tm = min(tm, T)
    full = lambda a: pl.BlockSpec(a.shape, lambda i: (0,) * a.ndim)
    row = lambda w_: pl.BlockSpec((tm, w_), lambda i: (i, 0))
    return pl.pallas_call(
        functools.partial(_merge_kernel, alpha=alpha),
        grid=(T // tm,),
        in_specs=[row(D), row(ya.shape[1]), row(ym.shape[1]), full(wg), full(bg), full(wa),
                  full(wm), full(wo), full(g), full(b)],
        out_specs=row(D),
        out_shape=jax.ShapeDtypeStruct((T, D), F32),
        compiler_params=pltpu.CompilerParams(
            dimension_semantics=("arbitrary",), vmem_limit_bytes=48 * 1024 * 1024),
        name="merge_branches",
    )(x2, ya, ym, wg, bg, wa, wm, wo, g, b)


N_GROUPS = 4
EXPERTS_PER_GROUP = 4
N_EXPERTS = N_GROUPS * EXPERTS_PER_GROUP


def _first_lane_of_max(vals, vmax, lane):
    return jnp.min(jnp.where(vals == vmax, lane, LANES), axis=1, keepdims=True)


MOE_SUB = 128


def _moe_kernel(x_ref, wr_ref, br_ref, wg_ref, wu_ref, wd_ref, g_ref, b_ref, o_ref,
                tril_ref, slot_ref, xg_ref, cw_ref, y_ref, tiles_ref, *, alpha):
    w = pl.program_id(0)
    e = pl.program_id(1)
    tm = x_ref.shape[0]
    slots = xg_ref.shape[0]
    lane = lax.broadcasted_iota(jnp.int32, (tm, LANES), 1)

    @pl.when(jnp.logical_and(w == 0, e == 0))
    def _():
        r_i = lax.broadcasted_iota(jnp.int32, (tm, tm), 0)
        c_i = lax.broadcasted_iota(jnp.int32, (tm, tm), 1)
        tril_ref[...] = jnp.where(c_i < r_i, 1.0, 0.0).astype(BF16)

    @pl.when(e == 0)
    def _():
        x = x_ref[...]
        x_hi = x.astype(BF16)
        x_lo = (x - x_hi.astype(F32)).astype(BF16)
        w_hi = wr_ref[...].astype(BF16)
        w_lo = (wr_ref[...] - w_hi.astype(F32)).astype(BF16)
        logits = (jnp.dot(x_hi, w_hi, preferred_element_type=F32)
                  + jnp.dot(x_lo, w_hi, preferred_element_type=F32)
                  + jnp.dot(x_hi, w_lo, preferred_element_type=F32)
                  + br_ref[...])
        g = jnp.where(lane < N_GROUPS, logits, NEG_INF)
        gmax = jnp.max(g, axis=1, keepdims=True)
        g_w = 1.0 / jnp.sum(jnp.exp(g - gmax), axis=1, keepdims=True)
        g_sel = _first_lane_of_max(g, gmax, lane)
        lo = N_GROUPS + EXPERTS_PER_GROUP * g_sel
        ev = jnp.where(jnp.logical_and(lane >= lo, lane < lo + EXPERTS_PER_GROUP), logits, NEG_INF)
        v1 = jnp.max(ev, axis=1, keepdims=True)
        i1 = _first_lane_of_max(ev, v1, lane)
        ev2 = jnp.where(lane == i1, NEG_INF, ev)
        v2 = jnp.max(ev2, axis=1, keepdims=True)
        i2 = _first_lane_of_max(ev2, v2, lane)
        r = jnp.exp(v2 - v1)
        p1 = 1.0 / (1.0 + r)
        p2 = r / (1.0 + r)
        comb = jnp.where(lane == i1, g_w * p1, 0.0) + jnp.where(lane == i2, g_w * p2, 0.0)
        comb_hi = comb.astype(BF16)
        comb_lo = (comb - comb_hi.astype(F32)).astype(BF16)

        onehot = jnp.where(lane == g_sel, 1.0, 0.0)
        before = jnp.dot(tril_ref[...], onehot.astype(BF16), preferred_element_type=F32)
        rank = jnp.sum(jnp.where(lane == g_sel, before, 0.0), axis=1, keepdims=True)
        total = jnp.sum(onehot, axis=0, keepdims=True)
        lane1 = lax.broadcasted_iota(jnp.int32, (1, LANES), 1)
        start = jnp.zeros((tm, 1), F32)
        first_tile = jnp.int32(0)
        for grp in range(N_GROUPS):
            n_tok = jnp.sum(jnp.where(lane1 == grp, total, 0.0)).astype(jnp.int32)
            n_tile = (n_tok + MOE_SUB - 1) // MOE_SUB
            tiles_ref[grp] = first_tile
            tiles_ref[N_GROUPS + grp] = n_tile
            start = jnp.where(g_sel == grp, (first_tile * MOE_SUB).astype(F32), start)
            first_tile = first_tile + n_tile
        slot = start + rank
        slot_ref[...] = slot
        slot_row = jnp.transpose(jnp.broadcast_to(slot, (tm, LANES)))[0:1]
        pick = slot_row == lax.broadcasted_iota(jnp.int32, (slots, tm), 0).astype(F32)
        pmat = jnp.where(pick, 1.0, 0.0).astype(BF16)
        xg_ref[...] = jnp.dot(pmat, x_hi, preferred_element_type=F32).astype(BF16)
        cw_ref[...] = (jnp.dot(pmat, comb_hi, preferred_element_type=F32)
                       + jnp.dot(pmat, comb_lo, preferred_element_type=F32))
        y_ref[...] = jnp.zeros(y_ref.shape, F32)

    grp = e // EXPERTS_PER_GROUP
    first = tiles_ref[grp]

    n_tile = tiles_ref[N_GROUPS + grp]

    def expert(tile, rows):
        base = pl.multiple_of(tile * MOE_SUB, MOE_SUB)
        xg = xg_ref[pl.ds(base, rows), :]
        hg = jnp.dot(xg, wg_ref[0], preferred_element_type=F32)
        hu = jnp.dot(xg, wu_ref[0], preferred_element_type=F32)
        hdn = (_silu(hg) * hu).astype(BF16)
        lane_s = lax.broadcasted_iota(jnp.int32, (rows, LANES), 1)
        cw = jnp.sum(jnp.where(lane_s == N_GROUPS + e, cw_ref[pl.ds(base, rows), :], 0.0),
                     axis=1, keepdims=True)
        y_ref[pl.ds(base, rows), :] += cw * jnp.dot(hdn, wd_ref[0], preferred_element_type=F32)

    def expert_pair(j, carry):
        expert(first + 2 * j, 2 * MOE_SUB)
        return carry
    lax.fori_loop(0, n_tile // 2, expert_pair, 0)

    @pl.when(n_tile % 2 == 1)
    def _():
        expert(first + n_tile - 1, MOE_SUB)

    @pl.when(e == pl.num_programs(1) - 1)
    def _():
        pick = slot_ref[...] == lax.broadcasted_iota(jnp.int32, (tm, slots), 1).astype(F32)
        pmat_t = jnp.where(pick, 1.0, 0.0).astype(BF16)
        moe = jnp.dot(pmat_t, y_ref[...].astype(BF16), preferred_element_type=F32)
        o_ref[...] = _layer_norm(alpha * x_ref[...] + moe, g_ref[...], b_ref[...])


def moe_layer(x2, wr, br, wg, wu, wd, g, b, *, alpha, tm=1024):
    T, D = x2.shape
    E, _, F = wg.shape
    tm = min(tm, T)
    slots = tm + N_GROUPS * MOE_SUB
    full = lambda a: pl.BlockSpec(a.shape, lambda i, e: (0,) * a.ndim)
    return pl.pallas_call(
        functools.partial(_moe_kernel, alpha=alpha),
        grid=(T // tm, E),
        in_specs=[pl.BlockSpec((tm, D), lambda i, e: (i, 0)), full(wr), full(br),
                  pl.BlockSpec((1, D, F), lambda i, e: (e, 0, 0)),
                  pl.BlockSpec((1, D, F), lambda i, e: (e, 0, 0)),
                  pl.BlockSpec((1, F, D), lambda i, e: (e, 0, 0)),
                  full(g), full(b)],
        out_specs=pl.BlockSpec((tm, D), lambda i, e: (i, 0)),
        out_shape=jax.ShapeDtypeStruct((T, D), F32),
        scratch_shapes=[pltpu.VMEM((tm, tm), BF16),
                        pltpu.VMEM((tm, 1), F32),
                        pltpu.VMEM((slots, D), BF16),
                        pltpu.VMEM((slots, LANES), F32),
                        pltpu.VMEM((slots, D), F32),
                        pltpu.SMEM((2 * N_GROUPS,), jnp.int32)],
        compiler_params=pltpu.CompilerParams(
            dimension_semantics=("arbitrary", "arbitrary"), vmem_limit_bytes=56 * 1024 * 1024),
        name="moe_layer",
    )(x2, wr, br, wg, wu, wd, g, b)


DEPTH = 1
DEEPNORM_ALPHA = (2.0 * DEPTH) ** 0.25


def _pad_cols(a, width):
    return jnp.pad(a, ((0, 0), (0, width - a.shape[1])))


def kernel(x, w_in, b_in, conv_m, gn_m_gain, w_branch_attn, w_branch_mlstm, w_out, ln1_gain, ln1_bias, w_router_group, b_router_group, w_router_expert, b_router_expert, w_exp_gate, w_exp_up, w_exp_down, ln2_gain, ln2_bias):
    B, S, D = x.shape
    aw = ATT_HEADS * ATT_HEAD_DIM
    iw = IDX_HEADS * IDX_DIM
    qw = MLSTM_HEADS * MLSTM_QK_DIM
    vw = MLSTM_HEADS * MLSTM_V_DIM
    widths = (aw, aw, aw, iw, IDX_DIM, IDX_HEADS, qw, qw, vw, MLSTM_HEADS, MLSTM_HEADS, vw, D, D)
    offs = [0]
    for w_ in widths:
        offs.append(offs[-1] + w_)
    col = lambda k: w_in[:, offs[k]:offs[k + 1]]
    bia = lambda k: b_in[offs[k]:offs[k + 1]]
    (A_Q, A_K, A_V, I_Q, I_K, I_W, M_Q, M_K, M_V, M_I, M_F, M_O, G_A, G_M) = range(14)

    wr = jnp.concatenate([col(A_K), _pad_cols(col(I_K), LANES)], 1).astype(BF16)
    br = jnp.concatenate([bia(A_K), jnp.pad(bia(I_K), (0, LANES - IDX_DIM))])[None, :]
    wc = jnp.concatenate([col(A_Q), col(A_V), col(I_Q), _pad_cols(col(I_W), 8)], 1).T.astype(BF16)
    bc = jnp.concatenate([bia(A_Q), bia(A_V), bia(I_Q), jnp.pad(bia(I_W), (0, 8 - IDX_HEADS))])[:, None]
    gpad = LANES - 2 * MLSTM_HEADS
    wm = jnp.concatenate([col(M_Q), col(M_K), col(M_I),
                          _pad_cols(col(M_F), MLSTM_HEADS + gpad)], 1).astype(BF16)
    bm = jnp.concatenate([bia(M_Q), bia(M_K), bia(M_I), jnp.pad(bia(M_F), (0, gpad))])[None, :]
    wmt = jnp.concatenate([col(M_V), col(M_O), col(M_I), col(M_F)], 1).T.astype(BF16)
    bmt = jnp.concatenate([bia(M_V), bia(M_O), bia(M_I), bia(M_F)])[:, None]
    wgate = jnp.concatenate([col(G_A), col(G_M)], 1).astype(BF16)
    bgate = jnp.concatenate([bia(G_A), bia(G_M)])[None, :]

    half = ATT_HEAD_DIM // 2
    inv = ROPE_THETA ** (-jnp.arange(0, ATT_HEAD_DIM, 2, dtype=F32) / ATT_HEAD_DIM)
    ang = jnp.arange(S, dtype=F32)[:, None] * inv[None, :]
    cos, sin = jnp.cos(ang), jnp.sin(ang)
    cos_r = jnp.tile(cos, (1, LANES // half))
    sin_r = jnp.tile(jnp.concatenate([-sin, sin], 1), (1, LANES // ATT_HEAD_DIM))

    qt, k, vt, qit, ki, wit = attn_projections(x, wr, br, wc, bc, cos_r, sin_r, cos.T, sin.T)
    y_attn = dsa_attention(qt, k, vt, qit, ki, wit, topk=min(IDX_TOPK_MAX, S // 4))

    mqt, mk, mvt, ogt, ifc, ift = mlstm_projections(x, wm, bm, wmt, bmt, conv_m)
    y_mlstm = mlstm_scan(mqt, mk, mvt, ogt, ifc, ift,
                         jnp.broadcast_to(gn_m_gain[:, None], (vw, LANES)))

    x1 = merge_branches(x.reshape(B * S, D), y_attn.reshape(B * S, aw), y_mlstm.reshape(B * S, vw),
                        wgate, bgate, w_branch_attn.astype(BF16), w_branch_mlstm.astype(BF16),
                        w_out.astype(BF16), ln1_gain[None, :], ln1_bias[None, :],
                        alpha=DEEPNORM_ALPHA)

    w_router = _pad_cols(jnp.concatenate([w_router_group, w_router_expert], 1), LANES)
    b_router = jnp.pad(jnp.concatenate([b_router_group, b_router_expert]),
                       (0, LANES - N_GROUPS - N_EXPERTS))[None, :]
    out = moe_layer(x1, w_router, b_router, w_exp_gate.astype(BF16), w_exp_up.astype(BF16),
                    w_exp_down.astype(BF16), ln2_gain[None, :], ln2_bias[None, :],
                    alpha=DEEPNORM_ALPHA)
    return out.reshape(B, S, D)
```

```python
import functools

import jax
import jax.numpy as jnp
from jax import lax
from jax.experimental import pallas as pl
from jax.experimental.pallas import tpu as pltpu

F32 = jnp.float32
BF16 = jnp.bfloat16
NEG_INF = float("-inf")
LOG2_E = 1.4426950408889634

ATT_HEADS = 8
ATT_HEAD_DIM = 64
IDX_HEADS = 4
IDX_DIM = 64
IDX_TOPK_MAX = 256
Q_BLOCK = 128
ROPE_THETA = 10000.0

LANES = 128
BF16_ROWS = 16
V7X_VMEM_BYTES = 64 * 1024 * 1024
VMEM_LIMIT_BIG = V7X_VMEM_BYTES * 7 // 8
VMEM_LIMIT_MID = V7X_VMEM_BYTES * 3 // 4


def _key_to_f32(u):
    ks = u ^ jnp.int32(-2 ** 31)
    bits = ks ^ ((ks >> 31) & jnp.int32(0x7FFFFFFF))
    return lax.bitcast_convert_type(bits, F32)


def _dsa_kernel(qt_ref, k_ref, vt_ref, qit_ref, ki_ref, wit_ref, o_ref,
                sc_ref, qm_ref, m_ref, l_ref, acc_ref, *, topk, kchunk):
    qb = pl.program_id(1)
    tq = o_ref.shape[1]
    heads = qt_ref.shape[1] // ATT_HEAD_DIM
    step = 2 * LANES
    n_chunk = (qb * tq + tq + kchunk - 1) // kchunk
    qpos = lax.broadcasted_iota(jnp.int32, (1, tq), 1) + qb * tq

    qit = qit_ref[0]
    zpad = jnp.zeros((LANES - IDX_DIM, tq), BF16)
    qi_pair = []
    for p in range(IDX_HEADS // 2):
        cols = [jnp.concatenate([qit[h * IDX_DIM:(h + 1) * IDX_DIM], zpad], axis=0)
                for h in (2 * p, 2 * p + 1)]
        qi_pair.append(jnp.concatenate(cols, axis=1))
    wit = wit_ref[0]

    achunk = min(2 * kchunk, sc_ref.shape[0])
    n_achunk = (qb * tq + tq + achunk - 1) // achunk
    pairs = IDX_HEADS // 2

    def score_body(c, carry):
        off = pl.multiple_of(c * achunk, achunk)

        def qk(i):
            ki = ki_ref[0, pl.ds(off + (i // pairs) * step, step), :]
            return jnp.dot(ki, qi_pair[i % pairs], preferred_element_type=F32)

        n_dots = (achunk // step) * pairs
        ahead = 3
        pending = [qk(i) for i in range(ahead)]
        for t in range(achunk // step):
            tot = None
            for p in range(pairs):
                i = t * pairs + p
                if i + ahead < n_dots:
                    pending.append(qk(i + ahead))
                s2 = pending.pop(0)
                for j in range(2):
                    h = 2 * p + j
                    s = jnp.maximum(s2[:, j * tq:(j + 1) * tq], 0.0) * wit[h:h + 1, :]
                    tot = s if tot is None else tot + s
            kpos = lax.broadcasted_iota(jnp.int32, (step, tq), 0) + (off + t * step)
            sc_ref[pl.ds(off + t * step, step), :] = jnp.where(kpos <= qpos, tot + 0.0, NEG_INF)
        return carry

    lax.fori_loop(0, n_achunk, score_body, 0)

    def count(pred):
        def body(j, acc):
            off = pl.multiple_of(j * kchunk, kchunk)
            for t in range(kchunk // LANES):
                x = sc_ref[pl.ds(off + t * LANES, LANES), :]
                acc = acc + jnp.where(pred(x, off + t * LANES), 1.0, 0.0)
            return acc
        acc = lax.fori_loop(0, n_chunk, body, jnp.zeros((LANES, tq), F32))
        return jnp.sum(acc, axis=0, keepdims=True)

    kf = float(topk)
    short = qpos < topk

    def bit_pass(i, state, frozen):
        u, cge, cgt = state
        cand = u | (jnp.int32(1) << (31 - i))
        thr = _key_to_f32(cand)
        cnt = count(lambda x, off: x >= thr)
        ok = jnp.logical_and(cnt >= kf, jnp.logical_not(frozen))
        fail = jnp.logical_and(cnt < kf, jnp.logical_not(frozen))
        return (jnp.where(ok, cand, u), jnp.where(ok, cnt, cge), jnp.where(fail, cnt, cgt))

    c_pos = count(lambda x, off: x > 0.0)
    never = jnp.zeros((1, tq), jnp.bool_)
    state = bit_pass(0, (jnp.zeros((1, tq), jnp.int32), jnp.zeros((1, tq), F32),
                         jnp.zeros((1, tq), F32)), never)
    frozen = jnp.logical_and(c_pos < kf, state[1] >= kf)
    state = (state[0], state[1], jnp.where(frozen, c_pos, state[2]))

    fixed_bits = 20
    state = lax.fori_loop(1, fixed_bits, lambda i, st: bit_pass(i, st, frozen), state)

    def all_settled(st):
        done = jnp.logical_or(jnp.logical_or(short, frozen), st[1] == kf)
        return jnp.min(jnp.where(done, 1.0, 0.0)) > 0.0

    def refine(carry):
        i, st, _ = carry
        st = bit_pass(i, st, frozen)
        st = bit_pass(i + 1, st, frozen)
        return i + 2, st, all_settled(st)

    _, (u, cge, cgt), _ = lax.while_loop(
        lambda c: jnp.logical_and(c[0] < 32, jnp.logical_not(c[2])),
        refine, (jnp.int32(fixed_bits), state, all_settled(state)))
    tau = jnp.where(short, NEG_INF, _key_to_f32(u))
    need = jnp.where(cge == kf, kf, kf - cgt)

    r_i = lax.broadcasted_iota(jnp.int32, (LANES, LANES), 0)
    c_i = lax.broadcasted_iota(jnp.int32, (LANES, LANES), 1)
    tril = jnp.where(c_i <= r_i, 1.0, 0.0).astype(BF16)

    def bias_body(c, seen):
        base = pl.multiple_of(c * achunk, achunk)
        ranks = [jnp.dot(tril, jnp.where(sc_ref[pl.ds(base + t * LANES, LANES), :] == tau,
                                         1.0, 0.0).astype(BF16), preferred_element_type=F32)
                 for t in range(achunk // LANES)]
        for t, rank in enumerate(ranks):
            x = sc_ref[pl.ds(base + t * LANES, LANES), :]
            keep = jnp.logical_or(x > tau, jnp.logical_and(x == tau, rank + seen <= need))
            kpos = lax.broadcasted_iota(jnp.int32, x.shape, 0) + (base + t * LANES)
            keep = jnp.logical_and(keep, kpos <= qpos)
            sc_ref[pl.ds(base + t * LANES, LANES), :] = jnp.where(keep, 0.0, NEG_INF)
            seen = seen + rank[LANES - 1:LANES, :]
        return seen

    lax.fori_loop(0, n_achunk, bias_body, jnp.zeros((1, tq), F32))

    qt = qt_ref[0]
    zrow = jnp.zeros((ATT_HEAD_DIM, tq), BF16)
    for p in range(heads // 2):
        a = qt[(2 * p) * ATT_HEAD_DIM:(2 * p + 1) * ATT_HEAD_DIM]
        b = qt[(2 * p + 1) * ATT_HEAD_DIM:(2 * p + 2) * ATT_HEAD_DIM]
        qm_ref[p] = jnp.concatenate([jnp.concatenate([a, zrow], axis=0),
                                     jnp.concatenate([zrow, b], axis=0)], axis=1)
    m_ref[...] = jnp.full(m_ref.shape, NEG_INF, F32)
    l_ref[...] = jnp.zeros(l_ref.shape, F32)
    acc_ref[...] = jnp.zeros(acc_ref.shape, F32)
    ones_rows = jnp.ones((BF16_ROWS, step), BF16)

    def attend(off, n_keys):
        stages = [(sub, p) for sub in range(n_keys // step) for p in range(heads // 2)]

        def qk(stage):
            sub, p = stage
            kp = k_ref[0, pl.ds(off + sub * step, step), p * LANES:(p + 1) * LANES]
            return jnp.dot(kp, qm_ref[p], preferred_element_type=F32)

        ahead = 4
        pending = [qk(st) for st in stages[:ahead]]
        for i, (sub, p) in enumerate(stages):
            if i + ahead < len(stages):
                pending.append(qk(stages[i + ahead]))
            s2 = pending.pop(0)
            koff = off + sub * step
            bias = sc_ref[pl.ds(koff, step), :]
            for j in range(2):
                h = 2 * p + j
                s = s2[:, j * tq:(j + 1) * tq] + bias
                m_old = m_ref[h]
                m_new = jnp.maximum(m_old, jnp.max(s, axis=0, keepdims=True))
                m_use = jnp.where(m_new == NEG_INF, 0.0, m_new)
                pexp = jnp.exp2(s - m_use).astype(BF16)
                alpha = jnp.exp2(m_old - m_use)
                vt = jnp.concatenate(
                    [vt_ref[0, h * ATT_HEAD_DIM:(h + 1) * ATT_HEAD_DIM, pl.ds(koff, step)],
                     ones_rows], axis=0)
                pv = jnp.dot(vt, pexp, preferred_element_type=F32)
                acc_ref[h] = alpha * acc_ref[h] + pv[0:ATT_HEAD_DIM]
                l_ref[h] = alpha * l_ref[h] + pv[ATT_HEAD_DIM:ATT_HEAD_DIM + 1]
                m_ref[h] = m_new

    big = min(4 * kchunk, sc_ref.shape[0])
    n_big = (n_chunk * kchunk) // big
    rest = n_chunk - n_big * (big // kchunk)

    def big_body(c, carry):
        attend(pl.multiple_of(c * big, big), big)
        return carry

    lax.fori_loop(0, n_big, big_body, 0)
    if big > kchunk:
        tail = pl.multiple_of(n_big * big, kchunk)

        @pl.when(rest >= 2)
        def _():
            attend(tail, 2 * kchunk)

        @pl.when(rest % 2 == 1)
        def _():
            attend(pl.multiple_of(tail + (rest - 1) * kchunk, kchunk), kchunk)

    out_t = jnp.concatenate([acc_ref[h] / l_ref[h] for h in range(heads)], axis=0)
    o_ref[0] = jnp.transpose(out_t).astype(o_ref.dtype)


def dsa_attention(qt, k, vt, qit, ki, wit, *, topk, kchunk=512):
    B, W, S = qt.shape
    tq = Q_BLOCK
    kchunk = min(kchunk, S)
    heads = W // ATT_HEAD_DIM
    kern = functools.partial(_dsa_kernel, topk=topk, kchunk=kchunk)
    qcol = lambda a: pl.BlockSpec((1, a.shape[1], tq), lambda b, i: (b, 0, i))
    whole = lambda a: pl.BlockSpec((1,) + a.shape[1:], lambda b, i: (b, 0, 0))
    return pl.pallas_call(
        kern,
        grid=(B, S // tq),
        in_specs=[qcol(qt), whole(k), whole(vt), qcol(qit), whole(ki), qcol(wit)],
        out_specs=pl.BlockSpec((1, tq, W), lambda b, i: (b, i, 0)),
        out_shape=jax.ShapeDtypeStruct((B, S, W), BF16),
        scratch_shapes=[
            pltpu.VMEM((S, tq), F32),
            pltpu.VMEM((heads // 2, LANES, 2 * tq), BF16),
            pltpu.VMEM((heads, 1, tq), F32),
            pltpu.VMEM((heads, 1, tq), F32),
            pltpu.VMEM((heads, ATT_HEAD_DIM, tq), F32),
        ],
        compiler_params=pltpu.CompilerParams(
            dimension_semantics=("arbitrary", "arbitrary"),
            vmem_limit_bytes=VMEM_LIMIT_BIG),
        name="dsa_attention",
    )(qt, k, vt, qit, ki, wit)


def _rope_rows(x, cos_t, sin_t):
    lane = lax.broadcasted_iota(jnp.int32, x.shape, 1)
    swapped = jnp.where((lane % ATT_HEAD_DIM) < ATT_HEAD_DIM // 2,
                        pltpu.roll(x, LANES - ATT_HEAD_DIM // 2, 1),
                        pltpu.roll(x, ATT_HEAD_DIM // 2, 1))
    return x * cos_t + swapped * sin_t


def _attn_proj_kernel(x_ref, wr_ref, br_ref, wc_ref, bc_ref, cos_ref, sin_ref, cost_ref, sint_ref,
                      qt_ref, k_ref, vt_ref, qit_ref, ki_ref, wit_ref, *, q_scale, wi_scale):
    xb = x_ref[0].astype(BF16)
    aw = k_ref.shape[2]
    iw = qit_ref.shape[1]
    half = ATT_HEAD_DIM // 2
    cos_t = cos_ref[...]
    sin_t = sin_ref[...]

    pk = jnp.dot(xb, wr_ref[...], preferred_element_type=F32) + br_ref[...]
    for j in range(aw // LANES):
        sl = slice(j * LANES, (j + 1) * LANES)
        k_ref[0, :, sl] = _rope_rows(pk[:, sl], cos_t, sin_t).astype(k_ref.dtype)
    ki_ref[0] = _rope_rows(pk[:, aw:aw + LANES], cos_t, sin_t).astype(ki_ref.dtype)

    pt = lax.dot_general(wc_ref[...], xb, (((1,), (1,)), ((), ())),
                         preferred_element_type=F32) + bc_ref[...]
    ct = cost_ref[...]
    st = sint_ref[...]

    def rope_cols(src0, dst_ref, nheads, scale):
        for h in range(nheads):
            r0 = src0 + h * ATT_HEAD_DIM
            x1 = pt[r0:r0 + half]
            x2 = pt[r0 + half:r0 + 2 * half]
            d0 = h * ATT_HEAD_DIM
            dst_ref[0, d0:d0 + half, :] = ((x1 * ct - x2 * st) * scale).astype(dst_ref.dtype)
            dst_ref[0, d0 + half:d0 + 2 * half, :] = ((x1 * st + x2 * ct) * scale).astype(dst_ref.dtype)

    rope_cols(0, qt_ref, aw // ATT_HEAD_DIM, q_scale)
    vt_ref[0] = pt[aw:2 * aw].astype(vt_ref.dtype)
    rope_cols(2 * aw, qit_ref, iw // IDX_DIM, 1.0)
    wit_ref[0] = pt[2 * aw + iw:] * wi_scale


def attn_projections(x, wr, br, wc, bc, cos_r, sin_r, cos_c, sin_c, *, tm=512):
    B, S, D = x.shape
    aw = ATT_HEADS * ATT_HEAD_DIM
    iw = IDX_HEADS * IDX_DIM
    tm = min(tm, S)
    kern = functools.partial(_attn_proj_kernel, q_scale=ATT_HEAD_DIM ** -0.5 * LOG2_E,
                             wi_scale=IDX_HEADS ** -0.5 * IDX_DIM ** -0.5)
    full = lambda a: pl.BlockSpec(a.shape, lambda b, i: (0,) * a.ndim)
    row = lambda w: pl.BlockSpec((1, tm, w), lambda b, i: (b, i, 0))
    colm = lambda r: pl.BlockSpec((1, r, tm), lambda b, i: (b, 0, i))
    return pl.pallas_call(
        kern,
        grid=(B, S // tm),
        in_specs=[row(D), full(wr), full(br), full(wc), full(bc),
                  pl.BlockSpec((tm, LANES), lambda b, i: (i, 0)),
                  pl.BlockSpec((tm, LANES), lambda b, i: (i, 0)),
                  pl.BlockSpec((ATT_HEAD_DIM // 2, tm), lambda b, i: (0, i)),
                  pl.BlockSpec((ATT_HEAD_DIM // 2, tm), lambda b, i: (0, i))],
        out_specs=[colm(aw), row(aw), colm(aw), colm(iw), row(LANES), colm(8)],
        out_shape=[jax.ShapeDtypeStruct((B, aw, S), BF16),
                   jax.ShapeDtypeStruct((B, S, aw), BF16),
                   jax.ShapeDtypeStruct((B, aw, S), BF16),
                   jax.ShapeDtypeStruct((B, iw, S), BF16),
                   jax.ShapeDtypeStruct((B, S, LANES), BF16),
                   jax.ShapeDtypeStruct((B, 8, S), F32)],
        compiler_params=pltpu.CompilerParams(
            dimension_semantics=("arbitrary", "arbitrary"),
            vmem_limit_bytes=VMEM_LIMIT_MID),
        name="attn_projections",
    )(x, wr, br, wc, bc, cos_r, sin_r, cos_c, sin_c)


MLSTM_HEADS = 8
MLSTM_QK_DIM = 64
MLSTM_V_DIM = 128
MLSTM_CHUNK = 128
CONV_WIDTH = 4
HALO = 8


def _silu(x):
    return x / (1.0 + jnp.exp(-x))


def _sigmoid(x):
    return 1.0 / (1.0 + jnp.exp(-x))


def _log_sigmoid(x):
    return jnp.minimum(x, 0.0) - jnp.log(1.0 + jnp.exp(-jnp.abs(x)))


def _mlstm_proj_kernel(x_ref, w_ref, b_ref, wt_ref, bt_ref, conv_ref,
                       mqt_ref, mk_ref, mvt_ref, ogt_ref, ifc_ref, ift_ref, ext_ref, *, q_scale):
    i = pl.program_id(1)
    tm = x_ref.shape[1]
    qkw = 2 * mk_ref.shape[2]
    vw = mvt_ref.shape[1]
    nh = ift_ref.shape[1] // 2
    xb = x_ref[0].astype(BF16)

    @pl.when(i == 0)
    def _():
        ext_ref[0:HALO, :] = jnp.zeros((HALO, qkw), F32)

    pqk = jnp.dot(xb, w_ref[:, 0:qkw], preferred_element_type=F32) + b_ref[:, 0:qkw]
    ext_ref[HALO:HALO + tm, :] = pqk
    acc = None
    for j in range(CONV_WIDTH):
        term = ext_ref[pl.ds(HALO - CONV_WIDTH + 1 + j, tm), :] * conv_ref[j:j + 1, :]
        acc = term if acc is None else acc + term
    ext_ref[0:HALO, :] = pqk[tm - HALO:tm, :]
    qk = _silu(acc)
    mqt_ref[0] = jnp.transpose(qk[:, 0:qkw // 2] * q_scale).astype(mqt_ref.dtype)
    mk_ref[0] = qk[:, qkw // 2:qkw].astype(mk_ref.dtype)

    pg = jnp.dot(xb, w_ref[:, qkw:], preferred_element_type=F32) + b_ref[:, qkw:]
    lane = lax.broadcasted_iota(jnp.int32, pg.shape, 1)
    ifc_ref[0] = jnp.where(lane < nh, pg, _log_sigmoid(pg))

    pt = lax.dot_general(wt_ref[...], xb, (((1,), (1,)), ((), ())),
                         preferred_element_type=F32) + bt_ref[...]
    mvt_ref[0] = pt[0:vw].astype(mvt_ref.dtype)
    ogt_ref[0] = _sigmoid(pt[vw:2 * vw]).astype(ogt_ref.dtype)
    gates = pt[2 * vw:]
    rowi = lax.broadcasted_iota(jnp.int32, gates.shape, 0)
    ift_ref[0] = jnp.where(rowi < nh, gates, _log_sigmoid(gates))


def mlstm_projections(x, w, b, wt, bt, conv, *, tm=512):
    B, S, D = x.shape
    qw = MLSTM_HEADS * MLSTM_QK_DIM
    vw = MLSTM_HEADS * MLSTM_V_DIM
    tm = min(tm, S)
    kern = functools.partial(_mlstm_proj_kernel, q_scale=MLSTM_QK_DIM ** -0.5)
    full = lambda a: pl.BlockSpec(a.shape, lambda b_, i: (0,) * a.ndim)
    row = lambda w_: pl.BlockSpec((1, tm, w_), lambda b_, i: (b_, i, 0))
    colm = lambda r: pl.BlockSpec((1, r, tm), lambda b_, i: (b_, 0, i))
    return pl.pallas_call(
        kern,
        grid=(B, S // tm),
        in_specs=[row(D), full(w), full(b), full(wt), full(bt), full(conv)],
        out_specs=[colm(qw), row(qw), colm(vw), colm(vw), row(LANES), colm(2 * MLSTM_HEADS)],
        out_shape=[jax.ShapeDtypeStruct((B, qw, S), BF16),
                   jax.ShapeDtypeStruct((B, S, qw), BF16),
                   jax.ShapeDtypeStruct((B, vw, S), BF16),
                   jax.ShapeDtypeStruct((B, vw, S), BF16),
                   jax.ShapeDtypeStruct((B, S, LANES), F32),
                   jax.ShapeDtypeStruct((B, 2 * MLSTM_HEADS, S), F32)],
        scratch_shapes=[pltpu.VMEM((HALO + tm, 2 * qw), F32)],
        compiler_params=pltpu.CompilerParams(
            dimension_semantics=("arbitrary", "arbitrary"),
            vmem_limit_bytes=VMEM_LIMIT_BIG),
        name="mlstm_projections",
    )(x, w, b, wt, bt, conv)


GN_EPS = 1e-6


def _mlstm_kernel(mqt_ref, mk_ref, mvt_ref, ogt_ref, ifc_ref, ift_ref, gain_ref, y_ref,
                  ct_ref, m_ref):
    c = pl.program_id(1)
    L = mk_ref.shape[1]
    nh = ift_ref.shape[1] // 2
    dk = mk_ref.shape[2] // nh
    dv = mvt_ref.shape[1] // nh

    @pl.when(c == 0)
    def _():
        ct_ref[...] = jnp.zeros(ct_ref.shape, F32)
        m_ref[...] = jnp.zeros(m_ref.shape, F32)

    r_i = lax.broadcasted_iota(jnp.int32, (L, L), 0)
    c_i = lax.broadcasted_iota(jnp.int32, (L, L), 1)
    causal_t = r_i <= c_i
    tril = jnp.where(c_i <= r_i, 1.0, 0.0)
    triu = jnp.where(causal_t, 1.0, 0.0)
    ifc = ifc_ref[0]
    ift = ift_ref[0]
    b_cols = jnp.dot(tril, ifc, preferred_element_type=F32, precision=lax.Precision.HIGHEST)
    b_rows = jnp.dot(ift, triu, preferred_element_type=F32, precision=lax.Precision.HIGHEST)
    ones_rows = jnp.where(lax.broadcasted_iota(jnp.int32, (BF16_ROWS, L), 0) == 0,
                          1.0, 0.0).astype(BF16)
    first_half_rows = lax.broadcasted_iota(jnp.int32, (2 * dk, L), 0) < dk
    first_half_lanes = lax.broadcasted_iota(jnp.int32, (L, 2 * dk), 1) < dk

    k_pair, q_heads, s_heads, inter_heads = [], [], [], []
    for p in range(nh // 2):
        k_pair.append(mk_ref[0, :, 2 * p * dk:2 * (p + 1) * dk])
        qt_pair = mqt_ref[0, 2 * p * dk:2 * (p + 1) * dk, :]
        zero = jnp.zeros_like(qt_pair)
        q_heads.append(jnp.where(first_half_rows, qt_pair, zero))
        q_heads.append(jnp.where(first_half_rows, zero, qt_pair))
    for h in range(nh):
        s_heads.append(jnp.dot(k_pair[h // 2], q_heads[h], preferred_element_type=F32))
        inter_heads.append(jnp.dot(ct_ref[h].astype(BF16), q_heads[h],
                                   preferred_element_type=F32))

    for h in range(nh):
        m = m_ref[h]
        a_col = ifc[:, h:h + 1] - b_cols[:, nh + h:nh + h + 1]
        amat = jnp.where(causal_t, a_col, NEG_INF)
        big_m = jnp.maximum(m, jnp.max(amat, axis=0, keepdims=True))
        decay = jnp.exp(amat - big_m)
        w_inter = jnp.exp(m - big_m)
        s = (s_heads[h] * decay).astype(BF16)
        vt_ext = jnp.concatenate([mvt_ref[0, h * dv:(h + 1) * dv, :], ones_rows], axis=0)
        tot = jnp.dot(vt_ext, s, preferred_element_type=F32) + w_inter * inter_heads[h]
        den = tot[dv:dv + 1]
        b_row = b_rows[nh + h:nh + h + 1, :]
        hh = tot[0:dv] / jnp.maximum(jnp.abs(den), jnp.exp(-(b_row + big_m)))
        mu = jnp.mean(hh, axis=0, keepdims=True)
        xc = hh - mu
        var = jnp.mean(xc * xc, axis=0, keepdims=True)
        hn = (xc * lax.rsqrt(var + GN_EPS) * gain_ref[h * dv:(h + 1) * dv, :]
              * ogt_ref[0, h * dv:(h + 1) * dv, :].astype(F32))
        y_ref[0, :, h * dv:(h + 1) * dv] = jnp.transpose(hn).astype(y_ref.dtype)

        b_last = b_cols[L - 1:L, nh + h:nh + h + 1]
        g_col = b_last + a_col
        m_new = jnp.maximum(b_last + m, jnp.max(g_col, axis=0, keepdims=True))
        carry = jnp.exp(b_last + m - m_new)
        head_lanes = first_half_lanes if h % 2 == 0 else jnp.logical_not(first_half_lanes)
        wk = jnp.where(head_lanes, k_pair[h // 2].astype(F32) * jnp.exp(g_col - m_new),
                       0.0).astype(BF16)
        ct_ref[h] = carry * ct_ref[h] + jnp.dot(vt_ext, wk, preferred_element_type=F32)
        m_ref[h] = m_new


def mlstm_scan(mqt, mk, mvt, ogt, ifc, ift, gain_b):
    B, S, qw = mk.shape
    vw = mvt.shape[1]
    L = min(MLSTM_CHUNK, S)
    nh = ift.shape[1] // 2
    row = lambda w_: pl.BlockSpec((1, L, w_), lambda b_, c: (b_, c, 0))
    colm = lambda r: pl.BlockSpec((1, r, L), lambda b_, c: (b_, 0, c))
    return pl.pallas_call(
        _mlstm_kernel,
        grid=(B, S // L),
        in_specs=[colm(qw), row(qw), colm(vw), colm(vw), row(LANES), colm(2 * nh),
                  pl.BlockSpec(gain_b.shape, lambda b_, c: (0, 0))],
        out_specs=row(vw),
        out_shape=jax.ShapeDtypeStruct((B, S, vw), BF16),
        scratch_shapes=[pltpu.VMEM((nh, vw // nh + BF16_ROWS, 2 * qw // nh), F32),
                        pltpu.VMEM((nh, 1, 1), F32)],
        compiler_params=pltpu.CompilerParams(
            dimension_semantics=("arbitrary", "arbitrary")),
        name="mlstm_scan",
    )(mqt, mk, mvt, ogt, ifc, ift, gain_b)


LN_EPS = 1e-5


def _layer_norm(z, gain, bias):
    mu = jnp.mean(z, axis=1, keepdims=True)
    zc = z - mu
    var = jnp.mean(zc * zc, axis=1, keepdims=True)
    return zc * lax.rsqrt(var + LN_EPS) * gain + bias


def _merge_kernel(x_ref, ya_ref, ym_ref, wg_ref, bg_ref, wa_ref, wm_ref, wo_ref, g_ref, b_ref,
                  o_ref, *, alpha):
    x = x_ref[...]
    xb = x.astype(BF16)
    d = x.shape[1]
    ga = _sigmoid(jnp.dot(xb, wg_ref[:, 0:d], preferred_element_type=F32) + bg_ref[:, 0:d])
    merged = ga * jnp.dot(ya_ref[...], wa_ref[...], preferred_element_type=F32)
    gm = _sigmoid(jnp.dot(xb, wg_ref[:, d:2 * d], preferred_element_type=F32) + bg_ref[:, d:2 * d])
    merged = merged + gm * jnp.dot(ym_ref[...], wm_ref[...], preferred_element_type=F32)
    z = alpha * x + jnp.dot(merged.astype(BF16), wo_ref[...], preferred_element_type=F32)
    o_ref[...] = _layer_norm(z, g_ref[...], b_ref[...])


def merge_branches(x2, ya, ym, wg, bg, wa, wm, wo, g, b, *, alpha, tm=512):
    T, D = x2.shape
    tm = min(tm, T)
    full = lambda a: pl.BlockSpec(a.shape, lambda i: (0,) * a.ndim)
    row = lambda w_: pl.BlockSpec((tm, w_), lambda i: (i, 0))
    return pl.pallas_call(
        functools.partial(_merge_kernel, alpha=alpha),
        grid=(T // tm,),
        in_specs=[row(D), row(ya.shape[1]), row(ym.shape[1]), full(wg), full(bg), full(wa),
                  full(wm), full(wo), full(g), full(b)],
        out_specs=row(D),
        out_shape=jax.ShapeDtypeStruct((T, D), F32),
        compiler_params=pltpu.CompilerParams(
            dimension_semantics=("arbitrary",), vmem_limit_bytes=VMEM_LIMIT_MID),
        name="merge_branches",
    )(x2, ya, ym, wg, bg, wa, wm, wo, g, b)


N_GROUPS = 4
EXPERTS_PER_GROUP = 4
N_EXPERTS = N_GROUPS * EXPERTS_PER_GROUP


def _first_lane_of_max(vals, vmax, lane):
    return jnp.min(jnp.where(vals == vmax, lane, LANES), axis=1, keepdims=True)


MOE_SUB = 128
EXPERTS_PER_STEP = 2


def _moe_kernel(x_ref, wr_ref, br_ref, wg_ref, wu_ref, wd_ref, g_ref, b_ref, o_ref,
                tril_ref, slot_ref, xg_ref, cw_ref, y_ref, tiles_ref, *, alpha):
    w = pl.program_id(0)
    e = pl.program_id(1)
    tm = x_ref.shape[0]
    slots = xg_ref.shape[0]
    lane = lax.broadcasted_iota(jnp.int32, (tm, LANES), 1)

    @pl.when(jnp.logical_and(w == 0, e == 0))
    def _():
        r_i = lax.broadcasted_iota(jnp.int32, (tm, tm), 0)
        c_i = lax.broadcasted_iota(jnp.int32, (tm, tm), 1)
        tril_ref[...] = jnp.where(c_i < r_i, 1.0, 0.0).astype(BF16)

    @pl.when(e == 0)
    def _():
        x = x_ref[...]
        x_hi = x.astype(BF16)
        x_lo = (x - x_hi.astype(F32)).astype(BF16)
        w_hi = wr_ref[...].astype(BF16)
        w_lo = (wr_ref[...] - w_hi.astype(F32)).astype(BF16)
        logits = (jnp.dot(x_hi, w_hi, preferred_element_type=F32)
                  + jnp.dot(x_lo, w_hi, preferred_element_type=F32)
                  + jnp.dot(x_hi, w_lo, preferred_element_type=F32)
                  + br_ref[...])
        g = jnp.where(lane < N_GROUPS, logits, NEG_INF)
        gmax = jnp.max(g, axis=1, keepdims=True)
        g_w = 1.0 / jnp.sum(jnp.exp(g - gmax), axis=1, keepdims=True)
        g_sel = _first_lane_of_max(g, gmax, lane)
        lo = N_GROUPS + EXPERTS_PER_GROUP * g_sel
        ev = jnp.where(jnp.logical_and(lane >= lo, lane < lo + EXPERTS_PER_GROUP), logits, NEG_INF)
        v1 = jnp.max(ev, axis=1, keepdims=True)
        i1 = _first_lane_of_max(ev, v1, lane)
        ev2 = jnp.where(lane == i1, NEG_INF, ev)
        v2 = jnp.max(ev2, axis=1, keepdims=True)
        i2 = _first_lane_of_max(ev2, v2, lane)
        r = jnp.exp(v2 - v1)
        p1 = 1.0 / (1.0 + r)
        p2 = r / (1.0 + r)
        comb = jnp.where(lane == i1, g_w * p1, 0.0) + jnp.where(lane == i2, g_w * p2, 0.0)
        comb_hi = comb.astype(BF16)
        comb_lo = (comb - comb_hi.astype(F32)).astype(BF16)

        onehot = jnp.where(lane == g_sel, 1.0, 0.0)
        before = jnp.dot(tril_ref[...], onehot.astype(BF16), preferred_element_type=F32)
        rank = jnp.sum(jnp.where(lane == g_sel, before, 0.0), axis=1, keepdims=True)
        total = jnp.sum(onehot, axis=0, keepdims=True)
        lane1 = lax.broadcasted_iota(jnp.int32, (1, LANES), 1)
        start = jnp.zeros((tm, 1), F32)
        first_tile = jnp.int32(0)
        for grp in range(N_GROUPS):
            n_tok = jnp.sum(jnp.where(lane1 == grp, total, 0.0)).astype(jnp.int32)
            n_tile = (n_tok + MOE_SUB - 1) // MOE_SUB
            tiles_ref[grp] = first_tile
            tiles_ref[N_GROUPS + grp] = n_tile
            start = jnp.where(g_sel == grp, (first_tile * MOE_SUB).astype(F32), start)
            first_tile = first_tile + n_tile
        slot = start + rank
        slot_ref[...] = slot
        slot_row = jnp.transpose(jnp.broadcast_to(slot, (tm, LANES)))[0:1]
        pick = slot_row == lax.broadcasted_iota(jnp.int32, (slots, tm), 0).astype(F32)
        pmat = jnp.where(pick, 1.0, 0.0).astype(BF16)
        xg_ref[...] = jnp.dot(pmat, x_hi, preferred_element_type=F32).astype(BF16)
        cw_ref[...] = (jnp.dot(pmat, comb_hi, preferred_element_type=F32)
                       + jnp.dot(pmat, comb_lo, preferred_element_type=F32))
        y_ref[...] = jnp.zeros(y_ref.shape, F32)

    first_expert = e * EXPERTS_PER_STEP
    grp = first_expert // EXPERTS_PER_GROUP
    first = tiles_ref[grp]
    n_tile = tiles_ref[N_GROUPS + grp]

    def expert(tile, rows):
        base = pl.multiple_of(tile * MOE_SUB, MOE_SUB)
        xg = xg_ref[pl.ds(base, rows), :]
        lane_s = lax.broadcasted_iota(jnp.int32, (rows, LANES), 1)
        out = None
        for k in range(EXPERTS_PER_STEP):
            hg = jnp.dot(xg, wg_ref[k], preferred_element_type=F32)
            hu = jnp.dot(xg, wu_ref[k], preferred_element_type=F32)
            hdn = (_silu(hg) * hu).astype(BF16)
            cw = jnp.sum(jnp.where(lane_s == N_GROUPS + first_expert + k,
                                   cw_ref[pl.ds(base, rows), :], 0.0), axis=1, keepdims=True)
            term = cw * jnp.dot(hdn, wd_ref[k], preferred_element_type=F32)
            out = term if out is None else out + term
        y_ref[pl.ds(base, rows), :] += out

    def expert_pair(j, carry):
        expert(first + 2 * j, 2 * MOE_SUB)
        return carry
    lax.fori_loop(0, n_tile // 2, expert_pair, 0)

    @pl.when(n_tile % 2 == 1)
    def _():
        expert(first + n_tile - 1, MOE_SUB)

    @pl.when(e == pl.num_programs(1) - 1)
    def _():
        pick = slot_ref[...] == lax.broadcasted_iota(jnp.int32, (tm, slots), 1).astype(F32)
        pmat_t = jnp.where(pick, 1.0, 0.0).astype(BF16)
        moe = jnp.dot(pmat_t, y_ref[...].astype(BF16), preferred_element_type=F32)
        o_ref[...] = _layer_norm(alpha * x_ref[...] + moe, g_ref[...], b_ref[...])


def moe_layer(x2, wr, br, wg, wu, wd, g, b, *, alpha, tm=1024):
    T, D = x2.shape
    E, _, F = wg.shape
    tm = min(tm, T)
    slots = tm + N_GROUPS * MOE_SUB
    full = lambda a: pl.BlockSpec(a.shape, lambda i, e: (0,) * a.ndim)
    return pl.pallas_call(
        functools.partial(_moe_kernel, alpha=alpha),
        grid=(T // tm, E // EXPERTS_PER_STEP),
        in_specs=[pl.BlockSpec((tm, D), lambda i, e: (i, 0)), full(wr), full(br),
                  pl.BlockSpec((EXPERTS_PER_STEP, D, F), lambda i, e: (e, 0, 0)),
                  pl.BlockSpec((EXPERTS_PER_STEP, D, F), lambda i, e: (e, 0, 0)),
                  pl.BlockSpec((EXPERTS_PER_STEP, F, D), lambda i, e: (e, 0, 0)),
                  full(g), full(b)],
        out_specs=pl.BlockSpec((tm, D), lambda i, e: (i, 0)),
        out_shape=jax.ShapeDtypeStruct((T, D), F32),
        scratch_shapes=[pltpu.VMEM((tm, tm), BF16),
                        pltpu.VMEM((tm, 1), F32),
                        pltpu.VMEM((slots, D), BF16),
                        pltpu.VMEM((slots, LANES), F32),
                        pltpu.VMEM((slots, D), F32),
                        pltpu.SMEM((2 * N_GROUPS,), jnp.int32)],
        compiler_params=pltpu.CompilerParams(
            dimension_semantics=("arbitrary", "arbitrary"), vmem_limit_bytes=VMEM_LIMIT_BIG),
        name="moe_layer",
    )(x2, wr, br, wg, wu, wd, g, b)


DEPTH = 1
DEEPNORM_ALPHA = (2.0 * DEPTH) ** 0.25


def _pad_cols(a, width):
    return jnp.pad(a, ((0, 0), (0, width - a.shape[1])))


def kernel(x, w_in, b_in, conv_m, gn_m_gain, w_branch_attn, w_branch_mlstm, w_out, ln1_gain, ln1_bias, w_router_group, b_router_group, w_router_expert, b_router_expert, w_exp_gate, w_exp_up, w_exp_down, ln2_gain, ln2_bias):
    B, S, D = x.shape
    aw = ATT_HEADS * ATT_HEAD_DIM
    iw = IDX_HEADS * IDX_DIM
    qw = MLSTM_HEADS * MLSTM_QK_DIM
    vw = MLSTM_HEADS * MLSTM_V_DIM
    widths = (aw, aw, aw, iw, IDX_DIM, IDX_HEADS, qw, qw, vw, MLSTM_HEADS, MLSTM_HEADS, vw, D, D)
    offs = [0]
    for w_ in widths:
        offs.append(offs[-1] + w_)
    col = lambda k: w_in[:, offs[k]:offs[k + 1]]
    bia = lambda k: b_in[offs[k]:offs[k + 1]]
    (A_Q, A_K, A_V, I_Q, I_K, I_W, M_Q, M_K, M_V, M_I, M_F, M_O, G_A, G_M) = range(14)

    wr = jnp.concatenate([col(A_K), _pad_cols(col(I_K), LANES)], 1).astype(BF16)
    br = jnp.concatenate([bia(A_K), jnp.pad(bia(I_K), (0, LANES - IDX_DIM))])[None, :]
    wc = jnp.concatenate([col(A_Q), col(A_V), col(I_Q), _pad_cols(col(I_W), 8)], 1).T.astype(BF16)
    bc = jnp.concatenate([bia(A_Q), bia(A_V), bia(I_Q), jnp.pad(bia(I_W), (0, 8 - IDX_HEADS))])[:, None]
    gpad = LANES - 2 * MLSTM_HEADS
    wm = jnp.concatenate([col(M_Q), col(M_K), col(M_I),
                          _pad_cols(col(M_F), MLSTM_HEADS + gpad)], 1).astype(BF16)
    bm = jnp.concatenate([bia(M_Q), bia(M_K), bia(M_I), jnp.pad(bia(M_F), (0, gpad))])[None, :]
    wmt = jnp.concatenate([col(M_V), col(M_O), col(M_I), col(M_F)], 1).T.astype(BF16)
    bmt = jnp.concatenate([bia(M_V), bia(M_O), bia(M_I), bia(M_F)])[:, None]
    wgate = jnp.concatenate([col(G_A), col(G_M)], 1).astype(BF16)
    bgate = jnp.concatenate([bia(G_A), bia(G_M)])[None, :]

    half = ATT_HEAD_DIM // 2
    inv = ROPE_THETA ** (-jnp.arange(0, ATT_HEAD_DIM, 2, dtype=F32) / ATT_HEAD_DIM)
    ang = jnp.arange(S, dtype=F32)[:, None] * inv[None, :]
    cos, sin = jnp.cos(ang), jnp.sin(ang)
    cos_r = jnp.tile(cos, (1, LANES // half))
    sin_r = jnp.tile(jnp.concatenate([-sin, sin], 1), (1, LANES // ATT_HEAD_DIM))

    qt, k, vt, qit, ki, wit = attn_projections(x, wr, br, wc, bc, cos_r, sin_r, cos.T, sin.T)
    y_attn = dsa_attention(qt, k, vt, qit, ki, wit, topk=min(IDX_TOPK_MAX, S // 4))

    mqt, mk, mvt, ogt, ifc, ift = mlstm_projections(x, wm, bm, wmt, bmt, conv_m)
    y_mlstm = mlstm_scan(mqt, mk, mvt, ogt, ifc, ift,
                         jnp.broadcast_to(gn_m_gain[:, None], (vw, LANES)))

    x1 = merge_branches(x.reshape(B * S, D), y_attn.reshape(B * S, aw), y_mlstm.reshape(B * S, vw),
                        wgate, bgate, w_branch_attn.astype(BF16), w_branch_mlstm.astype(BF16),
                        w_out.astype(BF16), ln1_gain[None, :], ln1_bias[None, :],
                        alpha=DEEPNORM_ALPHA)

    w_router = _pad_cols(jnp.concatenate([w_router_group, w_router_expert], 1), LANES)
    b_router = jnp.pad(jnp.concatenate([b_router_group, b_router_expert]),
                       (0, LANES - N_GROUPS - N_EXPERTS))[None, :]
    out = moe_layer(x1, w_router, b_router, w_exp_gate.astype(BF16), w_exp_up.astype(BF16),
                    w_exp_down.astype(BF16), ln2_gain[None, :], ln2_bias[None, :],
                    alpha=DEEPNORM_ALPHA)
    return out.reshape(B, S, D)
```

```python
import functools

import jax
import jax.numpy as jnp
from jax import lax
from jax.experimental import pallas as pl
from jax.experimental.pallas import tpu as pltpu

F32 = jnp.float32
BF16 = jnp.bfloat16
NEG_INF = float("-inf")
LOG2_E = 1.4426950408889634

ATT_HEADS = 8
ATT_HEAD_DIM = 64
IDX_HEADS = 4
IDX_DIM = 64
IDX_TOPK_MAX = 256
Q_BLOCK = 128
ROPE_THETA = 10000.0

LANES = 128
BF16_ROWS = 16
V7X_VMEM_BYTES = 64 * 1024 * 1024
VMEM_LIMIT_BIG = V7X_VMEM_BYTES * 7 // 8
VMEM_LIMIT_MID = V7X_VMEM_BYTES * 3 // 4


def _key_to_f32(u):
    ks = u ^ jnp.int32(-2 ** 31)
    bits = ks ^ ((ks >> 31) & jnp.int32(0x7FFFFFFF))
    return lax.bitcast_convert_type(bits, F32)


def _dsa_kernel(qt_ref, k_ref, vt_ref, qit_ref, ki_ref, wit_ref, o_ref,
                sc_ref, qm_ref, m_ref, l_ref, acc_ref, *, topk, kchunk):
    qb = pl.program_id(1)
    tq = o_ref.shape[1]
    heads = qt_ref.shape[1] // ATT_HEAD_DIM
    step = 2 * LANES
    n_chunk = (qb * tq + tq + kchunk - 1) // kchunk
    qpos = lax.broadcasted_iota(jnp.int32, (1, tq), 1) + qb * tq

    qit = qit_ref[0]
    zpad = jnp.zeros((LANES - IDX_DIM, tq), BF16)
    qi_pair = []
    for p in range(IDX_HEADS // 2):
        cols = [jnp.concatenate([qit[h * IDX_DIM:(h + 1) * IDX_DIM], zpad], axis=0)
                for h in (2 * p, 2 * p + 1)]
        qi_pair.append(jnp.concatenate(cols, axis=1))
    wit = wit_ref[0]

    achunk = min(2 * kchunk, sc_ref.shape[0])
    n_achunk = (qb * tq + tq + achunk - 1) // achunk
    pairs = IDX_HEADS // 2

    def score_body(c, carry):
        off = pl.multiple_of(c * achunk, achunk)

        def qk(i):
            ki = ki_ref[0, pl.ds(off + (i // pairs) * step, step), :]
            return jnp.dot(ki, qi_pair[i % pairs], preferred_element_type=F32)

        n_dots = (achunk // step) * pairs
        ahead = 3
        pending = [qk(i) for i in range(ahead)]
        for t in range(achunk // step):
            tot = None
            for p in range(pairs):
                i = t * pairs + p
                if i + ahead < n_dots:
                    pending.append(qk(i + ahead))
                s2 = pending.pop(0)
                for j in range(2):
                    h = 2 * p + j
                    s = jnp.maximum(s2[:, j * tq:(j + 1) * tq], 0.0) * wit[h:h + 1, :]
                    tot = s if tot is None else tot + s
            kpos = lax.broadcasted_iota(jnp.int32, (step, tq), 0) + (off + t * step)
            sc_ref[pl.ds(off + t * step, step), :] = jnp.where(kpos <= qpos, tot + 0.0, NEG_INF)
        return carry

    lax.fori_loop(0, n_achunk, score_body, 0)

    def count(pred):
        def body(j, acc):
            off = pl.multiple_of(j * kchunk, kchunk)
            for t in range(kchunk // LANES):
                x = sc_ref[pl.ds(off + t * LANES, LANES), :]
                acc = acc + jnp.where(pred(x, off + t * LANES), 1.0, 0.0)
            return acc
        acc = lax.fori_loop(0, n_chunk, body, jnp.zeros((LANES, tq), F32))
        return jnp.sum(acc, axis=0, keepdims=True)

    kf = float(topk)
    short = qpos < topk

    def bit_pass(i, state, frozen):
        u, cge, cgt = state
        cand = u | (jnp.int32(1) << (31 - i))
        thr = _key_to_f32(cand)
        cnt = count(lambda x, off: x >= thr)
        ok = jnp.logical_and(cnt >= kf, jnp.logical_not(frozen))
        fail = jnp.logical_and(cnt < kf, jnp.logical_not(frozen))
        return (jnp.where(ok, cand, u), jnp.where(ok, cnt, cge), jnp.where(fail, cnt, cgt))

    c_pos = count(lambda x, off: x > 0.0)
    never = jnp.zeros((1, tq), jnp.bool_)
    state = bit_pass(0, (jnp.zeros((1, tq), jnp.int32), jnp.zeros((1, tq), F32),
                         jnp.zeros((1, tq), F32)), never)
    frozen = jnp.logical_and(c_pos < kf, state[1] >= kf)
    state = (state[0], state[1], jnp.where(frozen, c_pos, state[2]))

    fixed_bits = 20
    state = lax.fori_loop(1, fixed_bits, lambda i, st: bit_pass(i, st, frozen), state)

    def all_settled(st):
        done = jnp.logical_or(jnp.logical_or(short, frozen), st[1] == kf)
        return jnp.min(jnp.where(done, 1.0, 0.0)) > 0.0

    def refine(carry):
        i, st, _ = carry
        st = bit_pass(i, st, frozen)
        st = bit_pass(i + 1, st, frozen)
        return i + 2, st, all_settled(st)

    _, (u, cge, cgt), _ = lax.while_loop(
        lambda c: jnp.logical_and(c[0] < 32, jnp.logical_not(c[2])),
        refine, (jnp.int32(fixed_bits), state, all_settled(state)))
    tau = jnp.where(short, NEG_INF, _key_to_f32(u))
    need = jnp.where(cge == kf, kf, kf - cgt)

    r_i = lax.broadcasted_iota(jnp.int32, (LANES, LANES), 0)
    c_i = lax.broadcasted_iota(jnp.int32, (LANES, LANES), 1)
    tril = jnp.where(c_i <= r_i, 1.0, 0.0).astype(BF16)

    def bias_body(c, seen):
        base = pl.multiple_of(c * achunk, achunk)
        ranks = [jnp.dot(tril, jnp.where(sc_ref[pl.ds(base + t * LANES, LANES), :] == tau,
                                         1.0, 0.0).astype(BF16), preferred_element_type=F32)
                 for t in range(achunk // LANES)]
        for t, rank in enumerate(ranks):
            x = sc_ref[pl.ds(base + t * LANES, LANES), :]
            keep = jnp.logical_or(x > tau, jnp.logical_and(x == tau, rank + seen <= need))
            kpos = lax.broadcasted_iota(jnp.int32, x.shape, 0) + (base + t * LANES)
            keep = jnp.logical_and(keep, kpos <= qpos)
            sc_ref[pl.ds(base + t * LANES, LANES), :] = jnp.where(keep, 0.0, NEG_INF)
            seen = seen + rank[LANES - 1:LANES, :]
        return seen

    lax.fori_loop(0, n_achunk, bias_body, jnp.zeros((1, tq), F32))

    qt = qt_ref[0]
    zrow = jnp.zeros((ATT_HEAD_DIM, tq), BF16)
    for p in range(heads // 2):
        a = qt[(2 * p) * ATT_HEAD_DIM:(2 * p + 1) * ATT_HEAD_DIM]
        b = qt[(2 * p + 1) * ATT_HEAD_DIM:(2 * p + 2) * ATT_HEAD_DIM]
        qm_ref[p] = jnp.concatenate([jnp.concatenate([a, zrow], axis=0),
                                     jnp.concatenate([zrow, b], axis=0)], axis=1)
    m_ref[...] = jnp.full(m_ref.shape, NEG_INF, F32)
    l_ref[...] = jnp.zeros(l_ref.shape, F32)
    acc_ref[...] = jnp.zeros(acc_ref.shape, F32)
    ones_rows = jnp.ones((BF16_ROWS, step), BF16)

    def attend(off, n_keys):
        stages = [(sub, p) for sub in range(n_keys // step) for p in range(heads // 2)]

        def qk(stage):
            sub, p = stage
            kp = k_ref[0, pl.ds(off + sub * step, step), p * LANES:(p + 1) * LANES]
            return jnp.dot(kp, qm_ref[p], preferred_element_type=F32)

        ahead = 4
        pending = [qk(st) for st in stages[:ahead]]
        for i, (sub, p) in enumerate(stages):
            if i + ahead < len(stages):
                pending.append(qk(stages[i + ahead]))
            s2 = pending.pop(0)
            koff = off + sub * step
            bias = sc_ref[pl.ds(koff, step), :]
            for j in range(2):
                h = 2 * p + j
                s = s2[:, j * tq:(j + 1) * tq] + bias
                m_old = m_ref[h]
                m_new = jnp.maximum(m_old, jnp.max(s, axis=0, keepdims=True))
                m_use = jnp.where(m_new == NEG_INF, 0.0, m_new)
                pexp = jnp.exp2(s - m_use).astype(BF16)
                alpha = jnp.exp2(m_old - m_use)
                vt = jnp.concatenate(
                    [vt_ref[0, h * ATT_HEAD_DIM:(h + 1) * ATT_HEAD_DIM, pl.ds(koff, step)],
                     ones_rows], axis=0)
                pv = jnp.dot(vt, pexp, preferred_element_type=F32)
                acc_ref[h] = alpha * acc_ref[h] + pv[0:ATT_HEAD_DIM]
                l_ref[h] = alpha * l_ref[h] + pv[ATT_HEAD_DIM:ATT_HEAD_DIM + 1]
                m_ref[h] = m_new

    big = min(4 * kchunk, sc_ref.shape[0])
    n_big = (n_chunk * kchunk) // big
    rest = n_chunk - n_big * (big // kchunk)

    def big_body(c, carry):
        attend(pl.multiple_of(c * big, big), big)
        return carry

    lax.fori_loop(0, n_big, big_body, 0)
    if big > kchunk:
        tail = pl.multiple_of(n_big * big, kchunk)

        @pl.when(rest >= 2)
        def _():
            attend(tail, 2 * kchunk)

        @pl.when(rest % 2 == 1)
        def _():
            attend(pl.multiple_of(tail + (rest - 1) * kchunk, kchunk), kchunk)

    out_t = jnp.concatenate([acc_ref[h] / l_ref[h] for h in range(heads)], axis=0)
    o_ref[0] = jnp.transpose(out_t).astype(o_ref.dtype)


def dsa_attention(qt, k, vt, qit, ki, wit, *, topk, kchunk=512):
    B, W, S = qt.shape
    tq = Q_BLOCK
    kchunk = min(kchunk, S)
    heads = W // ATT_HEAD_DIM
    kern = functools.partial(_dsa_kernel, topk=topk, kchunk=kchunk)
    qcol = lambda a: pl.BlockSpec((1, a.shape[1], tq), lambda b, i: (b, 0, i))
    whole = lambda a: pl.BlockSpec((1,) + a.shape[1:], lambda b, i: (b, 0, 0))
    return pl.pallas_call(
        kern,
        grid=(B, S // tq),
        in_specs=[qcol(qt), whole(k), whole(vt), qcol(qit), whole(ki), qcol(wit)],
        out_specs=pl.BlockSpec((1, tq, W), lambda b, i: (b, i, 0)),
        out_shape=jax.ShapeDtypeStruct((B, S, W), BF16),
        scratch_shapes=[
            pltpu.VMEM((S, tq), F32),
            pltpu.VMEM((heads // 2, LANES, 2 * tq), BF16),
            pltpu.VMEM((heads, 1, tq), F32),
            pltpu.VMEM((heads, 1, tq), F32),
            pltpu.VMEM((heads, ATT_HEAD_DIM, tq), F32),
        ],
        compiler_params=pltpu.CompilerParams(
            dimension_semantics=("arbitrary", "arbitrary"),
            vmem_limit_bytes=VMEM_LIMIT_BIG),
        name="dsa_attention",
    )(qt, k, vt, qit, ki, wit)


def _rope_rows(x, cos_t, sin_t):
    lane = lax.broadcasted_iota(jnp.int32, x.shape, 1)
    swapped = jnp.where((lane % ATT_HEAD_DIM) < ATT_HEAD_DIM // 2,
                        pltpu.roll(x, LANES - ATT_HEAD_DIM // 2, 1),
                        pltpu.roll(x, ATT_HEAD_DIM // 2, 1))
    return x * cos_t + swapped * sin_t


def _attn_proj_kernel(x_ref, wr_ref, br_ref, wc_ref, bc_ref, cos_ref, sin_ref, cost_ref, sint_ref,
                      qt_ref, k_ref, vt_ref, qit_ref, ki_ref, wit_ref, *, q_scale, wi_scale):
    xb = x_ref[0].astype(BF16)
    aw = k_ref.shape[2]
    iw = qit_ref.shape[1]
    half = ATT_HEAD_DIM // 2
    cos_t = cos_ref[...]
    sin_t = sin_ref[...]

    pk = jnp.dot(xb, wr_ref[...], preferred_element_type=F32) + br_ref[...]
    for j in range(aw // LANES):
        sl = slice(j * LANES, (j + 1) * LANES)
        k_ref[0, :, sl] = _rope_rows(pk[:, sl], cos_t, sin_t).astype(k_ref.dtype)
    ki_ref[0] = _rope_rows(pk[:, aw:aw + LANES], cos_t, sin_t).astype(ki_ref.dtype)

    pt = lax.dot_general(wc_ref[...], xb, (((1,), (1,)), ((), ())),
                         preferred_element_type=F32) + bc_ref[...]
    ct = cost_ref[...]
    st = sint_ref[...]

    def rope_cols(src0, dst_ref, nheads, scale):
        for h in range(nheads):
            r0 = src0 + h * ATT_HEAD_DIM
            x1 = pt[r0:r0 + half]
            x2 = pt[r0 + half:r0 + 2 * half]
            d0 = h * ATT_HEAD_DIM
            dst_ref[0, d0:d0 + half, :] = ((x1 * ct - x2 * st) * scale).astype(dst_ref.dtype)
            dst_ref[0, d0 + half:d0 + 2 * half, :] = ((x1 * st + x2 * ct) * scale).astype(dst_ref.dtype)

    rope_cols(0, qt_ref, aw // ATT_HEAD_DIM, q_scale)
    vt_ref[0] = pt[aw:2 * aw].astype(vt_ref.dtype)
    rope_cols(2 * aw, qit_ref, iw // IDX_DIM, 1.0)
    wit_ref[0] = pt[2 * aw + iw:] * wi_scale


def attn_projections(x, wr, br, wc, bc, cos_r, sin_r, cos_c, sin_c, *, tm=512):
    B, S, D = x.shape
    aw = ATT_HEADS * ATT_HEAD_DIM
    iw = IDX_HEADS * IDX_DIM
    tm = min(tm, S)
    kern = functools.partial(_attn_proj_kernel, q_scale=ATT_HEAD_DIM ** -0.5 * LOG2_E,
                             wi_scale=IDX_HEADS ** -0.5 * IDX_DIM ** -0.5)
    full = lambda a: pl.BlockSpec(a.shape, lambda b, i: (0,) * a.ndim)
    row = lambda w: pl.BlockSpec((1, tm, w), lambda b, i: (b, i, 0))
    colm = lambda r: pl.BlockSpec((1, r, tm), lambda b, i: (b, 0, i))
    return pl.pallas_call(
        kern,
        grid=(B, S // tm),
        in_specs=[row(D), full(wr), full(br), full(wc), full(bc),
                  pl.BlockSpec((tm, LANES), lambda b, i: (i, 0)),
                  pl.BlockSpec((tm, LANES), lambda b, i: (i, 0)),
                  pl.BlockSpec((ATT_HEAD_DIM // 2, tm), lambda b, i: (0, i)),
                  pl.BlockSpec((ATT_HEAD_DIM // 2, tm), lambda b, i: (0, i))],
        out_specs=[colm(aw), row(aw), colm(aw), colm(iw), row(LANES), colm(8)],
        out_shape=[jax.ShapeDtypeStruct((B, aw, S), BF16),
                   jax.ShapeDtypeStruct((B, S, aw), BF16),
                   jax.ShapeDtypeStruct((B, aw, S), BF16),
                   jax.ShapeDtypeStruct((B, iw, S), BF16),
                   jax.ShapeDtypeStruct((B, S, LANES), BF16),
                   jax.ShapeDtypeStruct((B, 8, S), F32)],
        compiler_params=pltpu.CompilerParams(
            dimension_semantics=("arbitrary", "arbitrary"),
            vmem_limit_bytes=VMEM_LIMIT_MID),
        name="attn_projections",
    )(x, wr, br, wc, bc, cos_r, sin_r, cos_c, sin_c)


MLSTM_HEADS = 8
MLSTM_QK_DIM = 64
MLSTM_V_DIM = 128
MLSTM_CHUNK = 128
CONV_WIDTH = 4
HALO = 8


def _silu(x):
    return x / (1.0 + jnp.exp(-x))


def _sigmoid(x):
    return 1.0 / (1.0 + jnp.exp(-x))


def _log_sigmoid(x):
    return jnp.minimum(x, 0.0) - jnp.log(1.0 + jnp.exp(-jnp.abs(x)))


def _mlstm_proj_kernel(x_ref, w_ref, b_ref, wt_ref, bt_ref, conv_ref,
                       mqt_ref, mk_ref, mvt_ref, ogt_ref, ifc_ref, ift_ref, ext_ref, *, q_scale):
    i = pl.program_id(1)
    tm = x_ref.shape[1]
    qkw = 2 * mk_ref.shape[2]
    vw = mvt_ref.shape[1]
    nh = ift_ref.shape[1] // 2
    xb = x_ref[0].astype(BF16)

    @pl.when(i == 0)
    def _():
        ext_ref[0:HALO, :] = jnp.zeros((HALO, qkw), F32)

    pqk = jnp.dot(xb, w_ref[:, 0:qkw], preferred_element_type=F32) + b_ref[:, 0:qkw]
    ext_ref[HALO:HALO + tm, :] = pqk
    acc = None
    for j in range(CONV_WIDTH):
        term = ext_ref[pl.ds(HALO - CONV_WIDTH + 1 + j, tm), :] * conv_ref[j:j + 1, :]
        acc = term if acc is None else acc + term
    ext_ref[0:HALO, :] = pqk[tm - HALO:tm, :]
    qk = _silu(acc)
    mqt_ref[0] = jnp.transpose(qk[:, 0:qkw // 2] * q_scale).astype(mqt_ref.dtype)
    mk_ref[0] = qk[:, qkw // 2:qkw].astype(mk_ref.dtype)

    pg = jnp.dot(xb, w_ref[:, qkw:], preferred_element_type=F32) + b_ref[:, qkw:]
    lane = lax.broadcasted_iota(jnp.int32, pg.shape, 1)
    ifc_ref[0] = jnp.where(lane < nh, pg, _log_sigmoid(pg))

    pt = lax.dot_general(wt_ref[...], xb, (((1,), (1,)), ((), ())),
                         preferred_element_type=F32) + bt_ref[...]
    mvt_ref[0] = pt[0:vw].astype(mvt_ref.dtype)
    ogt_ref[0] = _sigmoid(pt[vw:2 * vw]).astype(ogt_ref.dtype)
    gates = pt[2 * vw:]
    rowi = lax.broadcasted_iota(jnp.int32, gates.shape, 0)
    ift_ref[0] = jnp.where(rowi < nh, gates, _log_sigmoid(gates))


def mlstm_projections(x, w, b, wt, bt, conv, *, tm=512):
    B, S, D = x.shape
    qw = MLSTM_HEADS * MLSTM_QK_DIM
    vw = MLSTM_HEADS * MLSTM_V_DIM
    tm = min(tm, S)
    kern = functools.partial(_mlstm_proj_kernel, q_scale=MLSTM_QK_DIM ** -0.5)
    full = lambda a: pl.BlockSpec(a.shape, lambda b_, i: (0,) * a.ndim)
    row = lambda w_: pl.BlockSpec((1, tm, w_), lambda b_, i: (b_, i, 0))
    colm = lambda r: pl.BlockSpec((1, r, tm), lambda b_, i: (b_, 0, i))
    return pl.pallas_call(
        kern,
        grid=(B, S // tm),
        in_specs=[row(D), full(w), full(b), full(wt), full(bt), full(conv)],
        out_specs=[colm(qw), row(qw), colm(vw), colm(vw), row(LANES), colm(2 * MLSTM_HEADS)],
        out_shape=[jax.ShapeDtypeStruct((B, qw, S), BF16),
                   jax.ShapeDtypeStruct((B, S, qw), BF16),
                   jax.ShapeDtypeStruct((B, vw, S), BF16),
                   jax.ShapeDtypeStruct((B, vw, S), BF16),
                   jax.ShapeDtypeStruct((B, S, LANES), F32),
                   jax.ShapeDtypeStruct((B, 2 * MLSTM_HEADS, S), F32)],
        scratch_shapes=[pltpu.VMEM((HALO + tm, 2 * qw), F32)],
        compiler_params=pltpu.CompilerParams(
            dimension_semantics=("arbitrary", "arbitrary"),
            vmem_limit_bytes=VMEM_LIMIT_BIG),
        name="mlstm_projections",
    )(x, w, b, wt, bt, conv)


GN_EPS = 1e-6


def _mlstm_kernel(mqt_ref, mk_ref, mvt_ref, ogt_ref, ifc_ref, ift_ref, gain_ref, y_ref,
                  ct_ref, m_ref):
    c = pl.program_id(1)
    L = mk_ref.shape[1]
    nh = ift_ref.shape[1] // 2
    dk = mk_ref.shape[2] // nh
    dv = mvt_ref.shape[1] // nh

    @pl.when(c == 0)
    def _():
        ct_ref[...] = jnp.zeros(ct_ref.shape, F32)
        m_ref[...] = jnp.zeros(m_ref.shape, F32)

    r_i = lax.broadcasted_iota(jnp.int32, (L, L), 0)
    c_i = lax.broadcasted_iota(jnp.int32, (L, L), 1)
    causal_t = r_i <= c_i
    tril = jnp.where(c_i <= r_i, 1.0, 0.0)
    triu = jnp.where(causal_t, 1.0, 0.0)
    ifc = ifc_ref[0]
    ift = ift_ref[0]
    b_cols = jnp.dot(tril, ifc, preferred_element_type=F32, precision=lax.Precision.HIGHEST)
    b_rows = jnp.dot(ift, triu, preferred_element_type=F32, precision=lax.Precision.HIGHEST)
    ones_rows = jnp.where(lax.broadcasted_iota(jnp.int32, (BF16_ROWS, L), 0) == 0,
                          1.0, 0.0).astype(BF16)
    first_half_rows = lax.broadcasted_iota(jnp.int32, (2 * dk, L), 0) < dk
    first_half_lanes = lax.broadcasted_iota(jnp.int32, (L, 2 * dk), 1) < dk

    k_pair, q_heads, s_heads, inter_heads = [], [], [], []
    for p in range(nh // 2):
        k_pair.append(mk_ref[0, :, 2 * p * dk:2 * (p + 1) * dk])
        qt_pair = mqt_ref[0, 2 * p * dk:2 * (p + 1) * dk, :]
        zero = jnp.zeros_like(qt_pair)
        q_heads.append(jnp.where(first_half_rows, qt_pair, zero))
        q_heads.append(jnp.where(first_half_rows, zero, qt_pair))
    for h in range(nh):
        s_heads.append(jnp.dot(k_pair[h // 2], q_heads[h], preferred_element_type=F32))
        inter_heads.append(jnp.dot(ct_ref[h].astype(BF16), q_heads[h],
                                   preferred_element_type=F32))

    for h in range(nh):
        m = m_ref[h]
        a_col = ifc[:, h:h + 1] - b_cols[:, nh + h:nh + h + 1]
        amat = jnp.where(causal_t, a_col, NEG_INF)
        big_m = jnp.maximum(m, jnp.max(amat, axis=0, keepdims=True))
        decay = jnp.exp(amat - big_m)
        w_inter = jnp.exp(m - big_m)
        s = (s_heads[h] * decay).astype(BF16)
        vt_ext = jnp.concatenate([mvt_ref[0, h * dv:(h + 1) * dv, :], ones_rows], axis=0)
        tot = jnp.dot(vt_ext, s, preferred_element_type=F32) + w_inter * inter_heads[h]
        den = tot[dv:dv + 1]
        b_row = b_rows[nh + h:nh + h + 1, :]
        hh = tot[0:dv] / jnp.maximum(jnp.abs(den), jnp.exp(-(b_row + big_m)))
        mu = jnp.mean(hh, axis=0, keepdims=True)
        xc = hh - mu
        var = jnp.mean(xc * xc, axis=0, keepdims=True)
        hn = (xc * lax.rsqrt(var + GN_EPS) * gain_ref[h * dv:(h + 1) * dv, :]
              * ogt_ref[0, h * dv:(h + 1) * dv, :].astype(F32))
        y_ref[0, :, h * dv:(h + 1) * dv] = jnp.transpose(hn).astype(y_ref.dtype)

        b_last = b_cols[L - 1:L, nh + h:nh + h + 1]
        g_col = b_last + a_col
        m_new = jnp.maximum(b_last + m, jnp.max(g_col, axis=0, keepdims=True))
        carry = jnp.exp(b_last + m - m_new)
        head_lanes = first_half_lanes if h % 2 == 0 else jnp.logical_not(first_half_lanes)
        wk = jnp.where(head_lanes, k_pair[h // 2].astype(F32) * jnp.exp(g_col - m_new),
                       0.0).astype(BF16)
        ct_ref[h] = carry * ct_ref[h] + jnp.dot(vt_ext, wk, preferred_element_type=F32)
        m_ref[h] = m_new


def mlstm_scan(mqt, mk, mvt, ogt, ifc, ift, gain_b):
    B, S, qw = mk.shape
    vw = mvt.shape[1]
    L = min(MLSTM_CHUNK, S)
    nh = ift.shape[1] // 2
    row = lambda w_: pl.BlockSpec((1, L, w_), lambda b_, c: (b_, c, 0))
    colm = lambda r: pl.BlockSpec((1, r, L), lambda b_, c: (b_, 0, c))
    return pl.pallas_call(
        _mlstm_kernel,
        grid=(B, S // L),
        in_specs=[colm(qw), row(qw), colm(vw), colm(vw), row(LANES), colm(2 * nh),
                  pl.BlockSpec(gain_b.shape, lambda b_, c: (0, 0))],
        out_specs=row(vw),
        out_shape=jax.ShapeDtypeStruct((B, S, vw), BF16),
        scratch_shapes=[pltpu.VMEM((nh, vw // nh + BF16_ROWS, 2 * qw // nh), F32),
                        pltpu.VMEM((nh, 1, 1), F32)],
        compiler_params=pltpu.CompilerParams(
            dimension_semantics=("arbitrary", "arbitrary")),
        name="mlstm_scan",
    )(mqt, mk, mvt, ogt, ifc, ift, gain_b)


LN_EPS = 1e-5


def _layer_norm(z, gain, bias):
    mu = jnp.mean(z, axis=1, keepdims=True)
    zc = z - mu
    var = jnp.mean(zc * zc, axis=1, keepdims=True)
    return zc * lax.rsqrt(var + LN_EPS) * gain + bias


def _merge_kernel(x_ref, ya_ref, ym_ref, wg_ref, bg_ref, wa_ref, wm_ref, wo_ref, g_ref, b_ref,
                  o_ref, *, alpha):
    x = x_ref[...]
    xb = x.astype(BF16)
    d = x.shape[1]
    ga = _sigmoid(jnp.dot(xb, wg_ref[:, 0:d], preferred_element_type=F32) + bg_ref[:, 0:d])
    merged = ga * jnp.dot(ya_ref[...], wa_ref[...], preferred_element_type=F32)
    gm = _sigmoid(jnp.dot(xb, wg_ref[:, d:2 * d], preferred_element_type=F32) + bg_ref[:, d:2 * d])
    merged = merged + gm * jnp.dot(ym_ref[...], wm_ref[...], preferred_element_type=F32)
    z = alpha * x + jnp.dot(merged.astype(BF16), wo_ref[...], preferred_element_type=F32)
    o_ref[...] = _layer_norm(z, g_ref[...], b_ref[...])


def merge_branches(x2, ya, ym, wg, bg, wa, wm, wo, g, b, *, alpha, tm=512):
    T, D = x2.shape
    tm = min(tm, T)
    full = lambda a: pl.BlockSpec(a.shape, lambda i: (0,) * a.ndim)
    row = lambda w_: pl.BlockSpec((tm, w_), lambda i: (i, 0))
    return pl.pallas_call(
        functools.partial(_merge_kernel, alpha=alpha),
        grid=(T // tm,),
        in_specs=[row(D), row(ya.shape[1]), row(ym.shape[1]), full(wg), full(bg), full(wa),
                  full(wm), full(wo), full(g), full(b)],
        out_specs=row(D),
        out_shape=jax.ShapeDtypeStruct((T, D), F32),
        compiler_params=pltpu.CompilerParams(
            dimension_semantics=("arbitrary",), vmem_limit_bytes=VMEM_LIMIT_MID),
        name="merge_branches",
    )(x2, ya, ym, wg, bg, wa, wm, wo, g, b)


N_GROUPS = 4
EXPERTS_PER_GROUP = 4
N_EXPERTS = N_GROUPS * EXPERTS_PER_GROUP


def _first_lane_of_max(vals, vmax, lane):
    return jnp.min(jnp.where(vals == vmax, lane, LANES), axis=1, keepdims=True)


MOE_SUB = 128
EXPERTS_PER_STEP = 2


def _moe_kernel(x_ref, wr_ref, br_ref, wg_ref, wu_ref, wd_ref, g_ref, b_ref, o_ref,
                tril_ref, slot_ref, xg_ref, cw_ref, y_ref, tiles_ref, *, alpha):
    w = pl.program_id(0)
    e = pl.program_id(1)
    tm = x_ref.shape[0]
    slots = xg_ref.shape[0]
    lane = lax.broadcasted_iota(jnp.int32, (tm, LANES), 1)

    @pl.when(jnp.logical_and(w == 0, e == 0))
    def _():
        r_i = lax.broadcasted_iota(jnp.int32, (tm, tm), 0)
        c_i = lax.broadcasted_iota(jnp.int32, (tm, tm), 1)
        tril_ref[...] = jnp.where(c_i < r_i, 1.0, 0.0).astype(BF16)

    @pl.when(e == 0)
    def _():
        x = x_ref[...]
        x_hi = x.astype(BF16)
        x_lo = (x - x_hi.astype(F32)).astype(BF16)
        w_hi = wr_ref[...].astype(BF16)
        w_lo = (wr_ref[...] - w_hi.astype(F32)).astype(BF16)
        hi2 = jnp.dot(x_hi, jnp.concatenate([w_hi, w_lo], axis=1),
                      preferred_element_type=F32)
        logits = (hi2[:, 0:LANES] + jnp.dot(x_lo, w_hi, preferred_element_type=F32)
                  + hi2[:, LANES:2 * LANES] + br_ref[...])
        g = jnp.where(lane < N_GROUPS, logits, NEG_INF)
        gmax = jnp.max(g, axis=1, keepdims=True)
        g_w = 1.0 / jnp.sum(jnp.exp(g - gmax), axis=1, keepdims=True)
        g_sel = _first_lane_of_max(g, gmax, lane)
        lo = N_GROUPS + EXPERTS_PER_GROUP * g_sel
        ev = jnp.where(jnp.logical_and(lane >= lo, lane < lo + EXPERTS_PER_GROUP), logits, NEG_INF)
        v1 = jnp.max(ev, axis=1, keepdims=True)
        i1 = _first_lane_of_max(ev, v1, lane)
        ev2 = jnp.where(lane == i1, NEG_INF, ev)
        v2 = jnp.max(ev2, axis=1, keepdims=True)
        i2 = _first_lane_of_max(ev2, v2, lane)
        r = jnp.exp(v2 - v1)
        p1 = 1.0 / (1.0 + r)
        p2 = r / (1.0 + r)
        comb = jnp.where(lane == i1, g_w * p1, 0.0) + jnp.where(lane == i2, g_w * p2, 0.0)
        comb_hi = comb.astype(BF16)
        comb_lo = (comb - comb_hi.astype(F32)).astype(BF16)

        onehot = jnp.where(lane == g_sel, 1.0, 0.0)
        before = jnp.dot(tril_ref[...], onehot.astype(BF16), preferred_element_type=F32)
        rank = jnp.sum(jnp.where(lane == g_sel, before, 0.0), axis=1, keepdims=True)
        total = jnp.sum(onehot, axis=0, keepdims=True)
        lane1 = lax.broadcasted_iota(jnp.int32, (1, LANES), 1)
        start = jnp.zeros((tm, 1), F32)
        first_tile = jnp.int32(0)
        for grp in range(N_GROUPS):
            n_tok = jnp.sum(jnp.where(lane1 == grp, total, 0.0)).astype(jnp.int32)
            n_tile = (n_tok + MOE_SUB - 1) // MOE_SUB
            tiles_ref[grp] = first_tile
            tiles_ref[N_GROUPS + grp] = n_tile
            start = jnp.where(g_sel == grp, (first_tile * MOE_SUB).astype(F32), start)
            first_tile = first_tile + n_tile
        slot = start + rank
        slot_ref[...] = slot
        slot_row = jnp.transpose(jnp.broadcast_to(slot, (tm, LANES)))[0:1]
        pick = slot_row == lax.broadcasted_iota(jnp.int32, (slots, tm), 0).astype(F32)
        pmat = jnp.where(pick, 1.0, 0.0).astype(BF16)
        xg_ref[...] = jnp.dot(pmat, x_hi, preferred_element_type=F32).astype(BF16)
        cw2 = jnp.dot(pmat, jnp.concatenate([comb_hi, comb_lo], axis=1),
                      preferred_element_type=F32)
        cw_ref[...] = cw2[:, 0:LANES] + cw2[:, LANES:2 * LANES]
        y_ref[...] = jnp.zeros(y_ref.shape, F32)

    first_expert = e * EXPERTS_PER_STEP
    grp = first_expert // EXPERTS_PER_GROUP
    first = tiles_ref[grp]
    n_tile = tiles_ref[N_GROUPS + grp]

    def expert(tile, rows):
        base = pl.multiple_of(tile * MOE_SUB, MOE_SUB)
        xg = xg_ref[pl.ds(base, rows), :]
        lane_s = lax.broadcasted_iota(jnp.int32, (rows, LANES), 1)
        out = None
        for k in range(EXPERTS_PER_STEP):
            hg = jnp.dot(xg, wg_ref[k], preferred_element_type=F32)
            hu = jnp.dot(xg, wu_ref[k], preferred_element_type=F32)
            hdn = (_silu(hg) * hu).astype(BF16)
            cw = jnp.sum(jnp.where(lane_s == N_GROUPS + first_expert + k,
                                   cw_ref[pl.ds(base, rows), :], 0.0), axis=1, keepdims=True)
            term = cw * jnp.dot(hdn, wd_ref[k], preferred_element_type=F32)
            out = term if out is None else out + term
        y_ref[pl.ds(base, rows), :] += out

    def expert_pair(j, carry):
        expert(first + 2 * j, 2 * MOE_SUB)
        return carry
    lax.fori_loop(0, n_tile // 2, expert_pair, 0)

    @pl.when(n_tile % 2 == 1)
    def _():
        expert(first + n_tile - 1, MOE_SUB)

    @pl.when(e == pl.num_programs(1) - 1)
    def _():
        pick = slot_ref[...] == lax.broadcasted_iota(jnp.int32, (tm, slots), 1).astype(F32)
        pmat_t = jnp.where(pick, 1.0, 0.0).astype(BF16)
        moe = jnp.dot(pmat_t, y_ref[...].astype(BF16), preferred_element_type=F32)
        o_ref[...] = _layer_norm(alpha * x_ref[...] + moe, g_ref[...], b_ref[...])


def moe_layer(x2, wr, br, wg, wu, wd, g, b, *, alpha, tm=1024):
    T, D = x2.shape
    E, _, F = wg.shape
    tm = min(tm, T)
    slots = tm + N_GROUPS * MOE_SUB
    full = lambda a: pl.BlockSpec(a.shape, lambda i, e: (0,) * a.ndim)
    return pl.pallas_call(
        functools.partial(_moe_kernel, alpha=alpha),
        grid=(T // tm, E // EXPERTS_PER_STEP),
        in_specs=[pl.BlockSpec((tm, D), lambda i, e: (i, 0)), full(wr), full(br),
                  pl.BlockSpec((EXPERTS_PER_STEP, D, F), lambda i, e: (e, 0, 0)),
                  pl.BlockSpec((EXPERTS_PER_STEP, D, F), lambda i, e: (e, 0, 0)),
                  pl.BlockSpec((EXPERTS_PER_STEP, F, D), lambda i, e: (e, 0, 0)),
                  full(g), full(b)],
        out_specs=pl.BlockSpec((tm, D), lambda i, e: (i, 0)),
        out_shape=jax.ShapeDtypeStruct((T, D), F32),
        scratch_shapes=[pltpu.VMEM((tm, tm), BF16),
                        pltpu.VMEM((tm, 1), F32),
                        pltpu.VMEM((slots, D), BF16),
                        pltpu.VMEM((slots, LANES), F32),
                        pltpu.VMEM((slots, D), F32),
                        pltpu.SMEM((2 * N_GROUPS,), jnp.int32)],
        compiler_params=pltpu.CompilerParams(
            dimension_semantics=("arbitrary", "arbitrary"), vmem_limit_bytes=VMEM_LIMIT_BIG),
        name="moe_layer",
    )(x2, wr, br, wg, wu, wd, g, b)


DEPTH = 1
DEEPNORM_ALPHA = (2.0 * DEPTH) ** 0.25


def _pad_cols(a, width):
    return jnp.pad(a, ((0, 0), (0, width - a.shape[1])))


def kernel(x, w_in, b_in, conv_m, gn_m_gain, w_branch_attn, w_branch_mlstm, w_out, ln1_gain, ln1_bias, w_router_group, b_router_group, w_router_expert, b_router_expert, w_exp_gate, w_exp_up, w_exp_down, ln2_gain, ln2_bias):
    B, S, D = x.shape
    aw = ATT_HEADS * ATT_HEAD_DIM
    iw = IDX_HEADS * IDX_DIM
    qw = MLSTM_HEADS * MLSTM_QK_DIM
    vw = MLSTM_HEADS * MLSTM_V_DIM
    widths = (aw, aw, aw, iw, IDX_DIM, IDX_HEADS, qw, qw, vw, MLSTM_HEADS, MLSTM_HEADS, vw, D, D)
    offs = [0]
    for w_ in widths:
        offs.append(offs[-1] + w_)
    col = lambda k: w_in[:, offs[k]:offs[k + 1]]
    bia = lambda k: b_in[offs[k]:offs[k + 1]]
    (A_Q, A_K, A_V, I_Q, I_K, I_W, M_Q, M_K, M_V, M_I, M_F, M_O, G_A, G_M) = range(14)

    wr = jnp.concatenate([col(A_K), _pad_cols(col(I_K), LANES)], 1).astype(BF16)
    br = jnp.concatenate([bia(A_K), jnp.pad(bia(I_K), (0, LANES - IDX_DIM))])[None, :]
    wc = jnp.concatenate([col(A_Q), col(A_V), col(I_Q), _pad_cols(col(I_W), 8)], 1).T.astype(BF16)
    bc = jnp.concatenate([bia(A_Q), bia(A_V), bia(I_Q), jnp.pad(bia(I_W), (0, 8 - IDX_HEADS))])[:, None]
    gpad = LANES - 2 * MLSTM_HEADS
    wm = jnp.concatenate([col(M_Q), col(M_K), col(M_I),
                          _pad_cols(col(M_F), MLSTM_HEADS + gpad)], 1).astype(BF16)
    bm = jnp.concatenate([bia(M_Q), bia(M_K), bia(M_I), jnp.pad(bia(M_F), (0, gpad))])[None, :]
    wmt = jnp.concatenate([col(M_V), col(M_O), col(M_I), col(M_F)], 1).T.astype(BF16)
    bmt = jnp.concatenate([bia(M_V), bia(M_O), bia(M_I), bia(M_F)])[:, None]
    wgate = jnp.concatenate([col(G_A), col(G_M)], 1).astype(BF16)
    bgate = jnp.concatenate([bia(G_A), bia(G_M)])[None, :]

    half = ATT_HEAD_DIM // 2
    inv = ROPE_THETA ** (-jnp.arange(0, ATT_HEAD_DIM, 2, dtype=F32) / ATT_HEAD_DIM)
    ang = jnp.arange(S, dtype=F32)[:, None] * inv[None, :]
    cos, sin = jnp.cos(ang), jnp.sin(ang)
    cos_r = jnp.tile(cos, (1, LANES // half))
    sin_r = jnp.tile(jnp.concatenate([-sin, sin], 1), (1, LANES // ATT_HEAD_DIM))

    qt, k, vt, qit, ki, wit = attn_projections(x, wr, br, wc, bc, cos_r, sin_r, cos.T, sin.T)
    y_attn = dsa_attention(qt, k, vt, qit, ki, wit, topk=min(IDX_TOPK_MAX, S // 4))

    mqt, mk, mvt, ogt, ifc, ift = mlstm_projections(x, wm, bm, wmt, bmt, conv_m)
    y_mlstm = mlstm_scan(mqt, mk, mvt, ogt, ifc, ift,
                         jnp.broadcast_to(gn_m_gain[:, None], (vw, LANES)))

    x1 = merge_branches(x.reshape(B * S, D), y_attn.reshape(B * S, aw), y_mlstm.reshape(B * S, vw),
                        wgate, bgate, w_branch_attn.astype(BF16), w_branch_mlstm.astype(BF16),
                        w_out.astype(BF16), ln1_gain[None, :], ln1_bias[None, :],
                        alpha=DEEPNORM_ALPHA)

    w_router = _pad_cols(jnp.concatenate([w_router_group, w_router_expert], 1), LANES)
    b_router = jnp.pad(jnp.concatenate([b_router_group, b_router_expert]),
                       (0, LANES - N_GROUPS - N_EXPERTS))[None, :]
    out = moe_layer(x1, w_router, b_router, w_exp_gate.astype(BF16), w_exp_up.astype(BF16),
                    w_exp_down.astype(BF16), ln2_gain[None, :], ln2_bias[None, :],
                    alpha=DEEPNORM_ALPHA)
    return out.reshape(B, S, D)
```

```python
import functools

import jax
import jax.numpy as jnp
from jax import lax
from jax.experimental import pallas as pl
from jax.experimental.pallas import tpu as pltpu

F32 = jnp.float32
BF16 = jnp.bfloat16
NEG_INF = float("-inf")
LOG2_E = 1.4426950408889634

ATT_HEADS = 8
ATT_HEAD_DIM = 64
IDX_HEADS = 4
IDX_DIM = 64
IDX_TOPK_MAX = 256
Q_BLOCK = 128
ROPE_THETA = 10000.0

LANES = 128
BF16_ROWS = 16
V7X_VMEM_BYTES = 64 * 1024 * 1024
VMEM_LIMIT_BIG = V7X_VMEM_BYTES * 7 // 8
VMEM_LIMIT_MID = V7X_VMEM_BYTES * 3 // 4


def _key_to_f32(u):
    ks = u ^ jnp.int32(-2 ** 31)
    bits = ks ^ ((ks >> 31) & jnp.int32(0x7FFFFFFF))
    return lax.bitcast_convert_type(bits, F32)


def _dsa_kernel(qt_ref, k_ref, vt_ref, qit_ref, ki_ref, wit_ref, o_ref,
                sc_ref, qm_ref, m_ref, l_ref, acc_ref, *, topk, kchunk):
    qb = pl.program_id(1)
    tq = o_ref.shape[1]
    heads = qt_ref.shape[1] // ATT_HEAD_DIM
    step = 2 * LANES
    n_chunk = (qb * tq + tq + kchunk - 1) // kchunk
    qpos = lax.broadcasted_iota(jnp.int32, (1, tq), 1) + qb * tq

    qit = qit_ref[0]
    zpad = jnp.zeros((LANES - IDX_DIM, tq), BF16)
    qi_pair = []
    for p in range(IDX_HEADS // 2):
        cols = [jnp.concatenate([qit[h * IDX_DIM:(h + 1) * IDX_DIM], zpad], axis=0)
                for h in (2 * p, 2 * p + 1)]
        qi_pair.append(jnp.concatenate(cols, axis=1))
    wit = wit_ref[0]

    achunk = min(2 * kchunk, sc_ref.shape[0])
    n_achunk = (qb * tq + tq + achunk - 1) // achunk
    pairs = IDX_HEADS // 2

    def score_body(c, carry):
        off = pl.multiple_of(c * achunk, achunk)

        def qk(i):
            ki = ki_ref[0, pl.ds(off + (i // pairs) * step, step), :]
            return jnp.dot(ki, qi_pair[i % pairs], preferred_element_type=F32)

        n_dots = (achunk // step) * pairs
        ahead = 3
        pending = [qk(i) for i in range(ahead)]
        for t in range(achunk // step):
            tot = None
            for p in range(pairs):
                i = t * pairs + p
                if i + ahead < n_dots:
                    pending.append(qk(i + ahead))
                s2 = pending.pop(0)
                for j in range(2):
                    h = 2 * p + j
                    s = jnp.maximum(s2[:, j * tq:(j + 1) * tq], 0.0) * wit[h:h + 1, :]
                    tot = s if tot is None else tot + s
            kpos = lax.broadcasted_iota(jnp.int32, (step, tq), 0) + (off + t * step)
            sc_ref[pl.ds(off + t * step, step), :] = jnp.where(kpos <= qpos, tot + 0.0, NEG_INF)
        return carry

    lax.fori_loop(0, n_achunk, score_body, 0)

    def count(pred):
        def body(j, acc):
            off = pl.multiple_of(j * kchunk, kchunk)
            for t in range(kchunk // LANES):
                x = sc_ref[pl.ds(off + t * LANES, LANES), :]
                acc = acc + jnp.where(pred(x, off + t * LANES), 1.0, 0.0)
            return acc
        acc = lax.fori_loop(0, n_chunk, body, jnp.zeros((LANES, tq), F32))
        return jnp.sum(acc, axis=0, keepdims=True)

    kf = float(topk)
    short = qpos < topk

    def bit_pass(i, state, frozen):
        u, cge, cgt = state
        cand = u | (jnp.int32(1) << (31 - i))
        thr = _key_to_f32(cand)
        cnt = count(lambda x, off: x >= thr)
        ok = jnp.logical_and(cnt >= kf, jnp.logical_not(frozen))
        fail = jnp.logical_and(cnt < kf, jnp.logical_not(frozen))
        return (jnp.where(ok, cand, u), jnp.where(ok, cnt, cge), jnp.where(fail, cnt, cgt))

    c_pos = count(lambda x, off: x > 0.0)
    never = jnp.zeros((1, tq), jnp.bool_)
    state = bit_pass(0, (jnp.zeros((1, tq), jnp.int32), jnp.zeros((1, tq), F32),
                         jnp.zeros((1, tq), F32)), never)
    frozen = jnp.logical_and(c_pos < kf, state[1] >= kf)
    state = (state[0], state[1], jnp.where(frozen, c_pos, state[2]))

    fixed_bits = 20
    state = lax.fori_loop(1, fixed_bits, lambda i, st: bit_pass(i, st, frozen), state)

    def all_settled(st):
        done = jnp.logical_or(jnp.logical_or(short, frozen), st[1] == kf)
        return jnp.min(jnp.where(done, 1.0, 0.0)) > 0.0

    def refine(carry):
        i, st, _ = carry
        st = bit_pass(i, st, frozen)
        st = bit_pass(i + 1, st, frozen)
        return i + 2, st, all_settled(st)

    _, (u, cge, cgt), _ = lax.while_loop(
        lambda c: jnp.logical_and(c[0] < 32, jnp.logical_not(c[2])),
        refine, (jnp.int32(fixed_bits), state, all_settled(state)))
    tau = jnp.where(short, NEG_INF, _key_to_f32(u))
    need = jnp.where(cge == kf, kf, kf - cgt)
    need = jnp.where(short, -1.0, need)

    r_i = lax.broadcasted_iota(jnp.int32, (LANES, LANES), 0)
    c_i = lax.broadcasted_iota(jnp.int32, (LANES, LANES), 1)
    tril = jnp.where(c_i <= r_i, 1.0, 0.0).astype(BF16)

    def bias_body(c, seen):
        base = pl.multiple_of(c * achunk, achunk)
        ranks = [jnp.dot(tril, jnp.where(sc_ref[pl.ds(base + t * LANES, LANES), :] == tau,
                                         1.0, 0.0).astype(BF16), preferred_element_type=F32)
                 for t in range(achunk // LANES)]
        for t, rank in enumerate(ranks):
            x = sc_ref[pl.ds(base + t * LANES, LANES), :]
            keep = jnp.logical_or(x > tau, jnp.logical_and(x == tau, rank + seen <= need))
            sc_ref[pl.ds(base + t * LANES, LANES), :] = jnp.where(keep, 0.0, NEG_INF)
            seen = seen + rank[LANES - 1:LANES, :]
        return seen

    lax.fori_loop(0, n_achunk, bias_body, jnp.zeros((1, tq), F32))

    qt = qt_ref[0]
    zrow = jnp.zeros((ATT_HEAD_DIM, tq), BF16)
    for p in range(heads // 2):
        a = qt[(2 * p) * ATT_HEAD_DIM:(2 * p + 1) * ATT_HEAD_DIM]
        b = qt[(2 * p + 1) * ATT_HEAD_DIM:(2 * p + 2) * ATT_HEAD_DIM]
        qm_ref[p] = jnp.concatenate([jnp.concatenate([a, zrow], axis=0),
                                     jnp.concatenate([zrow, b], axis=0)], axis=1)
    m_ref[...] = jnp.full(m_ref.shape, NEG_INF, F32)
    l_ref[...] = jnp.zeros(l_ref.shape, F32)
    acc_ref[...] = jnp.zeros(acc_ref.shape, F32)
    ones_rows = jnp.ones((BF16_ROWS, step), BF16)

    def attend(off, n_keys):
        stages = [(sub, p) for sub in range(n_keys // step) for p in range(heads // 2)]

        def qk(stage):
            sub, p = stage
            kp = k_ref[0, pl.ds(off + sub * step, step), p * LANES:(p + 1) * LANES]
            return jnp.dot(kp, qm_ref[p], preferred_element_type=F32)

        ahead = 4
        pending = [qk(st) for st in stages[:ahead]]
        for i, (sub, p) in enumerate(stages):
            if i + ahead < len(stages):
                pending.append(qk(stages[i + ahead]))
            s2 = pending.pop(0)
            koff = off + sub * step
            bias = sc_ref[pl.ds(koff, step), :]
            for j in range(2):
                h = 2 * p + j
                s = s2[:, j * tq:(j + 1) * tq] + bias
                m_old = m_ref[h]
                m_new = jnp.maximum(m_old, jnp.max(s, axis=0, keepdims=True))
                m_use = jnp.where(m_new == NEG_INF, 0.0, m_new)
                pexp = jnp.exp2(s - m_use).astype(BF16)
                alpha = jnp.exp2(m_old - m_use)
                vt = jnp.concatenate(
                    [vt_ref[0, h * ATT_HEAD_DIM:(h + 1) * ATT_HEAD_DIM, pl.ds(koff, step)],
                     ones_rows], axis=0)
                pv = jnp.dot(vt, pexp, preferred_element_type=F32)
                acc_ref[h] = alpha * acc_ref[h] + pv[0:ATT_HEAD_DIM]
                l_ref[h] = alpha * l_ref[h] + pv[ATT_HEAD_DIM:ATT_HEAD_DIM + 1]
                m_ref[h] = m_new

    big = min(4 * kchunk, sc_ref.shape[0])
    n_big = (n_chunk * kchunk) // big
    rest = n_chunk - n_big * (big // kchunk)

    def big_body(c, carry):
        attend(pl.multiple_of(c * big, big), big)
        return carry

    lax.fori_loop(0, n_big, big_body, 0)
    if big > kchunk:
        tail = pl.multiple_of(n_big * big, kchunk)

        @pl.when(rest >= 2)
        def _():
            attend(tail, 2 * kchunk)

        @pl.when(rest % 2 == 1)
        def _():
            attend(pl.multiple_of(tail + (rest - 1) * kchunk, kchunk), kchunk)

    out_t = jnp.concatenate([acc_ref[h] / l_ref[h] for h in range(heads)], axis=0)
    o_ref[0] = jnp.transpose(out_t).astype(o_ref.dtype)


def dsa_attention(qt, k, vt, qit, ki, wit, *, topk, kchunk=512):
    B, W, S = qt.shape
    tq = Q_BLOCK
    kchunk = min(kchunk, S)
    heads = W // ATT_HEAD_DIM
    kern = functools.partial(_dsa_kernel, topk=topk, kchunk=kchunk)
    qcol = lambda a: pl.BlockSpec((1, a.shape[1], tq), lambda b, i: (b, 0, i))
    whole = lambda a: pl.BlockSpec((1,) + a.shape[1:], lambda b, i: (b, 0, 0))
    return pl.pallas_call(
        kern,
        grid=(B, S // tq),
        in_specs=[qcol(qt), whole(k), whole(vt), qcol(qit), whole(ki), qcol(wit)],
        out_specs=pl.BlockSpec((1, tq, W), lambda b, i: (b, i, 0)),
        out_shape=jax.ShapeDtypeStruct((B, S, W), BF16),
        scratch_shapes=[
            pltpu.VMEM((S, tq), F32),
            pltpu.VMEM((heads // 2, LANES, 2 * tq), BF16),
            pltpu.VMEM((heads, 1, tq), F32),
            pltpu.VMEM((heads, 1, tq), F32),
            pltpu.VMEM((heads, ATT_HEAD_DIM, tq), F32),
        ],
        compiler_params=pltpu.CompilerParams(
            dimension_semantics=("arbitrary", "arbitrary"),
            vmem_limit_bytes=VMEM_LIMIT_BIG),
        name="dsa_attention",
    )(qt, k, vt, qit, ki, wit)


def _rope_rows(x, cos_t, sin_t):
    lane = lax.broadcasted_iota(jnp.int32, x.shape, 1)
    swapped = jnp.where((lane % ATT_HEAD_DIM) < ATT_HEAD_DIM // 2,
                        pltpu.roll(x, LANES - ATT_HEAD_DIM // 2, 1),
                        pltpu.roll(x, ATT_HEAD_DIM // 2, 1))
    return x * cos_t + swapped * sin_t


def _attn_proj_kernel(x_ref, wr_ref, br_ref, wc_ref, bc_ref, cos_ref, sin_ref, cost_ref, sint_ref,
                      qt_ref, k_ref, vt_ref, qit_ref, ki_ref, wit_ref, *, q_scale, wi_scale):
    xb = x_ref[0].astype(BF16)
    aw = k_ref.shape[2]
    iw = qit_ref.shape[1]
    half = ATT_HEAD_DIM // 2
    cos_t = cos_ref[...]
    sin_t = sin_ref[...]

    pk = jnp.dot(xb, wr_ref[...], preferred_element_type=F32) + br_ref[...]
    for j in range(aw // LANES):
        sl = slice(j * LANES, (j + 1) * LANES)
        k_ref[0, :, sl] = _rope_rows(pk[:, sl], cos_t, sin_t).astype(k_ref.dtype)
    ki_ref[0] = _rope_rows(pk[:, aw:aw + LANES], cos_t, sin_t).astype(ki_ref.dtype)

    pt = lax.dot_general(wc_ref[...], xb, (((1,), (1,)), ((), ())),
                         preferred_element_type=F32) + bc_ref[...]
    ct = cost_ref[...]
    st = sint_ref[...]

    def rope_cols(src0, dst_ref, nheads, scale):
        for h in range(nheads):
            r0 = src0 + h * ATT_HEAD_DIM
            x1 = pt[r0:r0 + half]
            x2 = pt[r0 + half:r0 + 2 * half]
            d0 = h * ATT_HEAD_DIM
            dst_ref[0, d0:d0 + half, :] = ((x1 * ct - x2 * st) * scale).astype(dst_ref.dtype)
            dst_ref[0, d0 + half:d0 + 2 * half, :] = ((x1 * st + x2 * ct) * scale).astype(dst_ref.dtype)

    rope_cols(0, qt_ref, aw // ATT_HEAD_DIM, q_scale)
    vt_ref[0] = pt[aw:2 * aw].astype(vt_ref.dtype)
    rope_cols(2 * aw, qit_ref, iw // IDX_DIM, 1.0)
    wit_ref[0] = pt[2 * aw + iw:] * wi_scale


def attn_projections(x, wr, br, wc, bc, cos_r, sin_r, cos_c, sin_c, *, tm=512):
    B, S, D = x.shape
    aw = ATT_HEADS * ATT_HEAD_DIM
    iw = IDX_HEADS * IDX_DIM
    tm = min(tm, S)
    kern = functools.partial(_attn_proj_kernel, q_scale=ATT_HEAD_DIM ** -0.5 * LOG2_E,
                             wi_scale=IDX_HEADS ** -0.5 * IDX_DIM ** -0.5)
    full = lambda a: pl.BlockSpec(a.shape, lambda b, i: (0,) * a.ndim)
    row = lambda w: pl.BlockSpec((1, tm, w), lambda b, i: (b, i, 0))
    colm = lambda r: pl.BlockSpec((1, r, tm), lambda b, i: (b, 0, i))
    return pl.pallas_call(
        kern,
        grid=(B, S // tm),
        in_specs=[row(D), full(wr), full(br), full(wc), full(bc),
                  pl.BlockSpec((tm, LANES), lambda b, i: (i, 0)),
                  pl.BlockSpec((tm, LANES), lambda b, i: (i, 0)),
                  pl.BlockSpec((ATT_HEAD_DIM // 2, tm), lambda b, i: (0, i)),
                  pl.BlockSpec((ATT_HEAD_DIM // 2, tm), lambda b, i: (0, i))],
        out_specs=[colm(aw), row(aw), colm(aw), colm(iw), row(LANES), colm(8)],
        out_shape=[jax.ShapeDtypeStruct((B, aw, S), BF16),
                   jax.ShapeDtypeStruct((B, S, aw), BF16),
                   jax.ShapeDtypeStruct((B, aw, S), BF16),
                   jax.ShapeDtypeStruct((B, iw, S), BF16),
                   jax.ShapeDtypeStruct((B, S, LANES), BF16),
                   jax.ShapeDtypeStruct((B, 8, S), F32)],
        compiler_params=pltpu.CompilerParams(
            dimension_semantics=("arbitrary", "arbitrary"),
            vmem_limit_bytes=VMEM_LIMIT_MID),
        name="attn_projections",
    )(x, wr, br, wc, bc, cos_r, sin_r, cos_c, sin_c)


MLSTM_HEADS = 8
MLSTM_QK_DIM = 64
MLSTM_V_DIM = 128
MLSTM_CHUNK = 128
CONV_WIDTH = 4
HALO = 8


def _silu(x):
    return x / (1.0 + jnp.exp(-x))


def _sigmoid(x):
    return 1.0 / (1.0 + jnp.exp(-x))


def _log_sigmoid(x):
    return jnp.minimum(x, 0.0) - jnp.log(1.0 + jnp.exp(-jnp.abs(x)))


def _mlstm_proj_kernel(x_ref, w_ref, b_ref, wt_ref, bt_ref, conv_ref,
                       mqt_ref, mk_ref, mvt_ref, ogt_ref, ifc_ref, ift_ref, ext_ref, *, q_scale):
    i = pl.program_id(1)
    tm = x_ref.shape[1]
    qkw = 2 * mk_ref.shape[2]
    vw = mvt_ref.shape[1]
    nh = ift_ref.shape[1] // 2
    xb = x_ref[0].astype(BF16)

    @pl.when(i == 0)
    def _():
        ext_ref[0:HALO, :] = jnp.zeros((HALO, qkw), F32)

    pqk = jnp.dot(xb, w_ref[:, 0:qkw], preferred_element_type=F32) + b_ref[:, 0:qkw]
    ext_ref[HALO:HALO + tm, :] = pqk
    acc = None
    for j in range(CONV_WIDTH):
        term = ext_ref[pl.ds(HALO - CONV_WIDTH + 1 + j, tm), :] * conv_ref[j:j + 1, :]
        acc = term if acc is None else acc + term
    ext_ref[0:HALO, :] = pqk[tm - HALO:tm, :]
    qk = _silu(acc)
    mqt_ref[0] = jnp.transpose(qk[:, 0:qkw // 2] * q_scale).astype(mqt_ref.dtype)
    mk_ref[0] = qk[:, qkw // 2:qkw].astype(mk_ref.dtype)

    pg = jnp.dot(xb, w_ref[:, qkw:], preferred_element_type=F32) + b_ref[:, qkw:]
    lane = lax.broadcasted_iota(jnp.int32, pg.shape, 1)
    ifc_ref[0] = jnp.where(lane < nh, pg, _log_sigmoid(pg))

    pt = lax.dot_general(wt_ref[...], xb, (((1,), (1,)), ((), ())),
                         preferred_element_type=F32) + bt_ref[...]
    mvt_ref[0] = pt[0:vw].astype(mvt_ref.dtype)
    ogt_ref[0] = _sigmoid(pt[vw:2 * vw]).astype(ogt_ref.dtype)
    gates = pt[2 * vw:]
    rowi = lax.broadcasted_iota(jnp.int32, gates.shape, 0)
    ift_ref[0] = jnp.where(rowi < nh, gates, _log_sigmoid(gates))


def mlstm_projections(x, w, b, wt, bt, conv, *, tm=512):
    B, S, D = x.shape
    qw = MLSTM_HEADS * MLSTM_QK_DIM
    vw = MLSTM_HEADS * MLSTM_V_DIM
    tm = min(tm, S)
    kern = functools.partial(_mlstm_proj_kernel, q_scale=MLSTM_QK_DIM ** -0.5)
    full = lambda a: pl.BlockSpec(a.shape, lambda b_, i: (0,) * a.ndim)
    row = lambda w_: pl.BlockSpec((1, tm, w_), lambda b_, i: (b_, i, 0))
    colm = lambda r: pl.BlockSpec((1, r, tm), lambda b_, i: (b_, 0, i))
    return pl.pallas_call(
        kern,
        grid=(B, S // tm),
        in_specs=[row(D), full(w), full(b), full(wt), full(bt), full(conv)],
        out_specs=[colm(qw), row(qw), colm(vw), colm(vw), row(LANES), colm(2 * MLSTM_HEADS)],
        out_shape=[jax.ShapeDtypeStruct((B, qw, S), BF16),
                   jax.ShapeDtypeStruct((B, S, qw), BF16),
                   jax.ShapeDtypeStruct((B, vw, S), BF16),
                   jax.ShapeDtypeStruct((B, vw, S), BF16),
                   jax.ShapeDtypeStruct((B, S, LANES), F32),
                   jax.ShapeDtypeStruct((B, 2 * MLSTM_HEADS, S), F32)],
        scratch_shapes=[pltpu.VMEM((HALO + tm, 2 * qw), F32)],
        compiler_params=pltpu.CompilerParams(
            dimension_semantics=("arbitrary", "arbitrary"),
            vmem_limit_bytes=VMEM_LIMIT_BIG),
        name="mlstm_projections",
    )(x, w, b, wt, bt, conv)


GN_EPS = 1e-6


def _mlstm_kernel(mqt_ref, mk_ref, mvt_ref, ogt_ref, ifc_ref, ift_ref, gain_ref, y_ref,
                  ct_ref, m_ref):
    c = pl.program_id(1)
    L = mk_ref.shape[1]
    nh = ift_ref.shape[1] // 2
    dk = mk_ref.shape[2] // nh
    dv = mvt_ref.shape[1] // nh

    @pl.when(c == 0)
    def _():
        ct_ref[...] = jnp.zeros(ct_ref.shape, F32)
        m_ref[...] = jnp.zeros(m_ref.shape, F32)

    r_i = lax.broadcasted_iota(jnp.int32, (L, L), 0)
    c_i = lax.broadcasted_iota(jnp.int32, (L, L), 1)
    causal_t = r_i <= c_i
    tril = jnp.where(c_i <= r_i, 1.0, 0.0)
    triu = jnp.where(causal_t, 1.0, 0.0)
    ifc = ifc_ref[0]
    ift = ift_ref[0]
    b_cols = jnp.dot(tril, ifc, preferred_element_type=F32, precision=lax.Precision.HIGHEST)
    b_rows = jnp.dot(ift, triu, preferred_element_type=F32, precision=lax.Precision.HIGHEST)
    ones_rows = jnp.where(lax.broadcasted_iota(jnp.int32, (BF16_ROWS, L), 0) == 0,
                          1.0, 0.0).astype(BF16)
    first_half_rows = lax.broadcasted_iota(jnp.int32, (2 * dk, L), 0) < dk
    first_half_lanes = lax.broadcasted_iota(jnp.int32, (L, 2 * dk), 1) < dk

    k_pair, q_heads, s_heads, inter_heads = [], [], [], []
    for p in range(nh // 2):
        k_pair.append(mk_ref[0, :, 2 * p * dk:2 * (p + 1) * dk])
        qt_pair = mqt_ref[0, 2 * p * dk:2 * (p + 1) * dk, :]
        zero = jnp.zeros_like(qt_pair)
        q_heads.append(jnp.where(first_half_rows, qt_pair, zero))
        q_heads.append(jnp.where(first_half_rows, zero, qt_pair))
    for h in range(nh):
        s_heads.append(jnp.dot(k_pair[h // 2], q_heads[h], preferred_element_type=F32))
        inter_heads.append(jnp.dot(ct_ref[h].astype(BF16), q_heads[h],
                                   preferred_element_type=F32))

    for h in range(nh):
        m = m_ref[h]
        a_col = ifc[:, h:h + 1] - b_cols[:, nh + h:nh + h + 1]
        amat = jnp.where(causal_t, a_col, NEG_INF)
        big_m = jnp.maximum(m, jnp.max(amat, axis=0, keepdims=True))
        decay = jnp.exp(amat - big_m)
        w_inter = jnp.exp(m - big_m)
        s = (s_heads[h] * decay).astype(BF16)
        vt_ext = jnp.concatenate([mvt_ref[0, h * dv:(h + 1) * dv, :], ones_rows], axis=0)
        tot = jnp.dot(vt_ext, s, preferred_element_type=F32) + w_inter * inter_heads[h]
        den = tot[dv:dv + 1]
        b_row = b_rows[nh + h:nh + h + 1, :]
        hh = tot[0:dv] / jnp.maximum(jnp.abs(den), jnp.exp(-(b_row + big_m)))
        mu = jnp.mean(hh, axis=0, keepdims=True)
        xc = hh - mu
        var = jnp.mean(xc * xc, axis=0, keepdims=True)
        hn = (xc * lax.rsqrt(var + GN_EPS) * gain_ref[h * dv:(h + 1) * dv, :]
              * ogt_ref[0, h * dv:(h + 1) * dv, :].astype(F32))
        y_ref[0, :, h * dv:(h + 1) * dv] = jnp.transpose(hn).astype(y_ref.dtype)

        b_last = b_cols[L - 1:L, nh + h:nh + h + 1]
        g_col = b_last + a_col
        m_new = jnp.maximum(b_last + m, jnp.max(g_col, axis=0, keepdims=True))
        carry = jnp.exp(b_last + m - m_new)
        head_lanes = first_half_lanes if h % 2 == 0 else jnp.logical_not(first_half_lanes)
        wk = jnp.where(head_lanes, k_pair[h // 2].astype(F32) * jnp.exp(g_col - m_new),
                       0.0).astype(BF16)
        ct_ref[h] = carry * ct_ref[h] + jnp.dot(vt_ext, wk, preferred_element_type=F32)
        m_ref[h] = m_new


def mlstm_scan(mqt, mk, mvt, ogt, ifc, ift, gain_b):
    B, S, qw = mk.shape
    vw = mvt.shape[1]
    L = min(MLSTM_CHUNK, S)
    nh = ift.shape[1] // 2
    row = lambda w_: pl.BlockSpec((1, L, w_), lambda b_, c: (b_, c, 0))
    colm = lambda r: pl.BlockSpec((1, r, L), lambda b_, c: (b_, 0, c))
    return pl.pallas_call(
        _mlstm_kernel,
        grid=(B, S // L),
        in_specs=[colm(qw), row(qw), colm(vw), colm(vw), row(LANES), colm(2 * nh),
                  pl.BlockSpec(gain_b.shape, lambda b_, c: (0, 0))],
        out_specs=row(vw),
        out_shape=jax.ShapeDtypeStruct((B, S, vw), BF16),
        scratch_shapes=[pltpu.VMEM((nh, vw // nh + BF16_ROWS, 2 * qw // nh), F32),
                        pltpu.VMEM((nh, 1, 1), F32)],
        compiler_params=pltpu.CompilerParams(
            dimension_semantics=("arbitrary", "arbitrary")),
        name="mlstm_scan",
    )(mqt, mk, mvt, ogt, ifc, ift, gain_b)


LN_EPS = 1e-5


def _layer_norm(z, gain, bias):
    mu = jnp.mean(z, axis=1, keepdims=True)
    zc = z - mu
    var = jnp.mean(zc * zc, axis=1, keepdims=True)
    return zc * lax.rsqrt(var + LN_EPS) * gain + bias


def _merge_kernel(x_ref, ya_ref, ym_ref, wg_ref, bg_ref, wa_ref, wm_ref, wo_ref, g_ref, b_ref,
                  o_ref, *, alpha):
    x = x_ref[...]
    xb = x.astype(BF16)
    d = x.shape[1]
    ga = _sigmoid(jnp.dot(xb, wg_ref[:, 0:d], preferred_element_type=F32) + bg_ref[:, 0:d])
    merged = ga * jnp.dot(ya_ref[...], wa_ref[...], preferred_element_type=F32)
    gm = _sigmoid(jnp.dot(xb, wg_ref[:, d:2 * d], preferred_element_type=F32) + bg_ref[:, d:2 * d])
    merged = merged + gm * jnp.dot(ym_ref[...], wm_ref[...], preferred_element_type=F32)
    z = alpha * x + jnp.dot(merged.astype(BF16), wo_ref[...], preferred_element_type=F32)
    o_ref[...] = _layer_norm(z, g_ref[...], b_ref[...])


def merge_branches(x2, ya, ym, wg, bg, wa, wm, wo, g, b, *, alpha, tm=512):
    T, D = x2.shape
    tm = min(tm, T)
    full = lambda a: pl.BlockSpec(a.shape, lambda i: (0,) * a.ndim)
    row = lambda w_: pl.BlockSpec((tm, w_), lambda i: (i, 0))
    return pl.pallas_call(
        functools.partial(_merge_kernel, alpha=alpha),
        grid=(T // tm,),
        in_specs=[row(D), row(ya.shape[1]), row(ym.shape[1]), full(wg), full(bg), full(wa),
                  full(wm), full(wo), full(g), full(b)],
        out_specs=row(D),
        out_shape=jax.ShapeDtypeStruct((T, D), F32),
        compiler_params=pltpu.CompilerParams(
            dimension_semantics=("arbitrary",), vmem_limit_bytes=VMEM_LIMIT_MID),
        name="merge_branches",
    )(x2, ya, ym, wg, bg, wa, wm, wo, g, b)


N_GROUPS = 4
EXPERTS_PER_GROUP = 4
N_EXPERTS = N_GROUPS * EXPERTS_PER_GROUP


def _first_lane_of_max(vals, vmax, lane):
    return jnp.min(jnp.where(vals == vmax, lane, LANES), axis=1, keepdims=True)


MOE_SUB = 128
EXPERTS_PER_STEP = 2


def _moe_kernel(x_ref, wr_ref, br_ref, wg_ref, wu_ref, wd_ref, g_ref, b_ref, o_ref,
                tril_ref, slot_ref, xg_ref, cw_ref, y_ref, tiles_ref, *, alpha):
    w = pl.program_id(0)
    e = pl.program_id(1)
    tm = x_ref.shape[0]
    slots = xg_ref.shape[0]
    lane = lax.broadcasted_iota(jnp.int32, (tm, LANES), 1)

    @pl.when(jnp.logical_and(w == 0, e == 0))
    def _():
        r_i = lax.broadcasted_iota(jnp.int32, (tm, tm), 0)
        c_i = lax.broadcasted_iota(jnp.int32, (tm, tm), 1)
        tril_ref[...] = jnp.where(c_i < r_i, 1.0, 0.0).astype(BF16)

    @pl.when(e == 0)
    def _():
        x = x_ref[...]
        x_hi = x.astype(BF16)
        x_lo = (x - x_hi.astype(F32)).astype(BF16)
        w_hi = wr_ref[...].astype(BF16)
        w_lo = (wr_ref[...] - w_hi.astype(F32)).astype(BF16)
        hi2 = jnp.dot(x_hi, jnp.concatenate([w_hi, w_lo], axis=1),
                      preferred_element_type=F32)
        logits = (hi2[:, 0:LANES] + jnp.dot(x_lo, w_hi, preferred_element_type=F32)
                  + hi2[:, LANES:2 * LANES] + br_ref[...])
        g = jnp.where(lane < N_GROUPS, logits, NEG_INF)
        gmax = jnp.max(g, axis=1, keepdims=True)
        g_w = 1.0 / jnp.sum(jnp.exp(g - gmax), axis=1, keepdims=True)
        g_sel = _first_lane_of_max(g, gmax, lane)
        lo = N_GROUPS + EXPERTS_PER_GROUP * g_sel
        ev = jnp.where(jnp.logical_and(lane >= lo, lane < lo + EXPERTS_PER_GROUP), logits, NEG_INF)
        v1 = jnp.max(ev, axis=1, keepdims=True)
        i1 = _first_lane_of_max(ev, v1, lane)
        ev2 = jnp.where(lane == i1, NEG_INF, ev)
        v2 = jnp.max(ev2, axis=1, keepdims=True)
        i2 = _first_lane_of_max(ev2, v2, lane)
        r = jnp.exp(v2 - v1)
        p1 = 1.0 / (1.0 + r)
        p2 = r / (1.0 + r)
        comb = jnp.where(lane == i1, g_w * p1, 0.0) + jnp.where(lane == i2, g_w * p2, 0.0)
        comb_hi = comb.astype(BF16)
        comb_lo = (comb - comb_hi.astype(F32)).astype(BF16)

        onehot = jnp.where(lane == g_sel, 1.0, 0.0)
        before = jnp.dot(tril_ref[...], onehot.astype(BF16), preferred_element_type=F32)
        rank = jnp.sum(jnp.where(lane == g_sel, before, 0.0), axis=1, keepdims=True)
        total = jnp.sum(onehot, axis=0, keepdims=True)
        lane1 = lax.broadcasted_iota(jnp.int32, (1, LANES), 1)
        start = jnp.zeros((tm, 1), F32)
        first_tile = jnp.int32(0)
        for grp in range(N_GROUPS):
            n_tok = jnp.sum(jnp.where(lane1 == grp, total, 0.0)).astype(jnp.int32)
            n_tile = (n_tok + MOE_SUB - 1) // MOE_SUB
            tiles_ref[grp] = first_tile
            tiles_ref[N_GROUPS + grp] = n_tile
            start = jnp.where(g_sel == grp, (first_tile * MOE_SUB).astype(F32), start)
            first_tile = first_tile + n_tile
        slot = start + rank
        slot_ref[...] = slot
        slot_row = jnp.transpose(jnp.broadcast_to(slot, (tm, LANES)))[0:1]
        pick = slot_row == lax.broadcasted_iota(jnp.int32, (slots, tm), 0).astype(F32)
        pmat = jnp.where(pick, 1.0, 0.0).astype(BF16)
        xg_ref[...] = jnp.dot(pmat, x_hi, preferred_element_type=F32).astype(BF16)
        cw2 = jnp.dot(pmat, jnp.concatenate([comb_hi, comb_lo], axis=1),
                      preferred_element_type=F32)
        cw_ref[...] = cw2[:, 0:LANES] + cw2[:, LANES:2 * LANES]
        y_ref[...] = jnp.zeros(y_ref.shape, F32)

    first_expert = e * EXPERTS_PER_STEP
    grp = first_expert // EXPERTS_PER_GROUP
    first = tiles_ref[grp]
    n_tile = tiles_ref[N_GROUPS + grp]

    def expert(tile, rows):
        base = pl.multiple_of(tile * MOE_SUB, MOE_SUB)
        xg = xg_ref[pl.ds(base, rows), :]
        lane_s = lax.broadcasted_iota(jnp.int32, (rows, LANES), 1)
        out = None
        for k in range(EXPERTS_PER_STEP):
            hg = jnp.dot(xg, wg_ref[k], preferred_element_type=F32)
            hu = jnp.dot(xg, wu_ref[k], preferred_element_type=F32)
            hdn = (_silu(hg) * hu).astype(BF16)
            cw = jnp.sum(jnp.where(lane_s == N_GROUPS + first_expert + k,
                                   cw_ref[pl.ds(base, rows), :], 0.0), axis=1, keepdims=True)
            term = cw * jnp.dot(hdn, wd_ref[k], preferred_element_type=F32)
            out = term if out is None else out + term
        y_ref[pl.ds(base, rows), :] += out

    def expert_pair(j, carry):
        expert(first + 2 * j, 2 * MOE_SUB)
        return carry

    @pl.when(n_tile == 3)
    def _():
        expert(first, 3 * MOE_SUB)

    @pl.when(n_tile != 3)
    def _():
        lax.fori_loop(0, n_tile // 2, expert_pair, 0)

        @pl.when(n_tile % 2 == 1)
        def _():
            expert(first + n_tile - 1, MOE_SUB)

    @pl.when(e == pl.num_programs(1) - 1)
    def _():
        pick = slot_ref[...] == lax.broadcasted_iota(jnp.int32, (tm, slots), 1).astype(F32)
        pmat_t = jnp.where(pick, 1.0, 0.0).astype(BF16)
        moe = jnp.dot(pmat_t, y_ref[...].astype(BF16), preferred_element_type=F32)
        o_ref[...] = _layer_norm(alpha * x_ref[...] + moe, g_ref[...], b_ref[...])


def moe_layer(x2, wr, br, wg, wu, wd, g, b, *, alpha, tm=1024):
    T, D = x2.shape
    E, _, F = wg.shape
    tm = min(tm, T)
    slots = tm + N_GROUPS * MOE_SUB
    full = lambda a: pl.BlockSpec(a.shape, lambda i, e: (0,) * a.ndim)
    return pl.pallas_call(
        functools.partial(_moe_kernel, alpha=alpha),
        grid=(T // tm, E // EXPERTS_PER_STEP),
        in_specs=[pl.BlockSpec((tm, D), lambda i, e: (i, 0)), full(wr), full(br),
                  pl.BlockSpec((EXPERTS_PER_STEP, D, F), lambda i, e: (e, 0, 0)),
                  pl.BlockSpec((EXPERTS_PER_STEP, D, F), lambda i, e: (e, 0, 0)),
                  pl.BlockSpec((EXPERTS_PER_STEP, F, D), lambda i, e: (e, 0, 0)),
                  full(g), full(b)],
        out_specs=pl.BlockSpec((tm, D), lambda i, e: (i, 0)),
        out_shape=jax.ShapeDtypeStruct((T, D), F32),
        scratch_shapes=[pltpu.VMEM((tm, tm), BF16),
                        pltpu.VMEM((tm, 1), F32),
                        pltpu.VMEM((slots, D), BF16),
                        pltpu.VMEM((slots, LANES), F32),
                        pltpu.VMEM((slots, D), F32),
                        pltpu.SMEM((2 * N_GROUPS,), jnp.int32)],
        compiler_params=pltpu.CompilerParams(
            dimension_semantics=("arbitrary", "arbitrary"), vmem_limit_bytes=VMEM_LIMIT_BIG),
        name="moe_layer",
    )(x2, wr, br, wg, wu, wd, g, b)


DEPTH = 1
DEEPNORM_ALPHA = (2.0 * DEPTH) ** 0.25


def _pad_cols(a, width):
    return jnp.pad(a, ((0, 0), (0, width - a.shape[1])))


def kernel(x, w_in, b_in, conv_m, gn_m_gain, w_branch_attn, w_branch_mlstm, w_out, ln1_gain, ln1_bias, w_router_group, b_router_group, w_router_expert, b_router_expert, w_exp_gate, w_exp_up, w_exp_down, ln2_gain, ln2_bias):
    B, S, D = x.shape
    aw = ATT_HEADS * ATT_HEAD_DIM
    iw = IDX_HEADS * IDX_DIM
    qw = MLSTM_HEADS * MLSTM_QK_DIM
    vw = MLSTM_HEADS * MLSTM_V_DIM
    widths = (aw, aw, aw, iw, IDX_DIM, IDX_HEADS, qw, qw, vw, MLSTM_HEADS, MLSTM_HEADS, vw, D, D)
    offs = [0]
    for w_ in widths:
        offs.append(offs[-1] + w_)
    col = lambda k: w_in[:, offs[k]:offs[k + 1]]
    bia = lambda k: b_in[offs[k]:offs[k + 1]]
    (A_Q, A_K, A_V, I_Q, I_K, I_W, M_Q, M_K, M_V, M_I, M_F, M_O, G_A, G_M) = range(14)

    wr = jnp.concatenate([col(A_K), _pad_cols(col(I_K), LANES)], 1).astype(BF16)
    br = jnp.concatenate([bia(A_K), jnp.pad(bia(I_K), (0, LANES - IDX_DIM))])[None, :]
    wc = jnp.concatenate([col(A_Q), col(A_V), col(I_Q), _pad_cols(col(I_W), 8)], 1).T.astype(BF16)
    bc = jnp.concatenate([bia(A_Q), bia(A_V), bia(I_Q), jnp.pad(bia(I_W), (0, 8 - IDX_HEADS))])[:, None]
    gpad = LANES - 2 * MLSTM_HEADS
    wm = jnp.concatenate([col(M_Q), col(M_K), col(M_I),
                          _pad_cols(col(M_F), MLSTM_HEADS + gpad)], 1).astype(BF16)
    bm = jnp.concatenate([bia(M_Q), bia(M_K), bia(M_I), jnp.pad(bia(M_F), (0, gpad))])[None, :]
    wmt = jnp.concatenate([col(M_V), col(M_O), col(M_I), col(M_F)], 1).T.astype(BF16)
    bmt = jnp.concatenate([bia(M_V), bia(M_O), bia(M_I), bia(M_F)])[:, None]
    wgate = jnp.concatenate([col(G_A), col(G_M)], 1).astype(BF16)
    bgate = jnp.concatenate([bia(G_A), bia(G_M)])[None, :]

    half = ATT_HEAD_DIM // 2
    inv = ROPE_THETA ** (-jnp.arange(0, ATT_HEAD_DIM, 2, dtype=F32) / ATT_HEAD_DIM)
    ang = jnp.arange(S, dtype=F32)[:, None] * inv[None, :]
    cos, sin = jnp.cos(ang), jnp.sin(ang)
    cos_r = jnp.tile(cos, (1, LANES // half))
    sin_r = jnp.tile(jnp.concatenate([-sin, sin], 1), (1, LANES // ATT_HEAD_DIM))

    qt, k, vt, qit, ki, wit = attn_projections(x, wr, br, wc, bc, cos_r, sin_r, cos.T, sin.T)
    y_attn = dsa_attention(qt, k, vt, qit, ki, wit, topk=min(IDX_TOPK_MAX, S // 4))

    mqt, mk, mvt, ogt, ifc, ift = mlstm_projections(x, wm, bm, wmt, bmt, conv_m)
    y_mlstm = mlstm_scan(mqt, mk, mvt, ogt, ifc, ift,
                         jnp.broadcast_to(gn_m_gain[:, None], (vw, LANES)))

    x1 = merge_branches(x.reshape(B * S, D), y_attn.reshape(B * S, aw), y_mlstm.reshape(B * S, vw),
                        wgate, bgate, w_branch_attn.astype(BF16), w_branch_mlstm.astype(BF16),
                        w_out.astype(BF16), ln1_gain[None, :], ln1_bias[None, :],
                        alpha=DEEPNORM_ALPHA)

    w_router = _pad_cols(jnp.concatenate([w_router_group, w_router_expert], 1), LANES)
    b_router = jnp.pad(jnp.concatenate([b_router_group, b_router_expert]),
                       (0, LANES - N_GROUPS - N_EXPERTS))[None, :]
    out = moe_layer(x1, w_router, b_router, w_exp_gate.astype(BF16), w_exp_up.astype(BF16),
                    w_exp_down.astype(BF16), ln2_gain[None, :], ln2_bias[None, :],
                    alpha=DEEPNORM_ALPHA)
    return out.reshape(B, S, D)
```

```python
import functools

import jax
import jax.numpy as jnp
from jax import lax
from jax.experimental import pallas as pl
from jax.experimental.pallas import tpu as pltpu

F32 = jnp.float32
BF16 = jnp.bfloat16
NEG_INF = float("-inf")
LOG2_E = 1.4426950408889634

ATT_HEADS = 8
ATT_HEAD_DIM = 64
IDX_HEADS = 4
IDX_DIM = 64
IDX_TOPK_MAX = 256
Q_BLOCK = 128
ROPE_THETA = 10000.0

LANES = 128
BF16_ROWS = 16
V7X_VMEM_BYTES = 64 * 1024 * 1024
VMEM_LIMIT_BIG = V7X_VMEM_BYTES * 7 // 8
VMEM_LIMIT_MID = V7X_VMEM_BYTES * 3 // 4


def _key_to_f32(u):
    ks = u ^ jnp.int32(-2 ** 31)
    bits = ks ^ ((ks >> 31) & jnp.int32(0x7FFFFFFF))
    return lax.bitcast_convert_type(bits, F32)


def _dsa_kernel(qt_ref, k_ref, vt_ref, qit_ref, ki_ref, wit_ref, o_ref,
                sc_ref, qm_ref, m_ref, l_ref, acc_ref, *, topk, kchunk):
    qb = pl.program_id(1)
    tq = o_ref.shape[1]
    heads = qt_ref.shape[1] // ATT_HEAD_DIM
    step = 2 * LANES
    n_chunk = (qb * tq + tq + kchunk - 1) // kchunk
    qpos = lax.broadcasted_iota(jnp.int32, (1, tq), 1) + qb * tq

    qit = qit_ref[0]
    zpad = jnp.zeros((LANES - IDX_DIM, tq), BF16)
    qi_pair = []
    for p in range(IDX_HEADS // 2):
        cols = [jnp.concatenate([qit[h * IDX_DIM:(h + 1) * IDX_DIM], zpad], axis=0)
                for h in (2 * p, 2 * p + 1)]
        qi_pair.append(jnp.concatenate(cols, axis=1))
    wit = wit_ref[0]

    achunk = min(2 * kchunk, sc_ref.shape[0])
    n_achunk = (qb * tq + tq + achunk - 1) // achunk
    pairs = IDX_HEADS // 2

    def score_body(c, carry):
        off = pl.multiple_of(c * achunk, achunk)

        def qk(i):
            ki = ki_ref[0, pl.ds(off + (i // pairs) * step, step), :]
            return jnp.dot(ki, qi_pair[i % pairs], preferred_element_type=F32)

        n_dots = (achunk // step) * pairs
        ahead = 3
        pending = [qk(i) for i in range(ahead)]
        for t in range(achunk // step):
            tot = None
            for p in range(pairs):
                i = t * pairs + p
                if i + ahead < n_dots:
                    pending.append(qk(i + ahead))
                s2 = pending.pop(0)
                for j in range(2):
                    h = 2 * p + j
                    s = jnp.maximum(s2[:, j * tq:(j + 1) * tq], 0.0) * wit[h:h + 1, :]
                    tot = s if tot is None else tot + s
            kpos = lax.broadcasted_iota(jnp.int32, (step, tq), 0) + (off + t * step)
            sc_ref[pl.ds(off + t * step, step), :] = jnp.where(kpos <= qpos, tot + 0.0, NEG_INF)
        return carry

    lax.fori_loop(0, n_achunk, score_body, 0)

    def count(pred):
        def body(j, acc):
            off = pl.multiple_of(j * kchunk, kchunk)
            for t in range(kchunk // LANES):
                x = sc_ref[pl.ds(off + t * LANES, LANES), :]
                acc = acc + jnp.where(pred(x, off + t * LANES), 1.0, 0.0)
            return acc
        acc = lax.fori_loop(0, n_chunk, body, jnp.zeros((LANES, tq), F32))
        return jnp.sum(acc, axis=0, keepdims=True)

    kf = float(topk)
    short = qpos < topk

    def bit_pass(i, state, frozen):
        u, cge, cgt = state
        cand = u | (jnp.int32(1) << (31 - i))
        thr = _key_to_f32(cand)
        cnt = count(lambda x, off: x >= thr)
        ok = jnp.logical_and(cnt >= kf, jnp.logical_not(frozen))
        fail = jnp.logical_and(cnt < kf, jnp.logical_not(frozen))
        return (jnp.where(ok, cand, u), jnp.where(ok, cnt, cge), jnp.where(fail, cnt, cgt))

    c_pos = count(lambda x, off: x > 0.0)
    never = jnp.zeros((1, tq), jnp.bool_)
    state = bit_pass(0, (jnp.zeros((1, tq), jnp.int32), jnp.zeros((1, tq), F32),
                         jnp.zeros((1, tq), F32)), never)
    frozen = jnp.logical_and(c_pos < kf, state[1] >= kf)
    state = (state[0], state[1], jnp.where(frozen, c_pos, state[2]))

    fixed_bits = 22
    state = lax.fori_loop(1, fixed_bits, lambda i, st: bit_pass(i, st, frozen), state)

    def all_settled(st):
        done = jnp.logical_or(jnp.logical_or(short, frozen), st[1] == kf)
        return jnp.min(jnp.where(done, 1.0, 0.0)) > 0.0

    def refine(carry):
        i, st, _ = carry
        st = bit_pass(i, st, frozen)
        st = bit_pass(i + 1, st, frozen)
        return i + 2, st, all_settled(st)

    _, (u, cge, cgt), _ = lax.while_loop(
        lambda c: jnp.logical_and(c[0] < 32, jnp.logical_not(c[2])),
        refine, (jnp.int32(fixed_bits), state, all_settled(state)))
    tau = jnp.where(short, NEG_INF, _key_to_f32(u))
    need = jnp.where(cge == kf, kf, kf - cgt)
    need = jnp.where(short, -1.0, need)

    r_i = lax.broadcasted_iota(jnp.int32, (LANES, LANES), 0)
    c_i = lax.broadcasted_iota(jnp.int32, (LANES, LANES), 1)
    tril = jnp.where(c_i <= r_i, 1.0, 0.0).astype(BF16)

    def bias_body(c, seen):
        base = pl.multiple_of(c * achunk, achunk)
        ranks = [jnp.dot(tril, jnp.where(sc_ref[pl.ds(base + t * LANES, LANES), :] == tau,
                                         1.0, 0.0).astype(BF16), preferred_element_type=F32)
                 for t in range(achunk // LANES)]
        for t, rank in enumerate(ranks):
            x = sc_ref[pl.ds(base + t * LANES, LANES), :]
            keep = jnp.logical_or(x > tau, jnp.logical_and(x == tau, rank + seen <= need))
            sc_ref[pl.ds(base + t * LANES, LANES), :] = jnp.where(keep, 0.0, NEG_INF)
            seen = seen + rank[LANES - 1:LANES, :]
        return seen

    lax.fori_loop(0, n_achunk, bias_body, jnp.zeros((1, tq), F32))

    qt = qt_ref[0]
    zrow = jnp.zeros((ATT_HEAD_DIM, tq), BF16)
    for p in range(heads // 2):
        a = qt[(2 * p) * ATT_HEAD_DIM:(2 * p + 1) * ATT_HEAD_DIM]
        b = qt[(2 * p + 1) * ATT_HEAD_DIM:(2 * p + 2) * ATT_HEAD_DIM]
        qm_ref[p] = jnp.concatenate([jnp.concatenate([a, zrow], axis=0),
                                     jnp.concatenate([zrow, b], axis=0)], axis=1)
    m_ref[...] = jnp.full(m_ref.shape, NEG_INF, F32)
    l_ref[...] = jnp.zeros(l_ref.shape, F32)
    acc_ref[...] = jnp.zeros(acc_ref.shape, F32)
    ones_rows = jnp.ones((BF16_ROWS, step), BF16)

    def attend(off, n_keys):
        stages = [(sub, p) for sub in range(n_keys // step) for p in range(heads // 2)]

        def qk(stage):
            sub, p = stage
            kp = k_ref[0, pl.ds(off + sub * step, step), p * LANES:(p + 1) * LANES]
            return jnp.dot(kp, qm_ref[p], preferred_element_type=F32)

        ahead = 4
        pending = [qk(st) for st in stages[:ahead]]
        for i, (sub, p) in enumerate(stages):
            if i + ahead < len(stages):
                pending.append(qk(stages[i + ahead]))
            s2 = pending.pop(0)
            koff = off + sub * step
            bias = sc_ref[pl.ds(koff, step), :]
            for j in range(2):
                h = 2 * p + j
                s = s2[:, j * tq:(j + 1) * tq] + bias
                m_old = m_ref[h]
                m_new = jnp.maximum(m_old, jnp.max(s, axis=0, keepdims=True))
                m_use = jnp.where(m_new == NEG_INF, 0.0, m_new)
                pexp = jnp.exp2(s - m_use).astype(BF16)
                alpha = jnp.exp2(m_old - m_use)
                vt = jnp.concatenate(
                    [vt_ref[0, h * ATT_HEAD_DIM:(h + 1) * ATT_HEAD_DIM, pl.ds(koff, step)],
                     ones_rows], axis=0)
                pv = jnp.dot(vt, pexp, preferred_element_type=F32)
                acc_ref[h] = alpha * acc_ref[h] + pv[0:ATT_HEAD_DIM]
                l_ref[h] = alpha * l_ref[h] + pv[ATT_HEAD_DIM:ATT_HEAD_DIM + 1]
                m_ref[h] = m_new

    big = min(4 * kchunk, sc_ref.shape[0])
    n_big = (n_chunk * kchunk) // big
    rest = n_chunk - n_big * (big // kchunk)

    def big_body(c, carry):
        attend(pl.multiple_of(c * big, big), big)
        return carry

    lax.fori_loop(0, n_big, big_body, 0)
    if big > kchunk:
        tail = pl.multiple_of(n_big * big, kchunk)

        @pl.when(rest >= 2)
        def _():
            attend(tail, 2 * kchunk)

        @pl.when(rest % 2 == 1)
        def _():
            attend(pl.multiple_of(tail + (rest - 1) * kchunk, kchunk), kchunk)

    out_t = jnp.concatenate([acc_ref[h] / l_ref[h] for h in range(heads)], axis=0)
    o_ref[0] = jnp.transpose(out_t).astype(o_ref.dtype)


def dsa_attention(qt, k, vt, qit, ki, wit, *, topk, kchunk=512):
    B, W, S = qt.shape
    tq = Q_BLOCK
    kchunk = min(kchunk, S)
    heads = W // ATT_HEAD_DIM
    kern = functools.partial(_dsa_kernel, topk=topk, kchunk=kchunk)
    qcol = lambda a: pl.BlockSpec((1, a.shape[1], tq), lambda b, i: (b, 0, i))
    whole = lambda a: pl.BlockSpec((1,) + a.shape[1:], lambda b, i: (b, 0, 0))
    return pl.pallas_call(
        kern,
        grid=(B, S // tq),
        in_specs=[qcol(qt), whole(k), whole(vt), qcol(qit), whole(ki), qcol(wit)],
        out_specs=pl.BlockSpec((1, tq, W), lambda b, i: (b, i, 0)),
        out_shape=jax.ShapeDtypeStruct((B, S, W), BF16),
        scratch_shapes=[
            pltpu.VMEM((S, tq), F32),
            pltpu.VMEM((heads // 2, LANES, 2 * tq), BF16),
            pltpu.VMEM((heads, 1, tq), F32),
            pltpu.VMEM((heads, 1, tq), F32),
            pltpu.VMEM((heads, ATT_HEAD_DIM, tq), F32),
        ],
        compiler_params=pltpu.CompilerParams(
            dimension_semantics=("arbitrary", "arbitrary"),
            vmem_limit_bytes=VMEM_LIMIT_BIG),
        name="dsa_attention",
    )(qt, k, vt, qit, ki, wit)


def _rope_rows(x, cos_t, sin_t):
    lane = lax.broadcasted_iota(jnp.int32, x.shape, 1)
    swapped = jnp.where((lane % ATT_HEAD_DIM) < ATT_HEAD_DIM // 2,
                        pltpu.roll(x, LANES - ATT_HEAD_DIM // 2, 1),
                        pltpu.roll(x, ATT_HEAD_DIM // 2, 1))
    return x * cos_t + swapped * sin_t


def _attn_proj_kernel(x_ref, wr_ref, br_ref, wc_ref, bc_ref, cos_ref, sin_ref, cost_ref, sint_ref,
                      qt_ref, k_ref, vt_ref, qit_ref, ki_ref, wit_ref, *, q_scale, wi_scale):
    xb = x_ref[0].astype(BF16)
    aw = k_ref.shape[2]
    iw = qit_ref.shape[1]
    half = ATT_HEAD_DIM // 2
    cos_t = cos_ref[...]
    sin_t = sin_ref[...]

    pk = jnp.dot(xb, wr_ref[...], preferred_element_type=F32) + br_ref[...]
    for j in range(aw // LANES):
        sl = slice(j * LANES, (j + 1) * LANES)
        k_ref[0, :, sl] = _rope_rows(pk[:, sl], cos_t, sin_t).astype(k_ref.dtype)
    ki_ref[0] = _rope_rows(pk[:, aw:aw + LANES], cos_t, sin_t).astype(ki_ref.dtype)

    pt = lax.dot_general(wc_ref[...], xb, (((1,), (1,)), ((), ())),
                         preferred_element_type=F32) + bc_ref[...]
    ct = cost_ref[...]
    st = sint_ref[...]

    def rope_cols(src0, dst_ref, nheads, scale):
        for h in range(nheads):
            r0 = src0 + h * ATT_HEAD_DIM
            x1 = pt[r0:r0 + half]
            x2 = pt[r0 + half:r0 + 2 * half]
            d0 = h * ATT_HEAD_DIM
            dst_ref[0, d0:d0 + half, :] = ((x1 * ct - x2 * st) * scale).astype(dst_ref.dtype)
            dst_ref[0, d0 + half:d0 + 2 * half, :] = ((x1 * st + x2 * ct) * scale).astype(dst_ref.dtype)

    rope_cols(0, qt_ref, aw // ATT_HEAD_DIM, q_scale)
    vt_ref[0] = pt[aw:2 * aw].astype(vt_ref.dtype)
    rope_cols(2 * aw, qit_ref, iw // IDX_DIM, 1.0)
    wit_ref[0] = pt[2 * aw + iw:] * wi_scale


def attn_projections(x, wr, br, wc, bc, cos_r, sin_r, cos_c, sin_c, *, tm=512):
    B, S, D = x.shape
    aw = ATT_HEADS * ATT_HEAD_DIM
    iw = IDX_HEADS * IDX_DIM
    tm = min(tm, S)
    kern = functools.partial(_attn_proj_kernel, q_scale=ATT_HEAD_DIM ** -0.5 * LOG2_E,
                             wi_scale=IDX_HEADS ** -0.5 * IDX_DIM ** -0.5)
    full = lambda a: pl.BlockSpec(a.shape, lambda b, i: (0,) * a.ndim)
    row = lambda w: pl.BlockSpec((1, tm, w), lambda b, i: (b, i, 0))
    colm = lambda r: pl.BlockSpec((1, r, tm), lambda b, i: (b, 0, i))
    return pl.pallas_call(
        kern,
        grid=(B, S // tm),
        in_specs=[row(D), full(wr), full(br), full(wc), full(bc),
                  pl.BlockSpec((tm, LANES), lambda b, i: (i, 0)),
                  pl.BlockSpec((tm, LANES), lambda b, i: (i, 0)),
                  pl.BlockSpec((ATT_HEAD_DIM // 2, tm), lambda b, i: (0, i)),
                  pl.BlockSpec((ATT_HEAD_DIM // 2, tm), lambda b, i: (0, i))],
        out_specs=[colm(aw), row(aw), colm(aw), colm(iw), row(LANES), colm(8)],
        out_shape=[jax.ShapeDtypeStruct((B, aw, S), BF16),
                   jax.ShapeDtypeStruct((B, S, aw), BF16),
                   jax.ShapeDtypeStruct((B, aw, S), BF16),
                   jax.ShapeDtypeStruct((B, iw, S), BF16),
                   jax.ShapeDtypeStruct((B, S, LANES), BF16),
                   jax.ShapeDtypeStruct((B, 8, S), F32)],
        compiler_params=pltpu.CompilerParams(
            dimension_semantics=("arbitrary", "arbitrary"),
            vmem_limit_bytes=VMEM_LIMIT_MID),
        name="attn_projections",
    )(x, wr, br, wc, bc, cos_r, sin_r, cos_c, sin_c)


MLSTM_HEADS = 8
MLSTM_QK_DIM = 64
MLSTM_V_DIM = 128
MLSTM_CHUNK = 128
CONV_WIDTH = 4
HALO = 8


def _silu(x):
    return x / (1.0 + jnp.exp(-x))


def _sigmoid(x):
    return 1.0 / (1.0 + jnp.exp(-x))


def _log_sigmoid(x):
    return jnp.minimum(x, 0.0) - jnp.log(1.0 + jnp.exp(-jnp.abs(x)))


def _mlstm_proj_kernel(x_ref, w_ref, b_ref, wt_ref, bt_ref, conv_ref,
                       mqt_ref, mk_ref, mvt_ref, ogt_ref, ifc_ref, ift_ref, ext_ref, *, q_scale):
    i = pl.program_id(1)
    tm = x_ref.shape[1]
    qkw = 2 * mk_ref.shape[2]
    vw = mvt_ref.shape[1]
    nh = ift_ref.shape[1] // 2
    xb = x_ref[0].astype(BF16)

    @pl.when(i == 0)
    def _():
        ext_ref[0:HALO, :] = jnp.zeros((HALO, qkw), F32)

    pqk = jnp.dot(xb, w_ref[:, 0:qkw], preferred_element_type=F32) + b_ref[:, 0:qkw]
    ext_ref[HALO:HALO + tm, :] = pqk
    acc = None
    for j in range(CONV_WIDTH):
        term = ext_ref[pl.ds(HALO - CONV_WIDTH + 1 + j, tm), :] * conv_ref[j:j + 1, :]
        acc = term if acc is None else acc + term
    ext_ref[0:HALO, :] = pqk[tm - HALO:tm, :]
    qk = _silu(acc)
    mqt_ref[0] = jnp.transpose(qk[:, 0:qkw // 2] * q_scale).astype(mqt_ref.dtype)
    mk_ref[0] = qk[:, qkw // 2:qkw].astype(mk_ref.dtype)

    pg = jnp.dot(xb, w_ref[:, qkw:], preferred_element_type=F32) + b_ref[:, qkw:]
    lane = lax.broadcasted_iota(jnp.int32, pg.shape, 1)
    ifc_ref[0] = jnp.where(lane < nh, pg, _log_sigmoid(pg))

    pt = lax.dot_general(wt_ref[...], xb, (((1,), (1,)), ((), ())),
                         preferred_element_type=F32) + bt_ref[...]
    mvt_ref[0] = pt[0:vw].astype(mvt_ref.dtype)
    ogt_ref[0] = _sigmoid(pt[vw:2 * vw]).astype(ogt_ref.dtype)
    gates = pt[2 * vw:]
    rowi = lax.broadcasted_iota(jnp.int32, gates.shape, 0)
    ift_ref[0] = jnp.where(rowi < nh, gates, _log_sigmoid(gates))


def mlstm_projections(x, w, b, wt, bt, conv, *, tm=512):
    B, S, D = x.shape
    qw = MLSTM_HEADS * MLSTM_QK_DIM
    vw = MLSTM_HEADS * MLSTM_V_DIM
    tm = min(tm, S)
    kern = functools.partial(_mlstm_proj_kernel, q_scale=MLSTM_QK_DIM ** -0.5)
    full = lambda a: pl.BlockSpec(a.shape, lambda b_, i: (0,) * a.ndim)
    row = lambda w_: pl.BlockSpec((1, tm, w_), lambda b_, i: (b_, i, 0))
    colm = lambda r: pl.BlockSpec((1, r, tm), lambda b_, i: (b_, 0, i))
    return pl.pallas_call(
        kern,
        grid=(B, S // tm),
        in_specs=[row(D), full(w), full(b), full(wt), full(bt), full(conv)],
        out_specs=[colm(qw), row(qw), colm(vw), colm(vw), row(LANES), colm(2 * MLSTM_HEADS)],
        out_shape=[jax.ShapeDtypeStruct((B, qw, S), BF16),
                   jax.ShapeDtypeStruct((B, S, qw), BF16),
                   jax.ShapeDtypeStruct((B, vw, S), BF16),
                   jax.ShapeDtypeStruct((B, vw, S), BF16),
                   jax.ShapeDtypeStruct((B, S, LANES), F32),
                   jax.ShapeDtypeStruct((B, 2 * MLSTM_HEADS, S), F32)],
        scratch_shapes=[pltpu.VMEM((HALO + tm, 2 * qw), F32)],
        compiler_params=pltpu.CompilerParams(
            dimension_semantics=("arbitrary", "arbitrary"),
            vmem_limit_bytes=VMEM_LIMIT_BIG),
        name="mlstm_projections",
    )(x, w, b, wt, bt, conv)


GN_EPS = 1e-6


def _mlstm_kernel(mqt_ref, mk_ref, mvt_ref, ogt_ref, ifc_ref, ift_ref, gain_ref, y_ref,
                  ct_ref, m_ref):
    c = pl.program_id(1)
    L = mk_ref.shape[1]
    nh = ift_ref.shape[1] // 2
    dk = mk_ref.shape[2] // nh
    dv = mvt_ref.shape[1] // nh

    @pl.when(c == 0)
    def _():
        ct_ref[...] = jnp.zeros(ct_ref.shape, F32)
        m_ref[...] = jnp.zeros(m_ref.shape, F32)

    r_i = lax.broadcasted_iota(jnp.int32, (L, L), 0)
    c_i = lax.broadcasted_iota(jnp.int32, (L, L), 1)
    causal_t = r_i <= c_i
    tril = jnp.where(c_i <= r_i, 1.0, 0.0)
    triu = jnp.where(causal_t, 1.0, 0.0)
    ifc = ifc_ref[0]
    ift = ift_ref[0]
    b_cols = jnp.dot(tril, ifc, preferred_element_type=F32, precision=lax.Precision.HIGHEST)
    b_rows = jnp.dot(ift, triu, preferred_element_type=F32, precision=lax.Precision.HIGHEST)
    ones_rows = jnp.where(lax.broadcasted_iota(jnp.int32, (BF16_ROWS, L), 0) == 0,
                          1.0, 0.0).astype(BF16)
    first_half_rows = lax.broadcasted_iota(jnp.int32, (2 * dk, L), 0) < dk
    first_half_lanes = lax.broadcasted_iota(jnp.int32, (L, 2 * dk), 1) < dk

    k_pair, q_heads, s_heads, inter_heads = [], [], [], []
    for p in range(nh // 2):
        k_pair.append(mk_ref[0, :, 2 * p * dk:2 * (p + 1) * dk])
        qt_pair = mqt_ref[0, 2 * p * dk:2 * (p + 1) * dk, :]
        zero = jnp.zeros_like(qt_pair)
        q_heads.append(jnp.where(first_half_rows, qt_pair, zero))
        q_heads.append(jnp.where(first_half_rows, zero, qt_pair))
    for h in range(nh):
        s_heads.append(jnp.dot(k_pair[h // 2], q_heads[h], preferred_element_type=F32))
        inter_heads.append(jnp.dot(ct_ref[h].astype(BF16), q_heads[h],
                                   preferred_element_type=F32))

    for h in range(nh):
        m = m_ref[h]
        a_col = ifc[:, h:h + 1] - b_cols[:, nh + h:nh + h + 1]
        amat = jnp.where(causal_t, a_col, NEG_INF)
        big_m = jnp.maximum(m, jnp.max(amat, axis=0, keepdims=True))
        decay = jnp.exp(amat - big_m)
        w_inter = jnp.exp(m - big_m)
        s = (s_heads[h] * decay).astype(BF16)
        vt_ext = jnp.concatenate([mvt_ref[0, h * dv:(h + 1) * dv, :], ones_rows], axis=0)
        tot = jnp.dot(vt_ext, s, preferred_element_type=F32) + w_inter * inter_heads[h]
        den = tot[dv:dv + 1]
        b_row = b_rows[nh + h:nh + h + 1, :]
        hh = tot[0:dv] / jnp.maximum(jnp.abs(den), jnp.exp(-(b_row + big_m)))
        mu = jnp.mean(hh, axis=0, keepdims=True)
        xc = hh - mu
        var = jnp.mean(xc * xc, axis=0, keepdims=True)
        hn = (xc * lax.rsqrt(var + GN_EPS) * gain_ref[h * dv:(h + 1) * dv, :]
              * ogt_ref[0, h * dv:(h + 1) * dv, :].astype(F32))
        y_ref[0, :, h * dv:(h + 1) * dv] = jnp.transpose(hn).astype(y_ref.dtype)

        b_last = b_cols[L - 1:L, nh + h:nh + h + 1]
        g_col = b_last + a_col
        m_new = jnp.maximum(b_last + m, jnp.max(g_col, axis=0, keepdims=True))
        carry = jnp.exp(b_last + m - m_new)
        head_lanes = first_half_lanes if h % 2 == 0 else jnp.logical_not(first_half_lanes)
        wk = jnp.where(head_lanes, k_pair[h // 2].astype(F32) * jnp.exp(g_col - m_new),
                       0.0).astype(BF16)
        ct_ref[h] = carry * ct_ref[h] + jnp.dot(vt_ext, wk, preferred_element_type=F32)
        m_ref[h] = m_new


def mlstm_scan(mqt, mk, mvt, ogt, ifc, ift, gain_b):
    B, S, qw = mk.shape
    vw = mvt.shape[1]
    L = min(MLSTM_CHUNK, S)
    nh = ift.shape[1] // 2
    row = lambda w_: pl.BlockSpec((1, L, w_), lambda b_, c: (b_, c, 0))
    colm = lambda r: pl.BlockSpec((1, r, L), lambda b_, c: (b_, 0, c))
    return pl.pallas_call(
        _mlstm_kernel,
        grid=(B, S // L),
        in_specs=[colm(qw), row(qw), colm(vw), colm(vw), row(LANES), colm(2 * nh),
                  pl.BlockSpec(gain_b.shape, lambda b_, c: (0, 0))],
        out_specs=row(vw),
        out_shape=jax.ShapeDtypeStruct((B, S, vw), BF16),
        scratch_shapes=[pltpu.VMEM((nh, vw // nh + BF16_ROWS, 2 * qw // nh), F32),
                        pltpu.VMEM((nh, 1, 1), F32)],
        compiler_params=pltpu.CompilerParams(
            dimension_semantics=("arbitrary", "arbitrary")),
        name="mlstm_scan",
    )(mqt, mk, mvt, ogt, ifc, ift, gain_b)


LN_EPS = 1e-5


def _layer_norm(z, gain, bias):
    mu = jnp.mean(z, axis=1, keepdims=True)
    zc = z - mu
    var = jnp.mean(zc * zc, axis=1, keepdims=True)
    return zc * lax.rsqrt(var + LN_EPS) * gain + bias


def _merge_kernel(x_ref, ya_ref, ym_ref, wg_ref, bg_ref, wa_ref, wm_ref, wo_ref, g_ref, b_ref,
                  o_ref, *, alpha):
    x = x_ref[...]
    xb = x.astype(BF16)
    d = x.shape[1]
    ga = _sigmoid(jnp.dot(xb, wg_ref[:, 0:d], preferred_element_type=F32) + bg_ref[:, 0:d])
    merged = ga * jnp.dot(ya_ref[...], wa_ref[...], preferred_element_type=F32)
    gm = _sigmoid(jnp.dot(xb, wg_ref[:, d:2 * d], preferred_element_type=F32) + bg_ref[:, d:2 * d])
    merged = merged + gm * jnp.dot(ym_ref[...], wm_ref[...], preferred_element_type=F32)
    z = alpha * x + jnp.dot(merged.astype(BF16), wo_ref[...], preferred_element_type=F32)
    o_ref[...] = _layer_norm(z, g_ref[...], b_ref[...])


def merge_branches(x2, ya, ym, wg, bg, wa, wm, wo, g, b, *, alpha, tm=512):
    T, D = x2.shape
    tm = min(tm, T)
    full = lambda a: pl.BlockSpec(a.shape, lambda i: (0,) * a.ndim)
    row = lambda w_: pl.BlockSpec((tm, w_), lambda i: (i, 0))
    return pl.pallas_call(
        functools.partial(_merge_kernel, alpha=alpha),
        grid=(T // tm,),
        in_specs=[row(D), row(ya.shape[1]), row(ym.shape[1]), full(wg), full(bg), full(wa),
                  full(wm), full(wo), full(g), full(b)],
        out_specs=row(D),
        out_shape=jax.ShapeDtypeStruct((T, D), F32),
        compiler_params=pltpu.CompilerParams(
            dimension_semantics=("arbitrary",), vmem_limit_bytes=VMEM_LIMIT_MID),
        name="merge_branches",
    )(x2, ya, ym, wg, bg, wa, wm, wo, g, b)


N_GROUPS = 4
EXPERTS_PER_GROUP = 4
N_EXPERTS = N_GROUPS * EXPERTS_PER_GROUP


def _first_lane_of_max(vals, vmax, lane):
    return jnp.min(jnp.where(vals == vmax, lane, LANES), axis=1, keepdims=True)


MOE_SUB = 128
EXPERTS_PER_STEP = 2


def _moe_kernel(x_ref, wr_ref, br_ref, wg_ref, wu_ref, wd_ref, g_ref, b_ref, o_ref,
                tril_ref, slot_ref, xg_ref, cw_ref, y_ref, tiles_ref, *, alpha):
    w = pl.program_id(0)
    e = pl.program_id(1)
    tm = x_ref.shape[0]
    slots = xg_ref.shape[0]
    lane = lax.broadcasted_iota(jnp.int32, (tm, LANES), 1)

    @pl.when(jnp.logical_and(w == 0, e == 0))
    def _():
        r_i = lax.broadcasted_iota(jnp.int32, (tm, tm), 0)
        c_i = lax.broadcasted_iota(jnp.int32, (tm, tm), 1)
        tril_ref[...] = jnp.where(c_i < r_i, 1.0, 0.0).astype(BF16)

    @pl.when(e == 0)
    def _():
        x = x_ref[...]
        x_hi = x.astype(BF16)
        x_lo = (x - x_hi.astype(F32)).astype(BF16)
        w_hi = wr_ref[...].astype(BF16)
        w_lo = (wr_ref[...] - w_hi.astype(F32)).astype(BF16)
        hi2 = jnp.dot(x_hi, jnp.concatenate([w_hi, w_lo], axis=1),
                      preferred_element_type=F32)
        logits = (hi2[:, 0:LANES] + jnp.dot(x_lo, w_hi, preferred_element_type=F32)
                  + hi2[:, LANES:2 * LANES] + br_ref[...])
        g = jnp.where(lane < N_GROUPS, logits, NEG_INF)
        gmax = jnp.max(g, axis=1, keepdims=True)
        g_w = 1.0 / jnp.sum(jnp.exp(g - gmax), axis=1, keepdims=True)
        g_sel = _first_lane_of_max(g, gmax, lane)
        lo = N_GROUPS + EXPERTS_PER_GROUP * g_sel
        ev = jnp.where(jnp.logical_and(lane >= lo, lane < lo + EXPERTS_PER_GROUP), logits, NEG_INF)
        v1 = jnp.max(ev, axis=1, keepdims=True)
        i1 = _first_lane_of_max(ev, v1, lane)
        ev2 = jnp.where(lane == i1, NEG_INF, ev)
        v2 = jnp.max(ev2, axis=1, keepdims=True)
        i2 = _first_lane_of_max(ev2, v2, lane)
        r = jnp.exp(v2 - v1)
        p1 = 1.0 / (1.0 + r)
        p2 = r / (1.0 + r)
        comb = jnp.where(lane == i1, g_w * p1, 0.0) + jnp.where(lane == i2, g_w * p2, 0.0)
        comb_hi = comb.astype(BF16)
        comb_lo = (comb - comb_hi.astype(F32)).astype(BF16)

        onehot = jnp.where(lane == g_sel, 1.0, 0.0)
        before = jnp.dot(tril_ref[...], onehot.astype(BF16), preferred_element_type=F32)
        rank = jnp.sum(jnp.where(lane == g_sel, before, 0.0), axis=1, keepdims=True)
        total = jnp.sum(onehot, axis=0, keepdims=True)
        lane1 = lax.broadcasted_iota(jnp.int32, (1, LANES), 1)
        start = jnp.zeros((tm, 1), F32)
        first_tile = jnp.int32(0)
        for grp in range(N_GROUPS):
            n_tok = jnp.sum(jnp.where(lane1 == grp, total, 0.0)).astype(jnp.int32)
            n_tile = (n_tok + MOE_SUB - 1) // MOE_SUB
            tiles_ref[grp] = first_tile
            tiles_ref[N_GROUPS + grp] = n_tile
            start = jnp.where(g_sel == grp, (first_tile * MOE_SUB).astype(F32), start)
            first_tile = first_tile + n_tile
        slot = start + rank
        slot_ref[...] = slot
        slot_row = jnp.transpose(jnp.broadcast_to(slot, (tm, LANES)))[0:1]
        pick = slot_row == lax.broadcasted_iota(jnp.int32, (slots, tm), 0).astype(F32)
        pmat = jnp.where(pick, 1.0, 0.0).astype(BF16)
        xg_ref[...] = jnp.dot(pmat, x_hi, preferred_element_type=F32).astype(BF16)
        cw2 = jnp.dot(pmat, jnp.concatenate([comb_hi, comb_lo], axis=1),
                      preferred_element_type=F32)
        cw_ref[...] = cw2[:, 0:LANES] + cw2[:, LANES:2 * LANES]
        y_ref[...] = jnp.zeros(y_ref.shape, F32)

    first_expert = e * EXPERTS_PER_STEP
    grp = first_expert // EXPERTS_PER_GROUP
    first = tiles_ref[grp]
    n_tile = tiles_ref[N_GROUPS + grp]

    def expert(tile, rows):
        base = pl.multiple_of(tile * MOE_SUB, MOE_SUB)
        xg = xg_ref[pl.ds(base, rows), :]
        lane_s = lax.broadcasted_iota(jnp.int32, (rows, LANES), 1)
        out = None
        for k in range(EXPERTS_PER_STEP):
            hg = jnp.dot(xg, wg_ref[k], preferred_element_type=F32)
            hu = jnp.dot(xg, wu_ref[k], preferred_element_type=F32)
            hdn = (_silu(hg) * hu).astype(BF16)
            cw = jnp.sum(jnp.where(lane_s == N_GROUPS + first_expert + k,
                                   cw_ref[pl.ds(base, rows), :], 0.0), axis=1, keepdims=True)
            term = cw * jnp.dot(hdn, wd_ref[k], preferred_element_type=F32)
            out = term if out is None else out + term
        y_ref[pl.ds(base, rows), :] += out

    def expert_pair(j, carry):
        expert(first + 2 * j, 2 * MOE_SUB)
        return carry

    @pl.when(n_tile == 3)
    def _():
        expert(first, 3 * MOE_SUB)

    @pl.when(n_tile != 3)
    def _():
        lax.fori_loop(0, n_tile // 2, expert_pair, 0)

        @pl.when(n_tile % 2 == 1)
        def _():
            expert(first + n_tile - 1, MOE_SUB)

    @pl.when(e == pl.num_programs(1) - 1)
    def _():
        pick = slot_ref[...] == lax.broadcasted_iota(jnp.int32, (tm, slots), 1).astype(F32)
        pmat_t = jnp.where(pick, 1.0, 0.0).astype(BF16)
        moe = jnp.dot(pmat_t, y_ref[...].astype(BF16), preferred_element_type=F32)
        o_ref[...] = _layer_norm(alpha * x_ref[...] + moe, g_ref[...], b_ref[...])


def moe_layer(x2, wr, br, wg, wu, wd, g, b, *, alpha, tm=1024):
    T, D = x2.shape
    E, _, F = wg.shape
    tm = min(tm, T)
    slots = tm + N_GROUPS * MOE_SUB
    full = lambda a: pl.BlockSpec(a.shape, lambda i, e: (0,) * a.ndim)
    return pl.pallas_call(
        functools.partial(_moe_kernel, alpha=alpha),
        grid=(T // tm, E // EXPERTS_PER_STEP),
        in_specs=[pl.BlockSpec((tm, D), lambda i, e: (i, 0)), full(wr), full(br),
                  pl.BlockSpec((EXPERTS_PER_STEP, D, F), lambda i, e: (e, 0, 0)),
                  pl.BlockSpec((EXPERTS_PER_STEP, D, F), lambda i, e: (e, 0, 0)),
                  pl.BlockSpec((EXPERTS_PER_STEP, F, D), lambda i, e: (e, 0, 0)),
                  full(g), full(b)],
        out_specs=pl.BlockSpec((tm, D), lambda i, e: (i, 0)),
        out_shape=jax.ShapeDtypeStruct((T, D), F32),
        scratch_shapes=[pltpu.VMEM((tm, tm), BF16),
                        pltpu.VMEM((tm, 1), F32),
                        pltpu.VMEM((slots, D), BF16),
                        pltpu.VMEM((slots, LANES), F32),
                        pltpu.VMEM((slots, D), F32),
                        pltpu.SMEM((2 * N_GROUPS,), jnp.int32)],
        compiler_params=pltpu.CompilerParams(
            dimension_semantics=("arbitrary", "arbitrary"), vmem_limit_bytes=VMEM_LIMIT_BIG),
        name="moe_layer",
    )(x2, wr, br, wg, wu, wd, g, b)


DEPTH = 1
DEEPNORM_ALPHA = (2.0 * DEPTH) ** 0.25


def _pad_cols(a, width):
    return jnp.pad(a, ((0, 0), (0, width - a.shape[1])))


def kernel(x, w_in, b_in, conv_m, gn_m_gain, w_branch_attn, w_branch_mlstm, w_out, ln1_gain, ln1_bias, w_router_group, b_router_group, w_router_expert, b_router_expert, w_exp_gate, w_exp_up, w_exp_down, ln2_gain, ln2_bias):
    B, S, D = x.shape
    aw = ATT_HEADS * ATT_HEAD_DIM
    iw = IDX_HEADS * IDX_DIM
    qw = MLSTM_HEADS * MLSTM_QK_DIM
    vw = MLSTM_HEADS * MLSTM_V_DIM
    widths = (aw, aw, aw, iw, IDX_DIM, IDX_HEADS, qw, qw, vw, MLSTM_HEADS, MLSTM_HEADS, vw, D, D)
    offs = [0]
    for w_ in widths:
        offs.append(offs[-1] + w_)
    col = lambda k: w_in[:, offs[k]:offs[k + 1]]
    bia = lambda k: b_in[offs[k]:offs[k + 1]]
    (A_Q, A_K, A_V, I_Q, I_K, I_W, M_Q, M_K, M_V, M_I, M_F, M_O, G_A, G_M) = range(14)

    wr = jnp.concatenate([col(A_K), _pad_cols(col(I_K), LANES)], 1).astype(BF16)
    br = jnp.concatenate([bia(A_K), jnp.pad(bia(I_K), (0, LANES - IDX_DIM))])[None, :]
    wc = jnp.concatenate([col(A_Q), col(A_V), col(I_Q), _pad_cols(col(I_W), 8)], 1).T.astype(BF16)
    bc = jnp.concatenate([bia(A_Q), bia(A_V), bia(I_Q), jnp.pad(bia(I_W), (0, 8 - IDX_HEADS))])[:, None]
    gpad = LANES - 2 * MLSTM_HEADS
    wm = jnp.concatenate([col(M_Q), col(M_K), col(M_I),
                          _pad_cols(col(M_F), MLSTM_HEADS + gpad)], 1).astype(BF16)
    bm = jnp.concatenate([bia(M_Q), bia(M_K), bia(M_I), jnp.pad(bia(M_F), (0, gpad))])[None, :]
    wmt = jnp.concatenate([col(M_V), col(M_O), col(M_I), col(M_F)], 1).T.astype(BF16)
    bmt = jnp.concatenate([bia(M_V), bia(M_O), bia(M_I), bia(M_F)])[:, None]
    wgate = jnp.concatenate([col(G_A), col(G_M)], 1).astype(BF16)
    bgate = jnp.concatenate([bia(G_A), bia(G_M)])[None, :]

    half = ATT_HEAD_DIM // 2
    inv = ROPE_THETA ** (-jnp.arange(0, ATT_HEAD_DIM, 2, dtype=F32) / ATT_HEAD_DIM)
    ang = jnp.arange(S, dtype=F32)[:, None] * inv[None, :]
    cos, sin = jnp.cos(ang), jnp.sin(ang)
    cos_r = jnp.tile(cos, (1, LANES // half))
    sin_r = jnp.tile(jnp.concatenate([-sin, sin], 1), (1, LANES // ATT_HEAD_DIM))

    qt, k, vt, qit, ki, wit = attn_projections(x, wr, br, wc, bc, cos_r, sin_r, cos.T, sin.T)
    y_attn = dsa_attention(qt, k, vt, qit, ki, wit, topk=min(IDX_TOPK_MAX, S // 4))

    mqt, mk, mvt, ogt, ifc, ift = mlstm_projections(x, wm, bm, wmt, bmt, conv_m)
    y_mlstm = mlstm_scan(mqt, mk, mvt, ogt, ifc, ift,
                         jnp.broadcast_to(gn_m_gain[:, None], (vw, LANES)))

    x1 = merge_branches(x.reshape(B * S, D), y_attn.reshape(B * S, aw), y_mlstm.reshape(B * S, vw),
                        wgate, bgate, w_branch_attn.astype(BF16), w_branch_mlstm.astype(BF16),
                        w_out.astype(BF16), ln1_gain[None, :], ln1_bias[None, :],
                        alpha=DEEPNORM_ALPHA)

    w_router = _pad_cols(jnp.concatenate([w_router_group, w_router_expert], 1), LANES)
    b_router = jnp.pad(jnp.concatenate([b_router_group, b_router_expert]),
                       (0, LANES - N_GROUPS - N_EXPERTS))[None, :]
    out = moe_layer(x1, w_router, b_router, w_exp_gate.astype(BF16), w_exp_up.astype(BF16),
                    w_exp_down.astype(BF16), ln2_gain[None, :], ln2_bias[None, :],
                    alpha=DEEPNORM_ALPHA)
    return out.reshape(B, S, D)
```
